```python
import math
import jax, jax.numpy as jnp
from jax import lax
import numpy as np

D_MODEL = 2048
BATCH = 8
SEQ = 8192
DEPTH = 2

N_A = DEPTH // 2
N_B = DEPTH - N_A

POOL_WINDOWS = (2, 4, 8, 16)
N_POOL_GROUPS = len(POOL_WINDOWS)
POOL_GROUP = D_MODEL // N_POOL_GROUPS

N_HEADS = 16
QK_NOPE_DIM = 128
QK_ROPE_DIM = 64
QK_DIM = QK_NOPE_DIM + QK_ROPE_DIM
V_DIM = 128
Q_LORA_RANK = 512
KV_LORA_RANK = 512
ROPE_BASE = 10000.0
ATTN_SCALE = 1.0 / math.sqrt(QK_DIM)
Q_BLOCK = 128

FFN_MULT = 256
D_FF = ((8 * D_MODEL + 3 * FFN_MULT - 1) // (3 * FFN_MULT)) * FFN_MULT

NORM_EPS = 1e-6
POS_OFFSET_MAX = 4096

kernel_name = "yoco_pool_mla_hybrid"


def rms_norm(x, g):
    xf = x.astype(jnp.float32)
    y = xf * lax.rsqrt(jnp.mean(xf * xf, axis=-1, keepdims=True) + NORM_EPS)
    return (y * g.astype(jnp.float32)).astype(x.dtype)


def swiglu(h, w_gate, w_up, w_down):
    return (jax.nn.silu(h @ w_gate) * (h @ w_up)) @ w_down


def rope_tables(positions):
    half = QK_ROPE_DIM // 2
    inv_freq = ROPE_BASE ** (-jnp.arange(half, dtype=jnp.float32) / half)
    ang = positions.astype(jnp.float32)[..., None] * inv_freq
    return jnp.cos(ang), jnp.sin(ang)


def apply_rope(x, cos, sin):
    half = QK_ROPE_DIM // 2
    xf = x.astype(jnp.float32)
    x1, x2 = xf[..., :half], xf[..., half:]
    out = jnp.concatenate([x1 * cos - x2 * sin, x2 * cos + x1 * sin], axis=-1)
    return out.astype(x.dtype)


def multiscale_pool_mixer(h, pool_w, pool_scale):
    S = h.shape[1]
    hf = h.astype(jnp.float32)
    cs = jnp.cumsum(hf, axis=1)
    t = jnp.arange(S)
    outs = []
    for g, w in enumerate(POOL_WINDOWS):
        sl = slice(g * POOL_GROUP, (g + 1) * POOL_GROUP)
        c = cs[..., sl]
        c_prev = jnp.pad(c, ((0, 0), (w, 0), (0, 0)))[:, :S]
        count = jnp.minimum(t + 1, w).astype(jnp.float32)[None, :, None]
        diff = (c - c_prev) / count - hf[..., sl]
        outs.append(jnp.einsum('bsc,cd->bsd', diff.astype(h.dtype), pool_w[g]))
    return jnp.concatenate(outs, axis=-1) * pool_scale


def shared_mla_kv(h, w_kv_a, kv_latent_norm, w_kv_b, cos, sin):
    B, S, _ = h.shape
    kv = h @ w_kv_a
    c_kv = rms_norm(kv[..., :KV_LORA_RANK], kv_latent_norm)
    k_pe = apply_rope(kv[..., KV_LORA_RANK:], cos, sin)
    kv_up = (c_kv @ w_kv_b).reshape(B, S, N_HEADS, QK_NOPE_DIM + V_DIM)
    return kv_up[..., :QK_NOPE_DIM], k_pe, kv_up[..., QK_NOPE_DIM:]


def causal_mla_attention(q_nope, q_pe, k_nope, k_pe, v):
    S = q_nope.shape[1]
    outs = []
    for i in range(S // Q_BLOCK):
        s0, e = i * Q_BLOCK, (i + 1) * Q_BLOCK
        sc = (jnp.einsum('bqhd,bkhd->bhqk', q_nope[:, s0:e], k_nope[:, :e])
              + jnp.einsum('bqhr,bkr->bhqk', q_pe[:, s0:e], k_pe[:, :e]))
        sc = sc.astype(jnp.float32) * ATTN_SCALE
        mask = jnp.arange(e)[None, :] <= (s0 + jnp.arange(Q_BLOCK))[:, None]
        sc = jnp.where(mask, sc, -jnp.inf)
        p = jax.nn.softmax(sc, axis=-1).astype(v.dtype)
        outs.append(jnp.einsum('bhqk,bkhd->bqhd', p, v[:, :e]))
    return jnp.concatenate(outs, axis=1)


def mla_layer_mixer(h, w_q_a, q_latent_norm, w_q_b, w_o, k_nope, k_pe, v, cos, sin):
    B, S, _ = h.shape
    q = (rms_norm(h @ w_q_a, q_latent_norm) @ w_q_b).reshape(B, S, N_HEADS, QK_DIM)
    q_nope = q[..., :QK_NOPE_DIM]
    q_pe = apply_rope(q[..., QK_NOPE_DIM:], cos[:, :, None, :], sin[:, :, None, :])
    o = causal_mla_attention(q_nope, q_pe, k_nope, k_pe, v)
    return o.reshape(B, S, N_HEADS * V_DIM) @ w_o


def _fwd_setup_inputs(seed: int = 0) -> dict:
    key = jax.random.key(seed)
    ks = jax.random.split(key, 24)
    f32 = jnp.float32

    def dense(k, shape, fan_in):
        return jax.random.normal(k, shape, f32) * (fan_in ** -0.5)

    def gain(k, shape):
        return 1.0 + 0.05 * jax.random.normal(k, shape, f32)

    x = jax.random.normal(ks[0], (BATCH, SEQ, D_MODEL), f32)
    positions = (jax.random.randint(ks[1], (BATCH, 1), 0, POS_OFFSET_MAX, dtype=jnp.int32)
                 + jnp.arange(SEQ, dtype=jnp.int32)[None, :])
    return {
        "x": x,
        "positions": positions,
        "pool_norm": gain(ks[2], (N_A, D_MODEL)),
        "pool_w": dense(ks[3], (N_A, N_POOL_GROUPS, POOL_GROUP, POOL_GROUP), POOL_GROUP),
        "pool_scale": gain(ks[4], (N_A, D_MODEL)),
        "kv_in_norm": gain(ks[5], (D_MODEL,)),
        "w_kv_a": dense(ks[6], (D_MODEL, KV_LORA_RANK + QK_ROPE_DIM), D_MODEL),
        "kv_latent_norm": gain(ks[7], (KV_LORA_RANK,)),
        "w_kv_b": dense(ks[8], (KV_LORA_RANK, N_HEADS * (QK_NOPE_DIM + V_DIM)), KV_LORA_RANK),
        "attn_norm": gain(ks[9], (N_B, D_MODEL)),
        "w_q_a": dense(ks[10], (N_B, D_MODEL, Q_LORA_RANK), D_MODEL),
        "q_latent_norm": gain(ks[11], (N_B, Q_LORA_RANK)),
        "w_q_b": dense(ks[12], (N_B, Q_LORA_RANK, N_HEADS * QK_DIM), Q_LORA_RANK),
        "w_o": dense(ks[13], (N_B, N_HEADS * V_DIM, D_MODEL), N_HEADS * V_DIM),
        "ffn_norm": gain(ks[14], (DEPTH, D_MODEL)),
        "w_gate": dense(ks[15], (DEPTH, D_MODEL, D_FF), D_MODEL),
        "w_up": dense(ks[16], (DEPTH, D_MODEL, D_FF), D_MODEL),
        "w_down": dense(ks[17], (DEPTH, D_FF, D_MODEL), D_FF),
        "final_norm": gain(ks[18], (D_MODEL,)),
    }


def _fwd_reference(x, positions, pool_norm, pool_w, pool_scale, kv_in_norm, w_kv_a,
              kv_latent_norm, w_kv_b, attn_norm, w_q_a, q_latent_norm, w_q_b, w_o,
              ffn_norm, w_gate, w_up, w_down, final_norm):
    cos, sin = rope_tables(positions)
    k_nope = k_pe = v = None
    for l in range(DEPTH):
        if l == N_A:
            k_nope, k_pe, v = shared_mla_kv(rms_norm(x, kv_in_norm), w_kv_a,
                                            kv_latent_norm, w_kv_b, cos, sin)
        if l < N_A:
            x = x + multiscale_pool_mixer(rms_norm(x, pool_norm[l]), pool_w[l], pool_scale[l])
        else:
            b = l - N_A
            x = x + mla_layer_mixer(rms_norm(x, attn_norm[b]), w_q_a[b], q_latent_norm[b],
                                    w_q_b[b], w_o[b], k_nope, k_pe, v, cos, sin)
        x = x + swiglu(rms_norm(x, ffn_norm[l]), w_gate[l], w_up[l], w_down[l])
    return rms_norm(x, final_norm)


import jax as _jax
import jax.numpy as _jnp

TWIN_FORMAT = 'train_step'
FWD_PARAMS = ['x', 'positions', 'pool_norm', 'pool_w', 'pool_scale', 'kv_in_norm', 'w_kv_a', 'kv_latent_norm', 'w_kv_b', 'attn_norm', 'w_q_a', 'q_latent_norm', 'w_q_b', 'w_o', 'ffn_norm', 'w_gate', 'w_up', 'w_down', 'final_norm']
TWIN_WEIGHTS = ['pool_norm', 'pool_w', 'pool_scale', 'kv_in_norm', 'w_kv_a', 'kv_latent_norm', 'w_kv_b', 'attn_norm', 'w_q_a', 'q_latent_norm', 'w_q_b', 'w_o', 'ffn_norm', 'w_gate', 'w_up', 'w_down', 'final_norm']
TWIN_DIFF_INPUT = 'x'
TWIN_INPUTS = ['x', 'positions', 'pool_norm', 'pool_w', 'pool_scale', 'kv_in_norm', 'w_kv_a', 'kv_latent_norm', 'w_kv_b', 'attn_norm', 'w_q_a', 'q_latent_norm', 'w_q_b', 'w_o', 'ffn_norm', 'w_gate', 'w_up', 'w_down', 'final_norm', 'loss_target', 'm_pool_norm', 'm_pool_w', 'm_pool_scale', 'm_kv_in_norm', 'm_w_kv_a', 'm_kv_latent_norm', 'm_w_kv_b', 'm_attn_norm', 'm_w_q_a', 'm_q_latent_norm', 'm_w_q_b', 'm_w_o', 'm_ffn_norm', 'm_w_gate', 'm_w_up', 'm_w_down', 'm_final_norm', 'v_pool_norm', 'v_pool_w', 'v_pool_scale', 'v_kv_in_norm', 'v_w_kv_a', 'v_kv_latent_norm', 'v_w_kv_b', 'v_attn_norm', 'v_w_q_a', 'v_q_latent_norm', 'v_w_q_b', 'v_w_o', 'v_ffn_norm', 'v_w_gate', 'v_w_up', 'v_w_down', 'v_final_norm']
TWIN_OUTPUTS = ['loss', 'grad_x', 'grad_pool_norm', 'grad_pool_w', 'grad_pool_scale', 'grad_kv_in_norm', 'grad_w_kv_a', 'grad_kv_latent_norm', 'grad_w_kv_b', 'grad_attn_norm', 'grad_w_q_a', 'grad_q_latent_norm', 'grad_w_q_b', 'grad_w_o', 'grad_ffn_norm', 'grad_w_gate', 'grad_w_up', 'grad_w_down', 'grad_final_norm', 'delta_pool_norm', 'delta_pool_w', 'delta_pool_scale', 'delta_kv_in_norm', 'delta_w_kv_a', 'delta_kv_latent_norm', 'delta_w_kv_b', 'delta_attn_norm', 'delta_w_q_a', 'delta_q_latent_norm', 'delta_w_q_b', 'delta_w_o', 'delta_ffn_norm', 'delta_w_gate', 'delta_w_up', 'delta_w_down', 'delta_final_norm', 'new_m_pool_norm', 'new_m_pool_w', 'new_m_pool_scale', 'new_m_kv_in_norm', 'new_m_w_kv_a', 'new_m_kv_latent_norm', 'new_m_w_kv_b', 'new_m_attn_norm', 'new_m_w_q_a', 'new_m_q_latent_norm', 'new_m_w_q_b', 'new_m_w_o', 'new_m_ffn_norm', 'new_m_w_gate', 'new_m_w_up', 'new_m_w_down', 'new_m_final_norm', 'new_v_pool_norm', 'new_v_pool_w', 'new_v_pool_scale', 'new_v_kv_in_norm', 'new_v_w_kv_a', 'new_v_kv_latent_norm', 'new_v_w_kv_b', 'new_v_attn_norm', 'new_v_w_q_a', 'new_v_q_latent_norm', 'new_v_w_q_b', 'new_v_w_o', 'new_v_ffn_norm', 'new_v_w_gate', 'new_v_w_up', 'new_v_w_down', 'new_v_final_norm']
TWIN_LEAF_KINDS = {'loss': 'loss', 'grad_x': 'grad_x', 'grad_pool_norm': 'grad_w', 'grad_pool_w': 'grad_w', 'grad_pool_scale': 'grad_w', 'grad_kv_in_norm': 'grad_w', 'grad_w_kv_a': 'grad_w', 'grad_kv_latent_norm': 'grad_w', 'grad_w_kv_b': 'grad_w', 'grad_attn_norm': 'grad_w', 'grad_w_q_a': 'grad_w', 'grad_q_latent_norm': 'grad_w', 'grad_w_q_b': 'grad_w', 'grad_w_o': 'grad_w', 'grad_ffn_norm': 'grad_w', 'grad_w_gate': 'grad_w', 'grad_w_up': 'grad_w', 'grad_w_down': 'grad_w', 'grad_final_norm': 'grad_w', 'delta_pool_norm': 'delta_w', 'delta_pool_w': 'delta_w', 'delta_pool_scale': 'delta_w', 'delta_kv_in_norm': 'delta_w', 'delta_w_kv_a': 'delta_w', 'delta_kv_latent_norm': 'delta_w', 'delta_w_kv_b': 'delta_w', 'delta_attn_norm': 'delta_w', 'delta_w_q_a': 'delta_w', 'delta_q_latent_norm': 'delta_w', 'delta_w_q_b': 'delta_w', 'delta_w_o': 'delta_w', 'delta_ffn_norm': 'delta_w', 'delta_w_gate': 'delta_w', 'delta_w_up': 'delta_w', 'delta_w_down': 'delta_w', 'delta_final_norm': 'delta_w', 'new_m_pool_norm': 'new_m', 'new_m_pool_w': 'new_m', 'new_m_pool_scale': 'new_m', 'new_m_kv_in_norm': 'new_m', 'new_m_w_kv_a': 'new_m', 'new_m_kv_latent_norm': 'new_m', 'new_m_w_kv_b': 'new_m', 'new_m_attn_norm': 'new_m', 'new_m_w_q_a': 'new_m', 'new_m_q_latent_norm': 'new_m', 'new_m_w_q_b': 'new_m', 'new_m_w_o': 'new_m', 'new_m_ffn_norm': 'new_m', 'new_m_w_gate': 'new_m', 'new_m_w_up': 'new_m', 'new_m_w_down': 'new_m', 'new_m_final_norm': 'new_m', 'new_v_pool_norm': 'new_v', 'new_v_pool_w': 'new_v', 'new_v_pool_scale': 'new_v', 'new_v_kv_in_norm': 'new_v', 'new_v_w_kv_a': 'new_v', 'new_v_kv_latent_norm': 'new_v', 'new_v_w_kv_b': 'new_v', 'new_v_attn_norm': 'new_v', 'new_v_w_q_a': 'new_v', 'new_v_q_latent_norm': 'new_v', 'new_v_w_q_b': 'new_v', 'new_v_w_o': 'new_v', 'new_v_ffn_norm': 'new_v', 'new_v_w_gate': 'new_v', 'new_v_w_up': 'new_v', 'new_v_w_down': 'new_v', 'new_v_final_norm': 'new_v'}


def _forward(args):
    return _fwd_reference(*[args[k] for k in FWD_PARAMS])


def _output_shape():
    def fwd():
        inp = _fwd_setup_inputs(0)
        return _fwd_reference(*[inp[k] for k in FWD_PARAMS])
    out = _jax.eval_shape(fwd)
    return out.shape, out.dtype

N_MICROBATCH = 1
ADAM_LR = 0.001
ADAM_B1 = 0.9
ADAM_B2 = 0.999
ADAM_EPS = 1e-08
ADAM_WD = 0.01
ADAM_STEP = 10
PER_EXAMPLE_BATCH_AXIS = {'x': 0, 'positions': 0, 'loss_target': 0}
SHARED_INPUTS = []
_WEIGHT_DTYPES = {'pool_norm': _jnp.float32, 'pool_w': _jnp.float32, 'pool_scale': _jnp.float32, 'kv_in_norm': _jnp.float32, 'w_kv_a': _jnp.float32, 'kv_latent_norm': _jnp.float32, 'w_kv_b': _jnp.float32, 'attn_norm': _jnp.float32, 'w_q_a': _jnp.float32, 'q_latent_norm': _jnp.float32, 'w_q_b': _jnp.float32, 'w_o': _jnp.float32, 'ffn_norm': _jnp.float32, 'w_gate': _jnp.float32, 'w_up': _jnp.float32, 'w_down': _jnp.float32, 'final_norm': _jnp.float32}
MOMENT_SCALE = {'pool_norm': 1.011431e-01, 'pool_w': 1.025383e-01, 'pool_scale': 9.281912e-01, 'kv_in_norm': 2.493184e-02, 'w_kv_a': 4.616163e-02, 'kv_latent_norm': 4.827492e-02, 'w_kv_b': 1.635217e-02, 'attn_norm': 1.607437e-02, 'w_q_a': 3.250445e-02, 'q_latent_norm': 3.294623e-02, 'w_q_b': 1.329485e-02, 'w_o': 1.874372e-02, 'ffn_norm': 7.451522e-02, 'w_gate': 3.241078e-02, 'w_up': 3.137532e-02, 'w_down': 5.210261e-02, 'final_norm': 3.215385e+01}


def _to_microbatches(a, axis):
    t = _jnp.moveaxis(a, axis, 0)
    t = t.reshape((N_MICROBATCH, t.shape[0] // N_MICROBATCH) + t.shape[1:])
    return _jnp.moveaxis(t, 1, axis + 1)


def setup_inputs(seed: int = 0) -> dict:
    inp = _fwd_setup_inputs(seed)
    key = _jax.random.fold_in(_jax.random.key(seed), 7919)
    shape, _ = _output_shape()
    out = dict(inp)
    out["loss_target"] = _jax.random.normal(_jax.random.fold_in(key, 0), shape, _jnp.float32)
    for i, name in enumerate(TWIN_WEIGHTS):
        w = inp[name].astype(_jnp.float32)
        if MOMENT_SCALE is None:
            s = _jnp.sqrt(_jnp.mean(_jnp.square(w)) + 1e-30)
        else:
            s = MOMENT_SCALE[name]
        km, kv = _jax.random.split(_jax.random.fold_in(key, i + 1))
        out[name] = w
        out["m_" + name] = s * _jax.random.normal(km, w.shape, _jnp.float32)
        out["v_" + name] = (s * s) * _jax.random.uniform(kv, w.shape, _jnp.float32, 0.5, 1.5)
    if N_MICROBATCH > 1:
        for name, axis in PER_EXAMPLE_BATCH_AXIS.items():
            out[name] = _to_microbatches(out[name], axis)
    return {'x': out['x'], 'positions': out['positions'], 'pool_norm': out['pool_norm'], 'pool_w': out['pool_w'], 'pool_scale': out['pool_scale'], 'kv_in_norm': out['kv_in_norm'], 'w_kv_a': out['w_kv_a'], 'kv_latent_norm': out['kv_latent_norm'], 'w_kv_b': out['w_kv_b'], 'attn_norm': out['attn_norm'], 'w_q_a': out['w_q_a'], 'q_latent_norm': out['q_latent_norm'], 'w_q_b': out['w_q_b'], 'w_o': out['w_o'], 'ffn_norm': out['ffn_norm'], 'w_gate': out['w_gate'], 'w_up': out['w_up'], 'w_down': out['w_down'], 'final_norm': out['final_norm'], 'loss_target': out['loss_target'], 'm_pool_norm': out['m_pool_norm'], 'm_pool_w': out['m_pool_w'], 'm_pool_scale': out['m_pool_scale'], 'm_kv_in_norm': out['m_kv_in_norm'], 'm_w_kv_a': out['m_w_kv_a'], 'm_kv_latent_norm': out['m_kv_latent_norm'], 'm_w_kv_b': out['m_w_kv_b'], 'm_attn_norm': out['m_attn_norm'], 'm_w_q_a': out['m_w_q_a'], 'm_q_latent_norm': out['m_q_latent_norm'], 'm_w_q_b': out['m_w_q_b'], 'm_w_o': out['m_w_o'], 'm_ffn_norm': out['m_ffn_norm'], 'm_w_gate': out['m_w_gate'], 'm_w_up': out['m_w_up'], 'm_w_down': out['m_w_down'], 'm_final_norm': out['m_final_norm'], 'v_pool_norm': out['v_pool_norm'], 'v_pool_w': out['v_pool_w'], 'v_pool_scale': out['v_pool_scale'], 'v_kv_in_norm': out['v_kv_in_norm'], 'v_w_kv_a': out['v_w_kv_a'], 'v_kv_latent_norm': out['v_kv_latent_norm'], 'v_w_kv_b': out['v_w_kv_b'], 'v_attn_norm': out['v_attn_norm'], 'v_w_q_a': out['v_w_q_a'], 'v_q_latent_norm': out['v_q_latent_norm'], 'v_w_q_b': out['v_w_q_b'], 'v_w_o': out['v_w_o'], 'v_ffn_norm': out['v_ffn_norm'], 'v_w_gate': out['v_w_gate'], 'v_w_up': out['v_w_up'], 'v_w_down': out['v_w_down'], 'v_final_norm': out['v_final_norm']}


def _loss(weights, diff, rest, loss_target):
    with _jax.named_scope("forward"):
        args = {**rest, TWIN_DIFF_INPUT: diff, **{k: w.astype(_WEIGHT_DTYPES[k]) for k, w in weights.items()}}
        y = _forward(args)
    with _jax.named_scope("loss_head"):
        err = _jnp.square(y.astype(_jnp.float32) - loss_target)
        return 0.5 * _jnp.sum(_jnp.mean(err, axis=-1)) if err.ndim else 0.5 * err


def _adamw(w, g, m, v):
    m = ADAM_B1 * m + (1.0 - ADAM_B1) * g
    v = ADAM_B2 * v + (1.0 - ADAM_B2) * _jnp.square(g)
    m_hat = m / (1.0 - ADAM_B1 ** ADAM_STEP)
    v_hat = v / (1.0 - ADAM_B2 ** ADAM_STEP)
    delta = -ADAM_LR * (m_hat / (_jnp.sqrt(v_hat) + ADAM_EPS) + ADAM_WD * w)
    return delta, m, v


def reference(x, positions, pool_norm, pool_w, pool_scale, kv_in_norm, w_kv_a, kv_latent_norm, w_kv_b, attn_norm, w_q_a, q_latent_norm, w_q_b, w_o, ffn_norm, w_gate, w_up, w_down, final_norm, loss_target, m_pool_norm, m_pool_w, m_pool_scale, m_kv_in_norm, m_w_kv_a, m_kv_latent_norm, m_w_kv_b, m_attn_norm, m_w_q_a, m_q_latent_norm, m_w_q_b, m_w_o, m_ffn_norm, m_w_gate, m_w_up, m_w_down, m_final_norm, v_pool_norm, v_pool_w, v_pool_scale, v_kv_in_norm, v_w_kv_a, v_kv_latent_norm, v_w_kv_b, v_attn_norm, v_w_q_a, v_q_latent_norm, v_w_q_b, v_w_o, v_ffn_norm, v_w_gate, v_w_up, v_w_down, v_final_norm):
    given = dict(x=x, positions=positions, pool_norm=pool_norm, pool_w=pool_w, pool_scale=pool_scale, kv_in_norm=kv_in_norm, w_kv_a=w_kv_a, kv_latent_norm=kv_latent_norm, w_kv_b=w_kv_b, attn_norm=attn_norm, w_q_a=w_q_a, q_latent_norm=q_latent_norm, w_q_b=w_q_b, w_o=w_o, ffn_norm=ffn_norm, w_gate=w_gate, w_up=w_up, w_down=w_down, final_norm=final_norm, loss_target=loss_target, m_pool_norm=m_pool_norm, m_pool_w=m_pool_w, m_pool_scale=m_pool_scale, m_kv_in_norm=m_kv_in_norm, m_w_kv_a=m_w_kv_a, m_kv_latent_norm=m_kv_latent_norm, m_w_kv_b=m_w_kv_b, m_attn_norm=m_attn_norm, m_w_q_a=m_w_q_a, m_q_latent_norm=m_q_latent_norm, m_w_q_b=m_w_q_b, m_w_o=m_w_o, m_ffn_norm=m_ffn_norm, m_w_gate=m_w_gate, m_w_up=m_w_up, m_w_down=m_w_down, m_final_norm=m_final_norm, v_pool_norm=v_pool_norm, v_pool_w=v_pool_w, v_pool_scale=v_pool_scale, v_kv_in_norm=v_kv_in_norm, v_w_kv_a=v_w_kv_a, v_kv_latent_norm=v_kv_latent_norm, v_w_kv_b=v_w_kv_b, v_attn_norm=v_attn_norm, v_w_q_a=v_w_q_a, v_q_latent_norm=v_q_latent_norm, v_w_q_b=v_w_q_b, v_w_o=v_w_o, v_ffn_norm=v_ffn_norm, v_w_gate=v_w_gate, v_w_up=v_w_up, v_w_down=v_w_down, v_final_norm=v_final_norm)
    weights = {n: given[n] for n in TWIN_WEIGHTS}
    shared = {n: given[n] for n in SHARED_INPUTS}
    per_example = {n: given[n] for n in ['x', 'positions']}
    grad_fn = _jax.value_and_grad(_loss, argnums=(0, 1))

    def one_microbatch(ex, loss_target):
        ex = dict(ex)
        diff = ex.pop(TWIN_DIFF_INPUT)
        return grad_fn(weights, diff, {**shared, **ex}, loss_target)

    if N_MICROBATCH == 1:
        loss, (grad_w, grad_x) = one_microbatch(per_example, given["loss_target"])
    else:
        def body(carry, xs):
            loss_sum, grad_sum = carry
            l_k, (gw_k, gx_k) = one_microbatch(xs[0], xs[1])
            with _jax.named_scope("update"):
                return (loss_sum + l_k, _jax.tree.map(_jnp.add, grad_sum, gw_k)), gx_k

        init = (_jnp.zeros((), _jnp.float32), _jax.tree.map(_jnp.zeros_like, weights))
        (loss, grad_w), grad_x = _jax.lax.scan(body, init, (per_example, given["loss_target"]))
    with _jax.named_scope("update"):
        delta_w, new_m, new_v = {}, {}, {}
        for n in TWIN_WEIGHTS:
            delta_w[n], new_m[n], new_v[n] = _adamw(weights[n], grad_w[n], given["m_" + n], given["v_" + n])
    return (loss, grad_x, *[grad_w[n] for n in TWIN_WEIGHTS], *[delta_w[n] for n in TWIN_WEIGHTS],
            *[new_m[n] for n in TWIN_WEIGHTS], *[new_v[n] for n in TWIN_WEIGHTS])
```

```python
import functools
import math

import jax
import jax.numpy as jnp
from jax import lax
from jax.experimental import pallas as pl
from jax.experimental.pallas import tpu as pltpu

BF = jnp.bfloat16
F32 = jnp.float32
MESH = pl.DeviceIdType.MESH

N_HEADS = 16
NOPE = 128
ROPE = 64
VDIM = 128
HEAD_PAD = 256
ROPE_BASE = 10000.0
ATTN_SCALE = 1.0 / math.sqrt(NOPE + ROPE)
POOL_WINDOWS = (2, 4, 8, 16)
HALO = 16
NORM_EPS = 1e-6
ADAM_LR, ADAM_B1, ADAM_B2, ADAM_EPS, ADAM_WD, ADAM_STEP = 0.001, 0.9, 0.999, 1e-08, 0.01, 10
NEG = -1e30

V7X_VMEM_BYTES = 64 * 1024 * 1024
VMEM_CEILING = V7X_VMEM_BYTES - 8 * 1024 * 1024

NN = (((1,), (0,)), ((), ()))
NT = (((1,), (1,)), ((), ()))
TN = (((0,), (0,)), ((), ()))


def _cparams(sem, vmem_mb):
    return pltpu.CompilerParams(dimension_semantics=sem,
                                vmem_limit_bytes=min(vmem_mb * 1024 * 1024, VMEM_CEILING))


def _tile(n, pref):
    if n <= pref:
        return n
    t = (pref // 128) * 128
    while t > 128 and n % t:
        t -= 128
    assert n % t == 0, (n, pref)
    return t


def _sds(shape, dtype):
    return jax.ShapeDtypeStruct(shape, dtype)


def _matmul(name, pairs, pair_specs, dims, grid, nk, extra, extra_specs, out_shapes, out_specs,
            acc_shape, epilogue, vmem_mb):
    n_p, n_e, n_o = len(pairs) // 2, len(extra), len(out_shapes)

    def body(*refs):
        ab = refs[:2 * n_p]
        ex = refs[2 * n_p:2 * n_p + n_e]
        outs = refs[2 * n_p + n_e:2 * n_p + n_e + n_o]

        def partial_sum():
            tot = None
            for p in range(n_p):
                a = ab[2 * p][...]
                b = ab[2 * p + 1][...]
                if a.ndim > 2:
                    a = a.reshape(-1, a.shape[-1])
                if b.ndim > 2:
                    b = b.reshape(-1, b.shape[-1])
                d = lax.dot_general(a.astype(BF), b.astype(BF), dims, preferred_element_type=F32)
                tot = d if tot is None else tot + d
            return tot

        if nk == 1:
            epilogue(partial_sum(), ex, outs)
        else:
            acc = refs[-1]
            kk = pl.program_id(2)

            @pl.when(kk == 0)
            def _():
                acc[...] = partial_sum()

            @pl.when(kk > 0)
            def _():
                acc[...] += partial_sum()

            @pl.when(kk == nk - 1)
            def _():
                epilogue(acc[...], ex, outs)

    scratch = [] if nk == 1 else [pltpu.VMEM(acc_shape, F32)]
    res = pl.pallas_call(
        body, name=name, grid=grid,
        in_specs=list(pair_specs) + list(extra_specs),
        out_specs=list(out_specs), out_shape=list(out_shapes),
        scratch_shapes=scratch,
        compiler_params=_cparams(("arbitrary",) * 3, vmem_mb),
    )(*pairs, *extra)
    return res


def _epi_store(acc, ex, outs):
    outs[0][...] = acc.reshape(outs[0].shape).astype(outs[0].dtype)


def _epi_residual(acc, ex, outs):
    outs[0][...] = (acc + ex[0][...]).astype(outs[0].dtype)


def _swap_halves(x):
    lane = lax.broadcasted_iota(jnp.int32, x.shape, 1)
    return jnp.where((lane % 64) < 32, pltpu.roll(x, 96, 1), pltpu.roll(x, 32, 1))


def _mm_plain(name, a, b, out_dtype, res=None, tm=512, tn=1024):
    M, K = a.shape
    N = b.shape[1]
    tm, tn = _tile(M, tm), _tile(N, tn)
    extra, extra_specs, epi = [], [], _epi_store
    if res is not None:
        extra, extra_specs, epi = [res], [pl.BlockSpec((tm, tn), lambda i, j, k: (i, j))], _epi_residual
    return _matmul(
        name, [a, b],
        [pl.BlockSpec((tm, K), lambda i, j, k: (i, 0)), pl.BlockSpec((K, tn), lambda i, j, k: (0, j))],
        NN, (M // tm, N // tn, 1), 1, extra, extra_specs,
        [_sds((M, N), out_dtype)], [pl.BlockSpec((tm, tn), lambda i, j, k: (i, j))],
        None, epi, 40)[0]


def _mm_cols(name, a, b3, out_dtype, epilogue=_epi_store, extra=(), extra_specs=(), tm=512):
    M, K = a.shape
    G, _, n = b3.shape
    tm = _tile(M, tm)
    return _matmul(
        name, [a, b3],
        [pl.BlockSpec((tm, K), lambda i, j, k: (i, 0)), pl.BlockSpec((None, K, n), lambda i, j, k: (j, 0, 0))],
        NN, (M // tm, G, 1), 1, list(extra), list(extra_specs),
        [_sds((M, G * n), out_dtype)], [pl.BlockSpec((tm, n), lambda i, j, k: (i, j))],
        None, epilogue, 40)[0]


def _mm_nt_plain(name, a, b, out_dtype, tm=512, tn=1024):
    M, K = a.shape
    N = b.shape[0]
    tm, tn = _tile(M, tm), _tile(N, tn)
    return _matmul(
        name, [a, b],
        [pl.BlockSpec((tm, K), lambda i, j, k: (i, 0)), pl.BlockSpec((tn, K), lambda i, j, k: (j, 0))],
        NT, (M // tm, N // tn, 1), 1, [], [],
        [_sds((M, N), out_dtype)], [pl.BlockSpec((tm, tn), lambda i, j, k: (i, j))],
        None, _epi_store, 44)[0]


def _mm_nt_cols(name, a, b3, out_dtype, tm=512):
    M = a.shape[0]
    G, K, n = b3.shape
    tm = _tile(M, tm)
    return _matmul(
        name, [a, b3],
        [pl.BlockSpec((tm, n), lambda i, j, k: (i, k)), pl.BlockSpec((None, K, n), lambda i, j, k: (k, 0, 0))],
        NT, (M // tm, 1, G), G, [], [],
        [_sds((M, K), out_dtype)], [pl.BlockSpec((tm, K), lambda i, j, k: (i, 0))],
        (tm, K), _epi_store, 40)[0]


def _mm_tn_plain(name, a, b, out_dtype, tt=512, tn=1024):
    T, K = a.shape
    N = b.shape[1]
    tt, tn = _tile(T, tt), _tile(N, tn)
    return _matmul(
        name, [a, b],
        [pl.BlockSpec((tt, K), lambda i, j, k: (k, 0)), pl.BlockSpec((tt, tn), lambda i, j, k: (k, j))],
        TN, (1, N // tn, T // tt), T // tt, [], [],
        [_sds((K, N), out_dtype)], [pl.BlockSpec((K, tn), lambda i, j, k: (0, j))],
        (K, tn), _epi_store, 48)[0]


def _mm_tn_cols(name, a, b, G, out_dtype, tt=512):
    T, K = a.shape
    n = b.shape[1] // G
    tt = _tile(T, tt)
    return _matmul(
        name, [a, b],
        [pl.BlockSpec((tt, K), lambda i, j, k: (k, 0)), pl.BlockSpec((tt, n), lambda i, j, k: (k, j))],
        TN, (1, G, T // tt), T // tt, [], [],
        [_sds((G, K, n), out_dtype)], [pl.BlockSpec((None, K, n), lambda i, j, k: (j, 0, 0))],
        (K, n), _epi_store, 48)[0]


def _norm_fwd(name, x, gains, tb=512):
    T, D = x.shape
    G = gains.shape[0]
    tb = _tile(T, tb)

    def body(x_ref, g_ref, *outs):
        xv = x_ref[...]
        xh = xv * lax.rsqrt(jnp.mean(xv * xv, axis=-1, keepdims=True) + NORM_EPS)
        for g in range(G):
            outs[g][...] = (xh * g_ref[g:g + 1, :]).astype(BF)

    return pl.pallas_call(
        body, name=name, grid=(T // tb,),
        in_specs=[pl.BlockSpec((tb, D), lambda i: (i, 0)), pl.BlockSpec((G, D), lambda i: (0, 0))],
        out_specs=[pl.BlockSpec((tb, D), lambda i: (i, 0))] * G,
        out_shape=[_sds((T, D), BF)] * G,
        compiler_params=_cparams(("arbitrary",), 40),
    )(x, gains)


def _norm_bwd(name, x, gains, dhs, dres, tb=256):
    T, D = x.shape
    G = gains.shape[0]
    tb = _tile(T, tb)
    has_res = dres is not None

    def body(*refs):
        x_ref, g_ref = refs[0], refs[1]
        dh_refs = refs[2:2 + G]
        res_ref = refs[2 + G] if has_res else None
        dx_ref, dg_ref = refs[-2], refs[-1]
        i = pl.program_id(0)
        xv = x_ref[...]
        r = lax.rsqrt(jnp.mean(xv * xv, axis=-1, keepdims=True) + NORM_EPS)
        xh = xv * r
        dx = res_ref[...] if has_res else jnp.zeros_like(xv)
        rows = []
        for g in range(G):
            dh = dh_refs[g][...].astype(F32)
            dy = dh * g_ref[g:g + 1, :]
            dx = dx + r * (dy - xh * jnp.mean(dy * xh, axis=-1, keepdims=True))
            rows.append(jnp.sum(dh * xh, axis=0, keepdims=True))
        dx_ref[...] = dx

        @pl.when(i == 0)
        def _():
            for g in range(G):
                dg_ref[g:g + 1, :] = rows[g]

        @pl.when(i > 0)
        def _():
            for g in range(G):
                dg_ref[g:g + 1, :] += rows[g]

    blk = pl.BlockSpec((tb, D), lambda i: (i, 0))
    ins = [x, gains] + list(dhs) + ([dres] if has_res else [])
    in_specs = [blk, pl.BlockSpec((G, D), lambda i: (0, 0))] + [blk] * (G + (1 if has_res else 0))
    return pl.pallas_call(
        body, name=name, grid=(T // tb,), in_specs=in_specs,
        out_specs=[blk, pl.BlockSpec((G, D), lambda i: (0, 0))],
        out_shape=[_sds((T, D), F32), _sds((G, D), F32)],
        compiler_params=_cparams(("arbitrary",), 48),
    )(*ins)


def _pool_fwd(x, gain, tb=256):
    T, D = x.shape
    tb = _tile(T, tb)
    pg = D // len(POOL_WINDOWS)
    per = tb // HALO

    def body(x_ref, xp_ref, g_ref, diff_ref):
        i = pl.program_id(0)
        xx = jnp.concatenate([xp_ref[...], x_ref[...]], axis=0)
        h = xx * lax.rsqrt(jnp.mean(xx * xx, axis=-1, keepdims=True) + NORM_EPS) * g_ref[...]
        row = lax.broadcasted_iota(jnp.int32, (HALO + tb, 1), 0)
        h = jnp.where((row >= HALO) | (i > 0), h, 0.0)
        t = i * tb + row[HALO:] - HALO
        for g, w in enumerate(POOL_WINDOWS):
            hg = h[:, g * pg:(g + 1) * pg]
            s, k = hg, 1
            while k < w:
                s = s + pltpu.roll(s, k, 0)
                k *= 2
            cnt = jnp.minimum(t + 1, w).astype(F32)
            diff_ref[:, g * pg:(g + 1) * pg] = (s[HALO:] / cnt - hg[HALO:]).astype(BF)

    return pl.pallas_call(
        body, name="pool_fwd", grid=(T // tb,),
        in_specs=[pl.BlockSpec((tb, D), lambda i: (i, 0)),
                  pl.BlockSpec((HALO, D), lambda i: (jnp.maximum(i * per - 1, 0), 0)),
                  pl.BlockSpec((1, D), lambda i: (0, 0))],
        out_specs=pl.BlockSpec((tb, D), lambda i: (i, 0)),
        out_shape=_sds((T, D), BF),
        compiler_params=_cparams(("arbitrary",), 40),
    )(x, x, gain)


def _pool_bwd(x, gain, ddiff, dres, tb=256):
    T, D = x.shape
    tb = _tile(T, tb)
    pg = D // len(POOL_WINDOWS)
    per = tb // HALO
    nblk = T // HALO

    def body(x_ref, g_ref, dd_ref, ddn_ref, res_ref, dx_ref, dg_ref):
        i = pl.program_id(0)
        last = i == T // tb - 1
        dd = jnp.concatenate([dd_ref[...], ddn_ref[...]], axis=0)
        row = lax.broadcasted_iota(jnp.int32, (tb + HALO, 1), 0)
        dd = jnp.where((row < tb) | jnp.logical_not(last), dd, 0.0)
        t = i * tb + row
        parts = []
        for g, w in enumerate(POOL_WINDOWS):
            dg_ = dd[:, g * pg:(g + 1) * pg]
            e = dg_ / jnp.minimum(t + 1, w).astype(F32)
            s, k = e, 1
            while k < w:
                s = s + pltpu.roll(s, tb + HALO - k, 0)
                k *= 2
            parts.append(s[:tb] - dg_[:tb])
        dh = jnp.concatenate(parts, axis=1)
        xv = x_ref[...]
        r = lax.rsqrt(jnp.mean(xv * xv, axis=-1, keepdims=True) + NORM_EPS)
        xh = xv * r
        dy = dh * g_ref[...]
        dx_ref[...] = res_ref[...] + r * (dy - xh * jnp.mean(dy * xh, axis=-1, keepdims=True))
        part = jnp.sum(dh * xh, axis=0, keepdims=True)

        @pl.when(i == 0)
        def _():
            dg_ref[...] = part

        @pl.when(i > 0)
        def _():
            dg_ref[...] += part

    blk = pl.BlockSpec((tb, D), lambda i: (i, 0))
    return pl.pallas_call(
        body, name="pool_bwd", grid=(T // tb,),
        in_specs=[blk, pl.BlockSpec((1, D), lambda i: (0, 0)), blk,
                  pl.BlockSpec((HALO, D), lambda i: (jnp.minimum((i + 1) * per, nblk - 1), 0)), blk],
        out_specs=[blk, pl.BlockSpec((1, D), lambda i: (0, 0))],
        out_shape=[_sds((T, D), F32), _sds((1, D), F32)],
        compiler_params=_cparams(("arbitrary",), 48),
    )(x, gain, ddiff, ddiff, dres)


def _pool_w_spec(pg):
    return pl.BlockSpec((4, None, None, pg // 4, pg), lambda i, j, k: (0, j // 2, j % 2, 0, 0))


def _pool_mix(diff, pw, scale, x, tm=512):
    T, D = x.shape
    pg = D // 4
    tm = _tile(T, tm)

    def epi(acc, ex, outs):
        outs[0][...] = acc
        outs[1][...] = ex[1][...] + acc * ex[0][...]

    blk = pl.BlockSpec((tm, pg), lambda i, j, k: (i, j))
    return _matmul(
        "pool_mix", [diff, pw], [blk, _pool_w_spec(pg)], NN, (T // tm, 4, 1), 1,
        [scale, x], [pl.BlockSpec((1, pg), lambda i, j, k: (0, j)), blk],
        [_sds((T, D), F32), _sds((T, D), F32)], [blk, blk], None, epi, 32)


def _pool_dpre(dx, pre, scale, tb=512):
    T, D = dx.shape
    tb = _tile(T, tb)

    def body(dx_ref, pre_ref, s_ref, dpre_ref, ds_ref):
        i = pl.program_id(0)
        d = dx_ref[...]
        dpre_ref[...] = (d * s_ref[...]).astype(BF)
        part = jnp.sum(d * pre_ref[...], axis=0, keepdims=True)

        @pl.when(i == 0)
        def _():
            ds_ref[...] = part

        @pl.when(i > 0)
        def _():
            ds_ref[...] += part

    blk = pl.BlockSpec((tb, D), lambda i: (i, 0))
    vec = pl.BlockSpec((1, D), lambda i: (0, 0))
    return pl.pallas_call(
        body, name="pool_dpre", grid=(T // tb,), in_specs=[blk, blk, vec], out_specs=[blk, vec],
        out_shape=[_sds((T, D), BF), _sds((1, D), F32)],
        compiler_params=_cparams(("arbitrary",), 40),
    )(dx, pre, scale)


def _pool_ddiff(dpre, pw, tm=512):
    T, D = dpre.shape
    pg = D // 4
    tm = _tile(T, tm)
    blk = pl.BlockSpec((tm, pg), lambda i, j, k: (i, j))
    return _matmul("pool_ddiff", [dpre, pw], [blk, _pool_w_spec(pg)], NT, (T // tm, 4, 1), 1, [], [],
                   [_sds((T, D), F32)], [blk], None, _epi_store, 32)[0]


def _pool_dw(diff, dpre, tt=512):
    T, D = diff.shape
    pg = D // 4
    tt = _tile(T, tt)
    blk = pl.BlockSpec((tt, pg), lambda i, j, k: (k, j))
    return _matmul("pool_dw", [diff, dpre], [blk, blk], TN, (1, 4, T // tt), T // tt, [], [],
                   [_sds((4, 2, 2, pg // 4, pg), BF)], [_pool_w_spec(pg)], (pg, pg), _epi_store, 32)[0]


def _w_cols_spec(l, K, n):
    return pl.BlockSpec((None, None, K, n), lambda i, j, k: (j, l, 0, 0))


def _ffn_up(name, h, wg, wu, l, tm=512):
    T, D = h.shape
    n = wg.shape[-1]
    F = 4 * n
    tm = _tile(T, tm)

    def body(h_ref, wg_ref, wu_ref, g_ref, u_ref, a_ref):
        hv = h_ref[...]
        g = jnp.dot(hv, wg_ref[...], preferred_element_type=F32)
        u = jnp.dot(hv, wu_ref[...], preferred_element_type=F32)
        g_ref[...] = g.astype(BF)
        u_ref[...] = u.astype(BF)
        a_ref[...] = (g * jax.nn.sigmoid(g) * u).astype(BF)

    w_spec = pl.BlockSpec((None, None, D, n), lambda j, i: (j, l, 0, 0))
    o_spec = pl.BlockSpec((tm, n), lambda j, i: (i, j))
    return pl.pallas_call(
        body, name=name, grid=(4, T // tm),
        in_specs=[pl.BlockSpec((tm, D), lambda j, i: (i, 0)), w_spec, w_spec],
        out_specs=[o_spec] * 3, out_shape=[_sds((T, F), BF)] * 3,
        compiler_params=_cparams(("arbitrary", "arbitrary"), 56),
    )(h, wg, wu)


def _ffn_down(name, a, wd, l, x, tm=512, tn=1024):
    T, D = x.shape
    n = wd.shape[2]
    tm, tn = _tile(T, tm), _tile(D, tn)
    return _matmul(
        name, [a, wd],
        [pl.BlockSpec((tm, n), lambda i, j, k: (i, k)),
         pl.BlockSpec((None, None, n, tn), lambda i, j, k: (k, l, 0, j))],
        NN, (T // tm, D // tn, 4), 4, [x], [pl.BlockSpec((tm, tn), lambda i, j, k: (i, j))],
        [_sds((T, D), F32)], [pl.BlockSpec((tm, tn), lambda i, j, k: (i, j))],
        (tm, tn), _epi_residual, 40)[0]


def _ffn_bwd_act(name, dx, wd, l, g, u, tm=512):
    T, D = dx.shape
    n = wd.shape[2]
    tm = _tile(T, tm)

    def epi(acc, ex, outs):
        gv = ex[0][...].astype(F32)
        uv = ex[1][...].astype(F32)
        sig = jax.nn.sigmoid(gv)
        outs[0][...] = (acc * uv * (sig * (1.0 + gv * (1.0 - sig)))).astype(BF)
        outs[1][...] = (acc * (gv * sig)).astype(BF)

    blk = pl.BlockSpec((tm, n), lambda i, j, k: (i, j))
    return _matmul(
        name, [dx, wd],
        [pl.BlockSpec((tm, D), lambda i, j, k: (i, 0)), _w_cols_spec(l, n, D)],
        NT, (T // tm, 4, 1), 1, [g, u], [blk, blk],
        [_sds((T, 4 * n), BF)] * 2, [blk, blk], None, epi, 56)


def _ffn_dh(name, dg, wg, du, wu, l, tm=512):
    T = dg.shape[0]
    D, n = wg.shape[2], wg.shape[3]
    tm = _tile(T, tm)
    a_spec = pl.BlockSpec((tm, n), lambda i, j, k: (i, k))
    w_spec = pl.BlockSpec((None, None, D, n), lambda i, j, k: (k, l, 0, 0))
    return _matmul(
        name, [dg, wg, du, wu], [a_spec, w_spec, a_spec, w_spec], NT, (T // tm, 1, 4), 4, [], [],
        [_sds((T, D), F32)], [pl.BlockSpec((tm, D), lambda i, j, k: (i, 0))], (tm, D), _epi_store, 56)[0]


def _ffn_dw_cols(name, h, d, l, prev, tt=512):
    T, D = h.shape
    n = d.shape[1] // 4
    tt = _tile(T, tt)
    return _dw_into(name, h, d, prev, (4, 2, D, n),
                    pl.BlockSpec((tt, D), lambda i, j, k: (k, 0)),
                    pl.BlockSpec((tt, n), lambda i, j, k: (k, j)),
                    pl.BlockSpec((None, None, D, n), lambda i, j, k: (j, l, 0, 0)),
                    (1, 4, T // tt), (D, n))


def _ffn_dw_rows(name, a, dx, l, prev, tt=512, tn=1024):
    T, D = dx.shape
    n = a.shape[1] // 4
    tt, tn = _tile(T, tt), _tile(D, tn)
    nj = D // tn
    return _dw_into(name, a, dx, prev, (4, 2, n, D),
                    pl.BlockSpec((tt, n), lambda i, j, k: (k, j // nj)),
                    pl.BlockSpec((tt, tn), lambda i, j, k: (k, j % nj)),
                    pl.BlockSpec((None, None, n, tn), lambda i, j, k: (j // nj, l, 0, j % nj)),
                    (1, 4 * nj, T // tt), (n, tn))


def _dw_into(name, a, b, prev, shape, a_spec, b_spec, o_spec, grid, acc_shape):
    nk = grid[2]

    def body(*refs):
        a_ref, b_ref = refs[0], refs[1]
        o_ref, acc = refs[-2], refs[-1]
        kk = pl.program_id(2)
        d = lax.dot_general(a_ref[...].astype(BF), b_ref[...].astype(BF), TN, preferred_element_type=F32)

        @pl.when(kk == 0)
        def _():
            acc[...] = d

        @pl.when(kk > 0)
        def _():
            acc[...] += d

        @pl.when(kk == nk - 1)
        def _():
            o_ref[...] = acc[...].astype(o_ref.dtype)

    ins, in_specs, alias = [a, b], [a_spec, b_spec], {}
    if prev is not None:
        ins.append(prev)
        in_specs.append(pl.BlockSpec(memory_space=pl.ANY))
        alias = {2: 0}
    return pl.pallas_call(
        body, name=name, grid=grid, in_specs=in_specs, out_specs=o_spec, out_shape=_sds(shape, BF),
        scratch_shapes=[pltpu.VMEM(acc_shape, F32)], input_output_aliases=alias,
        compiler_params=_cparams(("arbitrary",) * 3, 56),
    )(*ins)


def _kv_post(kvp, gain, cos, sin, tb=512):
    T, W = kvp.shape
    KL = W - 128
    tb = _tile(T, tb)

    def body(kv_ref, g_ref, c_ref, s_ref, ckv_ref, kpe_ref):
        lat = kv_ref[:, :KL]
        ckv_ref[...] = (lat * lax.rsqrt(jnp.mean(lat * lat, axis=-1, keepdims=True) + NORM_EPS)
                        * g_ref[...]).astype(BF)
        pe = kv_ref[:, KL:]
        kpe_ref[...] = (pe * c_ref[...] + _swap_halves(pe) * s_ref[...]).astype(BF)

    tab = pl.BlockSpec((tb, 128), lambda i: (i, 0))
    return pl.pallas_call(
        body, name="kv_post", grid=(T // tb,),
        in_specs=[pl.BlockSpec((tb, W), lambda i: (i, 0)), pl.BlockSpec((1, KL), lambda i: (0, 0)), tab, tab],
        out_specs=[pl.BlockSpec((tb, KL), lambda i: (i, 0)), tab],
        out_shape=[_sds((T, KL), BF), _sds((T, 128), BF)],
        compiler_params=_cparams(("arbitrary",), 32),
    )(kvp, gain, cos, sin)


def _kv_post_bwd(kvp, gain, dckv, dkpe_heads, cos, sin, tb=256):
    T, W = kvp.shape
    KL = W - 128
    H = dkpe_heads.shape[0]
    tb = _tile(T, tb)

    def body(kv_ref, g_ref, dc_ref, dk_ref, c_ref, s_ref, out_ref, dg_ref):
        i = pl.program_id(0)
        lat = kv_ref[:, :KL]
        r = lax.rsqrt(jnp.mean(lat * lat, axis=-1, keepdims=True) + NORM_EPS)
        xh = lat * r
        dh = dc_ref[...]
        dy = dh * g_ref[...]
        out_ref[:, :KL] = (r * (dy - xh * jnp.mean(dy * xh, axis=-1, keepdims=True))).astype(BF)
        d = dk_ref[0]
        for h in range(1, H):
            d = d + dk_ref[h]
        out_ref[:, KL:] = (d * c_ref[...] - _swap_halves(d) * s_ref[...]).astype(BF)
        part = jnp.sum(dh * xh, axis=0, keepdims=True)

        @pl.when(i == 0)
        def _():
            dg_ref[...] = part

        @pl.when(i > 0)
        def _():
            dg_ref[...] += part

    tab = pl.BlockSpec((tb, 128), lambda i: (i, 0))
    vec = pl.BlockSpec((1, KL), lambda i: (0, 0))
    return pl.pallas_call(
        body, name="kv_post_bwd", grid=(T // tb,),
        in_specs=[pl.BlockSpec((tb, W), lambda i: (i, 0)), vec, pl.BlockSpec((tb, KL), lambda i: (i, 0)),
                  pl.BlockSpec((H, tb, 128), lambda i: (0, i, 0)), tab, tab],
        out_specs=[pl.BlockSpec((tb, W), lambda i: (i, 0)), vec],
        out_shape=[_sds((T, W), BF), _sds((1, KL), F32)],
        compiler_params=_cparams(("arbitrary",), 32),
    )(kvp, gain, dckv, dkpe_heads, cos, sin)


def _q_up(ql, wqb, cos, sin, tm=512):
    n = wqb.shape[2]
    tm = _tile(ql.shape[0], tm)

    def epi(acc, ex, outs):
        c, s = ex[0][...], ex[1][...]
        for j in range(n // HEAD_PAD):
            a0 = j * HEAD_PAD
            outs[0][:, a0:a0 + NOPE] = acc[:, a0:a0 + NOPE].astype(BF)
            pe = acc[:, a0 + NOPE:a0 + HEAD_PAD]
            outs[0][:, a0 + NOPE:a0 + HEAD_PAD] = (pe * c + _swap_halves(pe) * s).astype(BF)

    tab = pl.BlockSpec((tm, 128), lambda i, j, k: (i, 0))
    return _mm_cols("q_up", ql, wqb, BF, epilogue=epi, extra=[cos, sin], extra_specs=[tab, tab], tm=tm)


def _causal_mask(tb):
    r = lax.broadcasted_iota(jnp.int32, (tb, tb), 0)
    c = lax.broadcasted_iota(jnp.int32, (tb, tb), 1)
    return r, c


def _flash_fwd(qp, kvup, kpe, tb=512):
    T = qp.shape[0]
    H = qp.shape[1] // HEAD_PAD
    tb = _tile(T, tb)

    def body(q_ref, kv_ref, kpe_ref, o_ref, lse_ref, k_scr):
        iq = pl.program_id(1)

        @pl.when(iq == 0)
        def _():
            k_scr[:, :NOPE] = kv_ref[:, :NOPE]
            k_scr[:, NOPE:] = kpe_ref[...]

        q = q_ref[...]

        def blk(ik, carry, masked):
            m, l, acc = carry
            off = pl.multiple_of(ik * tb, tb)
            k = k_scr[pl.ds(off, tb), :]
            v = kv_ref[pl.ds(off, tb), NOPE:]
            s = lax.dot_general(q, k, NT, preferred_element_type=F32) * ATTN_SCALE
            if masked:
                r, c = _causal_mask(tb)
                s = jnp.where(c <= r, s, NEG)
            m2 = jnp.maximum(m, jnp.max(s, axis=-1, keepdims=True))
            p = jnp.exp(s - m2)
            a = jnp.exp(m - m2)
            l2 = a * l + jnp.sum(p, axis=-1, keepdims=True)
            acc2 = a * acc + lax.dot_general(p.astype(BF), v, NN, preferred_element_type=F32)
            return m2, l2, acc2

        init = (jnp.full((tb, 1), NEG, F32), jnp.zeros((tb, 1), F32), jnp.zeros((tb, VDIM), F32))
        carry = lax.fori_loop(0, iq, lambda ik, cr: blk(ik, cr, False), init)
        m, l, acc = blk(iq, carry, True)
        o_ref[...] = (acc / l).astype(BF)
        lse_ref[...] = m + jnp.log(l)

    return pl.pallas_call(
        body, name="flash_fwd", grid=(H, T // tb),
        in_specs=[pl.BlockSpec((tb, HEAD_PAD), lambda h, i: (i, h)),
                  pl.BlockSpec((T, HEAD_PAD), lambda h, i: (0, h)),
                  pl.BlockSpec((T, 128), lambda h, i: (0, 0))],
        out_specs=[pl.BlockSpec((tb, VDIM), lambda h, i: (i, h)),
                   pl.BlockSpec((None, tb, 1), lambda h, i: (h, i, 0))],
        out_shape=[_sds((T, H * VDIM), BF), _sds((H, T, 1), F32)],
        scratch_shapes=[pltpu.VMEM((T, HEAD_PAD), BF)],
        compiler_params=_cparams(("arbitrary", "arbitrary"), 48),
    )(qp, kvup, kpe)


def _attn_delta(o, do, tb=512):
    T = o.shape[0]
    H = o.shape[1] // VDIM
    tb = _tile(T, tb)

    def body(o_ref, do_ref, d_ref):
        d_ref[...] = jnp.sum(o_ref[...].astype(F32) * do_ref[...].astype(F32), axis=-1, keepdims=True)

    blk = pl.BlockSpec((tb, VDIM), lambda h, i: (i, h))
    return pl.pallas_call(
        body, name="attn_delta", grid=(H, T // tb), in_specs=[blk, blk],
        out_specs=pl.BlockSpec((None, tb, 1), lambda h, i: (h, i, 0)),
        out_shape=_sds((H, T, 1), F32),
        compiler_params=_cparams(("arbitrary", "arbitrary"), 32),
    )(o, do)


def _flash_dq(qp, kvup, kpe, do, lse, delta, cos, sin, tb=512):
    T = qp.shape[0]
    H = qp.shape[1] // HEAD_PAD
    tb = _tile(T, tb)

    def body(q_ref, kv_ref, kpe_ref, do_ref, lse_ref, dl_ref, c_ref, s_ref, dq_ref, k_scr):
        iq = pl.program_id(1)

        @pl.when(iq == 0)
        def _():
            k_scr[:, :NOPE] = kv_ref[:, :NOPE]
            k_scr[:, NOPE:] = kpe_ref[...]

        q = q_ref[...]
        dov = do_ref[...]
        lse_v = lse_ref[...]
        dl_v = dl_ref[...]

        def blk(ik, dq, masked):
            off = pl.multiple_of(ik * tb, tb)
            k = k_scr[pl.ds(off, tb), :]
            v = kv_ref[pl.ds(off, tb), NOPE:]
            s = lax.dot_general(q, k, NT, preferred_element_type=F32) * ATTN_SCALE
            if masked:
                r, c = _causal_mask(tb)
                s = jnp.where(c <= r, s, NEG)
            p = jnp.exp(s - lse_v)
            dp = lax.dot_general(dov, v, NT, preferred_element_type=F32)
            ds = (p * (dp - dl_v) * ATTN_SCALE).astype(BF)
            return dq + lax.dot_general(ds, k, NN, preferred_element_type=F32)

        dq = lax.fori_loop(0, iq, lambda ik, cr: blk(ik, cr, False), jnp.zeros((tb, HEAD_PAD), F32))
        dq = blk(iq, dq, True)
        dq_ref[:, :NOPE] = dq[:, :NOPE].astype(BF)
        dpe = dq[:, NOPE:]
        dq_ref[:, NOPE:] = (dpe * c_ref[...] - _swap_halves(dpe) * s_ref[...]).astype(BF)

    tab = pl.BlockSpec((tb, 128), lambda h, i: (i, 0))
    col = pl.BlockSpec((None, tb, 1), lambda h, i: (h, i, 0))
    return pl.pallas_call(
        body, name="flash_dq", grid=(H, T // tb),
        in_specs=[pl.BlockSpec((tb, HEAD_PAD), lambda h, i: (i, h)),
                  pl.BlockSpec((T, HEAD_PAD), lambda h, i: (0, h)),
                  pl.BlockSpec((T, 128), lambda h, i: (0, 0)),
                  pl.BlockSpec((tb, VDIM), lambda h, i: (i, h)), col, col, tab, tab],
        out_specs=pl.BlockSpec((tb, HEAD_PAD), lambda h, i: (i, h)),
        out_shape=_sds((T, H * HEAD_PAD), BF),
        scratch_shapes=[pltpu.VMEM((T, HEAD_PAD), BF)],
        compiler_params=_cparams(("arbitrary", "arbitrary"), 48),
    )(qp, kvup, kpe, do, lse, delta, cos, sin)


def _flash_dkv(qp, kvup, kpe, do, lse_rows, delta_rows, tb=512):
    T = qp.shape[0]
    H = qp.shape[1] // HEAD_PAD
    tb = _tile(T, tb)
    nq = T // tb

    def body(kv_ref, kpe_ref, q_ref, do_ref, lse_ref, dl_ref, dkv_ref, dkpe_ref):
        ik = pl.program_id(1)
        k = jnp.concatenate([kv_ref[:, :NOPE], kpe_ref[...]], axis=1)
        v = kv_ref[:, NOPE:]

        def blk(iq, carry, masked):
            dk, dv = carry
            off = pl.multiple_of(iq * tb, tb)
            q = q_ref[pl.ds(off, tb), :]
            dov = do_ref[pl.ds(off, tb), :]
            st = lax.dot_general(k, q, NT, preferred_element_type=F32) * ATTN_SCALE
            if masked:
                r, c = _causal_mask(tb)
                st = jnp.where(r <= c, st, NEG)
            pt = jnp.exp(st - lse_ref[iq])
            dv = dv + lax.dot_general(pt.astype(BF), dov, NN, preferred_element_type=F32)
            dpt = lax.dot_general(v, dov, NT, preferred_element_type=F32)
            dst = (pt * (dpt - dl_ref[iq]) * ATTN_SCALE).astype(BF)
            dk = dk + lax.dot_general(dst, q, NN, preferred_element_type=F32)
            return dk, dv

        carry = blk(ik, (jnp.zeros((tb, HEAD_PAD), F32), jnp.zeros((tb, VDIM), F32)), True)
        dk, dv = lax.fori_loop(ik + 1, nq, lambda iq, cr: blk(iq, cr, False), carry)
        dkv_ref[:, :NOPE] = dk[:, :NOPE].astype(BF)
        dkv_ref[:, NOPE:] = dv.astype(BF)
        dkpe_ref[...] = dk[:, NOPE:]

    rows = pl.BlockSpec((None, nq, 1, tb), lambda h, i: (h, 0, 0, 0))
    return pl.pallas_call(
        body, name="flash_dkv", grid=(H, nq),
        in_specs=[pl.BlockSpec((tb, HEAD_PAD), lambda h, i: (i, h)),
                  pl.BlockSpec((tb, 128), lambda h, i: (i, 0)),
                  pl.BlockSpec((T, HEAD_PAD), lambda h, i: (0, h)),
                  pl.BlockSpec((T, VDIM), lambda h, i: (0, h)), rows, rows],
        out_specs=[pl.BlockSpec((tb, HEAD_PAD), lambda h, i: (i, h)),
                   pl.BlockSpec((None, tb, 128), lambda h, i: (h, i, 0))],
        out_shape=[_sds((T, H * HEAD_PAD), BF), _sds((H, T, 128), F32)],
        compiler_params=_cparams(("arbitrary", "arbitrary"), 48),
    )(kvup, kpe, qp, do, lse_rows, delta_rows)


def _final_loss(x, gain, target, tb=256):
    T, D = x.shape
    tb = _tile(T, tb)

    def body(x_ref, g_ref, t_ref, loss_ref, dx_ref, dg_ref):
        i = pl.program_id(0)
        xv = x_ref[...]
        gv = g_ref[...]
        r = lax.rsqrt(jnp.mean(xv * xv, axis=-1, keepdims=True) + NORM_EPS)
        xh = xv * r
        e = xh * gv - t_ref[...]
        lpart = 0.5 * jnp.sum(jnp.mean(e * e, axis=-1, keepdims=True), axis=0, keepdims=True)
        dy = e / D
        dyg = dy * gv
        dx_ref[...] = r * (dyg - xh * jnp.mean(dyg * xh, axis=-1, keepdims=True))
        gpart = jnp.sum(dy * xh, axis=0, keepdims=True)

        @pl.when(i == 0)
        def _():
            loss_ref[...] = lpart
            dg_ref[...] = gpart

        @pl.when(i > 0)
        def _():
            loss_ref[...] += lpart
            dg_ref[...] += gpart

    blk = pl.BlockSpec((tb, D), lambda i: (i, 0))
    vec = pl.BlockSpec((1, D), lambda i: (0, 0))
    return pl.pallas_call(
        body, name="final_loss", grid=(T // tb,), in_specs=[blk, vec, blk],
        out_specs=[pl.BlockSpec((1, 1), lambda i: (0, 0)), blk, vec],
        out_shape=[_sds((1, 1), F32), _sds((T, D), F32), _sds((1, D), F32)],
        compiler_params=_cparams(("arbitrary",), 40),
    )(x, gain, target)


def _rows_view(a, lead):
    shape = a.shape
    return a.reshape((int(math.prod(shape[:lead])),) + tuple(shape[lead:]))


def _pair_add(name, a, b, tr=256):
    n, R, C = a.shape
    tr = _tile(R, tr) if R % 8 == 0 else R

    def body(a_ref, b_ref, o_ref):
        o_ref[...] = (a_ref[...].astype(F32) + b_ref[...].astype(F32)).astype(BF)

    blk = pl.BlockSpec((None, tr, C), lambda i, j: (i, j, 0))
    return pl.pallas_call(
        body, name=name, grid=(n, R // tr), in_specs=[blk, blk], out_specs=blk,
        out_shape=_sds((n, R, C), BF), compiler_params=_cparams(("arbitrary", "arbitrary"), 32),
    )(a, b)


def _sum4(name, a, tr=256):
    _, n, R, C = a.shape
    tr = _tile(R, tr) if R % 8 == 0 else R

    def body(a_ref, o_ref):
        s = a_ref[0].astype(F32)
        for k in range(1, 4):
            s = s + a_ref[k].astype(F32)
        o_ref[...] = s

    return pl.pallas_call(
        body, name=name, grid=(n, R // tr),
        in_specs=[pl.BlockSpec((4, None, tr, C), lambda i, j: (0, i, j, 0))],
        out_specs=pl.BlockSpec((None, tr, C), lambda i, j: (i, j, 0)),
        out_shape=_sds((n, R, C), F32), compiler_params=_cparams(("arbitrary", "arbitrary"), 32),
    )(a)


def _adamw(name, w, g, m, v, tr=128):
    R, C = w.shape
    tr = _tile(R, tr) if R % 8 == 0 else R
    c1 = 1.0 - ADAM_B1 ** ADAM_STEP
    c2 = 1.0 - ADAM_B2 ** ADAM_STEP

    def body(w_ref, g_ref, m_ref, v_ref, d_ref, nm_ref, nv_ref):
        gv = g_ref[...]
        nm = ADAM_B1 * m_ref[...] + (1.0 - ADAM_B1) * gv
        nv = ADAM_B2 * v_ref[...] + (1.0 - ADAM_B2) * (gv * gv)
        nm_ref[...] = nm
        nv_ref[...] = nv
        d_ref[...] = -ADAM_LR * ((nm / c1) / (jnp.sqrt(nv / c2) + ADAM_EPS) + ADAM_WD * w_ref[...])

    blk = pl.BlockSpec((tr, C), lambda i: (i, 0))
    return pl.pallas_call(
        body, name=name, grid=(R // tr,), in_specs=[blk] * 4, out_specs=[blk] * 3,
        out_shape=[_sds((R, C), F32)] * 3, compiler_params=_cparams(("arbitrary",), 32),
    )(w, g, m, v)


ANY = pl.BlockSpec(memory_space=pl.ANY)


def _place():
    x, y, c = lax.axis_index("x"), lax.axis_index("y"), lax.axis_index("c")
    chips = [(1 - x, y), (x, 1 - y), (1 - x, 1 - y)]
    return x, y, c, chips


def _gather_weights(shards):
    n = len(shards)

    def body(*refs):
        src, dst = refs[:n], refs[n:2 * n]
        send, recv, local = refs[2 * n:]
        x, y, c, chips = _place()
        me = 2 * x + y
        sibling = (x, y, 1 - c)

        def remote(i, k, s, d, to):
            return pltpu.make_async_remote_copy(src_ref=s, dst_ref=d, send_sem=send.at[6 * i + k],
                                                recv_sem=recv.at[6 * i + k], device_id=to, device_id_type=MESH)

        own = []
        for i in range(n):
            for half in range(2):
                cp = pltpu.make_async_copy(src[i].at[half], dst[i].at[me, half], local.at[2 * i + half])
                cp.start()
                own.append(cp)
        first = []
        for i in range(n):
            for j, chip in enumerate(chips):
                cp = remote(i, j, src[i].at[c], dst[i].at[me, c], (*chip, c))
                cp.start()
                first.append(cp)
        passed = []
        for i in range(n):
            for j, (px, py) in enumerate(chips):
                slot = dst[i].at[2 * px + py, c]
                remote(i, j, slot, slot, (px, py, c)).wait_recv()
                cp = remote(i, 3 + j, slot, slot, sibling)
                cp.start()
                passed.append(cp)
        for i in range(n):
            for j, (px, py) in enumerate(chips):
                slot = dst[i].at[2 * px + py, 1 - c]
                remote(i, 3 + j, slot, slot, sibling).wait_recv()
        for cp in first + passed:
            cp.wait_send()
        for cp in own:
            cp.wait()

    return pl.pallas_call(
        body, name="gather_weights", in_specs=[ANY] * n, out_specs=[ANY] * n,
        out_shape=[_sds((4,) + s.shape, s.dtype) for s in shards],
        scratch_shapes=[pltpu.SemaphoreType.DMA((6 * n,)), pltpu.SemaphoreType.DMA((6 * n,)),
                        pltpu.SemaphoreType.DMA((2 * n,))],
    )(*shards)


def _reduce_pair(grads):
    n = len(grads)

    def body(*refs):
        g, mine, theirs = refs[:n], refs[n:2 * n], refs[2 * n:3 * n]
        send, recv, local = refs[3 * n:]
        x, y, c, _ = _place()
        sibling = (x, y, 1 - c)
        cps, own = [], []
        for i in range(n):
            for k in range(4):
                cp = pltpu.make_async_remote_copy(
                    src_ref=g[i].at[k, 1 - c], dst_ref=theirs[i].at[k], send_sem=send.at[4 * i + k],
                    recv_sem=recv.at[4 * i + k], device_id=sibling, device_id_type=MESH)
                cp.start()
                cps.append(cp)
                lc = pltpu.make_async_copy(g[i].at[k, c], mine[i].at[k], local.at[4 * i + k])
                lc.start()
                own.append(lc)
        for cp in cps:
            cp.wait()
        for lc in own:
            lc.wait()

    outs = [_sds((4,) + a.shape[2:], a.dtype) for a in grads]
    res = pl.pallas_call(
        body, name="reduce_pair", in_specs=[ANY] * n, out_specs=[ANY] * (2 * n), out_shape=outs + outs,
        scratch_shapes=[pltpu.SemaphoreType.DMA((4 * n,)), pltpu.SemaphoreType.DMA((4 * n,)),
                        pltpu.SemaphoreType.DMA((4 * n,))],
    )(*grads)
    return res[:n], res[n:]


def _reduce_chips(sums):
    n = len(sums)

    def body(*refs):
        src, dst = refs[:n], refs[n:2 * n]
        send, recv, local = refs[2 * n:]
        x, y, c, chips = _place()
        me = 2 * x + y
        cps, own = [], []
        for i in range(n):
            lc = pltpu.make_async_copy(src[i].at[me], dst[i].at[me], local.at[i])
            lc.start()
            own.append(lc)
            for j, (px, py) in enumerate(chips):
                cp = pltpu.make_async_remote_copy(
                    src_ref=src[i].at[2 * px + py], dst_ref=dst[i].at[me], send_sem=send.at[3 * i + j],
                    recv_sem=recv.at[3 * i + j], device_id=(px, py, c), device_id_type=MESH)
                cp.start()
                cps.append(cp)
        for i in range(n):
            for j, (px, py) in enumerate(chips):
                slot = dst[i].at[2 * px + py]
                pltpu.make_async_remote_copy(
                    src_ref=slot, dst_ref=slot, send_sem=send.at[3 * i + j], recv_sem=recv.at[3 * i + j],
                    device_id=(px, py, c), device_id_type=MESH).wait_recv()
        for cp in cps:
            cp.wait_send()
        for lc in own:
            lc.wait()

    return pl.pallas_call(
        body, name="reduce_chips", in_specs=[ANY] * n, out_specs=[ANY] * n,
        out_shape=[_sds(a.shape, a.dtype) for a in sums],
        scratch_shapes=[pltpu.SemaphoreType.DMA((3 * n,)), pltpu.SemaphoreType.DMA((3 * n,)),
                        pltpu.SemaphoreType.DMA((n,))],
    )(*sums)


def _share_halves(halves):
    n = len(halves)

    def body(*refs):
        src, dst = refs[:n], refs[n:2 * n]
        send, recv, local = refs[2 * n:]
        x, y, c, _ = _place()
        cps, own = [], []
        for i in range(n):
            cp = pltpu.make_async_remote_copy(
                src_ref=src[i], dst_ref=dst[i].at[c], send_sem=send.at[i], recv_sem=recv.at[i],
                device_id=(x, y, 1 - c), device_id_type=MESH)
            cp.start()
            cps.append(cp)
            lc = pltpu.make_async_copy(src[i], dst[i].at[c], local.at[i])
            lc.start()
            own.append(lc)
        for i in range(n):
            pltpu.make_async_remote_copy(
                src_ref=src[i], dst_ref=dst[i].at[1 - c], send_sem=send.at[i], recv_sem=recv.at[i],
                device_id=(x, y, 1 - c), device_id_type=MESH).wait_recv()
        for cp in cps:
            cp.wait_send()
        for lc in own:
            lc.wait()

    return pl.pallas_call(
        body, name="share_halves", in_specs=[ANY] * n, out_specs=[ANY] * n,
        out_shape=[_sds((2,) + a.shape, a.dtype) for a in halves],
        scratch_shapes=[pltpu.SemaphoreType.DMA((n,)), pltpu.SemaphoreType.DMA((n,)),
                        pltpu.SemaphoreType.DMA((n,))],
    )(*halves)


def _allreduce_small(v):
    R, C = v.shape

    def body(v_ref, out_ref, land, send, recv):
        x, y, c, _ = _place()
        me = 4 * x + 2 * y + c
        cps = []
        for m in range(1, 8):
            fx, fy, fc = (m >> 2) & 1, (m >> 1) & 1, m & 1
            peer = (x ^ fx, y ^ fy, c ^ fc)
            cp = pltpu.make_async_remote_copy(
                src_ref=v_ref, dst_ref=land.at[me], send_sem=send.at[m - 1], recv_sem=recv.at[m - 1],
                device_id=peer, device_id_type=MESH)
            cp.start()
            cps.append(cp)
        land[me] = v_ref[...]
        for cp in cps:
            cp.wait()
        s = land[0]
        for d in range(1, 8):
            s = s + land[d]
        out_ref[...] = s

    return pl.pallas_call(
        body, name="allreduce_small",
        in_specs=[pl.BlockSpec(memory_space=pltpu.VMEM)], out_specs=pl.BlockSpec(memory_space=pltpu.VMEM),
        out_shape=_sds((R, C), F32),
        scratch_shapes=[pltpu.VMEM((8, R, C), F32), pltpu.SemaphoreType.DMA((7,)), pltpu.SemaphoreType.DMA((7,))],
    )(v)


def _halves(a):
    return a.reshape((2, a.shape[0] // 2) + a.shape[1:])


def _rope_tables(pos):
    half = ROPE // 2
    inv_freq = ROPE_BASE ** (-jnp.arange(half, dtype=F32) / half)
    ang = pos.astype(F32)[:, None] * inv_freq
    cos, sin = jnp.cos(ang), jnp.sin(ang)
    return jnp.tile(cos, (1, 4)), jnp.concatenate([-sin, sin, -sin, sin], axis=1)


def kernel(x, positions, pool_norm, pool_w, pool_scale, kv_in_norm, w_kv_a, kv_latent_norm, w_kv_b, attn_norm, w_q_a, q_latent_norm, w_q_b, w_o, ffn_norm, w_gate, w_up, w_down, final_norm, loss_target, m_pool_norm, m_pool_w, m_pool_scale, m_kv_in_norm, m_w_kv_a, m_kv_latent_norm, m_w_kv_b, m_attn_norm, m_w_q_a, m_q_latent_norm, m_w_q_b, m_w_o, m_ffn_norm, m_w_gate, m_w_up, m_w_down, m_final_norm, v_pool_norm, v_pool_w, v_pool_scale, v_kv_in_norm, v_w_kv_a, v_kv_latent_norm, v_w_kv_b, v_attn_norm, v_w_q_a, v_q_latent_norm, v_w_q_b, v_w_o, v_ffn_norm, v_w_gate, v_w_up, v_w_down, v_final_norm):
    T, D = x.shape[1], x.shape[2]
    H = N_HEADS
    KL = kv_latent_norm.shape[0]
    QL = q_latent_norm.shape[1]
    pg = D // 4
    x0, tgt = x[0], loss_target[0]
    cos, sin = _rope_tables(positions[0])
    chip = 2 * lax.axis_index("x") + lax.axis_index("y")

    n_kva = w_kv_a.shape[1]
    wkva_s = jnp.pad(w_kv_a, ((0, 0), (0, KL + 128 - n_kva))).astype(BF)
    hs = w_q_b.shape[2] // (NOPE + ROPE)
    wqb_s = jnp.pad(w_q_b[0].reshape(QL, hs, NOPE + ROPE), ((0, 0), (0, 0), (0, HEAD_PAD - NOPE - ROPE)))
    wqb_s = wqb_s.reshape(QL, hs * HEAD_PAD).astype(BF)
    shards = [
        _halves(pool_w[0].astype(BF)),
        _halves(wkva_s), _halves(w_kv_b.astype(BF)), _halves(w_q_a[0].astype(BF)), _halves(wqb_s),
        _halves(w_o[0].astype(BF)),
        w_gate.astype(BF), w_up.astype(BF), w_down.astype(BF),
        jnp.stack([pool_norm, pool_scale]),
    ]
    PW, WKVA, WKVB, WQA, WQB, WO, WG, WU, WD, PV = _gather_weights(shards)
    WKVA = WKVA.reshape(D, KL + 128)
    WKVB = WKVB.reshape(4, KL, -1)
    WQA = WQA.reshape(D, QL)
    WQB = WQB.reshape(4, QL, -1)
    WO = WO.reshape(H * VDIM, D)
    pv = jnp.transpose(PV.reshape(4, 2, pg), (1, 0, 2)).reshape(2, D)
    pn_full, ps_full = pv[0:1], pv[1:2]

    diff = _pool_fwd(x0, pn_full)
    pre, x1 = _pool_mix(diff, PW, ps_full, x0)
    (h1,) = _norm_fwd("norm_ffn0", x1, ffn_norm[0:1])
    g0, u0, a0 = _ffn_up("ffn_up0", h1, WG, WU, 0)
    x2 = _ffn_down("ffn_down0", a0, WD, 0, x1)
    hk, ha = _norm_fwd("norm_attn", x2, jnp.stack([kv_in_norm, attn_norm[0]]))
    kvp = _mm_plain("kv_a", hk, WKVA, F32)
    ckv, kpe = _kv_post(kvp, kv_latent_norm[None], cos, sin)
    kvup = _mm_cols("kv_b", ckv, WKVB, BF)
    qa = _mm_plain("q_a", ha, WQA, F32)
    (ql,) = _norm_fwd("norm_q", qa, q_latent_norm)
    qp = _q_up(ql, WQB, cos, sin)
    o, lse = _flash_fwd(qp, kvup, kpe)
    x3 = _mm_plain("attn_out", o, WO, F32, res=x2)
    (h3,) = _norm_fwd("norm_ffn1", x3, ffn_norm[1:2])
    g1, u1, a1 = _ffn_up("ffn_up1", h3, WG, WU, 1)
    x4 = _ffn_down("ffn_down1", a1, WD, 1, x3)
    loss_part, dx4, d_final = _final_loss(x4, final_norm[None], tgt)

    dWD = _ffn_dw_rows("ffn_dwd1", a1, dx4, 1, None)
    dg1, du1 = _ffn_bwd_act("ffn_bwd_act1", dx4, WD, 1, g1, u1)
    dWG = _ffn_dw_cols("ffn_dwg1", h3, dg1, 1, None)
    dWU = _ffn_dw_cols("ffn_dwu1", h3, du1, 1, None)
    dh3 = _ffn_dh("ffn_dh1", dg1, WG, du1, WU, 1)
    dx3, d_ffn1 = _norm_bwd("norm_ffn1_bwd", x3, ffn_norm[1:2], [dh3], dx4)

    do = _mm_nt_plain("attn_out_dx", dx3, WO, BF)
    dWO = _mm_tn_plain("attn_out_dw", o, dx3, BF)
    delta = _attn_delta(o, do)
    tb = _tile(T, 512)
    lse_rows = lse.reshape(H, T // tb, 1, tb)
    delta_rows = delta.reshape(H, T // tb, 1, tb)
    dqp = _flash_dq(qp, kvup, kpe, do, lse, delta, cos, sin)
    dkvup, dkpe_h = _flash_dkv(qp, kvup, kpe, do, lse_rows, delta_rows)
    dql = _mm_nt_cols("q_b_dx", dqp, WQB, F32)
    dWQB = _mm_tn_cols("q_b_dw", ql, dqp, 4, BF)
    dqa, d_qln = _norm_bwd("norm_q_bwd", qa, q_latent_norm, [dql], None)
    dha = _mm_nt_plain("q_a_dx", dqa, WQA, F32)
    dWQA = _mm_tn_plain("q_a_dw", ha, dqa, BF)
    dckv = _mm_nt_cols("kv_b_dx", dkvup, WKVB, F32)
    dWKVB = _mm_tn_cols("kv_b_dw", ckv, dkvup, 4, BF)
    dkvp, d_kvln = _kv_post_bwd(kvp, kv_latent_norm[None], dckv, dkpe_h, cos, sin)
    dhk = _mm_nt_plain("kv_a_dx", dkvp, WKVA, F32)
    dWKVA = _mm_tn_plain("kv_a_dw", hk, dkvp, BF)
    dx2, d_n2 = _norm_bwd("norm_attn_bwd", x2, jnp.stack([kv_in_norm, attn_norm[0]]), [dhk, dha], dx3)

    dWD = _ffn_dw_rows("ffn_dwd0", a0, dx2, 0, dWD)
    dg0, du0 = _ffn_bwd_act("ffn_bwd_act0", dx2, WD, 0, g0, u0)
    dWG = _ffn_dw_cols("ffn_dwg0", h1, dg0, 0, dWG)
    dWU = _ffn_dw_cols("ffn_dwu0", h1, du0, 0, dWU)
    dh1 = _ffn_dh("ffn_dh0", dg0, WG, du0, WU, 0)
    dx1, d_ffn0 = _norm_bwd("norm_ffn0_bwd", x1, ffn_norm[0:1], [dh1], dx2)

    dpre, d_ps = _pool_dpre(dx1, pre, ps_full)
    ddiff = _pool_ddiff(dpre, PW)
    dPW = _pool_dw(diff, dpre)
    dx0, d_pn = _pool_bwd(x0, pn_full, ddiff, dx1)

    full = [dPW, dWKVA.reshape(4, 2, D // 8, KL + 128), dWKVB.reshape(4, 2, KL // 2, -1),
            dWQA.reshape(4, 2, D // 8, QL), dWQB.reshape(4, 2, QL // 2, -1),
            dWO.reshape(4, 2, H * VDIM // 8, D), dWG, dWU, dWD]
    mine, theirs = _reduce_pair(full)
    sums = [_pair_add(f"pair_add{i}", _rows_view(a, a.ndim - 2), _rows_view(b, b.ndim - 2)).reshape(a.shape)
            for i, (a, b) in enumerate(zip(mine, theirs))]
    landed = _reduce_chips(sums)
    halves = []
    for i, a in enumerate(landed):
        a4 = a.reshape((4, int(math.prod(a.shape[1:-2]))) + a.shape[-2:])
        halves.append(_sum4(f"chip_sum{i}", a4).reshape(a.shape[1:]))
    gPW, gWKVA, gWKVB, gWQA, gWQB, gWO, gWG, gWU, gWD = _share_halves(halves)

    lat = jnp.concatenate([d_kvln[0], d_qln[0], jnp.zeros((D - KL - QL,), F32)])
    lrow = jnp.pad(loss_part[0], (0, D - 1))
    small = jnp.stack([d_n2[0], d_n2[1], d_ffn0[0], d_ffn1[0], d_final[0], lat, d_pn[0], d_ps[0], lrow]
                      + [jnp.zeros((D,), F32)] * 7)
    red = _allreduce_small(small)
    loss = red[8, 0]
    g_kv_in, g_attn, g_final = red[0], red[1:2], red[4]
    g_ffn = red[2:4]
    g_kvln, g_qln = red[5, :KL], red[5:6, KL:KL + QL]
    g_pn = lax.dynamic_slice(red[6:7], (0, chip * pg), (1, pg))
    g_ps = lax.dynamic_slice(red[7:8], (0, chip * pg), (1, pg))

    grads = {
        "pool_norm": g_pn, "pool_w": gPW.reshape(pool_w.shape), "pool_scale": g_ps, "kv_in_norm": g_kv_in,
        "w_kv_a": gWKVA.reshape(w_kv_a.shape[0], KL + 128)[:, :n_kva], "kv_latent_norm": g_kvln,
        "w_kv_b": gWKVB.reshape(w_kv_b.shape), "attn_norm": g_attn, "w_q_a": gWQA.reshape(w_q_a.shape),
        "q_latent_norm": g_qln,
        "w_q_b": gWQB.reshape(QL, hs, HEAD_PAD)[:, :, :NOPE + ROPE].reshape(w_q_b.shape),
        "w_o": gWO.reshape(w_o.shape), "ffn_norm": g_ffn, "w_gate": gWG, "w_up": gWU, "w_down": gWD,
        "final_norm": g_final,
    }
    weights = dict(pool_norm=pool_norm, pool_w=pool_w, pool_scale=pool_scale, kv_in_norm=kv_in_norm, w_kv_a=w_kv_a,
                   kv_latent_norm=kv_latent_norm, w_kv_b=w_kv_b, attn_norm=attn_norm, w_q_a=w_q_a,
                   q_latent_norm=q_latent_norm, w_q_b=w_q_b, w_o=w_o, ffn_norm=ffn_norm, w_gate=w_gate, w_up=w_up,
                   w_down=w_down, final_norm=final_norm)
    ms = dict(pool_norm=m_pool_norm, pool_w=m_pool_w, pool_scale=m_pool_scale, kv_in_norm=m_kv_in_norm,
              w_kv_a=m_w_kv_a, kv_latent_norm=m_kv_latent_norm, w_kv_b=m_w_kv_b, attn_norm=m_attn_norm,
              w_q_a=m_w_q_a, q_latent_norm=m_q_latent_norm, w_q_b=m_w_q_b, w_o=m_w_o, ffn_norm=m_ffn_norm,
              w_gate=m_w_gate, w_up=m_w_up, w_down=m_w_down, final_norm=m_final_norm)
    vs = dict(pool_norm=v_pool_norm, pool_w=v_pool_w, pool_scale=v_pool_scale, kv_in_norm=v_kv_in_norm,
              w_kv_a=v_w_kv_a, kv_latent_norm=v_kv_latent_norm, w_kv_b=v_w_kv_b, attn_norm=v_attn_norm,
              w_q_a=v_w_q_a, q_latent_norm=v_q_latent_norm, w_q_b=v_w_q_b, w_o=v_w_o, ffn_norm=v_ffn_norm,
              w_gate=v_w_gate, w_up=v_w_up, w_down=v_w_down, final_norm=v_final_norm)
    names = list(weights)

    def as2d(a):
        return a.reshape((-1, a.shape[-1]))

    big = ["pool_w", "w_kv_a", "w_kv_b", "w_q_a", "w_q_b", "w_o", "w_gate", "w_up", "w_down"]
    delta_w, new_m, new_v = {}, {}, {}
    for nm in big:
        d, m2, v2 = _adamw("adamw_" + nm, as2d(weights[nm]), as2d(grads[nm]), as2d(ms[nm]), as2d(vs[nm]))
        shp = weights[nm].shape
        delta_w[nm], new_m[nm], new_v[nm] = d.reshape(shp), m2.reshape(shp), v2.reshape(shp)
    groups = [["kv_in_norm", "attn_norm", "ffn_norm", "final_norm"], ["kv_latent_norm", "q_latent_norm"],
              ["pool_norm", "pool_scale"]]
    for gi, grp in enumerate(groups):
        cat = lambda t: jnp.concatenate([as2d(t[nm]) for nm in grp], axis=0)
        d, m2, v2 = _adamw(f"adamw_vec{gi}", cat(weights), cat(grads), cat(ms), cat(vs))
        r0 = 0
        for nm in grp:
            shp = weights[nm].shape
            r = as2d(weights[nm]).shape[0]
            delta_w[nm], new_m[nm], new_v[nm] = (d[r0:r0 + r].reshape(shp), m2[r0:r0 + r].reshape(shp),
                                                 v2[r0:r0 + r].reshape(shp))
            r0 += r

    return (loss, dx0[None], *[grads[nm].reshape(weights[nm].shape) for nm in names],
            *[delta_w[nm] for nm in names], *[new_m[nm] for nm in names], *[new_v[nm] for nm in names])
```

```python
import functools
import math

import jax
import jax.numpy as jnp
from jax import lax
from jax.experimental import pallas as pl
from jax.experimental.pallas import tpu as pltpu

BF = jnp.bfloat16
F32 = jnp.float32
MESH = pl.DeviceIdType.MESH

N_HEADS = 16
NOPE = 128
ROPE = 64
VDIM = 128
HEAD_PAD = 256
ROPE_BASE = 10000.0
ATTN_SCALE = 1.0 / math.sqrt(NOPE + ROPE)
POOL_WINDOWS = (2, 4, 8, 16)
HALO = 16
NORM_EPS = 1e-6
ADAM_LR, ADAM_B1, ADAM_B2, ADAM_EPS, ADAM_WD, ADAM_STEP = 0.001, 0.9, 0.999, 1e-08, 0.01, 10
NEG = -1e30

V7X_VMEM_BYTES = 64 * 1024 * 1024
VMEM_CEILING = V7X_VMEM_BYTES - 8 * 1024 * 1024

NN = (((1,), (0,)), ((), ()))
NT = (((1,), (1,)), ((), ()))
TN = (((0,), (0,)), ((), ()))


def _cparams(sem, vmem_mb):
    return pltpu.CompilerParams(dimension_semantics=sem,
                                vmem_limit_bytes=min(vmem_mb * 1024 * 1024, VMEM_CEILING))


def _tile(n, pref):
    if n <= pref:
        return n
    t = (pref // 128) * 128
    while t > 128 and n % t:
        t -= 128
    assert n % t == 0, (n, pref)
    return t


def _sds(shape, dtype):
    return jax.ShapeDtypeStruct(shape, dtype)


def _matmul(name, pairs, pair_specs, dims, grid, nk, extra, extra_specs, out_shapes, out_specs,
            acc_shape, epilogue, vmem_mb):
    n_p, n_e, n_o = len(pairs) // 2, len(extra), len(out_shapes)

    def body(*refs):
        ab = refs[:2 * n_p]
        ex = refs[2 * n_p:2 * n_p + n_e]
        outs = refs[2 * n_p + n_e:2 * n_p + n_e + n_o]

        def partial_sum():
            tot = None
            for p in range(n_p):
                a = ab[2 * p][...]
                b = ab[2 * p + 1][...]
                if a.ndim > 2:
                    a = a.reshape(-1, a.shape[-1])
                if b.ndim > 2:
                    b = b.reshape(-1, b.shape[-1])
                d = lax.dot_general(a.astype(BF), b.astype(BF), dims, preferred_element_type=F32)
                tot = d if tot is None else tot + d
            return tot

        if nk == 1:
            epilogue(partial_sum(), ex, outs)
        else:
            acc = refs[-1]
            kk = pl.program_id(2)

            @pl.when(kk == 0)
            def _():
                acc[...] = partial_sum()

            @pl.when(kk > 0)
            def _():
                acc[...] += partial_sum()

            @pl.when(kk == nk - 1)
            def _():
                epilogue(acc[...], ex, outs)

    scratch = [] if nk == 1 else [pltpu.VMEM(acc_shape, F32)]
    res = pl.pallas_call(
        body, name=name, grid=grid,
        in_specs=list(pair_specs) + list(extra_specs),
        out_specs=list(out_specs), out_shape=list(out_shapes),
        scratch_shapes=scratch,
        compiler_params=_cparams(("arbitrary",) * 3, vmem_mb),
    )(*pairs, *extra)
    return res


def _epi_store(acc, ex, outs):
    outs[0][...] = acc.reshape(outs[0].shape).astype(outs[0].dtype)


def _epi_residual(acc, ex, outs):
    outs[0][...] = (acc + ex[0][...]).astype(outs[0].dtype)


def _swap_halves(x):
    lane = lax.broadcasted_iota(jnp.int32, x.shape, 1)
    return jnp.where((lane % 64) < 32, pltpu.roll(x, 96, 1), pltpu.roll(x, 32, 1))


def _mm_plain(name, a, b, out_dtype, res=None, tm=512, tn=1024):
    M, K = a.shape
    N = b.shape[1]
    tm, tn = _tile(M, tm), _tile(N, tn)
    extra, extra_specs, epi = [], [], _epi_store
    if res is not None:
        extra, extra_specs, epi = [res], [pl.BlockSpec((tm, tn), lambda i, j, k: (i, j))], _epi_residual
    return _matmul(
        name, [a, b],
        [pl.BlockSpec((tm, K), lambda i, j, k: (i, 0)), pl.BlockSpec((K, tn), lambda i, j, k: (0, j))],
        NN, (M // tm, N // tn, 1), 1, extra, extra_specs,
        [_sds((M, N), out_dtype)], [pl.BlockSpec((tm, tn), lambda i, j, k: (i, j))],
        None, epi, 40)[0]


def _mm_cols(name, a, b3, out_dtype, epilogue=_epi_store, extra=(), extra_specs=(), tm=512):
    M, K = a.shape
    G, _, n = b3.shape
    tm = _tile(M, tm)
    return _matmul(
        name, [a, b3],
        [pl.BlockSpec((tm, K), lambda i, j, k: (i, 0)), pl.BlockSpec((None, K, n), lambda i, j, k: (j, 0, 0))],
        NN, (M // tm, G, 1), 1, list(extra), list(extra_specs),
        [_sds((M, G * n), out_dtype)], [pl.BlockSpec((tm, n), lambda i, j, k: (i, j))],
        None, epilogue, 40)[0]


def _mm_nt_plain(name, a, b, out_dtype, tm=512, tn=1024):
    M, K = a.shape
    N = b.shape[0]
    tm, tn = _tile(M, tm), _tile(N, tn)
    return _matmul(
        name, [a, b],
        [pl.BlockSpec((tm, K), lambda i, j, k: (i, 0)), pl.BlockSpec((tn, K), lambda i, j, k: (j, 0))],
        NT, (M // tm, N // tn, 1), 1, [], [],
        [_sds((M, N), out_dtype)], [pl.BlockSpec((tm, tn), lambda i, j, k: (i, j))],
        None, _epi_store, 44)[0]


def _mm_nt_cols(name, a, b3, out_dtype, tm=512):
    M = a.shape[0]
    G, K, n = b3.shape
    tm = _tile(M, tm)
    return _matmul(
        name, [a, b3],
        [pl.BlockSpec((tm, n), lambda i, j, k: (i, k)), pl.BlockSpec((None, K, n), lambda i, j, k: (k, 0, 0))],
        NT, (M // tm, 1, G), G, [], [],
        [_sds((M, K), out_dtype)], [pl.BlockSpec((tm, K), lambda i, j, k: (i, 0))],
        (tm, K), _epi_store, 40)[0]


def _mm_tn_plain(name, a, b, out_dtype, tt=512, tn=1024):
    T, K = a.shape
    N = b.shape[1]
    tt, tn = _tile(T, tt), _tile(N, tn)
    return _matmul(
        name, [a, b],
        [pl.BlockSpec((tt, K), lambda i, j, k: (k, 0)), pl.BlockSpec((tt, tn), lambda i, j, k: (k, j))],
        TN, (1, N // tn, T // tt), T // tt, [], [],
        [_sds((K, N), out_dtype)], [pl.BlockSpec((K, tn), lambda i, j, k: (0, j))],
        (K, tn), _epi_store, 48)[0]


def _mm_tn_cols(name, a, b, G, out_dtype, tt=512):
    T, K = a.shape
    n = b.shape[1] // G
    tt = _tile(T, tt)
    return _matmul(
        name, [a, b],
        [pl.BlockSpec((tt, K), lambda i, j, k: (k, 0)), pl.BlockSpec((tt, n), lambda i, j, k: (k, j))],
        TN, (1, G, T // tt), T // tt, [], [],
        [_sds((G, K, n), out_dtype)], [pl.BlockSpec((None, K, n), lambda i, j, k: (j, 0, 0))],
        (K, n), _epi_store, 48)[0]


def _norm_fwd(name, x, gains, tb=512):
    T, D = x.shape
    G = gains.shape[0]
    tb = _tile(T, tb)

    def body(x_ref, g_ref, *outs):
        xv = x_ref[...]
        xh = xv * lax.rsqrt(jnp.mean(xv * xv, axis=-1, keepdims=True) + NORM_EPS)
        for g in range(G):
            outs[g][...] = (xh * g_ref[g:g + 1, :]).astype(BF)

    return pl.pallas_call(
        body, name=name, grid=(T // tb,),
        in_specs=[pl.BlockSpec((tb, D), lambda i: (i, 0)), pl.BlockSpec((G, D), lambda i: (0, 0))],
        out_specs=[pl.BlockSpec((tb, D), lambda i: (i, 0))] * G,
        out_shape=[_sds((T, D), BF)] * G,
        compiler_params=_cparams(("arbitrary",), 40),
    )(x, gains)


def _norm_bwd(name, x, gains, dhs, dres, tb=256):
    T, D = x.shape
    G = gains.shape[0]
    tb = _tile(T, tb)
    has_res = dres is not None

    def body(*refs):
        x_ref, g_ref = refs[0], refs[1]
        dh_refs = refs[2:2 + G]
        res_ref = refs[2 + G] if has_res else None
        dx_ref, dg_ref = refs[-2], refs[-1]
        i = pl.program_id(0)
        xv = x_ref[...]
        r = lax.rsqrt(jnp.mean(xv * xv, axis=-1, keepdims=True) + NORM_EPS)
        xh = xv * r
        dx = res_ref[...] if has_res else jnp.zeros_like(xv)
        rows = []
        for g in range(G):
            dh = dh_refs[g][...].astype(F32)
            dy = dh * g_ref[g:g + 1, :]
            dx = dx + r * (dy - xh * jnp.mean(dy * xh, axis=-1, keepdims=True))
            rows.append(jnp.sum(dh * xh, axis=0, keepdims=True))
        dx_ref[...] = dx

        @pl.when(i == 0)
        def _():
            for g in range(G):
                dg_ref[g:g + 1, :] = rows[g]

        @pl.when(i > 0)
        def _():
            for g in range(G):
                dg_ref[g:g + 1, :] += rows[g]

    blk = pl.BlockSpec((tb, D), lambda i: (i, 0))
    ins = [x, gains] + list(dhs) + ([dres] if has_res else [])
    in_specs = [blk, pl.BlockSpec((G, D), lambda i: (0, 0))] + [blk] * (G + (1 if has_res else 0))
    return pl.pallas_call(
        body, name=name, grid=(T // tb,), in_specs=in_specs,
        out_specs=[blk, pl.BlockSpec((G, D), lambda i: (0, 0))],
        out_shape=[_sds((T, D), F32), _sds((G, D), F32)],
        compiler_params=_cparams(("arbitrary",), 48),
    )(*ins)


def _pool_fwd(x, gain, tb=256):
    T, D = x.shape
    tb = _tile(T, tb)
    pg = D // len(POOL_WINDOWS)
    per = tb // HALO

    def body(x_ref, xp_ref, g_ref, diff_ref):
        i = pl.program_id(0)
        xx = jnp.concatenate([xp_ref[...], x_ref[...]], axis=0)
        h = xx * lax.rsqrt(jnp.mean(xx * xx, axis=-1, keepdims=True) + NORM_EPS) * g_ref[...]
        row = lax.broadcasted_iota(jnp.int32, (HALO + tb, 1), 0)
        h = jnp.where((row >= HALO) | (i > 0), h, 0.0)
        t = i * tb + row[HALO:] - HALO
        for g, w in enumerate(POOL_WINDOWS):
            hg = h[:, g * pg:(g + 1) * pg]
            s, k = hg, 1
            while k < w:
                s = s + pltpu.roll(s, k, 0)
                k *= 2
            cnt = jnp.minimum(t + 1, w).astype(F32)
            diff_ref[:, g * pg:(g + 1) * pg] = (s[HALO:] / cnt - hg[HALO:]).astype(BF)

    return pl.pallas_call(
        body, name="pool_fwd", grid=(T // tb,),
        in_specs=[pl.BlockSpec((tb, D), lambda i: (i, 0)),
                  pl.BlockSpec((HALO, D), lambda i: (jnp.maximum(i * per - 1, 0), 0)),
                  pl.BlockSpec((1, D), lambda i: (0, 0))],
        out_specs=pl.BlockSpec((tb, D), lambda i: (i, 0)),
        out_shape=_sds((T, D), BF),
        compiler_params=_cparams(("arbitrary",), 40),
    )(x, x, gain)


def _pool_bwd(x, gain, ddiff, dres, tb=256):
    T, D = x.shape
    tb = _tile(T, tb)
    pg = D // len(POOL_WINDOWS)
    per = tb // HALO
    nblk = T // HALO

    def body(x_ref, g_ref, dd_ref, ddn_ref, res_ref, dx_ref, dg_ref):
        i = pl.program_id(0)
        last = i == T // tb - 1
        dd = jnp.concatenate([dd_ref[...], ddn_ref[...]], axis=0)
        row = lax.broadcasted_iota(jnp.int32, (tb + HALO, 1), 0)
        dd = jnp.where((row < tb) | jnp.logical_not(last), dd, 0.0)
        t = i * tb + row
        parts = []
        for g, w in enumerate(POOL_WINDOWS):
            dg_ = dd[:, g * pg:(g + 1) * pg]
            e = dg_ / jnp.minimum(t + 1, w).astype(F32)
            s, k = e, 1
            while k < w:
                s = s + pltpu.roll(s, tb + HALO - k, 0)
                k *= 2
            parts.append(s[:tb] - dg_[:tb])
        dh = jnp.concatenate(parts, axis=1)
        xv = x_ref[...]
        r = lax.rsqrt(jnp.mean(xv * xv, axis=-1, keepdims=True) + NORM_EPS)
        xh = xv * r
        dy = dh * g_ref[...]
        dx_ref[...] = res_ref[...] + r * (dy - xh * jnp.mean(dy * xh, axis=-1, keepdims=True))
        part = jnp.sum(dh * xh, axis=0, keepdims=True)

        @pl.when(i == 0)
        def _():
            dg_ref[...] = part

        @pl.when(i > 0)
        def _():
            dg_ref[...] += part

    blk = pl.BlockSpec((tb, D), lambda i: (i, 0))
    return pl.pallas_call(
        body, name="pool_bwd", grid=(T // tb,),
        in_specs=[blk, pl.BlockSpec((1, D), lambda i: (0, 0)), blk,
                  pl.BlockSpec((HALO, D), lambda i: (jnp.minimum((i + 1) * per, nblk - 1), 0)), blk],
        out_specs=[blk, pl.BlockSpec((1, D), lambda i: (0, 0))],
        out_shape=[_sds((T, D), F32), _sds((1, D), F32)],
        compiler_params=_cparams(("arbitrary",), 48),
    )(x, gain, ddiff, ddiff, dres)


def _pool_w_spec(pg):
    return pl.BlockSpec((4, None, None, pg // 4, pg), lambda i, j, k: (0, j // 2, j % 2, 0, 0))


def _pool_mix(diff, pw, scale, x, tm=512):
    T, D = x.shape
    pg = D // 4
    tm = _tile(T, tm)

    def epi(acc, ex, outs):
        outs[0][...] = acc
        outs[1][...] = ex[1][...] + acc * ex[0][...]

    blk = pl.BlockSpec((tm, pg), lambda i, j, k: (i, j))
    return _matmul(
        "pool_mix", [diff, pw], [blk, _pool_w_spec(pg)], NN, (T // tm, 4, 1), 1,
        [scale, x], [pl.BlockSpec((1, pg), lambda i, j, k: (0, j)), blk],
        [_sds((T, D), F32), _sds((T, D), F32)], [blk, blk], None, epi, 32)


def _pool_dpre(dx, pre, scale, tb=512):
    T, D = dx.shape
    tb = _tile(T, tb)

    def body(dx_ref, pre_ref, s_ref, dpre_ref, ds_ref):
        i = pl.program_id(0)
        d = dx_ref[...]
        dpre_ref[...] = (d * s_ref[...]).astype(BF)
        part = jnp.sum(d * pre_ref[...], axis=0, keepdims=True)

        @pl.when(i == 0)
        def _():
            ds_ref[...] = part

        @pl.when(i > 0)
        def _():
            ds_ref[...] += part

    blk = pl.BlockSpec((tb, D), lambda i: (i, 0))
    vec = pl.BlockSpec((1, D), lambda i: (0, 0))
    return pl.pallas_call(
        body, name="pool_dpre", grid=(T // tb,), in_specs=[blk, blk, vec], out_specs=[blk, vec],
        out_shape=[_sds((T, D), BF), _sds((1, D), F32)],
        compiler_params=_cparams(("arbitrary",), 40),
    )(dx, pre, scale)


def _pool_ddiff(dpre, pw, tm=512):
    T, D = dpre.shape
    pg = D // 4
    tm = _tile(T, tm)
    blk = pl.BlockSpec((tm, pg), lambda i, j, k: (i, j))
    return _matmul("pool_ddiff", [dpre, pw], [blk, _pool_w_spec(pg)], NT, (T // tm, 4, 1), 1, [], [],
                   [_sds((T, D), F32)], [blk], None, _epi_store, 32)[0]


def _pool_dw(diff, dpre, tt=512):
    T, D = diff.shape
    pg = D // 4
    tt = _tile(T, tt)
    blk = pl.BlockSpec((tt, pg), lambda i, j, k: (k, j))
    return _matmul("pool_dw", [diff, dpre], [blk, blk], TN, (1, 4, T // tt), T // tt, [], [],
                   [_sds((4, 2, 2, pg // 4, pg), BF)], [_pool_w_spec(pg)], (pg, pg), _epi_store, 32)[0]


def _w_cols_spec(l, K, n):
    return pl.BlockSpec((None, None, K, n), lambda i, j, k: (j, l, 0, 0))


def _ffn_up(name, h, wg, wu, l, tm=512):
    T, D = h.shape
    n = wg.shape[-1]
    F = 4 * n
    tm = _tile(T, tm)

    def body(h_ref, wg_ref, wu_ref, g_ref, u_ref, a_ref):
        hv = h_ref[...]
        g = jnp.dot(hv, wg_ref[...], preferred_element_type=F32)
        u = jnp.dot(hv, wu_ref[...], preferred_element_type=F32)
        g_ref[...] = g.astype(BF)
        u_ref[...] = u.astype(BF)
        a_ref[...] = (g * jax.nn.sigmoid(g) * u).astype(BF)

    w_spec = pl.BlockSpec((None, None, D, n), lambda j, i: (j, l, 0, 0))
    o_spec = pl.BlockSpec((tm, n), lambda j, i: (i, j))
    return pl.pallas_call(
        body, name=name, grid=(4, T // tm),
        in_specs=[pl.BlockSpec((tm, D), lambda j, i: (i, 0)), w_spec, w_spec],
        out_specs=[o_spec] * 3, out_shape=[_sds((T, F), BF)] * 3,
        compiler_params=_cparams(("arbitrary", "arbitrary"), 56),
    )(h, wg, wu)


def _ffn_down(name, a, wd, l, x, tm=512, tn=1024):
    T, D = x.shape
    n = wd.shape[2]
    tm, tn = _tile(T, tm), _tile(D, tn)
    return _matmul(
        name, [a, wd],
        [pl.BlockSpec((tm, n), lambda i, j, k: (i, k)),
         pl.BlockSpec((None, None, n, tn), lambda i, j, k: (k, l, 0, j))],
        NN, (T // tm, D // tn, 4), 4, [x], [pl.BlockSpec((tm, tn), lambda i, j, k: (i, j))],
        [_sds((T, D), F32)], [pl.BlockSpec((tm, tn), lambda i, j, k: (i, j))],
        (tm, tn), _epi_residual, 40)[0]


def _ffn_bwd_act(name, dx, wd, l, g, u, tm=512):
    T, D = dx.shape
    n = wd.shape[2]
    tm = _tile(T, tm)

    def epi(acc, ex, outs):
        gv = ex[0][...].astype(F32)
        uv = ex[1][...].astype(F32)
        sig = jax.nn.sigmoid(gv)
        outs[0][...] = (acc * uv * (sig * (1.0 + gv * (1.0 - sig)))).astype(BF)
        outs[1][...] = (acc * (gv * sig)).astype(BF)

    blk = pl.BlockSpec((tm, n), lambda i, j, k: (i, j))
    return _matmul(
        name, [dx, wd],
        [pl.BlockSpec((tm, D), lambda i, j, k: (i, 0)), _w_cols_spec(l, n, D)],
        NT, (T // tm, 4, 1), 1, [g, u], [blk, blk],
        [_sds((T, 4 * n), BF)] * 2, [blk, blk], None, epi, 56)


def _ffn_dh(name, dg, wg, du, wu, l, tm=512):
    T = dg.shape[0]
    D, n = wg.shape[2], wg.shape[3]
    tm = _tile(T, tm)
    a_spec = pl.BlockSpec((tm, n), lambda i, j, k: (i, k))
    w_spec = pl.BlockSpec((None, None, D, n), lambda i, j, k: (k, l, 0, 0))
    return _matmul(
        name, [dg, wg, du, wu], [a_spec, w_spec, a_spec, w_spec], NT, (T // tm, 1, 4), 4, [], [],
        [_sds((T, D), F32)], [pl.BlockSpec((tm, D), lambda i, j, k: (i, 0))], (tm, D), _epi_store, 56)[0]


def _ffn_dw_cols(name, h, d, l, prev, tt=512):
    T, D = h.shape
    n = d.shape[1] // 4
    tt = _tile(T, tt)
    return _dw_into(name, h, d, prev, (4, 2, D, n),
                    pl.BlockSpec((tt, D), lambda i, j, k: (k, 0)),
                    pl.BlockSpec((tt, n), lambda i, j, k: (k, j)),
                    pl.BlockSpec((None, None, D, n), lambda i, j, k: (j, l, 0, 0)),
                    (1, 4, T // tt), (D, n))


def _ffn_dw_rows(name, a, dx, l, prev, tt=512, tn=1024):
    T, D = dx.shape
    n = a.shape[1] // 4
    tt, tn = _tile(T, tt), _tile(D, tn)
    nj = D // tn
    return _dw_into(name, a, dx, prev, (4, 2, n, D),
                    pl.BlockSpec((tt, n), lambda i, j, k: (k, j // nj)),
                    pl.BlockSpec((tt, tn), lambda i, j, k: (k, j % nj)),
                    pl.BlockSpec((None, None, n, tn), lambda i, j, k: (j // nj, l, 0, j % nj)),
                    (1, 4 * nj, T // tt), (n, tn))


def _dw_into(name, a, b, prev, shape, a_spec, b_spec, o_spec, grid, acc_shape):
    nk = grid[2]

    def body(*refs):
        a_ref, b_ref = refs[0], refs[1]
        o_ref, acc = refs[-2], refs[-1]
        kk = pl.program_id(2)
        d = lax.dot_general(a_ref[...].astype(BF), b_ref[...].astype(BF), TN, preferred_element_type=F32)

        @pl.when(kk == 0)
        def _():
            acc[...] = d

        @pl.when(kk > 0)
        def _():
            acc[...] += d

        @pl.when(kk == nk - 1)
        def _():
            o_ref[...] = acc[...].astype(o_ref.dtype)

    ins, in_specs, alias = [a, b], [a_spec, b_spec], {}
    if prev is not None:
        ins.append(prev)
        in_specs.append(pl.BlockSpec(memory_space=pl.ANY))
        alias = {2: 0}
    return pl.pallas_call(
        body, name=name, grid=grid, in_specs=in_specs, out_specs=o_spec, out_shape=_sds(shape, BF),
        scratch_shapes=[pltpu.VMEM(acc_shape, F32)], input_output_aliases=alias,
        compiler_params=_cparams(("arbitrary",) * 3, 56),
    )(*ins)


def _kv_post(kvp, gain, cos, sin, tb=512):
    T, W = kvp.shape
    KL = W - 128
    tb = _tile(T, tb)

    def body(kv_ref, g_ref, c_ref, s_ref, ckv_ref, kpe_ref):
        lat = kv_ref[:, :KL]
        ckv_ref[...] = (lat * lax.rsqrt(jnp.mean(lat * lat, axis=-1, keepdims=True) + NORM_EPS)
                        * g_ref[...]).astype(BF)
        pe = kv_ref[:, KL:]
        kpe_ref[...] = (pe * c_ref[...] + _swap_halves(pe) * s_ref[...]).astype(BF)

    tab = pl.BlockSpec((tb, 128), lambda i: (i, 0))
    return pl.pallas_call(
        body, name="kv_post", grid=(T // tb,),
        in_specs=[pl.BlockSpec((tb, W), lambda i: (i, 0)), pl.BlockSpec((1, KL), lambda i: (0, 0)), tab, tab],
        out_specs=[pl.BlockSpec((tb, KL), lambda i: (i, 0)), tab],
        out_shape=[_sds((T, KL), BF), _sds((T, 128), BF)],
        compiler_params=_cparams(("arbitrary",), 32),
    )(kvp, gain, cos, sin)


def _kv_post_bwd(kvp, gain, dckv, dkpe_heads, cos, sin, tb=256):
    T, W = kvp.shape
    KL = W - 128
    H = dkpe_heads.shape[0]
    tb = _tile(T, tb)

    def body(kv_ref, g_ref, dc_ref, dk_ref, c_ref, s_ref, out_ref, dg_ref):
        i = pl.program_id(0)
        lat = kv_ref[:, :KL]
        r = lax.rsqrt(jnp.mean(lat * lat, axis=-1, keepdims=True) + NORM_EPS)
        xh = lat * r
        dh = dc_ref[...]
        dy = dh * g_ref[...]
        out_ref[:, :KL] = (r * (dy - xh * jnp.mean(dy * xh, axis=-1, keepdims=True))).astype(BF)
        d = dk_ref[0]
        for h in range(1, H):
            d = d + dk_ref[h]
        out_ref[:, KL:] = (d * c_ref[...] - _swap_halves(d) * s_ref[...]).astype(BF)
        part = jnp.sum(dh * xh, axis=0, keepdims=True)

        @pl.when(i == 0)
        def _():
            dg_ref[...] = part

        @pl.when(i > 0)
        def _():
            dg_ref[...] += part

    tab = pl.BlockSpec((tb, 128), lambda i: (i, 0))
    vec = pl.BlockSpec((1, KL), lambda i: (0, 0))
    return pl.pallas_call(
        body, name="kv_post_bwd", grid=(T // tb,),
        in_specs=[pl.BlockSpec((tb, W), lambda i: (i, 0)), vec, pl.BlockSpec((tb, KL), lambda i: (i, 0)),
                  pl.BlockSpec((H, tb, 128), lambda i: (0, i, 0)), tab, tab],
        out_specs=[pl.BlockSpec((tb, W), lambda i: (i, 0)), vec],
        out_shape=[_sds((T, W), BF), _sds((1, KL), F32)],
        compiler_params=_cparams(("arbitrary",), 32),
    )(kvp, gain, dckv, dkpe_heads, cos, sin)


def _q_up(ql, wqb, cos, sin, tm=512):
    n = wqb.shape[2]
    tm = _tile(ql.shape[0], tm)

    def epi(acc, ex, outs):
        c, s = ex[0][...] * LOG2_SCALE, ex[1][...] * LOG2_SCALE
        for j in range(n // HEAD_PAD):
            a0 = j * HEAD_PAD
            outs[0][:, a0:a0 + NOPE] = (acc[:, a0:a0 + NOPE] * LOG2_SCALE).astype(BF)
            pe = acc[:, a0 + NOPE:a0 + HEAD_PAD]
            outs[0][:, a0 + NOPE:a0 + HEAD_PAD] = (pe * c + _swap_halves(pe) * s).astype(BF)

    tab = pl.BlockSpec((tm, 128), lambda i, j, k: (i, 0))
    return _mm_cols("q_up", ql, wqb, BF, epilogue=epi, extra=[cos, sin], extra_specs=[tab, tab], tm=tm)


def _causal_mask(tb):
    r = lax.broadcasted_iota(jnp.int32, (tb, tb), 0)
    c = lax.broadcasted_iota(jnp.int32, (tb, tb), 1)
    return r, c


HP = 2
LOG2_SCALE = ATTN_SCALE * math.log2(math.e)


def _fill_keys(k_scr, kv_ref, kpe_ref):
    for hh in range(HP):
        k_scr[hh, :, :NOPE] = kv_ref[:, hh * HEAD_PAD:hh * HEAD_PAD + NOPE]
        k_scr[hh, :, NOPE:] = kpe_ref[...]


def _flash_fwd(qp, kvup, kpe, tb=512):
    T = qp.shape[0]
    H = qp.shape[1] // HEAD_PAD
    tb = _tile(T, tb)

    def body(q_ref, kv_ref, kpe_ref, o_ref, lse_ref, k_scr):
        iq = pl.program_id(1)

        @pl.when(iq == 0)
        def _():
            _fill_keys(k_scr, kv_ref, kpe_ref)

        qs = [q_ref[:, hh * HEAD_PAD:(hh + 1) * HEAD_PAD] for hh in range(HP)]

        def blk(ik, carry, masked):
            off = pl.multiple_of(ik * tb, tb)
            ss = []
            for hh in range(HP):
                k = k_scr[hh, pl.ds(off, tb), :]
                s = lax.dot_general(qs[hh], k, NT, preferred_element_type=F32)
                if masked:
                    r, c = _causal_mask(tb)
                    s = jnp.where(c <= r, s, NEG)
                ss.append(s)
            ps = []
            for hh in range(HP):
                m, l, acc = carry[hh]
                m2 = jnp.maximum(m, jnp.max(ss[hh], axis=-1, keepdims=True))
                p = jnp.exp2(ss[hh] - m2)
                a = jnp.exp2(m - m2)
                ps.append((m2, a, a * l + jnp.sum(p, axis=-1, keepdims=True), p.astype(BF)))
            out = []
            for hh in range(HP):
                m2, a, l2, p = ps[hh]
                v = kv_ref[pl.ds(off, tb), hh * HEAD_PAD + NOPE:(hh + 1) * HEAD_PAD]
                out.append((m2, l2, a * carry[hh][2] + lax.dot_general(p, v, NN, preferred_element_type=F32)))
            return tuple(out)

        one = (jnp.full((tb, 1), NEG, F32), jnp.zeros((tb, 1), F32), jnp.zeros((tb, VDIM), F32))
        carry = lax.fori_loop(0, iq, lambda ik, cr: blk(ik, cr, False), (one,) * HP)
        carry = blk(iq, carry, True)
        for hh in range(HP):
            m, l, acc = carry[hh]
            o_ref[:, hh * VDIM:(hh + 1) * VDIM] = (acc / l).astype(BF)
            lse_ref[hh] = m + jnp.log2(l)

    return pl.pallas_call(
        body, name="flash_fwd", grid=(H // HP, T // tb),
        in_specs=[pl.BlockSpec((tb, HP * HEAD_PAD), lambda h, i: (i, h)),
                  pl.BlockSpec((T, HP * HEAD_PAD), lambda h, i: (0, h)),
                  pl.BlockSpec((T, 128), lambda h, i: (0, 0))],
        out_specs=[pl.BlockSpec((tb, HP * VDIM), lambda h, i: (i, h)),
                   pl.BlockSpec((HP, tb, 1), lambda h, i: (h, i, 0))],
        out_shape=[_sds((T, H * VDIM), BF), _sds((H, T, 1), F32)],
        scratch_shapes=[pltpu.VMEM((HP, T, HEAD_PAD), BF)],
        compiler_params=_cparams(("arbitrary", "arbitrary"), 56),
    )(qp, kvup, kpe)


def _flash_dq(qp, kvup, kpe, o, do, lse, cos, sin, tb=512):
    T = qp.shape[0]
    H = qp.shape[1] // HEAD_PAD
    tb = _tile(T, tb)

    def body(q_ref, kv_ref, kpe_ref, o_ref, do_ref, lse_ref, c_ref, s_ref, dq_ref, dl_ref, k_scr):
        iq = pl.program_id(1)

        @pl.when(iq == 0)
        def _():
            _fill_keys(k_scr, kv_ref, kpe_ref)

        qs, dos, lses, dls = [], [], [], []
        for hh in range(HP):
            qs.append(q_ref[:, hh * HEAD_PAD:(hh + 1) * HEAD_PAD])
            dov = do_ref[:, hh * VDIM:(hh + 1) * VDIM]
            dos.append(dov)
            lses.append(lse_ref[hh])
            dl = jnp.sum(o_ref[:, hh * VDIM:(hh + 1) * VDIM].astype(F32) * dov.astype(F32), axis=-1, keepdims=True)
            dl_ref[hh] = dl
            dls.append(dl)

        def blk(ik, dqs, masked):
            off = pl.multiple_of(ik * tb, tb)
            ks, ss, dps = [], [], []
            for hh in range(HP):
                k = k_scr[hh, pl.ds(off, tb), :]
                v = kv_ref[pl.ds(off, tb), hh * HEAD_PAD + NOPE:(hh + 1) * HEAD_PAD]
                s = lax.dot_general(qs[hh], k, NT, preferred_element_type=F32)
                if masked:
                    r, c = _causal_mask(tb)
                    s = jnp.where(c <= r, s, NEG)
                ks.append(k)
                ss.append(s)
                dps.append(lax.dot_general(dos[hh], v, NT, preferred_element_type=F32))
            dss = [(jnp.exp2(ss[hh] - lses[hh]) * (dps[hh] - dls[hh])).astype(BF) for hh in range(HP)]
            return tuple(dqs[hh] + lax.dot_general(dss[hh], ks[hh], NN, preferred_element_type=F32)
                         for hh in range(HP))

        dqs = lax.fori_loop(0, iq, lambda ik, cr: blk(ik, cr, False), (jnp.zeros((tb, HEAD_PAD), F32),) * HP)
        dqs = blk(iq, dqs, True)
        for hh in range(HP):
            dq = dqs[hh] * ATTN_SCALE
            a0 = hh * HEAD_PAD
            dq_ref[:, a0:a0 + NOPE] = dq[:, :NOPE].astype(BF)
            dpe = dq[:, NOPE:]
            dq_ref[:, a0 + NOPE:a0 + HEAD_PAD] = (dpe * c_ref[...] - _swap_halves(dpe) * s_ref[...]).astype(BF)

    tab = pl.BlockSpec((tb, 128), lambda h, i: (i, 0))
    col = pl.BlockSpec((HP, tb, 1), lambda h, i: (h, i, 0))
    ov = pl.BlockSpec((tb, HP * VDIM), lambda h, i: (i, h))
    return pl.pallas_call(
        body, name="flash_dq", grid=(H // HP, T // tb),
        in_specs=[pl.BlockSpec((tb, HP * HEAD_PAD), lambda h, i: (i, h)),
                  pl.BlockSpec((T, HP * HEAD_PAD), lambda h, i: (0, h)),
                  pl.BlockSpec((T, 128), lambda h, i: (0, 0)), ov, ov, col, tab, tab],
        out_specs=[pl.BlockSpec((tb, HP * HEAD_PAD), lambda h, i: (i, h)), col],
        out_shape=[_sds((T, H * HEAD_PAD), BF), _sds((H, T, 1), F32)],
        scratch_shapes=[pltpu.VMEM((HP, T, HEAD_PAD), BF)],
        compiler_params=_cparams(("arbitrary", "arbitrary"), 56),
    )(qp, kvup, kpe, o, do, lse, cos, sin)


def _flash_dkv(qp, kvup, kpe, do, lse_rows, delta_rows, tb=512):
    T = qp.shape[0]
    H = qp.shape[1] // HEAD_PAD
    tb = _tile(T, tb)
    nq = T // tb

    def body(kv_ref, kpe_ref, q_ref, do_ref, lse_ref, dl_ref, dkv_ref, dkpe_ref):
        ik = pl.program_id(1)
        ks = [jnp.concatenate([kv_ref[:, hh * HEAD_PAD:hh * HEAD_PAD + NOPE], kpe_ref[...]], axis=1)
              for hh in range(HP)]
        vs = [kv_ref[:, hh * HEAD_PAD + NOPE:(hh + 1) * HEAD_PAD] for hh in range(HP)]

        def blk(iq, carry, masked):
            off = pl.multiple_of(iq * tb, tb)
            qv, dov, sts, dpts = [], [], [], []
            for hh in range(HP):
                q = q_ref[pl.ds(off, tb), hh * HEAD_PAD:(hh + 1) * HEAD_PAD]
                d = do_ref[pl.ds(off, tb), hh * VDIM:(hh + 1) * VDIM]
                st = lax.dot_general(ks[hh], q, NT, preferred_element_type=F32)
                if masked:
                    r, c = _causal_mask(tb)
                    st = jnp.where(r <= c, st, NEG)
                qv.append(q)
                dov.append(d)
                sts.append(st)
                dpts.append(lax.dot_general(vs[hh], d, NT, preferred_element_type=F32))
            pts = [jnp.exp2(sts[hh] - lse_ref[hh, iq]) for hh in range(HP)]
            dsts = [(pts[hh] * (dpts[hh] - dl_ref[hh, iq])).astype(BF) for hh in range(HP)]
            out = []
            for hh in range(HP):
                dk, dv = carry[hh]
                dv = dv + lax.dot_general(pts[hh].astype(BF), dov[hh], NN, preferred_element_type=F32)
                dk = dk + lax.dot_general(dsts[hh], qv[hh], NN, preferred_element_type=F32)
                out.append((dk, dv))
            return tuple(out)

        zero = (jnp.zeros((tb, HEAD_PAD), F32), jnp.zeros((tb, VDIM), F32))
        carry = blk(ik, (zero,) * HP, True)
        carry = lax.fori_loop(ik + 1, nq, lambda iq, cr: blk(iq, cr, False), carry)
        dkpe = None
        for hh in range(HP):
            dk, dv = carry[hh]
            dk = dk * (ATTN_SCALE / LOG2_SCALE)
            a0 = hh * HEAD_PAD
            dkv_ref[:, a0:a0 + NOPE] = dk[:, :NOPE].astype(BF)
            dkv_ref[:, a0 + NOPE:a0 + HEAD_PAD] = dv.astype(BF)
            dkpe = dk[:, NOPE:] if dkpe is None else dkpe + dk[:, NOPE:]
        dkpe_ref[...] = dkpe

    rows = pl.BlockSpec((HP, nq, 1, tb), lambda h, i: (h, 0, 0, 0))
    return pl.pallas_call(
        body, name="flash_dkv", grid=(H // HP, nq),
        in_specs=[pl.BlockSpec((tb, HP * HEAD_PAD), lambda h, i: (i, h)),
                  pl.BlockSpec((tb, 128), lambda h, i: (i, 0)),
                  pl.BlockSpec((T, HP * HEAD_PAD), lambda h, i: (0, h)),
                  pl.BlockSpec((T, HP * VDIM), lambda h, i: (0, h)), rows, rows],
        out_specs=[pl.BlockSpec((tb, HP * HEAD_PAD), lambda h, i: (i, h)),
                   pl.BlockSpec((None, tb, 128), lambda h, i: (h, i, 0))],
        out_shape=[_sds((T, H * HEAD_PAD), BF), _sds((H // HP, T, 128), F32)],
        compiler_params=_cparams(("arbitrary", "arbitrary"), 56),
    )(kvup, kpe, qp, do, lse_rows, delta_rows)


def _final_loss(x, gain, target, tb=256):
    T, D = x.shape
    tb = _tile(T, tb)

    def body(x_ref, g_ref, t_ref, loss_ref, dx_ref, dg_ref):
        i = pl.program_id(0)
        xv = x_ref[...]
        gv = g_ref[...]
        r = lax.rsqrt(jnp.mean(xv * xv, axis=-1, keepdims=True) + NORM_EPS)
        xh = xv * r
        e = xh * gv - t_ref[...]
        lpart = 0.5 * jnp.sum(jnp.mean(e * e, axis=-1, keepdims=True), axis=0, keepdims=True)
        dy = e / D
        dyg = dy * gv
        dx_ref[...] = r * (dyg - xh * jnp.mean(dyg * xh, axis=-1, keepdims=True))
        gpart = jnp.sum(dy * xh, axis=0, keepdims=True)

        @pl.when(i == 0)
        def _():
            loss_ref[...] = lpart
            dg_ref[...] = gpart

        @pl.when(i > 0)
        def _():
            loss_ref[...] += lpart
            dg_ref[...] += gpart

    blk = pl.BlockSpec((tb, D), lambda i: (i, 0))
    vec = pl.BlockSpec((1, D), lambda i: (0, 0))
    return pl.pallas_call(
        body, name="final_loss", grid=(T // tb,), in_specs=[blk, vec, blk],
        out_specs=[pl.BlockSpec((1, 1), lambda i: (0, 0)), blk, vec],
        out_shape=[_sds((1, 1), F32), _sds((T, D), F32), _sds((1, D), F32)],
        compiler_params=_cparams(("arbitrary",), 40),
    )(x, gain, target)


def _row_tile(R, pref=256):
    return _tile(R, pref) if R % 128 == 0 else R


def _prefetch_call(body, name, grid, in_specs, out_specs, out_shape, scalar, operands, vmem_mb=32):
    return pl.pallas_call(
        body, name=name, out_shape=out_shape,
        grid_spec=pltpu.PrefetchScalarGridSpec(num_scalar_prefetch=1, grid=grid, in_specs=in_specs,
                                               out_specs=out_specs),
        compiler_params=_cparams(("arbitrary",) * len(grid), vmem_mb),
    )(scalar, *operands)


def _place_own(name, s, chip):
    _, n, R, C = s.shape
    tr = _row_tile(R, 512)

    def body(chip_ref, s_ref, o_ref):
        o_ref[...] = s_ref[...]

    return _prefetch_call(
        body, name, (2, n, R // tr),
        [pl.BlockSpec((None, None, tr, C), lambda h, j, r, cr: (h, j, r, 0))],
        pl.BlockSpec((None, None, None, tr, C), lambda h, j, r, cr: (cr[0], h, j, r, 0)),
        _sds((4,) + s.shape, s.dtype), chip, [s])


def _pair_add(name, g, theirs, core):
    _, _, n, R, C = g.shape
    tr = _row_tile(R)

    def body(c_ref, a_ref, b_ref, o_ref):
        o_ref[...] = (a_ref[...].astype(F32) + b_ref[...].astype(F32)).astype(BF)

    blk = pl.BlockSpec((None, None, tr, C), lambda k, j, r, cr: (k, j, r, 0))
    return _prefetch_call(
        body, name, (4, n, R // tr),
        [pl.BlockSpec((None, None, None, tr, C), lambda k, j, r, cr: (k, cr[0], j, r, 0)), blk], blk,
        _sds(theirs.shape, BF), core, [g, theirs])


def _chip_sum(name, sums, landed, chip):
    _, n, R, C = sums.shape
    tr = _row_tile(R)

    def body(chip_ref, own_ref, l_ref, o_ref):
        s = own_ref[...].astype(F32)
        for j in range(3):
            s = s + l_ref[j].astype(F32)
        o_ref[...] = s

    return _prefetch_call(
        body, name, (n, R // tr),
        [pl.BlockSpec((None, None, tr, C), lambda j, r, cr: (cr[0], j, r, 0)),
         pl.BlockSpec((3, None, tr, C), lambda j, r, cr: (0, j, r, 0))],
        pl.BlockSpec((None, tr, C), lambda j, r, cr: (j, r, 0)),
        _sds((n, R, C), F32), chip, [sums, landed])


def _adamw_halves(name, w, m, v, g_mine, g_theirs, core, tr=128):
    _, R, C = w.shape
    tr = _row_tile(R, tr)
    c1 = 1.0 - ADAM_B1 ** ADAM_STEP
    c2 = 1.0 - ADAM_B2 ** ADAM_STEP

    def body(c_ref, w_ref, m_ref, v_ref, gm_ref, gt_ref, g_ref, d_ref, nm_ref, nv_ref):
        gv = jnp.where(pl.program_id(0) == c_ref[0], gm_ref[...], gt_ref[...])
        nm = ADAM_B1 * m_ref[...] + (1.0 - ADAM_B1) * gv
        nv = ADAM_B2 * v_ref[...] + (1.0 - ADAM_B2) * (gv * gv)
        g_ref[...] = gv
        nm_ref[...] = nm
        nv_ref[...] = nv
        d_ref[...] = -ADAM_LR * ((nm / c1) / (jnp.sqrt(nv / c2) + ADAM_EPS) + ADAM_WD * w_ref[...])

    full = pl.BlockSpec((None, tr, C), lambda h, r, cr: (h, r, 0))
    half = pl.BlockSpec((tr, C), lambda h, r, cr: (r, 0))
    return _prefetch_call(
        body, name, (2, R // tr), [full, full, full, half, half], [full] * 4,
        [_sds(w.shape, F32)] * 4, core, [w, m, v, g_mine, g_theirs])


def _adamw(name, w, g, m, v, tr=128):
    R, C = w.shape
    tr = _tile(R, tr) if R % 8 == 0 else R
    c1 = 1.0 - ADAM_B1 ** ADAM_STEP
    c2 = 1.0 - ADAM_B2 ** ADAM_STEP

    def body(w_ref, g_ref, m_ref, v_ref, d_ref, nm_ref, nv_ref):
        gv = g_ref[...]
        nm = ADAM_B1 * m_ref[...] + (1.0 - ADAM_B1) * gv
        nv = ADAM_B2 * v_ref[...] + (1.0 - ADAM_B2) * (gv * gv)
        nm_ref[...] = nm
        nv_ref[...] = nv
        d_ref[...] = -ADAM_LR * ((nm / c1) / (jnp.sqrt(nv / c2) + ADAM_EPS) + ADAM_WD * w_ref[...])

    blk = pl.BlockSpec((tr, C), lambda i: (i, 0))
    return pl.pallas_call(
        body, name=name, grid=(R // tr,), in_specs=[blk] * 4, out_specs=[blk] * 3,
        out_shape=[_sds((R, C), F32)] * 3, compiler_params=_cparams(("arbitrary",), 32),
    )(w, g, m, v)


ANY = pl.BlockSpec(memory_space=pl.ANY)


def _place():
    x, y, c = lax.axis_index("x"), lax.axis_index("y"), lax.axis_index("c")
    chips = [(1 - x, y), (x, 1 - y), (1 - x, 1 - y)]
    return x, y, c, chips


def _gather_weights(shards, placed):
    n = len(shards)

    def body(*refs):
        src, dst = refs[:n], refs[2 * n:3 * n]
        send, recv = refs[3 * n:]
        x, y, c, chips = _place()
        me = 2 * x + y
        sibling = (x, y, 1 - c)

        def remote(i, k, s, d, to):
            return pltpu.make_async_remote_copy(src_ref=s, dst_ref=d, send_sem=send.at[6 * i + k],
                                                recv_sem=recv.at[6 * i + k], device_id=to, device_id_type=MESH)

        first = []
        for i in range(n):
            for j, chip in enumerate(chips):
                cp = remote(i, j, src[i].at[c], dst[i].at[me, c], (*chip, c))
                cp.start()
                first.append(cp)
        passed = []
        for i in range(n):
            for j, (px, py) in enumerate(chips):
                slot = dst[i].at[2 * px + py, c]
                remote(i, j, slot, slot, (px, py, c)).wait_recv()
                cp = remote(i, 3 + j, slot, slot, sibling)
                cp.start()
                passed.append(cp)
        for i in range(n):
            for j, (px, py) in enumerate(chips):
                slot = dst[i].at[2 * px + py, 1 - c]
                remote(i, 3 + j, slot, slot, sibling).wait_recv()
        for cp in first + passed:
            cp.wait_send()

    return pl.pallas_call(
        body, name="gather_weights", in_specs=[ANY] * (2 * n), out_specs=[ANY] * n,
        out_shape=[_sds(p.shape, p.dtype) for p in placed],
        input_output_aliases={n + i: i for i in range(n)},
        scratch_shapes=[pltpu.SemaphoreType.DMA((6 * n,)), pltpu.SemaphoreType.DMA((6 * n,))],
    )(*shards, *placed)


def _reduce_pair(grads):
    n = len(grads)

    def body(*refs):
        g, theirs = refs[:n], refs[n:2 * n]
        send, recv = refs[2 * n:]
        x, y, c, _ = _place()
        sibling = (x, y, 1 - c)
        cps = []
        for i in range(n):
            for k in range(4):
                cp = pltpu.make_async_remote_copy(
                    src_ref=g[i].at[k, 1 - c], dst_ref=theirs[i].at[k], send_sem=send.at[4 * i + k],
                    recv_sem=recv.at[4 * i + k], device_id=sibling, device_id_type=MESH)
                cp.start()
                cps.append(cp)
        for cp in cps:
            cp.wait()

    return pl.pallas_call(
        body, name="reduce_pair", in_specs=[ANY] * n, out_specs=[ANY] * n,
        out_shape=[_sds((4,) + a.shape[2:], a.dtype) for a in grads],
        scratch_shapes=[pltpu.SemaphoreType.DMA((4 * n,)), pltpu.SemaphoreType.DMA((4 * n,))],
    )(*grads)


def _reduce_chips(sums):
    n = len(sums)

    def body(*refs):
        src, dst = refs[:n], refs[n:2 * n]
        send, recv = refs[2 * n:]
        x, y, c, chips = _place()
        cps = []
        for i in range(n):
            for j, (px, py) in enumerate(chips):
                cp = pltpu.make_async_remote_copy(
                    src_ref=src[i].at[2 * px + py], dst_ref=dst[i].at[j], send_sem=send.at[3 * i + j],
                    recv_sem=recv.at[3 * i + j], device_id=(px, py, c), device_id_type=MESH)
                cp.start()
                cps.append(cp)
        for cp in cps:
            cp.wait()

    return pl.pallas_call(
        body, name="reduce_chips", in_specs=[ANY] * n, out_specs=[ANY] * n,
        out_shape=[_sds((3,) + a.shape[1:], a.dtype) for a in sums],
        scratch_shapes=[pltpu.SemaphoreType.DMA((3 * n,)), pltpu.SemaphoreType.DMA((3 * n,))],
    )(*sums)


def _share_halves(halves):
    n = len(halves)

    def body(*refs):
        src, dst = refs[:n], refs[n:2 * n]
        send, recv = refs[2 * n:]
        x, y, c, _ = _place()
        cps = []
        for i in range(n):
            cp = pltpu.make_async_remote_copy(
                src_ref=src[i], dst_ref=dst[i], send_sem=send.at[i], recv_sem=recv.at[i],
                device_id=(x, y, 1 - c), device_id_type=MESH)
            cp.start()
            cps.append(cp)
        for cp in cps:
            cp.wait()

    return pl.pallas_call(
        body, name="share_halves", in_specs=[ANY] * n, out_specs=[ANY] * n,
        out_shape=[_sds(a.shape, a.dtype) for a in halves],
        scratch_shapes=[pltpu.SemaphoreType.DMA((n,)), pltpu.SemaphoreType.DMA((n,))],
    )(*halves)


def _allreduce_small(v):
    R, C = v.shape

    def body(v_ref, out_ref, land, send, recv):
        x, y, c, _ = _place()
        me = 4 * x + 2 * y + c
        cps = []
        for m in range(1, 8):
            fx, fy, fc = (m >> 2) & 1, (m >> 1) & 1, m & 1
            peer = (x ^ fx, y ^ fy, c ^ fc)
            cp = pltpu.make_async_remote_copy(
                src_ref=v_ref, dst_ref=land.at[me], send_sem=send.at[m - 1], recv_sem=recv.at[m - 1],
                device_id=peer, device_id_type=MESH)
            cp.start()
            cps.append(cp)
        land[me] = v_ref[...]
        for cp in cps:
            cp.wait()
        s = land[0]
        for d in range(1, 8):
            s = s + land[d]
        out_ref[...] = s

    return pl.pallas_call(
        body, name="allreduce_small",
        in_specs=[pl.BlockSpec(memory_space=pltpu.VMEM)], out_specs=pl.BlockSpec(memory_space=pltpu.VMEM),
        out_shape=_sds((R, C), F32),
        scratch_shapes=[pltpu.VMEM((8, R, C), F32), pltpu.SemaphoreType.DMA((7,)), pltpu.SemaphoreType.DMA((7,))],
    )(v)


def _halves(a):
    return a.reshape((2, a.shape[0] // 2) + a.shape[1:])


def _canon(a, lead):
    piece = a.shape[lead:]
    return a.reshape(a.shape[:lead] + (int(math.prod(piece[:-2])),) + piece[-2:])


def _rope_tables(pos):
    half = ROPE // 2
    inv_freq = ROPE_BASE ** (-jnp.arange(half, dtype=F32) / half)
    ang = pos.astype(F32)[:, None] * inv_freq
    cos, sin = jnp.cos(ang), jnp.sin(ang)
    return jnp.tile(cos, (1, 4)), jnp.concatenate([-sin, sin, -sin, sin], axis=1)


def kernel(x, positions, pool_norm, pool_w, pool_scale, kv_in_norm, w_kv_a, kv_latent_norm, w_kv_b, attn_norm, w_q_a, q_latent_norm, w_q_b, w_o, ffn_norm, w_gate, w_up, w_down, final_norm, loss_target, m_pool_norm, m_pool_w, m_pool_scale, m_kv_in_norm, m_w_kv_a, m_kv_latent_norm, m_w_kv_b, m_attn_norm, m_w_q_a, m_q_latent_norm, m_w_q_b, m_w_o, m_ffn_norm, m_w_gate, m_w_up, m_w_down, m_final_norm, v_pool_norm, v_pool_w, v_pool_scale, v_kv_in_norm, v_w_kv_a, v_kv_latent_norm, v_w_kv_b, v_attn_norm, v_w_q_a, v_q_latent_norm, v_w_q_b, v_w_o, v_ffn_norm, v_w_gate, v_w_up, v_w_down, v_final_norm):
    T, D = x.shape[1], x.shape[2]
    H = N_HEADS
    KL = kv_latent_norm.shape[0]
    QL = q_latent_norm.shape[1]
    pg = D // 4
    x0, tgt = x[0], loss_target[0]
    cos, sin = _rope_tables(positions[0])
    chip = 2 * lax.axis_index("x") + lax.axis_index("y")

    n_kva = w_kv_a.shape[1]
    wkva_s = jnp.pad(w_kv_a, ((0, 0), (0, KL + 128 - n_kva))).astype(BF)
    hs = w_q_b.shape[2] // (NOPE + ROPE)
    wqb_s = jnp.pad(w_q_b[0].reshape(QL, hs, NOPE + ROPE), ((0, 0), (0, 0), (0, HEAD_PAD - NOPE - ROPE)))
    wqb_s = wqb_s.reshape(QL, hs * HEAD_PAD).astype(BF)
    shards = [
        _halves(pool_w[0].astype(BF)),
        _halves(wkva_s), _halves(w_kv_b.astype(BF)), _halves(w_q_a[0].astype(BF)), _halves(wqb_s),
        _halves(w_o[0].astype(BF)),
        w_gate.astype(BF), w_up.astype(BF), w_down.astype(BF),
        jnp.stack([pool_norm, pool_scale]),
    ]
    shards = [_canon(s, 1) for s in shards]
    chip_s = chip.astype(jnp.int32).reshape(1)
    core_s = lax.axis_index("c").astype(jnp.int32).reshape(1)
    placed = [_place_own(f"place_own{i}", s, chip_s) for i, s in enumerate(shards)]
    PW, WKVA, WKVB, WQA, WQB, WO, WG, WU, WD, PV = _gather_weights(shards, placed)
    WG, WU, WD = (a.reshape((4, 2) + a.shape[3:]) for a in (WG, WU, WD))
    WKVA = WKVA.reshape(D, KL + 128)
    WKVB = WKVB.reshape(4, KL, -1)
    WQA = WQA.reshape(D, QL)
    WQB = WQB.reshape(4, QL, -1)
    WO = WO.reshape(H * VDIM, D)
    pv = jnp.transpose(PV.reshape(4, 2, pg), (1, 0, 2)).reshape(2, D)
    pn_full, ps_full = pv[0:1], pv[1:2]

    diff = _pool_fwd(x0, pn_full)
    pre, x1 = _pool_mix(diff, PW, ps_full, x0)
    (h1,) = _norm_fwd("norm_ffn0", x1, ffn_norm[0:1])
    g0, u0, a0 = _ffn_up("ffn_up0", h1, WG, WU, 0)
    x2 = _ffn_down("ffn_down0", a0, WD, 0, x1)
    hk, ha = _norm_fwd("norm_attn", x2, jnp.stack([kv_in_norm, attn_norm[0]]))
    kvp = _mm_plain("kv_a", hk, WKVA, F32)
    ckv, kpe = _kv_post(kvp, kv_latent_norm[None], cos, sin)
    kvup = _mm_cols("kv_b", ckv, WKVB, BF)
    qa = _mm_plain("q_a", ha, WQA, F32)
    (ql,) = _norm_fwd("norm_q", qa, q_latent_norm)
    qp = _q_up(ql, WQB, cos, sin)
    o, lse = _flash_fwd(qp, kvup, kpe)
    x3 = _mm_plain("attn_out", o, WO, F32, res=x2)
    (h3,) = _norm_fwd("norm_ffn1", x3, ffn_norm[1:2])
    g1, u1, a1 = _ffn_up("ffn_up1", h3, WG, WU, 1)
    x4 = _ffn_down("ffn_down1", a1, WD, 1, x3)
    loss_part, dx4, d_final = _final_loss(x4, final_norm[None], tgt)

    dWD = _ffn_dw_rows("ffn_dwd1", a1, dx4, 1, None)
    dg1, du1 = _ffn_bwd_act("ffn_bwd_act1", dx4, WD, 1, g1, u1)
    dWG = _ffn_dw_cols("ffn_dwg1", h3, dg1, 1, None)
    dWU = _ffn_dw_cols("ffn_dwu1", h3, du1, 1, None)
    dh3 = _ffn_dh("ffn_dh1", dg1, WG, du1, WU, 1)
    dx3, d_ffn1 = _norm_bwd("norm_ffn1_bwd", x3, ffn_norm[1:2], [dh3], dx4)

    do = _mm_nt_plain("attn_out_dx", dx3, WO, BF)
    dWO = _mm_tn_plain("attn_out_dw", o, dx3, BF)
    dqp, delta = _flash_dq(qp, kvup, kpe, o, do, lse, cos, sin)
    tb = _tile(T, 512)
    lse_rows = lse.reshape(H, T // tb, 1, tb)
    delta_rows = delta.reshape(H, T // tb, 1, tb)
    dkvup, dkpe_h = _flash_dkv(qp, kvup, kpe, do, lse_rows, delta_rows)
    dql = _mm_nt_cols("q_b_dx", dqp, WQB, F32)
    dWQB = _mm_tn_cols("q_b_dw", ql, dqp, 4, BF)
    dqa, d_qln = _norm_bwd("norm_q_bwd", qa, q_latent_norm, [dql], None)
    dha = _mm_nt_plain("q_a_dx", dqa, WQA, F32)
    dWQA = _mm_tn_plain("q_a_dw", ha, dqa, BF)
    dckv = _mm_nt_cols("kv_b_dx", dkvup, WKVB, F32)
    dWKVB = _mm_tn_cols("kv_b_dw", ckv, dkvup, 4, BF)
    dkvp, d_kvln = _kv_post_bwd(kvp, kv_latent_norm[None], dckv, dkpe_h, cos, sin)
    dhk = _mm_nt_plain("kv_a_dx", dkvp, WKVA, F32)
    dWKVA = _mm_tn_plain("kv_a_dw", hk, dkvp, BF)
    dx2, d_n2 = _norm_bwd("norm_attn_bwd", x2, jnp.stack([kv_in_norm, attn_norm[0]]), [dhk, dha], dx3)

    dWD = _ffn_dw_rows("ffn_dwd0", a0, dx2, 0, dWD)
    dg0, du0 = _ffn_bwd_act("ffn_bwd_act0", dx2, WD, 0, g0, u0)
    dWG = _ffn_dw_cols("ffn_dwg0", h1, dg0, 0, dWG)
    dWU = _ffn_dw_cols("ffn_dwu0", h1, du0, 0, dWU)
    dh1 = _ffn_dh("ffn_dh0", dg0, WG, du0, WU, 0)
    dx1, d_ffn0 = _norm_bwd("norm_ffn0_bwd", x1, ffn_norm[0:1], [dh1], dx2)

    dpre, d_ps = _pool_dpre(dx1, pre, ps_full)
    ddiff = _pool_ddiff(dpre, PW)
    dPW = _pool_dw(diff, dpre)
    dx0, d_pn = _pool_bwd(x0, pn_full, ddiff, dx1)

    full = [dPW, dWKVA.reshape(4, 2, D // 8, KL + 128), dWKVB.reshape(4, 2, KL // 2, -1),
            dWQA.reshape(4, 2, D // 8, QL), dWQB.reshape(4, 2, QL // 2, -1),
            dWO.reshape(4, 2, H * VDIM // 8, D), dWG, dWU, dWD]
    full = [_canon(a, 2) for a in full]
    theirs = _reduce_pair(full)
    sums = [_pair_add(f"pair_add{i}", a, b, core_s) for i, (a, b) in enumerate(zip(full, theirs))]
    landed = _reduce_chips(sums)
    halves = [_chip_sum(f"chip_sum{i}", a, b, chip_s) for i, (a, b) in enumerate(zip(sums, landed))]
    other = _share_halves(halves)

    def rows(a):
        return a.reshape((-1, a.shape[-1]))

    def unpad_kva(a):
        return rows(a)[:, :n_kva]

    def unpad_qb(a):
        return rows(a).reshape(-1, hs, HEAD_PAD)[:, :, :NOPE + ROPE].reshape(-1, hs * (NOPE + ROPE))

    fix = [rows, unpad_kva, rows, rows, unpad_qb, rows, rows, rows, rows]
    big = ["pool_w", "w_kv_a", "w_kv_b", "w_q_a", "w_q_b", "w_o", "w_gate", "w_up", "w_down"]
    g_mine = {nm: f(a) for nm, f, a in zip(big, fix, halves)}
    g_other = {nm: f(a) for nm, f, a in zip(big, fix, other)}

    lat = jnp.concatenate([d_kvln[0], d_qln[0], jnp.zeros((D - KL - QL,), F32)])
    lrow = jnp.pad(loss_part[0], (0, D - 1))
    small = jnp.stack([d_n2[0], d_n2[1], d_ffn0[0], d_ffn1[0], d_final[0], lat, d_pn[0], d_ps[0], lrow]
                      + [jnp.zeros((D,), F32)] * 7)
    red = _allreduce_small(small)
    loss = red[8, 0]
    g_kv_in, g_attn, g_final = red[0], red[1:2], red[4]
    g_ffn = red[2:4]
    g_kvln, g_qln = red[5, :KL], red[5:6, KL:KL + QL]
    g_pn = lax.dynamic_slice(red[6:7], (0, chip * pg), (1, pg))
    g_ps = lax.dynamic_slice(red[7:8], (0, chip * pg), (1, pg))

    grads = {"pool_norm": g_pn, "pool_scale": g_ps, "kv_in_norm": g_kv_in, "kv_latent_norm": g_kvln,
             "attn_norm": g_attn, "q_latent_norm": g_qln, "ffn_norm": g_ffn, "final_norm": g_final}
    weights = dict(pool_norm=pool_norm, pool_w=pool_w, pool_scale=pool_scale, kv_in_norm=kv_in_norm, w_kv_a=w_kv_a,
                   kv_latent_norm=kv_latent_norm, w_kv_b=w_kv_b, attn_norm=attn_norm, w_q_a=w_q_a,
                   q_latent_norm=q_latent_norm, w_q_b=w_q_b, w_o=w_o, ffn_norm=ffn_norm, w_gate=w_gate, w_up=w_up,
                   w_down=w_down, final_norm=final_norm)
    ms = dict(pool_norm=m_pool_norm, pool_w=m_pool_w, pool_scale=m_pool_scale, kv_in_norm=m_kv_in_norm,
              w_kv_a=m_w_kv_a, kv_latent_norm=m_kv_latent_norm, w_kv_b=m_w_kv_b, attn_norm=m_attn_norm,
              w_q_a=m_w_q_a, q_latent_norm=m_q_latent_norm, w_q_b=m_w_q_b, w_o=m_w_o, ffn_norm=m_ffn_norm,
              w_gate=m_w_gate, w_up=m_w_up, w_down=m_w_down, final_norm=m_final_norm)
    vs = dict(pool_norm=v_pool_norm, pool_w=v_pool_w, pool_scale=v_pool_scale, kv_in_norm=v_kv_in_norm,
              w_kv_a=v_w_kv_a, kv_latent_norm=v_kv_latent_norm, w_kv_b=v_w_kv_b, attn_norm=v_attn_norm,
              w_q_a=v_w_q_a, q_latent_norm=v_q_latent_norm, w_q_b=v_w_q_b, w_o=v_w_o, ffn_norm=v_ffn_norm,
              w_gate=v_w_gate, w_up=v_w_up, w_down=v_w_down, final_norm=v_final_norm)
    names = list(weights)

    def as2d(a):
        return a.reshape((-1, a.shape[-1]))

    def two_halves(a):
        a = as2d(a)
        return a.reshape(2, a.shape[0] // 2, a.shape[1])

    delta_w, new_m, new_v = {}, {}, {}
    for nm in big:
        g, d, m2, v2 = _adamw_halves("adamw_" + nm, two_halves(weights[nm]), two_halves(ms[nm]),
                                     two_halves(vs[nm]), g_mine[nm], g_other[nm], core_s)
        shp = weights[nm].shape
        grads[nm], delta_w[nm], new_m[nm], new_v[nm] = g.reshape(shp), d.reshape(shp), m2.reshape(shp), v2.reshape(shp)
    groups = [["kv_in_norm", "attn_norm", "ffn_norm", "final_norm"], ["kv_latent_norm", "q_latent_norm"],
              ["pool_norm", "pool_scale"]]
    for gi, grp in enumerate(groups):
        cat = lambda t: jnp.concatenate([as2d(t[nm]) for nm in grp], axis=0)
        d, m2, v2 = _adamw(f"adamw_vec{gi}", cat(weights), cat(grads), cat(ms), cat(vs))
        r0 = 0
        for nm in grp:
            shp = weights[nm].shape
            r = as2d(weights[nm]).shape[0]
            delta_w[nm], new_m[nm], new_v[nm] = (d[r0:r0 + r].reshape(shp), m2[r0:r0 + r].reshape(shp),
                                                 v2[r0:r0 + r].reshape(shp))
            r0 += r

    return (loss, dx0[None], *[grads[nm].reshape(weights[nm].shape) for nm in names],
            *[delta_w[nm] for nm in names], *[new_m[nm] for nm in names], *[new_v[nm] for nm in names])
```

```python
import functools
import math

import jax
import jax.numpy as jnp
from jax import lax
from jax.experimental import pallas as pl
from jax.experimental.pallas import tpu as pltpu

BF = jnp.bfloat16
F32 = jnp.float32
MESH = pl.DeviceIdType.MESH

N_HEADS = 16
NOPE = 128
ROPE = 64
VDIM = 128
HEAD_PAD = 256
ROPE_BASE = 10000.0
ATTN_SCALE = 1.0 / math.sqrt(NOPE + ROPE)
POOL_WINDOWS = (2, 4, 8, 16)
HALO = 16
NORM_EPS = 1e-6
ADAM_LR, ADAM_B1, ADAM_B2, ADAM_EPS, ADAM_WD, ADAM_STEP = 0.001, 0.9, 0.999, 1e-08, 0.01, 10
NEG = -1e30

V7X_VMEM_BYTES = 64 * 1024 * 1024
VMEM_CEILING = V7X_VMEM_BYTES - 8 * 1024 * 1024

NN = (((1,), (0,)), ((), ()))
NT = (((1,), (1,)), ((), ()))
TN = (((0,), (0,)), ((), ()))


def _cparams(sem, vmem_mb):
    return pltpu.CompilerParams(dimension_semantics=sem,
                                vmem_limit_bytes=min(vmem_mb * 1024 * 1024, VMEM_CEILING))


def _tile(n, pref):
    if n <= pref:
        return n
    t = (pref // 128) * 128
    while t > 128 and n % t:
        t -= 128
    assert n % t == 0, (n, pref)
    return t


def _sds(shape, dtype):
    return jax.ShapeDtypeStruct(shape, dtype)


ANY = pl.BlockSpec(memory_space=pl.ANY)


class _Comm:
    def __init__(self, ins, out_shapes, alias, n_sems, start, finish):
        self.ins, self.out_shapes, self.alias, self.n_sems = list(ins), list(out_shapes), dict(alias), n_sems
        self.start, self.finish = start, finish


def _call(name, body, grid, in_specs, out_specs, out_shape, scratch, vmem_mb, operands, comm=None):
    in_specs, out_specs, out_shape, scratch = list(in_specs), list(out_specs), list(out_shape), list(scratch)
    params = _cparams(("arbitrary",) * len(grid), vmem_mb)
    if comm is None:
        res = pl.pallas_call(body, name=name, grid=grid, in_specs=in_specs, out_specs=out_specs,
                             out_shape=out_shape, scratch_shapes=scratch, compiler_params=params)(*operands)
        return list(res), []
    n_in, n_out, n_scr = len(in_specs), len(out_specs), len(scratch)
    c_in, c_out = len(comm.ins), len(comm.out_shapes)

    def hosted(*refs):
        ins, cin = refs[:n_in], refs[n_in:n_in + c_in]
        o0 = n_in + c_in
        outs, cout = refs[o0:o0 + n_out], refs[o0 + n_out:o0 + n_out + c_out]
        s0 = o0 + n_out + c_out
        scr, (send, recv) = refs[s0:s0 + n_scr], refs[s0 + n_scr:]
        ids = [pl.program_id(d) for d in range(len(grid))]
        first = functools.reduce(jnp.logical_and, [i == 0 for i in ids])
        last = functools.reduce(jnp.logical_and, [i == g - 1 for i, g in zip(ids, grid)])

        @pl.when(first)
        def _():
            comm.start(cin, cout, send, recv)

        body(*ins, *outs, *scr)

        @pl.when(last)
        def _():
            comm.finish(cin, cout, send, recv)

    res = pl.pallas_call(
        hosted, name=name, grid=grid, in_specs=in_specs + [ANY] * c_in, out_specs=out_specs + [ANY] * c_out,
        out_shape=out_shape + comm.out_shapes,
        scratch_shapes=scratch + [pltpu.SemaphoreType.DMA((comm.n_sems,)), pltpu.SemaphoreType.DMA((comm.n_sems,))],
        input_output_aliases={n_in + k: n_out + v for k, v in comm.alias.items()}, compiler_params=params,
    )(*operands, *comm.ins)
    return list(res[:n_out]), list(res[n_out:])


def _comm_call(name, comm):
    c_in, c_out = len(comm.ins), len(comm.out_shapes)

    def body(*refs):
        cin, cout = refs[:c_in], refs[c_in:c_in + c_out]
        send, recv = refs[c_in + c_out:]
        comm.start(cin, cout, send, recv)
        comm.finish(cin, cout, send, recv)

    return pl.pallas_call(
        body, name=name, in_specs=[ANY] * c_in, out_specs=[ANY] * c_out, out_shape=comm.out_shapes,
        scratch_shapes=[pltpu.SemaphoreType.DMA((comm.n_sems,)), pltpu.SemaphoreType.DMA((comm.n_sems,))],
        input_output_aliases=dict(comm.alias),
    )(*comm.ins)


def _matmul(name, pairs, pair_specs, dims, grid, nk, extra, extra_specs, out_shapes, out_specs,
            acc_shape, epilogue, vmem_mb, comm=None):
    n_p, n_e, n_o = len(pairs) // 2, len(extra), len(out_shapes)

    def body(*refs):
        ab = refs[:2 * n_p]
        ex = refs[2 * n_p:2 * n_p + n_e]
        outs = refs[2 * n_p + n_e:2 * n_p + n_e + n_o]

        def partial_sum():
            tot = None
            for p in range(n_p):
                a = ab[2 * p][...]
                b = ab[2 * p + 1][...]
                if a.ndim > 2:
                    a = a.reshape(-1, a.shape[-1])
                if b.ndim > 2:
                    b = b.reshape(-1, b.shape[-1])
                d = lax.dot_general(a.astype(BF), b.astype(BF), dims, preferred_element_type=F32)
                tot = d if tot is None else tot + d
            return tot

        if nk == 1:
            epilogue(partial_sum(), ex, outs)
        else:
            acc = refs[-1]
            kk = pl.program_id(2)

            @pl.when(kk == 0)
            def _():
                acc[...] = partial_sum()

            @pl.when(kk > 0)
            def _():
                acc[...] += partial_sum()

            @pl.when(kk == nk - 1)
            def _():
                epilogue(acc[...], ex, outs)

    scratch = [] if nk == 1 else [pltpu.VMEM(acc_shape, F32)]
    res, comm_res = _call(name, body, grid, list(pair_specs) + list(extra_specs), out_specs, out_shapes,
                          scratch, vmem_mb, list(pairs) + list(extra), comm)
    return res if comm is None else (res, comm_res)


def _epi_store(acc, ex, outs):
    outs[0][...] = acc.reshape(outs[0].shape).astype(outs[0].dtype)


def _epi_residual(acc, ex, outs):
    outs[0][...] = (acc + ex[0][...]).astype(outs[0].dtype)


def _swap_halves(x):
    lane = lax.broadcasted_iota(jnp.int32, x.shape, 1)
    return jnp.where((lane % 64) < 32, pltpu.roll(x, 96, 1), pltpu.roll(x, 32, 1))


def _mm_plain(name, a, b, out_dtype, res=None, tm=512, tn=1024):
    M, K = a.shape
    N = b.shape[1]
    tm, tn = _tile(M, tm), _tile(N, tn)
    extra, extra_specs, epi = [], [], _epi_store
    if res is not None:
        extra, extra_specs, epi = [res], [pl.BlockSpec((tm, tn), lambda i, j, k: (i, j))], _epi_residual
    return _matmul(
        name, [a, b],
        [pl.BlockSpec((tm, K), lambda i, j, k: (i, 0)), pl.BlockSpec((K, tn), lambda i, j, k: (0, j))],
        NN, (M // tm, N // tn, 1), 1, extra, extra_specs,
        [_sds((M, N), out_dtype)], [pl.BlockSpec((tm, tn), lambda i, j, k: (i, j))],
        None, epi, 40)[0]


def _mm_cols(name, a, b3, out_dtype, epilogue=_epi_store, extra=(), extra_specs=(), tm=512):
    M, K = a.shape
    G, _, n = b3.shape
    tm = _tile(M, tm)
    return _matmul(
        name, [a, b3],
        [pl.BlockSpec((tm, K), lambda i, j, k: (i, 0)), pl.BlockSpec((None, K, n), lambda i, j, k: (j, 0, 0))],
        NN, (M // tm, G, 1), 1, list(extra), list(extra_specs),
        [_sds((M, G * n), out_dtype)], [pl.BlockSpec((tm, n), lambda i, j, k: (i, j))],
        None, epilogue, 40)[0]


def _mm_nt_plain(name, a, b, out_dtype, tm=512, tn=1024):
    M, K = a.shape
    N = b.shape[0]
    tm, tn = _tile(M, tm), _tile(N, tn)
    return _matmul(
        name, [a, b],
        [pl.BlockSpec((tm, K), lambda i, j, k: (i, 0)), pl.BlockSpec((tn, K), lambda i, j, k: (j, 0))],
        NT, (M // tm, N // tn, 1), 1, [], [],
        [_sds((M, N), out_dtype)], [pl.BlockSpec((tm, tn), lambda i, j, k: (i, j))],
        None, _epi_store, 44)[0]


def _mm_nt_cols(name, a, b3, out_dtype, tm=512):
    M = a.shape[0]
    G, K, n = b3.shape
    tm = _tile(M, tm)
    return _matmul(
        name, [a, b3],
        [pl.BlockSpec((tm, n), lambda i, j, k: (i, k)), pl.BlockSpec((None, K, n), lambda i, j, k: (k, 0, 0))],
        NT, (M // tm, 1, G), G, [], [],
        [_sds((M, K), out_dtype)], [pl.BlockSpec((tm, K), lambda i, j, k: (i, 0))],
        (tm, K), _epi_store, 40)[0]


def _mm_tn_plain(name, a, b, out_dtype, tt=512, tn=1024):
    T, K = a.shape
    N = b.shape[1]
    tt, tn = _tile(T, tt), _tile(N, tn)
    return _matmul(
        name, [a, b],
        [pl.BlockSpec((tt, K), lambda i, j, k: (k, 0)), pl.BlockSpec((tt, tn), lambda i, j, k: (k, j))],
        TN, (1, N // tn, T // tt), T // tt, [], [],
        [_sds((K, N), out_dtype)], [pl.BlockSpec((K, tn), lambda i, j, k: (0, j))],
        (K, tn), _epi_store, 48)[0]


def _mm_tn_cols(name, a, b, G, out_dtype, tt=1024):
    T, K = a.shape
    n = b.shape[1] // G
    tt = _tile(T, tt)
    return _matmul(
        name, [a, b],
        [pl.BlockSpec((tt, K), lambda i, j, k: (k, 0)), pl.BlockSpec((tt, n), lambda i, j, k: (k, j))],
        TN, (1, G, T // tt), T // tt, [], [],
        [_sds((G, K, n), out_dtype)], [pl.BlockSpec((None, K, n), lambda i, j, k: (j, 0, 0))],
        (K, n), _epi_store, 48)[0]


def _norm_fwd(name, x, gains, tb=512):
    T, D = x.shape
    G = gains.shape[0]
    tb = _tile(T, tb)

    def body(x_ref, g_ref, *outs):
        xv = x_ref[...]
        xh = xv * lax.rsqrt(jnp.mean(xv * xv, axis=-1, keepdims=True) + NORM_EPS)
        for g in range(G):
            outs[g][...] = (xh * g_ref[g:g + 1, :]).astype(BF)

    return pl.pallas_call(
        body, name=name, grid=(T // tb,),
        in_specs=[pl.BlockSpec((tb, D), lambda i: (i, 0)), pl.BlockSpec((G, D), lambda i: (0, 0))],
        out_specs=[pl.BlockSpec((tb, D), lambda i: (i, 0))] * G,
        out_shape=[_sds((T, D), BF)] * G,
        compiler_params=_cparams(("arbitrary",), 40),
    )(x, gains)


def _norm_bwd(name, x, gains, dhs, dres, tb=256):
    T, D = x.shape
    G = gains.shape[0]
    tb = _tile(T, tb)
    has_res = dres is not None

    def body(*refs):
        x_ref, g_ref = refs[0], refs[1]
        dh_refs = refs[2:2 + G]
        res_ref = refs[2 + G] if has_res else None
        dx_ref, dg_ref = refs[-2], refs[-1]
        i = pl.program_id(0)
        xv = x_ref[...]
        r = lax.rsqrt(jnp.mean(xv * xv, axis=-1, keepdims=True) + NORM_EPS)
        xh = xv * r
        dx = res_ref[...] if has_res else jnp.zeros_like(xv)
        rows = []
        for g in range(G):
            dh = dh_refs[g][...].astype(F32)
            dy = dh * g_ref[g:g + 1, :]
            dx = dx + r * (dy - xh * jnp.mean(dy * xh, axis=-1, keepdims=True))
            rows.append(jnp.sum(dh * xh, axis=0, keepdims=True))
        dx_ref[...] = dx

        @pl.when(i == 0)
        def _():
            for g in range(G):
                dg_ref[g:g + 1, :] = rows[g]

        @pl.when(i > 0)
        def _():
            for g in range(G):
                dg_ref[g:g + 1, :] += rows[g]

    blk = pl.BlockSpec((tb, D), lambda i: (i, 0))
    ins = [x, gains] + list(dhs) + ([dres] if has_res else [])
    in_specs = [blk, pl.BlockSpec((G, D), lambda i: (0, 0))] + [blk] * (G + (1 if has_res else 0))
    return pl.pallas_call(
        body, name=name, grid=(T // tb,), in_specs=in_specs,
        out_specs=[blk, pl.BlockSpec((G, D), lambda i: (0, 0))],
        out_shape=[_sds((T, D), F32), _sds((G, D), F32)],
        compiler_params=_cparams(("arbitrary",), 48),
    )(*ins)


def _pool_fwd(x, gain, tb=256):
    T, D = x.shape
    tb = _tile(T, tb)
    pg = D // len(POOL_WINDOWS)
    per = tb // HALO

    def body(x_ref, xp_ref, g_ref, diff_ref):
        i = pl.program_id(0)
        xx = jnp.concatenate([xp_ref[...], x_ref[...]], axis=0)
        h = xx * lax.rsqrt(jnp.mean(xx * xx, axis=-1, keepdims=True) + NORM_EPS) * g_ref[...]
        row = lax.broadcasted_iota(jnp.int32, (HALO + tb, 1), 0)
        h = jnp.where((row >= HALO) | (i > 0), h, 0.0)
        t = i * tb + row[HALO:] - HALO
        for g, w in enumerate(POOL_WINDOWS):
            hg = h[:, g * pg:(g + 1) * pg]
            s, k = hg, 1
            while k < w:
                s = s + pltpu.roll(s, k, 0)
                k *= 2
            cnt = jnp.minimum(t + 1, w).astype(F32)
            diff_ref[:, g * pg:(g + 1) * pg] = (s[HALO:] / cnt - hg[HALO:]).astype(BF)

    return pl.pallas_call(
        body, name="pool_fwd", grid=(T // tb,),
        in_specs=[pl.BlockSpec((tb, D), lambda i: (i, 0)),
                  pl.BlockSpec((HALO, D), lambda i: (jnp.maximum(i * per - 1, 0), 0)),
                  pl.BlockSpec((1, D), lambda i: (0, 0))],
        out_specs=pl.BlockSpec((tb, D), lambda i: (i, 0)),
        out_shape=_sds((T, D), BF),
        compiler_params=_cparams(("arbitrary",), 40),
    )(x, x, gain)


def _pool_bwd(x, gain, ddiff, dres, tb=256):
    T, D = x.shape
    tb = _tile(T, tb)
    pg = D // len(POOL_WINDOWS)
    per = tb // HALO
    nblk = T // HALO

    def body(x_ref, g_ref, dd_ref, ddn_ref, res_ref, dx_ref, dg_ref):
        i = pl.program_id(0)
        last = i == T // tb - 1
        dd = jnp.concatenate([dd_ref[...], ddn_ref[...]], axis=0)
        row = lax.broadcasted_iota(jnp.int32, (tb + HALO, 1), 0)
        dd = jnp.where((row < tb) | jnp.logical_not(last), dd, 0.0)
        t = i * tb + row
        parts = []
        for g, w in enumerate(POOL_WINDOWS):
            dg_ = dd[:, g * pg:(g + 1) * pg]
            e = dg_ / jnp.minimum(t + 1, w).astype(F32)
            s, k = e, 1
            while k < w:
                s = s + pltpu.roll(s, tb + HALO - k, 0)
                k *= 2
            parts.append(s[:tb] - dg_[:tb])
        dh = jnp.concatenate(parts, axis=1)
        xv = x_ref[...]
        r = lax.rsqrt(jnp.mean(xv * xv, axis=-1, keepdims=True) + NORM_EPS)
        xh = xv * r
        dy = dh * g_ref[...]
        dx_ref[...] = res_ref[...] + r * (dy - xh * jnp.mean(dy * xh, axis=-1, keepdims=True))
        part = jnp.sum(dh * xh, axis=0, keepdims=True)

        @pl.when(i == 0)
        def _():
            dg_ref[...] = part

        @pl.when(i > 0)
        def _():
            dg_ref[...] += part

    blk = pl.BlockSpec((tb, D), lambda i: (i, 0))
    return pl.pallas_call(
        body, name="pool_bwd", grid=(T // tb,),
        in_specs=[blk, pl.BlockSpec((1, D), lambda i: (0, 0)), blk,
                  pl.BlockSpec((HALO, D), lambda i: (jnp.minimum((i + 1) * per, nblk - 1), 0)), blk],
        out_specs=[blk, pl.BlockSpec((1, D), lambda i: (0, 0))],
        out_shape=[_sds((T, D), F32), _sds((1, D), F32)],
        compiler_params=_cparams(("arbitrary",), 48),
    )(x, gain, ddiff, ddiff, dres)


def _pool_w_spec(pg):
    return pl.BlockSpec((4, None, None, pg // 4, pg), lambda i, j, k: (0, j // 2, j % 2, 0, 0))


def _pool_mix(diff, pw, scale, x, tm=512):
    T, D = x.shape
    pg = D // 4
    tm = _tile(T, tm)

    def epi(acc, ex, outs):
        outs[0][...] = acc
        outs[1][...] = ex[1][...] + acc * ex[0][...]

    blk = pl.BlockSpec((tm, pg), lambda i, j, k: (i, j))
    return _matmul(
        "pool_mix", [diff, pw], [blk, _pool_w_spec(pg)], NN, (T // tm, 4, 1), 1,
        [scale, x], [pl.BlockSpec((1, pg), lambda i, j, k: (0, j)), blk],
        [_sds((T, D), F32), _sds((T, D), F32)], [blk, blk], None, epi, 32)


def _pool_dpre(dx, pre, scale, tb=512):
    T, D = dx.shape
    tb = _tile(T, tb)

    def body(dx_ref, pre_ref, s_ref, dpre_ref, ds_ref):
        i = pl.program_id(0)
        d = dx_ref[...]
        dpre_ref[...] = (d * s_ref[...]).astype(BF)
        part = jnp.sum(d * pre_ref[...], axis=0, keepdims=True)

        @pl.when(i == 0)
        def _():
            ds_ref[...] = part

        @pl.when(i > 0)
        def _():
            ds_ref[...] += part

    blk = pl.BlockSpec((tb, D), lambda i: (i, 0))
    vec = pl.BlockSpec((1, D), lambda i: (0, 0))
    return pl.pallas_call(
        body, name="pool_dpre", grid=(T // tb,), in_specs=[blk, blk, vec], out_specs=[blk, vec],
        out_shape=[_sds((T, D), BF), _sds((1, D), F32)],
        compiler_params=_cparams(("arbitrary",), 40),
    )(dx, pre, scale)


def _pool_ddiff(dpre, pw, tm=512):
    T, D = dpre.shape
    pg = D // 4
    tm = _tile(T, tm)
    blk = pl.BlockSpec((tm, pg), lambda i, j, k: (i, j))
    return _matmul("pool_ddiff", [dpre, pw], [blk, _pool_w_spec(pg)], NT, (T // tm, 4, 1), 1, [], [],
                   [_sds((T, D), F32)], [blk], None, _epi_store, 32)[0]


def _pool_dw(diff, dpre, tt=512):
    T, D = diff.shape
    pg = D // 4
    tt = _tile(T, tt)
    blk = pl.BlockSpec((tt, pg), lambda i, j, k: (k, j))
    return _matmul("pool_dw", [diff, dpre], [blk, blk], TN, (1, 4, T // tt), T // tt, [], [],
                   [_sds((4, 2, 2, pg // 4, pg), BF)], [_pool_w_spec(pg)], (pg, pg), _epi_store, 32)[0]


def _ffn_up(name, h, wg, wu, tm=512, comm=None):
    T, D = h.shape
    n = wg.shape[-1]
    F = 4 * n
    tm = _tile(T, tm)

    def body(h_ref, wg_ref, wu_ref, g_ref, u_ref, a_ref):
        hv = h_ref[...]
        g = jnp.dot(hv, wg_ref[...], preferred_element_type=F32)
        u = jnp.dot(hv, wu_ref[...], preferred_element_type=F32)
        g_ref[...] = g.astype(BF)
        u_ref[...] = u.astype(BF)
        a_ref[...] = (g * jax.nn.sigmoid(g) * u).astype(BF)

    w_spec = pl.BlockSpec((None, D, n), lambda j, i: (j, 0, 0))
    o_spec = pl.BlockSpec((tm, n), lambda j, i: (i, j))
    return _call(name, body, (4, T // tm), [pl.BlockSpec((tm, D), lambda j, i: (i, 0)), w_spec, w_spec],
                 [o_spec] * 3, [_sds((T, F), BF)] * 3, [], 56, [h, wg, wu], comm)


def _ffn_down(name, a, wd, x, tm=512, tn=1024):
    T, D = x.shape
    n = wd.shape[1]
    tm, tn = _tile(T, tm), _tile(D, tn)
    return _matmul(
        name, [a, wd],
        [pl.BlockSpec((tm, n), lambda i, j, k: (i, k)),
         pl.BlockSpec((None, n, tn), lambda i, j, k: (k, 0, j))],
        NN, (T // tm, D // tn, 4), 4, [x], [pl.BlockSpec((tm, tn), lambda i, j, k: (i, j))],
        [_sds((T, D), F32)], [pl.BlockSpec((tm, tn), lambda i, j, k: (i, j))],
        (tm, tn), _epi_residual, 40)[0]


def _ffn_bwd_act(name, dx, wd, g, u, tm=512):
    T, D = dx.shape
    n = wd.shape[1]
    tm = _tile(T, tm)

    def epi(acc, ex, outs):
        gv = ex[0][...].astype(F32)
        uv = ex[1][...].astype(F32)
        sig = jax.nn.sigmoid(gv)
        outs[0][...] = (acc * uv * (sig * (1.0 + gv * (1.0 - sig)))).astype(BF)
        outs[1][...] = (acc * (gv * sig)).astype(BF)

    blk = pl.BlockSpec((tm, n), lambda i, j, k: (i, j))
    return _matmul(
        name, [dx, wd],
        [pl.BlockSpec((tm, D), lambda i, j, k: (i, 0)), pl.BlockSpec((None, n, D), lambda i, j, k: (j, 0, 0))],
        NT, (T // tm, 4, 1), 1, [g, u], [blk, blk],
        [_sds((T, 4 * n), BF)] * 2, [blk, blk], None, epi, 56)


def _ffn_dh(name, dg, wg, du, wu, tm=512, comm=None):
    T = dg.shape[0]
    D, n = wg.shape[1], wg.shape[2]
    tm = _tile(T, tm)
    a_spec = pl.BlockSpec((tm, n), lambda i, j, k: (i, k))
    w_spec = pl.BlockSpec((None, D, n), lambda i, j, k: (k, 0, 0))
    r = _matmul(
        name, [dg, wg, du, wu], [a_spec, w_spec, a_spec, w_spec], NT, (T // tm, 1, 4), 4, [], [],
        [_sds((T, D), F32)], [pl.BlockSpec((tm, D), lambda i, j, k: (i, 0))], (tm, D), _epi_store, 56, comm)
    return (r[0], []) if comm is None else (r[0][0], r[1])


def _mm_tn_rows(name, a, b, G, out_dtype, tt=1024, tn=1024):
    T, N = b.shape
    n = a.shape[1] // G
    tt, tn = _tile(T, tt), _tile(N, tn)
    nj = N // tn
    return _matmul(
        name, [a, b],
        [pl.BlockSpec((tt, n), lambda i, j, k: (k, j // nj)), pl.BlockSpec((tt, tn), lambda i, j, k: (k, j % nj))],
        TN, (1, G * nj, T // tt), T // tt, [], [],
        [_sds((G, n, N), out_dtype)], [pl.BlockSpec((None, n, tn), lambda i, j, k: (j // nj, 0, j % nj))],
        (n, tn), _epi_store, 48)[0]


def _kv_post(kvp, gain, cos, sin, tb=512):
    T, W = kvp.shape
    KL = W - 128
    tb = _tile(T, tb)

    def body(kv_ref, g_ref, c_ref, s_ref, ckv_ref, kpe_ref):
        lat = kv_ref[:, :KL]
        ckv_ref[...] = (lat * lax.rsqrt(jnp.mean(lat * lat, axis=-1, keepdims=True) + NORM_EPS)
                        * g_ref[...]).astype(BF)
        pe = kv_ref[:, KL:]
        kpe_ref[...] = (pe * c_ref[...] + _swap_halves(pe) * s_ref[...]).astype(BF)

    tab = pl.BlockSpec((tb, 128), lambda i: (i, 0))
    return pl.pallas_call(
        body, name="kv_post", grid=(T // tb,),
        in_specs=[pl.BlockSpec((tb, W), lambda i: (i, 0)), pl.BlockSpec((1, KL), lambda i: (0, 0)), tab, tab],
        out_specs=[pl.BlockSpec((tb, KL), lambda i: (i, 0)), tab],
        out_shape=[_sds((T, KL), BF), _sds((T, 128), BF)],
        compiler_params=_cparams(("arbitrary",), 32),
    )(kvp, gain, cos, sin)


def _kv_post_bwd(kvp, gain, dckv, dkpe_heads, cos, sin, tb=256):
    T, W = kvp.shape
    KL = W - 128
    H = dkpe_heads.shape[0]
    tb = _tile(T, tb)

    def body(kv_ref, g_ref, dc_ref, dk_ref, c_ref, s_ref, out_ref, dg_ref):
        i = pl.program_id(0)
        lat = kv_ref[:, :KL]
        r = lax.rsqrt(jnp.mean(lat * lat, axis=-1, keepdims=True) + NORM_EPS)
        xh = lat * r
        dh = dc_ref[...]
        dy = dh * g_ref[...]
        out_ref[:, :KL] = (r * (dy - xh * jnp.mean(dy * xh, axis=-1, keepdims=True))).astype(BF)
        d = dk_ref[0]
        for h in range(1, H):
            d = d + dk_ref[h]
        out_ref[:, KL:] = (d * c_ref[...] - _swap_halves(d) * s_ref[...]).astype(BF)
        part = jnp.sum(dh * xh, axis=0, keepdims=True)

        @pl.when(i == 0)
        def _():
            dg_ref[...] = part

        @pl.when(i > 0)
        def _():
            dg_ref[...] += part

    tab = pl.BlockSpec((tb, 128), lambda i: (i, 0))
    vec = pl.BlockSpec((1, KL), lambda i: (0, 0))
    return pl.pallas_call(
        body, name="kv_post_bwd", grid=(T // tb,),
        in_specs=[pl.BlockSpec((tb, W), lambda i: (i, 0)), vec, pl.BlockSpec((tb, KL), lambda i: (i, 0)),
                  pl.BlockSpec((H, tb, 128), lambda i: (0, i, 0)), tab, tab],
        out_specs=[pl.BlockSpec((tb, W), lambda i: (i, 0)), vec],
        out_shape=[_sds((T, W), BF), _sds((1, KL), F32)],
        compiler_params=_cparams(("arbitrary",), 32),
    )(kvp, gain, dckv, dkpe_heads, cos, sin)


def _q_up(ql, wqb, cos, sin, tm=512):
    n = wqb.shape[2]
    tm = _tile(ql.shape[0], tm)

    def epi(acc, ex, outs):
        c, s = ex[0][...] * LOG2_SCALE, ex[1][...] * LOG2_SCALE
        for j in range(n // HEAD_PAD):
            a0 = j * HEAD_PAD
            outs[0][:, a0:a0 + NOPE] = (acc[:, a0:a0 + NOPE] * LOG2_SCALE).astype(BF)
            pe = acc[:, a0 + NOPE:a0 + HEAD_PAD]
            outs[0][:, a0 + NOPE:a0 + HEAD_PAD] = (pe * c + _swap_halves(pe) * s).astype(BF)

    tab = pl.BlockSpec((tm, 128), lambda i, j, k: (i, 0))
    return _mm_cols("q_up", ql, wqb, BF, epilogue=epi, extra=[cos, sin], extra_specs=[tab, tab], tm=tm)


def _causal_mask(tb):
    r = lax.broadcasted_iota(jnp.int32, (tb, tb), 0)
    c = lax.broadcasted_iota(jnp.int32, (tb, tb), 1)
    return r, c


HP = 2
LOG2_SCALE = ATTN_SCALE * math.log2(math.e)


def _fill_keys(k_scr, kv_ref, kpe_ref):
    for hh in range(HP):
        k_scr[hh, :, :NOPE] = kv_ref[:, hh * HEAD_PAD:hh * HEAD_PAD + NOPE]
        k_scr[hh, :, NOPE:] = kpe_ref[...]


def _flash_fwd(qp, kvup, kpe, tb=512, comm=None):
    T = qp.shape[0]
    H = qp.shape[1] // HEAD_PAD
    tb = _tile(T, tb)

    def body(q_ref, kv_ref, kpe_ref, o_ref, lse_ref, k_scr):
        iq = pl.program_id(1)

        @pl.when(iq == 0)
        def _():
            _fill_keys(k_scr, kv_ref, kpe_ref)

        qs = [q_ref[:, hh * HEAD_PAD:(hh + 1) * HEAD_PAD] for hh in range(HP)]

        def blk(ik, carry, masked):
            off = pl.multiple_of(ik * tb, tb)
            ss = []
            for hh in range(HP):
                k = k_scr[hh, pl.ds(off, tb), :]
                s = lax.dot_general(qs[hh], k, NT, preferred_element_type=F32)
                if masked:
                    r, c = _causal_mask(tb)
                    s = jnp.where(c <= r, s, NEG)
                ss.append(s)
            ps = []
            for hh in range(HP):
                m, l, acc = carry[hh]
                m2 = jnp.maximum(m, jnp.max(ss[hh], axis=-1, keepdims=True))
                p = jnp.exp2(ss[hh] - m2)
                a = jnp.exp2(m - m2)
                ps.append((m2, a, a * l + jnp.sum(p, axis=-1, keepdims=True), p.astype(BF)))
            out = []
            for hh in range(HP):
                m2, a, l2, p = ps[hh]
                v = kv_ref[pl.ds(off, tb), hh * HEAD_PAD + NOPE:(hh + 1) * HEAD_PAD]
                out.append((m2, l2, a * carry[hh][2] + lax.dot_general(p, v, NN, preferred_element_type=F32)))
            return tuple(out)

        one = (jnp.full((tb, 1), NEG, F32), jnp.zeros((tb, 1), F32), jnp.zeros((tb, VDIM), F32))
        carry = lax.fori_loop(0, iq, lambda ik, cr: blk(ik, cr, False), (one,) * HP)
        carry = blk(iq, carry, True)
        for hh in range(HP):
            m, l, acc = carry[hh]
            o_ref[:, hh * VDIM:(hh + 1) * VDIM] = (acc / l).astype(BF)
            lse_ref[hh] = m + jnp.log2(l)

    return _call(
        "flash_fwd", body, (H // HP, T // tb),
        [pl.BlockSpec((tb, HP * HEAD_PAD), lambda h, i: (i, h)),
         pl.BlockSpec((T, HP * HEAD_PAD), lambda h, i: (0, h)),
         pl.BlockSpec((T, 128), lambda h, i: (0, 0))],
        [pl.BlockSpec((tb, HP * VDIM), lambda h, i: (i, h)),
         pl.BlockSpec((HP, tb, 1), lambda h, i: (h, i, 0))],
        [_sds((T, H * VDIM), BF), _sds((H, T, 1), F32)],
        [pltpu.VMEM((HP, T, HEAD_PAD), BF)], 56, [qp, kvup, kpe], comm)


def _flash_dq(qp, kvup, kpe, o, do, lse, cos, sin, tb=512, comm=None):
    T = qp.shape[0]
    H = qp.shape[1] // HEAD_PAD
    tb = _tile(T, tb)

    def body(q_ref, kv_ref, kpe_ref, o_ref, do_ref, lse_ref, c_ref, s_ref, dq_ref, dl_ref, k_scr):
        iq = pl.program_id(1)

        @pl.when(iq == 0)
        def _():
            _fill_keys(k_scr, kv_ref, kpe_ref)

        qs, dos, lses, dls = [], [], [], []
        for hh in range(HP):
            qs.append(q_ref[:, hh * HEAD_PAD:(hh + 1) * HEAD_PAD])
            dov = do_ref[:, hh * VDIM:(hh + 1) * VDIM]
            dos.append(dov)
            lses.append(lse_ref[hh])
            dl = jnp.sum(o_ref[:, hh * VDIM:(hh + 1) * VDIM].astype(F32) * dov.astype(F32), axis=-1, keepdims=True)
            dl_ref[hh] = dl
            dls.append(dl)

        def blk(ik, dqs, masked):
            off = pl.multiple_of(ik * tb, tb)
            ks, ss, dps = [], [], []
            for hh in range(HP):
                k = k_scr[hh, pl.ds(off, tb), :]
                v = kv_ref[pl.ds(off, tb), hh * HEAD_PAD + NOPE:(hh + 1) * HEAD_PAD]
                s = lax.dot_general(qs[hh], k, NT, preferred_element_type=F32)
                if masked:
                    r, c = _causal_mask(tb)
                    s = jnp.where(c <= r, s, NEG)
                ks.append(k)
                ss.append(s)
                dps.append(lax.dot_general(dos[hh], v, NT, preferred_element_type=F32))
            dss = [(jnp.exp2(ss[hh] - lses[hh]) * (dps[hh] - dls[hh])).astype(BF) for hh in range(HP)]
            return tuple(dqs[hh] + lax.dot_general(dss[hh], ks[hh], NN, preferred_element_type=F32)
                         for hh in range(HP))

        dqs = lax.fori_loop(0, iq, lambda ik, cr: blk(ik, cr, False), (jnp.zeros((tb, HEAD_PAD), F32),) * HP)
        dqs = blk(iq, dqs, True)
        for hh in range(HP):
            dq = dqs[hh] * ATTN_SCALE
            a0 = hh * HEAD_PAD
            dq_ref[:, a0:a0 + NOPE] = dq[:, :NOPE].astype(BF)
            dpe = dq[:, NOPE:]
            dq_ref[:, a0 + NOPE:a0 + HEAD_PAD] = (dpe * c_ref[...] - _swap_halves(dpe) * s_ref[...]).astype(BF)

    tab = pl.BlockSpec((tb, 128), lambda h, i: (i, 0))
    col = pl.BlockSpec((HP, tb, 1), lambda h, i: (h, i, 0))
    ov = pl.BlockSpec((tb, HP * VDIM), lambda h, i: (i, h))
    return _call(
        "flash_dq", body, (H // HP, T // tb),
        [pl.BlockSpec((tb, HP * HEAD_PAD), lambda h, i: (i, h)),
         pl.BlockSpec((T, HP * HEAD_PAD), lambda h, i: (0, h)),
         pl.BlockSpec((T, 128), lambda h, i: (0, 0)), ov, ov, col, tab, tab],
        [pl.BlockSpec((tb, HP * HEAD_PAD), lambda h, i: (i, h)), col],
        [_sds((T, H * HEAD_PAD), BF), _sds((H, T, 1), F32)],
        [pltpu.VMEM((HP, T, HEAD_PAD), BF)], 56, [qp, kvup, kpe, o, do, lse, cos, sin], comm)


def _flash_dkv(qp, kvup, kpe, do, lse_rows, delta_rows, tb=512):
    T = qp.shape[0]
    H = qp.shape[1] // HEAD_PAD
    tb = _tile(T, tb)
    nq = T // tb

    def body(kv_ref, kpe_ref, q_ref, do_ref, lse_ref, dl_ref, dkv_ref, dkpe_ref):
        ik = pl.program_id(1)
        ks = [jnp.concatenate([kv_ref[:, hh * HEAD_PAD:hh * HEAD_PAD + NOPE], kpe_ref[...]], axis=1)
              for hh in range(HP)]
        vs = [kv_ref[:, hh * HEAD_PAD + NOPE:(hh + 1) * HEAD_PAD] for hh in range(HP)]

        def blk(iq, carry, masked):
            off = pl.multiple_of(iq * tb, tb)
            qv, dov, sts, dpts = [], [], [], []
            for hh in range(HP):
                q = q_ref[pl.ds(off, tb), hh * HEAD_PAD:(hh + 1) * HEAD_PAD]
                d = do_ref[pl.ds(off, tb), hh * VDIM:(hh + 1) * VDIM]
                st = lax.dot_general(ks[hh], q, NT, preferred_element_type=F32)
                if masked:
                    r, c = _causal_mask(tb)
                    st = jnp.where(r <= c, st, NEG)
                qv.append(q)
                dov.append(d)
                sts.append(st)
                dpts.append(lax.dot_general(vs[hh], d, NT, preferred_element_type=F32))
            pts = [jnp.exp2(sts[hh] - lse_ref[hh, iq]) for hh in range(HP)]
            dsts = [(pts[hh] * (dpts[hh] - dl_ref[hh, iq])).astype(BF) for hh in range(HP)]
            out = []
            for hh in range(HP):
                dk, dv = carry[hh]
                dv = dv + lax.dot_general(pts[hh].astype(BF), dov[hh], NN, preferred_element_type=F32)
                dk = dk + lax.dot_general(dsts[hh], qv[hh], NN, preferred_element_type=F32)
                out.append((dk, dv))
            return tuple(out)

        zero = (jnp.zeros((tb, HEAD_PAD), F32), jnp.zeros((tb, VDIM), F32))
        carry = blk(ik, (zero,) * HP, True)
        carry = lax.fori_loop(ik + 1, nq, lambda iq, cr: blk(iq, cr, False), carry)
        dkpe = None
        for hh in range(HP):
            dk, dv = carry[hh]
            dk = dk * (ATTN_SCALE / LOG2_SCALE)
            a0 = hh * HEAD_PAD
            dkv_ref[:, a0:a0 + NOPE] = dk[:, :NOPE].astype(BF)
            dkv_ref[:, a0 + NOPE:a0 + HEAD_PAD] = dv.astype(BF)
            dkpe = dk[:, NOPE:] if dkpe is None else dkpe + dk[:, NOPE:]
        dkpe_ref[...] = dkpe

    rows = pl.BlockSpec((HP, nq, 1, tb), lambda h, i: (h, 0, 0, 0))
    return pl.pallas_call(
        body, name="flash_dkv", grid=(H // HP, nq),
        in_specs=[pl.BlockSpec((tb, HP * HEAD_PAD), lambda h, i: (i, h)),
                  pl.BlockSpec((tb, 128), lambda h, i: (i, 0)),
                  pl.BlockSpec((T, HP * HEAD_PAD), lambda h, i: (0, h)),
                  pl.BlockSpec((T, HP * VDIM), lambda h, i: (0, h)), rows, rows],
        out_specs=[pl.BlockSpec((tb, HP * HEAD_PAD), lambda h, i: (i, h)),
                   pl.BlockSpec((None, tb, 128), lambda h, i: (h, i, 0))],
        out_shape=[_sds((T, H * HEAD_PAD), BF), _sds((H // HP, T, 128), F32)],
        compiler_params=_cparams(("arbitrary", "arbitrary"), 56),
    )(kvup, kpe, qp, do, lse_rows, delta_rows)


def _final_loss(x, gain, target, tb=256):
    T, D = x.shape
    tb = _tile(T, tb)

    def body(x_ref, g_ref, t_ref, loss_ref, dx_ref, dg_ref):
        i = pl.program_id(0)
        xv = x_ref[...]
        gv = g_ref[...]
        r = lax.rsqrt(jnp.mean(xv * xv, axis=-1, keepdims=True) + NORM_EPS)
        xh = xv * r
        e = xh * gv - t_ref[...]
        lpart = 0.5 * jnp.sum(jnp.mean(e * e, axis=-1, keepdims=True), axis=0, keepdims=True)
        dy = e / D
        dyg = dy * gv
        dx_ref[...] = r * (dyg - xh * jnp.mean(dyg * xh, axis=-1, keepdims=True))
        gpart = jnp.sum(dy * xh, axis=0, keepdims=True)

        @pl.when(i == 0)
        def _():
            loss_ref[...] = lpart
            dg_ref[...] = gpart

        @pl.when(i > 0)
        def _():
            loss_ref[...] += lpart
            dg_ref[...] += gpart

    blk = pl.BlockSpec((tb, D), lambda i: (i, 0))
    vec = pl.BlockSpec((1, D), lambda i: (0, 0))
    return pl.pallas_call(
        body, name="final_loss", grid=(T // tb,), in_specs=[blk, vec, blk],
        out_specs=[pl.BlockSpec((1, 1), lambda i: (0, 0)), blk, vec],
        out_shape=[_sds((1, 1), F32), _sds((T, D), F32), _sds((1, D), F32)],
        compiler_params=_cparams(("arbitrary",), 40),
    )(x, gain, target)


def _row_tile(R, pref=256):
    t = (min(R, pref) // 16) * 16
    while t >= 16:
        if R % t == 0:
            return t
        t -= 16
    return R


def _prefetch_call(body, name, grid, in_specs, out_specs, out_shape, scalar, operands, vmem_mb=32):
    return pl.pallas_call(
        body, name=name, out_shape=out_shape,
        grid_spec=pltpu.PrefetchScalarGridSpec(num_scalar_prefetch=1, grid=grid, in_specs=in_specs,
                                               out_specs=out_specs),
        compiler_params=_cparams(("arbitrary",) * len(grid), vmem_mb),
    )(scalar, *operands)


def _place_own(name, s, chip):
    _, n, R, C = s.shape
    tr = _row_tile(R, 512)

    def body(chip_ref, s_ref, o_ref):
        o_ref[...] = s_ref[...]

    return _prefetch_call(
        body, name, (2, n, R // tr),
        [pl.BlockSpec((None, None, tr, C), lambda h, j, r, cr: (h, j, r, 0))],
        pl.BlockSpec((None, None, None, tr, C), lambda h, j, r, cr: (cr[0], h, j, r, 0)),
        _sds((4,) + s.shape, s.dtype), chip, [s])


def _pair_add(name, g, theirs, core):
    _, _, n, R, C = g.shape
    tr = _row_tile(R)

    def body(c_ref, a_ref, b_ref, o_ref):
        o_ref[...] = (a_ref[...].astype(F32) + b_ref[...].astype(F32)).astype(BF)

    blk = pl.BlockSpec((None, None, tr, C), lambda k, j, r, cr: (k, j, r, 0))
    return _prefetch_call(
        body, name, (4, n, R // tr),
        [pl.BlockSpec((None, None, None, tr, C), lambda k, j, r, cr: (k, cr[0], j, r, 0)), blk], blk,
        _sds(theirs.shape, BF), core, [g, theirs])


def _chip_sum(name, sums, landed, chip):
    _, n, R, C = sums.shape
    tr = _row_tile(R)

    def body(chip_ref, own_ref, l_ref, o_ref):
        s = own_ref[...].astype(F32)
        for j in range(3):
            s = s + l_ref[j].astype(F32)
        o_ref[...] = s

    return _prefetch_call(
        body, name, (n, R // tr),
        [pl.BlockSpec((None, None, tr, C), lambda j, r, cr: (cr[0], j, r, 0)),
         pl.BlockSpec((3, None, tr, C), lambda j, r, cr: (0, j, r, 0))],
        pl.BlockSpec((None, tr, C), lambda j, r, cr: (j, r, 0)),
        _sds((n, R, C), F32), chip, [sums, landed])


def _adamw_halves(name, w, m, v, g_mine, g_theirs, core, tr=128):
    L, _, R, C = w.shape
    tr = _row_tile(R, tr)
    c1 = 1.0 - ADAM_B1 ** ADAM_STEP
    c2 = 1.0 - ADAM_B2 ** ADAM_STEP

    def body(c_ref, w_ref, m_ref, v_ref, gm_ref, gt_ref, g_ref, d_ref, nm_ref, nv_ref):
        gv = jnp.where(pl.program_id(1) == c_ref[0], gm_ref[...], gt_ref[...])
        nm = ADAM_B1 * m_ref[...] + (1.0 - ADAM_B1) * gv
        nv = ADAM_B2 * v_ref[...] + (1.0 - ADAM_B2) * (gv * gv)
        g_ref[...] = gv
        nm_ref[...] = nm
        nv_ref[...] = nv
        d_ref[...] = -ADAM_LR * ((nm / c1) / (jnp.sqrt(nv / c2) + ADAM_EPS) + ADAM_WD * w_ref[...])

    full = pl.BlockSpec((None, None, tr, C), lambda l, h, r, cr: (l, h, r, 0))
    half = pl.BlockSpec((None, tr, C), lambda l, h, r, cr: (l, r, 0))
    return _prefetch_call(
        body, name, (L, 2, R // tr), [full, full, full, half, half], [full] * 4,
        [_sds(w.shape, F32)] * 4, core, [w, m, v, g_mine, g_theirs])


def _adamw(name, w, g, m, v, tr=128):
    R, C = w.shape
    tr = _tile(R, tr) if R % 8 == 0 else R
    c1 = 1.0 - ADAM_B1 ** ADAM_STEP
    c2 = 1.0 - ADAM_B2 ** ADAM_STEP

    def body(w_ref, g_ref, m_ref, v_ref, d_ref, nm_ref, nv_ref):
        gv = g_ref[...]
        nm = ADAM_B1 * m_ref[...] + (1.0 - ADAM_B1) * gv
        nv = ADAM_B2 * v_ref[...] + (1.0 - ADAM_B2) * (gv * gv)
        nm_ref[...] = nm
        nv_ref[...] = nv
        d_ref[...] = -ADAM_LR * ((nm / c1) / (jnp.sqrt(nv / c2) + ADAM_EPS) + ADAM_WD * w_ref[...])

    blk = pl.BlockSpec((tr, C), lambda i: (i, 0))
    return pl.pallas_call(
        body, name=name, grid=(R // tr,), in_specs=[blk] * 4, out_specs=[blk] * 3,
        out_shape=[_sds((R, C), F32)] * 3, compiler_params=_cparams(("arbitrary",), 32),
    )(w, g, m, v)


def _place():
    x, y, c = lax.axis_index("x"), lax.axis_index("y"), lax.axis_index("c")
    chips = [(1 - x, y), (x, 1 - y), (1 - x, 1 - y)]
    return x, y, c, chips


def _gather_comm(shards, placed):
    n = len(shards)

    def copies(cin, cout, send, recv):
        src, dst = cin[:n], cout
        x, y, c, chips = _place()
        me = 2 * x + y

        def remote(i, k, s, d, to):
            return pltpu.make_async_remote_copy(src_ref=s, dst_ref=d, send_sem=send.at[6 * i + k],
                                                recv_sem=recv.at[6 * i + k], device_id=to, device_id_type=MESH)

        first = [remote(i, j, src[i].at[c], dst[i].at[me, c], (*chip, c))
                 for i in range(n) for j, chip in enumerate(chips)]
        return remote, first, dst, (x, y, c, chips)

    def start(cin, cout, send, recv):
        for cp in copies(cin, cout, send, recv)[1]:
            cp.start()

    def finish(cin, cout, send, recv):
        remote, first, dst, (x, y, c, chips) = copies(cin, cout, send, recv)
        sibling = (x, y, 1 - c)
        passed = []
        for i in range(n):
            for j, (px, py) in enumerate(chips):
                slot = dst[i].at[2 * px + py, c]
                remote(i, j, slot, slot, (px, py, c)).wait_recv()
                cp = remote(i, 3 + j, slot, slot, sibling)
                cp.start()
                passed.append(cp)
        for i in range(n):
            for j, (px, py) in enumerate(chips):
                slot = dst[i].at[2 * px + py, 1 - c]
                remote(i, 3 + j, slot, slot, sibling).wait_recv()
        for cp in first + passed:
            cp.wait_send()

    return _Comm(list(shards) + list(placed), [_sds(p.shape, p.dtype) for p in placed],
                 {n + i: i for i in range(n)}, 6 * n, start, finish)


def _pair_comm(grads):
    n = len(grads)

    def copies(cin, cout, send, recv):
        x, y, c, _ = _place()
        return [pltpu.make_async_remote_copy(
            src_ref=cin[i].at[k, 1 - c], dst_ref=cout[i].at[k], send_sem=send.at[4 * i + k],
            recv_sem=recv.at[4 * i + k], device_id=(x, y, 1 - c), device_id_type=MESH)
            for i in range(n) for k in range(4)]

    def start(cin, cout, send, recv):
        for cp in copies(cin, cout, send, recv):
            cp.start()

    def finish(cin, cout, send, recv):
        for cp in copies(cin, cout, send, recv):
            cp.wait()

    return _Comm(grads, [_sds((4,) + a.shape[2:], a.dtype) for a in grads], {}, 4 * n, start, finish)


def _chips_comm(sums):
    n = len(sums)

    def copies(cin, cout, send, recv):
        x, y, c, chips = _place()
        return [pltpu.make_async_remote_copy(
            src_ref=cin[i].at[2 * px + py], dst_ref=cout[i].at[j], send_sem=send.at[3 * i + j],
            recv_sem=recv.at[3 * i + j], device_id=(px, py, c), device_id_type=MESH)
            for i in range(n) for j, (px, py) in enumerate(chips)]

    def start(cin, cout, send, recv):
        for cp in copies(cin, cout, send, recv):
            cp.start()

    def finish(cin, cout, send, recv):
        for cp in copies(cin, cout, send, recv):
            cp.wait()

    return _Comm(sums, [_sds((3,) + a.shape[1:], a.dtype) for a in sums], {}, 3 * n, start, finish)


def _share_comm(halves):
    n = len(halves)

    def copies(cin, cout, send, recv):
        x, y, c, _ = _place()
        return [pltpu.make_async_remote_copy(
            src_ref=cin[i], dst_ref=cout[i], send_sem=send.at[i], recv_sem=recv.at[i],
            device_id=(x, y, 1 - c), device_id_type=MESH) for i in range(n)]

    def start(cin, cout, send, recv):
        for cp in copies(cin, cout, send, recv):
            cp.start()

    def finish(cin, cout, send, recv):
        for cp in copies(cin, cout, send, recv):
            cp.wait()

    return _Comm(halves, [_sds(a.shape, a.dtype) for a in halves], {}, n, start, finish)


def _allreduce_small(v):
    R, C = v.shape

    def body(v_ref, out_ref, land, send, recv):
        x, y, c, _ = _place()
        me = 4 * x + 2 * y + c
        cps = []
        for m in range(1, 8):
            fx, fy, fc = (m >> 2) & 1, (m >> 1) & 1, m & 1
            peer = (x ^ fx, y ^ fy, c ^ fc)
            cp = pltpu.make_async_remote_copy(
                src_ref=v_ref, dst_ref=land.at[me], send_sem=send.at[m - 1], recv_sem=recv.at[m - 1],
                device_id=peer, device_id_type=MESH)
            cp.start()
            cps.append(cp)
        land[me] = v_ref[...]
        for cp in cps:
            cp.wait()
        s = land[0]
        for d in range(1, 8):
            s = s + land[d]
        out_ref[...] = s

    return pl.pallas_call(
        body, name="allreduce_small",
        in_specs=[pl.BlockSpec(memory_space=pltpu.VMEM)], out_specs=pl.BlockSpec(memory_space=pltpu.VMEM),
        out_shape=_sds((R, C), F32),
        scratch_shapes=[pltpu.VMEM((8, R, C), F32), pltpu.SemaphoreType.DMA((7,)), pltpu.SemaphoreType.DMA((7,))],
    )(v)


def _halves(a):
    return a.reshape((2, a.shape[0] // 2) + a.shape[1:])


def _canon(a, lead):
    piece = a.shape[lead:]
    return a.reshape(a.shape[:lead] + (int(math.prod(piece[:-2])),) + piece[-2:])


def _rope_tables(pos):
    half = ROPE // 2
    inv_freq = ROPE_BASE ** (-jnp.arange(half, dtype=F32) / half)
    ang = pos.astype(F32)[:, None] * inv_freq
    cos, sin = jnp.cos(ang), jnp.sin(ang)
    return jnp.tile(cos, (1, 4)), jnp.concatenate([-sin, sin, -sin, sin], axis=1)


def kernel(x, positions, pool_norm, pool_w, pool_scale, kv_in_norm, w_kv_a, kv_latent_norm, w_kv_b, attn_norm, w_q_a, q_latent_norm, w_q_b, w_o, ffn_norm, w_gate, w_up, w_down, final_norm, loss_target, m_pool_norm, m_pool_w, m_pool_scale, m_kv_in_norm, m_w_kv_a, m_kv_latent_norm, m_w_kv_b, m_attn_norm, m_w_q_a, m_q_latent_norm, m_w_q_b, m_w_o, m_ffn_norm, m_w_gate, m_w_up, m_w_down, m_final_norm, v_pool_norm, v_pool_w, v_pool_scale, v_kv_in_norm, v_w_kv_a, v_kv_latent_norm, v_w_kv_b, v_attn_norm, v_w_q_a, v_q_latent_norm, v_w_q_b, v_w_o, v_ffn_norm, v_w_gate, v_w_up, v_w_down, v_final_norm):
    T, D = x.shape[1], x.shape[2]
    H = N_HEADS
    KL = kv_latent_norm.shape[0]
    QL = q_latent_norm.shape[1]
    pg = D // 4
    x0, tgt = x[0], loss_target[0]
    cos, sin = _rope_tables(positions[0])
    chip = 2 * lax.axis_index("x") + lax.axis_index("y")

    n_kva = w_kv_a.shape[1]
    wkva_s = jnp.pad(w_kv_a, ((0, 0), (0, KL + 128 - n_kva)))
    hs = w_q_b.shape[2] // (NOPE + ROPE)
    wqb_s = jnp.pad(w_q_b[0].reshape(QL, hs, NOPE + ROPE), ((0, 0), (0, 0), (0, HEAD_PAD - NOPE - ROPE)))
    wqb_s = wqb_s.reshape(QL, hs * HEAD_PAD)
    chip_s = chip.astype(jnp.int32).reshape(1)
    core_s = lax.axis_index("c").astype(jnp.int32).reshape(1)

    def halved(a):
        return _canon(_halves(a.astype(BF)), 1)

    def gather_group(tag, shards):
        placed = [_place_own(f"place_own{tag}_{i}", s, chip_s) for i, s in enumerate(shards)]
        return _gather_comm(shards, placed)

    def whole(a, *shape):
        return a.reshape(shape)

    n_ff = w_gate.shape[2]
    group0 = gather_group("0", [halved(pool_w[0]), _canon(jnp.stack([pool_norm, pool_scale]), 1),
                                halved(w_gate[0]), halved(w_up[0])])
    group1 = gather_group("1", [halved(w_down[0]), halved(wkva_s), halved(w_kv_b), halved(w_q_a[0]),
                                halved(wqb_s), halved(w_o[0])])
    group2 = gather_group("2", [halved(w_gate[1]), halved(w_up[1]), halved(w_down[1])])
    PW, PV, WG0, WU0 = _comm_call("gather0", group0)
    WG0, WU0 = whole(WG0, 4, D, n_ff), whole(WU0, 4, D, n_ff)
    pv = jnp.transpose(PV.reshape(4, 2, pg), (1, 0, 2)).reshape(2, D)
    pn_full, ps_full = pv[0:1], pv[1:2]

    diff = _pool_fwd(x0, pn_full)
    pre, x1 = _pool_mix(diff, PW, ps_full, x0)
    (h1,) = _norm_fwd("norm_ffn0", x1, ffn_norm[0:1])
    (g0, u0, a0), (WD0, WKVA, WKVB, WQA, WQB, WO) = _ffn_up("ffn_up0", h1, WG0, WU0, comm=group1)
    WD0 = whole(WD0, 4, n_ff, D)
    WKVA = whole(WKVA, D, KL + 128)
    WKVB = whole(WKVB, 4, KL, -1)
    WQA = whole(WQA, D, QL)
    WQB = whole(WQB, 4, QL, -1)
    WO = whole(WO, H * VDIM, D)
    x2 = _ffn_down("ffn_down0", a0, WD0, x1)
    hk, ha = _norm_fwd("norm_attn", x2, jnp.stack([kv_in_norm, attn_norm[0]]))
    kvp = _mm_plain("kv_a", hk, WKVA, F32)
    ckv, kpe = _kv_post(kvp, kv_latent_norm[None], cos, sin)
    kvup = _mm_cols("kv_b", ckv, WKVB, BF)
    qa = _mm_plain("q_a", ha, WQA, F32)
    (ql,) = _norm_fwd("norm_q", qa, q_latent_norm)
    qp = _q_up(ql, WQB, cos, sin)
    (o, lse), (WG1, WU1, WD1) = _flash_fwd(qp, kvup, kpe, comm=group2)
    WG1, WU1, WD1 = whole(WG1, 4, D, n_ff), whole(WU1, 4, D, n_ff), whole(WD1, 4, n_ff, D)
    x3 = _mm_plain("attn_out", o, WO, F32, res=x2)
    (h3,) = _norm_fwd("norm_ffn1", x3, ffn_norm[1:2])
    (g1, u1, a1), _ = _ffn_up("ffn_up1", h3, WG1, WU1)
    x4 = _ffn_down("ffn_down1", a1, WD1, x3)
    loss_part, dx4, d_final = _final_loss(x4, final_norm[None], tgt)

    def pair_stage(tag, grads):
        full = [_canon(a.reshape((4, 2, a.shape[1] // 2) + a.shape[2:]), 2) for a in grads]
        theirs = _comm_call("reduce_pair" + tag, _pair_comm(full))
        return [_pair_add(f"pair_add{tag}_{i}", a, b, core_s) for i, (a, b) in enumerate(zip(full, theirs))]

    dWD1 = _mm_tn_rows("ffn_dwd1", a1, dx4, 4, BF)
    dg1, du1 = _ffn_bwd_act("ffn_bwd_act1", dx4, WD1, g1, u1)
    dWG1 = _mm_tn_cols("ffn_dwg1", h3, dg1, 4, BF)
    dWU1 = _mm_tn_cols("ffn_dwu1", h3, du1, 4, BF)
    sums1 = pair_stage("1", [dWG1, dWU1, dWD1])
    dh3, _ = _ffn_dh("ffn_dh1", dg1, WG1, du1, WU1)
    dx3, d_ffn1 = _norm_bwd("norm_ffn1_bwd", x3, ffn_norm[1:2], [dh3], dx4)

    do = _mm_nt_plain("attn_out_dx", dx3, WO, BF)
    dWO = _mm_tn_plain("attn_out_dw", o, dx3, BF)
    (dqp, delta), landed1 = _flash_dq(qp, kvup, kpe, o, do, lse, cos, sin, comm=_chips_comm(sums1))
    tb = _tile(T, 512)
    lse_rows = lse.reshape(H, T // tb, 1, tb)
    delta_rows = delta.reshape(H, T // tb, 1, tb)
    dkvup, dkpe_h = _flash_dkv(qp, kvup, kpe, do, lse_rows, delta_rows)
    dql = _mm_nt_cols("q_b_dx", dqp, WQB, F32)
    dWQB = _mm_tn_cols("q_b_dw", ql, dqp, 4, BF)
    dqa, d_qln = _norm_bwd("norm_q_bwd", qa, q_latent_norm, [dql], None)
    dha = _mm_nt_plain("q_a_dx", dqa, WQA, F32)
    dWQA = _mm_tn_plain("q_a_dw", ha, dqa, BF)
    dckv = _mm_nt_cols("kv_b_dx", dkvup, WKVB, F32)
    dWKVB = _mm_tn_cols("kv_b_dw", ckv, dkvup, 4, BF)
    dkvp, d_kvln = _kv_post_bwd(kvp, kv_latent_norm[None], dckv, dkpe_h, cos, sin)
    dhk = _mm_nt_plain("kv_a_dx", dkvp, WKVA, F32)
    dWKVA = _mm_tn_plain("kv_a_dw", hk, dkvp, BF)
    dx2, d_n2 = _norm_bwd("norm_attn_bwd", x2, jnp.stack([kv_in_norm, attn_norm[0]]), [dhk, dha], dx3)

    dWD0 = _mm_tn_rows("ffn_dwd0", a0, dx2, 4, BF)
    dg0, du0 = _ffn_bwd_act("ffn_bwd_act0", dx2, WD0, g0, u0)
    dWG0 = _mm_tn_cols("ffn_dwg0", h1, dg0, 4, BF)
    dWU0 = _mm_tn_cols("ffn_dwu0", h1, du0, 4, BF)
    sums0 = pair_stage("0", [dWG0, dWU0, dWD0])
    dh1, landed0 = _ffn_dh("ffn_dh0", dg0, WG0, du0, WU0, comm=_chips_comm(sums0))
    dx1, d_ffn0 = _norm_bwd("norm_ffn0_bwd", x1, ffn_norm[0:1], [dh1], dx2)

    dpre, d_ps = _pool_dpre(dx1, pre, ps_full)
    ddiff = _pool_ddiff(dpre, PW)
    dPW = _pool_dw(diff, dpre)
    dx0, d_pn = _pool_bwd(x0, pn_full, ddiff, dx1)

    sums_a = pair_stage("a", [dPW, dWKVA.reshape(4, D // 4, KL + 128), dWKVB, dWQA.reshape(4, D // 4, QL), dWQB,
                              dWO.reshape(4, H * VDIM // 4, D)])
    landed_a = _comm_call("reduce_chips", _chips_comm(sums_a))
    sums = sums_a + sums0 + sums1
    landed = list(landed_a) + list(landed0) + list(landed1)
    halves = [_chip_sum(f"chip_sum{i}", a, b, chip_s) for i, (a, b) in enumerate(zip(sums, landed))]
    other = _comm_call("share_halves", _share_comm(halves))

    def rows(a):
        return a.reshape((1, -1, a.shape[-1]))

    def unpad_kva(a):
        return rows(a)[:, :, :n_kva]

    def unpad_qb(a):
        return rows(a).reshape(1, -1, hs, HEAD_PAD)[:, :, :, :NOPE + ROPE].reshape(1, -1, hs * (NOPE + ROPE))

    def layers(a0_, a1_):
        return jnp.concatenate([rows(a0_), rows(a1_)], axis=0)

    def by_name(hv):
        return {"pool_w": rows(hv[0]), "w_kv_a": unpad_kva(hv[1]), "w_kv_b": rows(hv[2]), "w_q_a": rows(hv[3]),
                "w_q_b": unpad_qb(hv[4]), "w_o": rows(hv[5]), "w_gate": layers(hv[6], hv[9]),
                "w_up": layers(hv[7], hv[10]), "w_down": layers(hv[8], hv[11])}

    big = ["pool_w", "w_kv_a", "w_kv_b", "w_q_a", "w_q_b", "w_o", "w_gate", "w_up", "w_down"]
    g_mine, g_other = by_name(halves), by_name(other)

    lat = jnp.concatenate([d_kvln[0], d_qln[0], jnp.zeros((D - KL - QL,), F32)])
    lrow = jnp.pad(loss_part[0], (0, D - 1))
    small = jnp.stack([d_n2[0], d_n2[1], d_ffn0[0], d_ffn1[0], d_final[0], lat, d_pn[0], d_ps[0], lrow]
                      + [jnp.zeros((D,), F32)] * 7)
    red = _allreduce_small(small)
    loss = red[8, 0]
    g_kv_in, g_attn, g_final = red[0], red[1:2], red[4]
    g_ffn = red[2:4]
    g_kvln, g_qln = red[5, :KL], red[5:6, KL:KL + QL]
    g_pn = lax.dynamic_slice(red[6:7], (0, chip * pg), (1, pg))
    g_ps = lax.dynamic_slice(red[7:8], (0, chip * pg), (1, pg))

    grads = {"pool_norm": g_pn, "pool_scale": g_ps, "kv_in_norm": g_kv_in, "kv_latent_norm": g_kvln,
             "attn_norm": g_attn, "q_latent_norm": g_qln, "ffn_norm": g_ffn, "final_norm": g_final}
    weights = dict(pool_norm=pool_norm, pool_w=pool_w, pool_scale=pool_scale, kv_in_norm=kv_in_norm, w_kv_a=w_kv_a,
                   kv_latent_norm=kv_latent_norm, w_kv_b=w_kv_b, attn_norm=attn_norm, w_q_a=w_q_a,
                   q_latent_norm=q_latent_norm, w_q_b=w_q_b, w_o=w_o, ffn_norm=ffn_norm, w_gate=w_gate, w_up=w_up,
                   w_down=w_down, final_norm=final_norm)
    ms = dict(pool_norm=m_pool_norm, pool_w=m_pool_w, pool_scale=m_pool_scale, kv_in_norm=m_kv_in_norm,
              w_kv_a=m_w_kv_a, kv_latent_norm=m_kv_latent_norm, w_kv_b=m_w_kv_b, attn_norm=m_attn_norm,
              w_q_a=m_w_q_a, q_latent_norm=m_q_latent_norm, w_q_b=m_w_q_b, w_o=m_w_o, ffn_norm=m_ffn_norm,
              w_gate=m_w_gate, w_up=m_w_up, w_down=m_w_down, final_norm=m_final_norm)
    vs = dict(pool_norm=v_pool_norm, pool_w=v_pool_w, pool_scale=v_pool_scale, kv_in_norm=v_kv_in_norm,
              w_kv_a=v_w_kv_a, kv_latent_norm=v_kv_latent_norm, w_kv_b=v_w_kv_b, attn_norm=v_attn_norm,
              w_q_a=v_w_q_a, q_latent_norm=v_q_latent_norm, w_q_b=v_w_q_b, w_o=v_w_o, ffn_norm=v_ffn_norm,
              w_gate=v_w_gate, w_up=v_w_up, w_down=v_w_down, final_norm=v_final_norm)
    names = list(weights)

    def as2d(a):
        return a.reshape((-1, a.shape[-1]))

    delta_w, new_m, new_v = {}, {}, {}
    for nm in big:
        n_layers = g_mine[nm].shape[0]

        def two_halves(a):
            a = as2d(a)
            return a.reshape(n_layers, 2, a.shape[0] // (2 * n_layers), a.shape[1])

        g, d, m2, v2 = _adamw_halves("adamw_" + nm, two_halves(weights[nm]), two_halves(ms[nm]),
                                     two_halves(vs[nm]), g_mine[nm], g_other[nm], core_s)
        shp = weights[nm].shape
        grads[nm], delta_w[nm], new_m[nm], new_v[nm] = g.reshape(shp), d.reshape(shp), m2.reshape(shp), v2.reshape(shp)
    groups = [["kv_in_norm", "attn_norm", "ffn_norm", "final_norm"], ["kv_latent_norm", "q_latent_norm"],
              ["pool_norm", "pool_scale"]]
    for gi, grp in enumerate(groups):
        cat = lambda t: jnp.concatenate([as2d(t[nm]) for nm in grp], axis=0)
        d, m2, v2 = _adamw(f"adamw_vec{gi}", cat(weights), cat(grads), cat(ms), cat(vs))
        r0 = 0
        for nm in grp:
            shp = weights[nm].shape
            r = as2d(weights[nm]).shape[0]
            delta_w[nm], new_m[nm], new_v[nm] = (d[r0:r0 + r].reshape(shp), m2[r0:r0 + r].reshape(shp),
                                                 v2[r0:r0 + r].reshape(shp))
            r0 += r

    return (loss, dx0[None], *[grads[nm].reshape(weights[nm].shape) for nm in names],
            *[delta_w[nm] for nm in names], *[new_m[nm] for nm in names], *[new_v[nm] for nm in names])
```

```python
import functools
import math

import jax
import jax.numpy as jnp
from jax import lax
from jax.experimental import pallas as pl
from jax.experimental.pallas import tpu as pltpu

BF = jnp.bfloat16
F32 = jnp.float32
MESH = pl.DeviceIdType.MESH

N_HEADS = 16
NOPE = 128
ROPE = 64
VDIM = 128
HEAD_PAD = 256
ROPE_BASE = 10000.0
ATTN_SCALE = 1.0 / math.sqrt(NOPE + ROPE)
POOL_WINDOWS = (2, 4, 8, 16)
HALO = 16
NORM_EPS = 1e-6
ADAM_LR, ADAM_B1, ADAM_B2, ADAM_EPS, ADAM_WD, ADAM_STEP = 0.001, 0.9, 0.999, 1e-08, 0.01, 10
NEG = -1e30

V7X_VMEM_BYTES = 64 * 1024 * 1024
VMEM_CEILING = V7X_VMEM_BYTES - 8 * 1024 * 1024

NN = (((1,), (0,)), ((), ()))
NT = (((1,), (1,)), ((), ()))
TN = (((0,), (0,)), ((), ()))


def _cparams(sem, vmem_mb):
    return pltpu.CompilerParams(dimension_semantics=sem,
                                vmem_limit_bytes=min(vmem_mb * 1024 * 1024, VMEM_CEILING))


def _tile(n, pref):
    if n <= pref:
        return n
    t = (pref // 128) * 128
    while t > 128 and n % t:
        t -= 128
    assert n % t == 0, (n, pref)
    return t


def _sds(shape, dtype):
    return jax.ShapeDtypeStruct(shape, dtype)


ANY = pl.BlockSpec(memory_space=pl.ANY)


class _Comm:
    def __init__(self, ins, out_shapes, alias, n_sems, start, finish):
        self.ins, self.out_shapes, self.alias, self.n_sems = list(ins), list(out_shapes), dict(alias), n_sems
        self.start, self.finish = start, finish


def _call(name, body, grid, in_specs, out_specs, out_shape, scratch, vmem_mb, operands, comm=None):
    in_specs, out_specs, out_shape, scratch = list(in_specs), list(out_specs), list(out_shape), list(scratch)
    params = _cparams(("arbitrary",) * len(grid), vmem_mb)
    if comm is None:
        res = pl.pallas_call(body, name=name, grid=grid, in_specs=in_specs, out_specs=out_specs,
                             out_shape=out_shape, scratch_shapes=scratch, compiler_params=params)(*operands)
        return list(res), []
    n_in, n_out, n_scr = len(in_specs), len(out_specs), len(scratch)
    c_in, c_out = len(comm.ins), len(comm.out_shapes)

    def hosted(*refs):
        ins, cin = refs[:n_in], refs[n_in:n_in + c_in]
        o0 = n_in + c_in
        outs, cout = refs[o0:o0 + n_out], refs[o0 + n_out:o0 + n_out + c_out]
        s0 = o0 + n_out + c_out
        scr, (send, recv) = refs[s0:s0 + n_scr], refs[s0 + n_scr:]
        ids = [pl.program_id(d) for d in range(len(grid))]
        first = functools.reduce(jnp.logical_and, [i == 0 for i in ids])
        last = functools.reduce(jnp.logical_and, [i == g - 1 for i, g in zip(ids, grid)])

        @pl.when(first)
        def _():
            comm.start(cin, cout, send, recv)

        body(*ins, *outs, *scr)

        @pl.when(last)
        def _():
            comm.finish(cin, cout, send, recv)

    res = pl.pallas_call(
        hosted, name=name, grid=grid, in_specs=in_specs + [ANY] * c_in, out_specs=out_specs + [ANY] * c_out,
        out_shape=out_shape + comm.out_shapes,
        scratch_shapes=scratch + [pltpu.SemaphoreType.DMA((comm.n_sems,)), pltpu.SemaphoreType.DMA((comm.n_sems,))],
        input_output_aliases={n_in + k: n_out + v for k, v in comm.alias.items()}, compiler_params=params,
    )(*operands, *comm.ins)
    return list(res[:n_out]), list(res[n_out:])


def _comm_call(name, comm):
    c_in, c_out = len(comm.ins), len(comm.out_shapes)

    def body(*refs):
        cin, cout = refs[:c_in], refs[c_in:c_in + c_out]
        send, recv = refs[c_in + c_out:]
        comm.start(cin, cout, send, recv)
        comm.finish(cin, cout, send, recv)

    return pl.pallas_call(
        body, name=name, in_specs=[ANY] * c_in, out_specs=[ANY] * c_out, out_shape=comm.out_shapes,
        scratch_shapes=[pltpu.SemaphoreType.DMA((comm.n_sems,)), pltpu.SemaphoreType.DMA((comm.n_sems,))],
        input_output_aliases=dict(comm.alias),
    )(*comm.ins)


def _matmul(name, pairs, pair_specs, dims, grid, nk, extra, extra_specs, out_shapes, out_specs,
            acc_shape, epilogue, vmem_mb, comm=None):
    n_p, n_e, n_o = len(pairs) // 2, len(extra), len(out_shapes)

    def body(*refs):
        ab = refs[:2 * n_p]
        ex = refs[2 * n_p:2 * n_p + n_e]
        outs = refs[2 * n_p + n_e:2 * n_p + n_e + n_o]

        def partial_sum():
            tot = None
            for p in range(n_p):
                a = ab[2 * p][...]
                b = ab[2 * p + 1][...]
                if a.ndim > 2:
                    a = a.reshape(-1, a.shape[-1])
                if b.ndim > 2:
                    b = b.reshape(-1, b.shape[-1])
                d = lax.dot_general(a.astype(BF), b.astype(BF), dims, preferred_element_type=F32)
                tot = d if tot is None else tot + d
            return tot

        if nk == 1:
            epilogue(partial_sum(), ex, outs)
        else:
            acc = refs[-1]
            kk = pl.program_id(2)

            @pl.when(kk == 0)
            def _():
                acc[...] = partial_sum()

            @pl.when(kk > 0)
            def _():
                acc[...] += partial_sum()

            @pl.when(kk == nk - 1)
            def _():
                epilogue(acc[...], ex, outs)

    scratch = [] if nk == 1 else [pltpu.VMEM(acc_shape, F32)]
    res, comm_res = _call(name, body, grid, list(pair_specs) + list(extra_specs), out_specs, out_shapes,
                          scratch, vmem_mb, list(pairs) + list(extra), comm)
    return res if comm is None else (res, comm_res)


def _epi_store(acc, ex, outs):
    outs[0][...] = acc.reshape(outs[0].shape).astype(outs[0].dtype)


def _epi_residual(acc, ex, outs):
    outs[0][...] = (acc + ex[0][...]).astype(outs[0].dtype)


def _swap_halves(x):
    lane = lax.broadcasted_iota(jnp.int32, x.shape, 1)
    return jnp.where((lane % 64) < 32, pltpu.roll(x, 96, 1), pltpu.roll(x, 32, 1))


def _mm_plain(name, a, b, out_dtype, res=None, tm=512, tn=1024):
    M, K = a.shape
    N = b.shape[1]
    tm, tn = _tile(M, tm), _tile(N, tn)
    extra, extra_specs, epi = [], [], _epi_store
    if res is not None:
        extra, extra_specs, epi = [res], [pl.BlockSpec((tm, tn), lambda i, j, k: (i, j))], _epi_residual
    return _matmul(
        name, [a, b],
        [pl.BlockSpec((tm, K), lambda i, j, k: (i, 0)), pl.BlockSpec((K, tn), lambda i, j, k: (0, j))],
        NN, (M // tm, N // tn, 1), 1, extra, extra_specs,
        [_sds((M, N), out_dtype)], [pl.BlockSpec((tm, tn), lambda i, j, k: (i, j))],
        None, epi, 40)[0]


def _mm_cols(name, a, b3, out_dtype, epilogue=_epi_store, extra=(), extra_specs=(), tm=512):
    M, K = a.shape
    G, _, n = b3.shape
    tm = _tile(M, tm)
    return _matmul(
        name, [a, b3],
        [pl.BlockSpec((tm, K), lambda i, j, k: (i, 0)), pl.BlockSpec((None, K, n), lambda i, j, k: (j, 0, 0))],
        NN, (M // tm, G, 1), 1, list(extra), list(extra_specs),
        [_sds((M, G * n), out_dtype)], [pl.BlockSpec((tm, n), lambda i, j, k: (i, j))],
        None, epilogue, 40)[0]


def _mm_nt_plain(name, a, b, out_dtype, tm=512, tn=1024):
    M, K = a.shape
    N = b.shape[0]
    tm, tn = _tile(M, tm), _tile(N, tn)
    return _matmul(
        name, [a, b],
        [pl.BlockSpec((tm, K), lambda i, j, k: (i, 0)), pl.BlockSpec((tn, K), lambda i, j, k: (j, 0))],
        NT, (M // tm, N // tn, 1), 1, [], [],
        [_sds((M, N), out_dtype)], [pl.BlockSpec((tm, tn), lambda i, j, k: (i, j))],
        None, _epi_store, 44)[0]


def _mm_nt_cols(name, a, b3, out_dtype, tm=512):
    M = a.shape[0]
    G, K, n = b3.shape
    tm = _tile(M, tm)
    return _matmul(
        name, [a, b3],
        [pl.BlockSpec((tm, n), lambda i, j, k: (i, k)), pl.BlockSpec((None, K, n), lambda i, j, k: (k, 0, 0))],
        NT, (M // tm, 1, G), G, [], [],
        [_sds((M, K), out_dtype)], [pl.BlockSpec((tm, K), lambda i, j, k: (i, 0))],
        (tm, K), _epi_store, 40)[0]


def _mm_tn_plain(name, a, b, out_dtype, tt=512, tn=1024):
    T, K = a.shape
    N = b.shape[1]
    tt, tn = _tile(T, tt), _tile(N, tn)
    return _matmul(
        name, [a, b],
        [pl.BlockSpec((tt, K), lambda i, j, k: (k, 0)), pl.BlockSpec((tt, tn), lambda i, j, k: (k, j))],
        TN, (1, N // tn, T // tt), T // tt, [], [],
        [_sds((K, N), out_dtype)], [pl.BlockSpec((K, tn), lambda i, j, k: (0, j))],
        (K, tn), _epi_store, 48)[0]


def _mm_tn_cols(name, a, b, G, out_dtype, tt=1024):
    T, K = a.shape
    n = b.shape[1] // G
    tt = _tile(T, tt)
    return _matmul(
        name, [a, b],
        [pl.BlockSpec((tt, K), lambda i, j, k: (k, 0)), pl.BlockSpec((tt, n), lambda i, j, k: (k, j))],
        TN, (1, G, T // tt), T // tt, [], [],
        [_sds((G, K, n), out_dtype)], [pl.BlockSpec((None, K, n), lambda i, j, k: (j, 0, 0))],
        (K, n), _epi_store, 48)[0]


def _norm_fwd(name, x, gains, tb=512):
    T, D = x.shape
    G = gains.shape[0]
    tb = _tile(T, tb)

    def body(x_ref, g_ref, *outs):
        xv = x_ref[...]
        xh = xv * lax.rsqrt(jnp.mean(xv * xv, axis=-1, keepdims=True) + NORM_EPS)
        for g in range(G):
            outs[g][...] = (xh * g_ref[g:g + 1, :]).astype(BF)

    return pl.pallas_call(
        body, name=name, grid=(T // tb,),
        in_specs=[pl.BlockSpec((tb, D), lambda i: (i, 0)), pl.BlockSpec((G, D), lambda i: (0, 0))],
        out_specs=[pl.BlockSpec((tb, D), lambda i: (i, 0))] * G,
        out_shape=[_sds((T, D), BF)] * G,
        compiler_params=_cparams(("arbitrary",), 40),
    )(x, gains)


def _norm_bwd(name, x, gains, dhs, dres, tb=256):
    T, D = x.shape
    G = gains.shape[0]
    tb = _tile(T, tb)
    has_res = dres is not None

    def body(*refs):
        x_ref, g_ref = refs[0], refs[1]
        dh_refs = refs[2:2 + G]
        res_ref = refs[2 + G] if has_res else None
        dx_ref, dg_ref, dxb_ref = refs[-3], refs[-2], refs[-1]
        i = pl.program_id(0)
        xv = x_ref[...]
        r = lax.rsqrt(jnp.mean(xv * xv, axis=-1, keepdims=True) + NORM_EPS)
        xh = xv * r
        dx = res_ref[...] if has_res else jnp.zeros_like(xv)
        rows = []
        for g in range(G):
            dh = dh_refs[g][...].astype(F32)
            dy = dh * g_ref[g:g + 1, :]
            dx = dx + r * (dy - xh * jnp.mean(dy * xh, axis=-1, keepdims=True))
            rows.append(jnp.sum(dh * xh, axis=0, keepdims=True))
        dx_ref[...] = dx
        dxb_ref[...] = dx.astype(BF)

        @pl.when(i == 0)
        def _():
            for g in range(G):
                dg_ref[g:g + 1, :] = rows[g]

        @pl.when(i > 0)
        def _():
            for g in range(G):
                dg_ref[g:g + 1, :] += rows[g]

    blk = pl.BlockSpec((tb, D), lambda i: (i, 0))
    ins = [x, gains] + list(dhs) + ([dres] if has_res else [])
    in_specs = [blk, pl.BlockSpec((G, D), lambda i: (0, 0))] + [blk] * (G + (1 if has_res else 0))
    return pl.pallas_call(
        body, name=name, grid=(T // tb,), in_specs=in_specs,
        out_specs=[blk, pl.BlockSpec((G, D), lambda i: (0, 0)), blk],
        out_shape=[_sds((T, D), F32), _sds((G, D), F32), _sds((T, D), BF)],
        compiler_params=_cparams(("arbitrary",), 48),
    )(*ins)


def _pool_fwd(x, gain, tb=256):
    T, D = x.shape
    tb = _tile(T, tb)
    pg = D // len(POOL_WINDOWS)
    per = tb // HALO

    def body(x_ref, xp_ref, g_ref, diff_ref):
        i = pl.program_id(0)
        xx = jnp.concatenate([xp_ref[...], x_ref[...]], axis=0)
        h = xx * lax.rsqrt(jnp.mean(xx * xx, axis=-1, keepdims=True) + NORM_EPS) * g_ref[...]
        row = lax.broadcasted_iota(jnp.int32, (HALO + tb, 1), 0)
        h = jnp.where((row >= HALO) | (i > 0), h, 0.0)
        t = i * tb + row[HALO:] - HALO
        for g, w in enumerate(POOL_WINDOWS):
            hg = h[:, g * pg:(g + 1) * pg]
            s, k = hg, 1
            while k < w:
                s = s + pltpu.roll(s, k, 0)
                k *= 2
            cnt = jnp.minimum(t + 1, w).astype(F32)
            diff_ref[:, g * pg:(g + 1) * pg] = (s[HALO:] / cnt - hg[HALO:]).astype(BF)

    return pl.pallas_call(
        body, name="pool_fwd", grid=(T // tb,),
        in_specs=[pl.BlockSpec((tb, D), lambda i: (i, 0)),
                  pl.BlockSpec((HALO, D), lambda i: (jnp.maximum(i * per - 1, 0), 0)),
                  pl.BlockSpec((1, D), lambda i: (0, 0))],
        out_specs=pl.BlockSpec((tb, D), lambda i: (i, 0)),
        out_shape=_sds((T, D), BF),
        compiler_params=_cparams(("arbitrary",), 40),
    )(x, x, gain)


def _pool_bwd(x, gain, ddiff, dres, tb=256):
    T, D = x.shape
    tb = _tile(T, tb)
    pg = D // len(POOL_WINDOWS)
    per = tb // HALO
    nblk = T // HALO

    def body(x_ref, g_ref, dd_ref, ddn_ref, res_ref, dx_ref, dg_ref):
        i = pl.program_id(0)
        last = i == T // tb - 1
        dd = jnp.concatenate([dd_ref[...], ddn_ref[...]], axis=0)
        row = lax.broadcasted_iota(jnp.int32, (tb + HALO, 1), 0)
        dd = jnp.where((row < tb) | jnp.logical_not(last), dd, 0.0)
        t = i * tb + row
        parts = []
        for g, w in enumerate(POOL_WINDOWS):
            dg_ = dd[:, g * pg:(g + 1) * pg]
            e = dg_ / jnp.minimum(t + 1, w).astype(F32)
            s, k = e, 1
            while k < w:
                s = s + pltpu.roll(s, tb + HALO - k, 0)
                k *= 2
            parts.append(s[:tb] - dg_[:tb])
        dh = jnp.concatenate(parts, axis=1)
        xv = x_ref[...]
        r = lax.rsqrt(jnp.mean(xv * xv, axis=-1, keepdims=True) + NORM_EPS)
        xh = xv * r
        dy = dh * g_ref[...]
        dx_ref[...] = res_ref[...] + r * (dy - xh * jnp.mean(dy * xh, axis=-1, keepdims=True))
        part = jnp.sum(dh * xh, axis=0, keepdims=True)

        @pl.when(i == 0)
        def _():
            dg_ref[...] = part

        @pl.when(i > 0)
        def _():
            dg_ref[...] += part

    blk = pl.BlockSpec((tb, D), lambda i: (i, 0))
    return pl.pallas_call(
        body, name="pool_bwd", grid=(T // tb,),
        in_specs=[blk, pl.BlockSpec((1, D), lambda i: (0, 0)), blk,
                  pl.BlockSpec((HALO, D), lambda i: (jnp.minimum((i + 1) * per, nblk - 1), 0)), blk],
        out_specs=[blk, pl.BlockSpec((1, D), lambda i: (0, 0))],
        out_shape=[_sds((T, D), F32), _sds((1, D), F32)],
        compiler_params=_cparams(("arbitrary",), 48),
    )(x, gain, ddiff, ddiff, dres)


def _pool_w_spec(pg):
    return pl.BlockSpec((4, None, None, pg // 4, pg), lambda i, j, k: (0, j // 2, j % 2, 0, 0))


def _pool_mix(diff, pw, scale, x, tm=512):
    T, D = x.shape
    pg = D // 4
    tm = _tile(T, tm)

    def epi(acc, ex, outs):
        outs[0][...] = acc
        outs[1][...] = ex[1][...] + acc * ex[0][...]

    blk = pl.BlockSpec((tm, pg), lambda i, j, k: (i, j))
    return _matmul(
        "pool_mix", [diff, pw], [blk, _pool_w_spec(pg)], NN, (T // tm, 4, 1), 1,
        [scale, x], [pl.BlockSpec((1, pg), lambda i, j, k: (0, j)), blk],
        [_sds((T, D), F32), _sds((T, D), F32)], [blk, blk], None, epi, 32)


def _pool_dpre(dx, pre, scale, tb=512):
    T, D = dx.shape
    tb = _tile(T, tb)

    def body(dx_ref, pre_ref, s_ref, dpre_ref, ds_ref):
        i = pl.program_id(0)
        d = dx_ref[...]
        dpre_ref[...] = (d * s_ref[...]).astype(BF)
        part = jnp.sum(d * pre_ref[...], axis=0, keepdims=True)

        @pl.when(i == 0)
        def _():
            ds_ref[...] = part

        @pl.when(i > 0)
        def _():
            ds_ref[...] += part

    blk = pl.BlockSpec((tb, D), lambda i: (i, 0))
    vec = pl.BlockSpec((1, D), lambda i: (0, 0))
    return pl.pallas_call(
        body, name="pool_dpre", grid=(T // tb,), in_specs=[blk, blk, vec], out_specs=[blk, vec],
        out_shape=[_sds((T, D), BF), _sds((1, D), F32)],
        compiler_params=_cparams(("arbitrary",), 40),
    )(dx, pre, scale)


def _pool_ddiff(dpre, pw, tm=512):
    T, D = dpre.shape
    pg = D // 4
    tm = _tile(T, tm)
    blk = pl.BlockSpec((tm, pg), lambda i, j, k: (i, j))
    return _matmul("pool_ddiff", [dpre, pw], [blk, _pool_w_spec(pg)], NT, (T // tm, 4, 1), 1, [], [],
                   [_sds((T, D), F32)], [blk], None, _epi_store, 32)[0]


def _pool_dw(diff, dpre, tt=512):
    T, D = diff.shape
    pg = D // 4
    tt = _tile(T, tt)
    blk = pl.BlockSpec((tt, pg), lambda i, j, k: (k, j))
    return _matmul("pool_dw", [diff, dpre], [blk, blk], TN, (1, 4, T // tt), T // tt, [], [],
                   [_sds((4, 2, 2, pg // 4, pg), BF)], [_pool_w_spec(pg)], (pg, pg), _epi_store, 32)[0]


def _ffn_up(name, h, wg, wu, tm=512, comm=None):
    T, D = h.shape
    n = wg.shape[-1]
    F = 4 * n
    tm = _tile(T, tm)

    def body(h_ref, wg_ref, wu_ref, g_ref, u_ref, a_ref):
        hv = h_ref[...]
        g = jnp.dot(hv, wg_ref[...], preferred_element_type=F32)
        u = jnp.dot(hv, wu_ref[...], preferred_element_type=F32)
        g_ref[...] = g.astype(BF)
        u_ref[...] = u.astype(BF)
        a_ref[...] = (g * jax.nn.sigmoid(g) * u).astype(BF)

    w_spec = pl.BlockSpec((None, D, n), lambda j, i: (j, 0, 0))
    o_spec = pl.BlockSpec((tm, n), lambda j, i: (i, j))
    return _call(name, body, (4, T // tm), [pl.BlockSpec((tm, D), lambda j, i: (i, 0)), w_spec, w_spec],
                 [o_spec] * 3, [_sds((T, F), BF)] * 3, [], 56, [h, wg, wu], comm)


def _ffn_down(name, a, wd, x, tm=512, tn=1024):
    T, D = x.shape
    n = wd.shape[1]
    tm, tn = _tile(T, tm), _tile(D, tn)
    return _matmul(
        name, [a, wd],
        [pl.BlockSpec((tm, n), lambda i, j, k: (i, k)),
         pl.BlockSpec((None, n, tn), lambda i, j, k: (k, 0, j))],
        NN, (T // tm, D // tn, 4), 4, [x], [pl.BlockSpec((tm, tn), lambda i, j, k: (i, j))],
        [_sds((T, D), F32)], [pl.BlockSpec((tm, tn), lambda i, j, k: (i, j))],
        (tm, tn), _epi_residual, 40)[0]


def _ffn_bwd_act(name, dx, wd, g, u, tm=512):
    T, D = dx.shape
    n = wd.shape[1]
    tm = _tile(T, tm)

    def epi(acc, ex, outs):
        gv = ex[0][...].astype(F32)
        uv = ex[1][...].astype(F32)
        sig = jax.nn.sigmoid(gv)
        outs[0][...] = (acc * uv * (sig * (1.0 + gv * (1.0 - sig)))).astype(BF)
        outs[1][...] = (acc * (gv * sig)).astype(BF)

    blk = pl.BlockSpec((tm, n), lambda i, j, k: (i, j))
    return _matmul(
        name, [dx, wd],
        [pl.BlockSpec((tm, D), lambda i, j, k: (i, 0)), pl.BlockSpec((None, n, D), lambda i, j, k: (j, 0, 0))],
        NT, (T // tm, 4, 1), 1, [g, u], [blk, blk],
        [_sds((T, 4 * n), BF)] * 2, [blk, blk], None, epi, 56)


def _ffn_dh(name, dg, wg, du, wu, tm=512, comm=None):
    T = dg.shape[0]
    D, n = wg.shape[1], wg.shape[2]
    tm = _tile(T, tm)
    a_spec = pl.BlockSpec((tm, n), lambda i, j, k: (i, k))
    w_spec = pl.BlockSpec((None, D, n), lambda i, j, k: (k, 0, 0))
    r = _matmul(
        name, [dg, wg, du, wu], [a_spec, w_spec, a_spec, w_spec], NT, (T // tm, 1, 4), 4, [], [],
        [_sds((T, D), F32)], [pl.BlockSpec((tm, D), lambda i, j, k: (i, 0))], (tm, D), _epi_store, 56, comm)
    return (r[0], []) if comm is None else (r[0][0], r[1])


def _mm_tn_rows(name, a, b, G, out_dtype, tt=1024, tn=1024):
    T, N = b.shape
    n = a.shape[1] // G
    tt, tn = _tile(T, tt), _tile(N, tn)
    nj = N // tn
    return _matmul(
        name, [a, b],
        [pl.BlockSpec((tt, n), lambda i, j, k: (k, j // nj)), pl.BlockSpec((tt, tn), lambda i, j, k: (k, j % nj))],
        TN, (1, G * nj, T // tt), T // tt, [], [],
        [_sds((G, n, N), out_dtype)], [pl.BlockSpec((None, n, tn), lambda i, j, k: (j // nj, 0, j % nj))],
        (n, tn), _epi_store, 48)[0]


def _kv_post(kvp, gain, cos, sin, tb=512):
    T, W = kvp.shape
    KL = W - 128
    tb = _tile(T, tb)

    def body(kv_ref, g_ref, c_ref, s_ref, ckv_ref, kpe_ref):
        lat = kv_ref[:, :KL]
        ckv_ref[...] = (lat * lax.rsqrt(jnp.mean(lat * lat, axis=-1, keepdims=True) + NORM_EPS)
                        * g_ref[...]).astype(BF)
        pe = kv_ref[:, KL:]
        kpe_ref[...] = (pe * c_ref[...] + _swap_halves(pe) * s_ref[...]).astype(BF)

    tab = pl.BlockSpec((tb, 128), lambda i: (i, 0))
    return pl.pallas_call(
        body, name="kv_post", grid=(T // tb,),
        in_specs=[pl.BlockSpec((tb, W), lambda i: (i, 0)), pl.BlockSpec((1, KL), lambda i: (0, 0)), tab, tab],
        out_specs=[pl.BlockSpec((tb, KL), lambda i: (i, 0)), tab],
        out_shape=[_sds((T, KL), BF), _sds((T, 128), BF)],
        compiler_params=_cparams(("arbitrary",), 32),
    )(kvp, gain, cos, sin)


def _kv_post_bwd(kvp, gain, dckv, dkpe_heads, cos, sin, tb=256):
    T, W = kvp.shape
    KL = W - 128
    H = dkpe_heads.shape[0]
    tb = _tile(T, tb)

    def body(kv_ref, g_ref, dc_ref, dk_ref, c_ref, s_ref, out_ref, dg_ref):
        i = pl.program_id(0)
        lat = kv_ref[:, :KL]
        r = lax.rsqrt(jnp.mean(lat * lat, axis=-1, keepdims=True) + NORM_EPS)
        xh = lat * r
        dh = dc_ref[...]
        dy = dh * g_ref[...]
        out_ref[:, :KL] = (r * (dy - xh * jnp.mean(dy * xh, axis=-1, keepdims=True))).astype(BF)
        d = dk_ref[0]
        for h in range(1, H):
            d = d + dk_ref[h]
        out_ref[:, KL:] = (d * c_ref[...] - _swap_halves(d) * s_ref[...]).astype(BF)
        part = jnp.sum(dh * xh, axis=0, keepdims=True)

        @pl.when(i == 0)
        def _():
            dg_ref[...] = part

        @pl.when(i > 0)
        def _():
            dg_ref[...] += part

    tab = pl.BlockSpec((tb, 128), lambda i: (i, 0))
    vec = pl.BlockSpec((1, KL), lambda i: (0, 0))
    return pl.pallas_call(
        body, name="kv_post_bwd", grid=(T // tb,),
        in_specs=[pl.BlockSpec((tb, W), lambda i: (i, 0)), vec, pl.BlockSpec((tb, KL), lambda i: (i, 0)),
                  pl.BlockSpec((H, tb, 128), lambda i: (0, i, 0)), tab, tab],
        out_specs=[pl.BlockSpec((tb, W), lambda i: (i, 0)), vec],
        out_shape=[_sds((T, W), BF), _sds((1, KL), F32)],
        compiler_params=_cparams(("arbitrary",), 32),
    )(kvp, gain, dckv, dkpe_heads, cos, sin)


def _q_up(ql, wqb, cos, sin, tm=512):
    n = wqb.shape[2]
    tm = _tile(ql.shape[0], tm)

    def epi(acc, ex, outs):
        c, s = ex[0][...] * LOG2_SCALE, ex[1][...] * LOG2_SCALE
        for j in range(n // HEAD_PAD):
            a0 = j * HEAD_PAD
            outs[0][:, a0:a0 + NOPE] = (acc[:, a0:a0 + NOPE] * LOG2_SCALE).astype(BF)
            pe = acc[:, a0 + NOPE:a0 + HEAD_PAD]
            outs[0][:, a0 + NOPE:a0 + HEAD_PAD] = (pe * c + _swap_halves(pe) * s).astype(BF)

    tab = pl.BlockSpec((tm, 128), lambda i, j, k: (i, 0))
    return _mm_cols("q_up", ql, wqb, BF, epilogue=epi, extra=[cos, sin], extra_specs=[tab, tab], tm=tm)


def _causal_mask(tb):
    r = lax.broadcasted_iota(jnp.int32, (tb, tb), 0)
    c = lax.broadcasted_iota(jnp.int32, (tb, tb), 1)
    return r, c


HP = 2
LOG2_SCALE = ATTN_SCALE * math.log2(math.e)


def _fill_keys(k_scr, kv_ref, kpe_ref):
    for hh in range(HP):
        k_scr[hh, :, :NOPE] = kv_ref[:, hh * HEAD_PAD:hh * HEAD_PAD + NOPE]
        k_scr[hh, :, NOPE:] = kpe_ref[...]


def _flash_fwd(qp, kvup, kpe, tb=512, comm=None):
    T = qp.shape[0]
    H = qp.shape[1] // HEAD_PAD
    tb = _tile(T, tb)

    def body(q_ref, kv_ref, kpe_ref, o_ref, lse_ref, k_scr):
        iq = pl.program_id(1)

        @pl.when(iq == 0)
        def _():
            _fill_keys(k_scr, kv_ref, kpe_ref)

        qs = [q_ref[:, hh * HEAD_PAD:(hh + 1) * HEAD_PAD] for hh in range(HP)]

        def blk(ik, carry, masked):
            off = pl.multiple_of(ik * tb, tb)
            ss = []
            for hh in range(HP):
                k = k_scr[hh, pl.ds(off, tb), :]
                s = lax.dot_general(qs[hh], k, NT, preferred_element_type=F32)
                if masked:
                    r, c = _causal_mask(tb)
                    s = jnp.where(c <= r, s, NEG)
                ss.append(s)
            ps = []
            for hh in range(HP):
                m, l, acc = carry[hh]
                m2 = jnp.maximum(m, jnp.max(ss[hh], axis=-1, keepdims=True))
                p = jnp.exp2(ss[hh] - m2)
                a = jnp.exp2(m - m2)
                ps.append((m2, a, a * l + jnp.sum(p, axis=-1, keepdims=True), p.astype(BF)))
            out = []
            for hh in range(HP):
                m2, a, l2, p = ps[hh]
                v = kv_ref[pl.ds(off, tb), hh * HEAD_PAD + NOPE:(hh + 1) * HEAD_PAD]
                out.append((m2, l2, a * carry[hh][2] + lax.dot_general(p, v, NN, preferred_element_type=F32)))
            return tuple(out)

        one = (jnp.full((tb, 1), NEG, F32), jnp.zeros((tb, 1), F32), jnp.zeros((tb, VDIM), F32))
        carry = lax.fori_loop(0, iq, lambda ik, cr: blk(ik, cr, False), (one,) * HP)
        carry = blk(iq, carry, True)
        for hh in range(HP):
            m, l, acc = carry[hh]
            o_ref[:, hh * VDIM:(hh + 1) * VDIM] = (acc / l).astype(BF)
            lse_ref[hh] = m + jnp.log2(l)

    return _call(
        "flash_fwd", body, (H // HP, T // tb),
        [pl.BlockSpec((tb, HP * HEAD_PAD), lambda h, i: (i, h)),
         pl.BlockSpec((T, HP * HEAD_PAD), lambda h, i: (0, h)),
         pl.BlockSpec((T, 128), lambda h, i: (0, 0))],
        [pl.BlockSpec((tb, HP * VDIM), lambda h, i: (i, h)),
         pl.BlockSpec((HP, tb, 1), lambda h, i: (h, i, 0))],
        [_sds((T, H * VDIM), BF), _sds((H, T, 1), F32)],
        [pltpu.VMEM((HP, T, HEAD_PAD), BF)], 56, [qp, kvup, kpe], comm)


def _attn_delta(o, do, tb=512):
    T = o.shape[0]
    H = o.shape[1] // VDIM
    tb = _tile(T, tb)

    def body(o_ref, do_ref, d_ref):
        for h in range(H):
            cols = slice(h * VDIM, (h + 1) * VDIM)
            d_ref[h] = jnp.sum(o_ref[:, cols].astype(F32) * do_ref[:, cols].astype(F32), axis=-1, keepdims=True)

    blk = pl.BlockSpec((tb, H * VDIM), lambda i: (i, 0))
    return pl.pallas_call(
        body, name="attn_delta", grid=(T // tb,), in_specs=[blk, blk],
        out_specs=pl.BlockSpec((H, tb, 1), lambda i: (0, i, 0)), out_shape=_sds((H, T, 1), F32),
        compiler_params=_cparams(("arbitrary",), 32),
    )(o, do)


def _flash_bwd(qp, kvup, kpe, do, lse_rows, delta_rows, tb=512, comm=None):
    T = qp.shape[0]
    H = qp.shape[1] // HEAD_PAD
    tb = _tile(T, tb)
    nq = T // tb

    def body(kv_ref, kpe_ref, q_ref, do_ref, lse_ref, dl_ref, dkv_ref, dkpe_ref, dq_ref):
        ik = pl.program_id(1)

        @pl.when(ik == 0)
        def _():
            dq_ref[...] = jnp.zeros_like(dq_ref)

        k = jnp.concatenate([kv_ref[:, :NOPE], kpe_ref[...]], axis=1)
        v = kv_ref[:, NOPE:]

        def blk(iq, carry, masked):
            dk, dv = carry
            off = pl.multiple_of(iq * tb, tb)
            q = q_ref[pl.ds(off, tb), :]
            d = do_ref[pl.ds(off, tb), :]
            st = lax.dot_general(k, q, NT, preferred_element_type=F32)
            if masked:
                r, c = _causal_mask(tb)
                st = jnp.where(r <= c, st, NEG)
            dpt = lax.dot_general(v, d, NT, preferred_element_type=F32)
            pt = jnp.exp2(st - lse_ref[iq])
            dst = (pt * (dpt - dl_ref[iq])).astype(BF)
            dv = dv + lax.dot_general(pt.astype(BF), d, NN, preferred_element_type=F32)
            dk = dk + lax.dot_general(dst, q, NN, preferred_element_type=F32)
            dq_ref[pl.ds(off, tb), :] += lax.dot_general(dst, k, TN, preferred_element_type=F32)
            return dk, dv

        carry = blk(ik, (jnp.zeros((tb, HEAD_PAD), F32), jnp.zeros((tb, VDIM), F32)), True)
        dk, dv = lax.fori_loop(ik + 1, nq, lambda iq, cr: blk(iq, cr, False), carry)
        dk = dk * (ATTN_SCALE / LOG2_SCALE)
        dkv_ref[:, :NOPE] = dk[:, :NOPE].astype(BF)
        dkv_ref[:, NOPE:] = dv.astype(BF)
        dkpe_ref[...] = dk[:, NOPE:]

    rows = pl.BlockSpec((None, nq, 1, tb), lambda h, i: (h, 0, 0, 0))
    return _call(
        "flash_bwd", body, (H, nq),
        [pl.BlockSpec((tb, HEAD_PAD), lambda h, i: (i, h)), pl.BlockSpec((tb, 128), lambda h, i: (i, 0)),
         pl.BlockSpec((T, HEAD_PAD), lambda h, i: (0, h)), pl.BlockSpec((T, VDIM), lambda h, i: (0, h)), rows, rows],
        [pl.BlockSpec((tb, HEAD_PAD), lambda h, i: (i, h)), pl.BlockSpec((None, tb, 128), lambda h, i: (h, i, 0)),
         pl.BlockSpec((T, HEAD_PAD), lambda h, i: (0, h))],
        [_sds((T, H * HEAD_PAD), BF), _sds((H, T, 128), F32), _sds((T, H * HEAD_PAD), F32)],
        [], 56, [kvup, kpe, qp, do, lse_rows, delta_rows], comm)


def _dq_finish(dq_raw, cos, sin, tb=256):
    T, W = dq_raw.shape
    tb = _tile(T, tb)

    def body(dq_ref, c_ref, s_ref, o_ref):
        c, s = c_ref[...] * ATTN_SCALE, s_ref[...] * ATTN_SCALE
        for h in range(W // HEAD_PAD):
            a0 = h * HEAD_PAD
            o_ref[:, a0:a0 + NOPE] = (dq_ref[:, a0:a0 + NOPE] * ATTN_SCALE).astype(BF)
            dpe = dq_ref[:, a0 + NOPE:a0 + HEAD_PAD]
            o_ref[:, a0 + NOPE:a0 + HEAD_PAD] = (dpe * c - _swap_halves(dpe) * s).astype(BF)

    blk = pl.BlockSpec((tb, W), lambda i: (i, 0))
    tab = pl.BlockSpec((tb, 128), lambda i: (i, 0))
    return pl.pallas_call(
        body, name="dq_finish", grid=(T // tb,), in_specs=[blk, tab, tab], out_specs=blk,
        out_shape=_sds((T, W), BF), compiler_params=_cparams(("arbitrary",), 40),
    )(dq_raw, cos, sin)


def _final_loss(x, gain, target, tb=256):
    T, D = x.shape
    tb = _tile(T, tb)

    def body(x_ref, g_ref, t_ref, loss_ref, dx_ref, dg_ref, dxb_ref):
        i = pl.program_id(0)
        xv = x_ref[...]
        gv = g_ref[...]
        r = lax.rsqrt(jnp.mean(xv * xv, axis=-1, keepdims=True) + NORM_EPS)
        xh = xv * r
        e = xh * gv - t_ref[...]
        lpart = 0.5 * jnp.sum(jnp.mean(e * e, axis=-1, keepdims=True), axis=0, keepdims=True)
        dy = e / D
        dyg = dy * gv
        dx = r * (dyg - xh * jnp.mean(dyg * xh, axis=-1, keepdims=True))
        dx_ref[...] = dx
        dxb_ref[...] = dx.astype(BF)
        gpart = jnp.sum(dy * xh, axis=0, keepdims=True)

        @pl.when(i == 0)
        def _():
            loss_ref[...] = lpart
            dg_ref[...] = gpart

        @pl.when(i > 0)
        def _():
            loss_ref[...] += lpart
            dg_ref[...] += gpart

    blk = pl.BlockSpec((tb, D), lambda i: (i, 0))
    vec = pl.BlockSpec((1, D), lambda i: (0, 0))
    return pl.pallas_call(
        body, name="final_loss", grid=(T // tb,), in_specs=[blk, vec, blk],
        out_specs=[pl.BlockSpec((1, 1), lambda i: (0, 0)), blk, vec, blk],
        out_shape=[_sds((1, 1), F32), _sds((T, D), F32), _sds((1, D), F32), _sds((T, D), BF)],
        compiler_params=_cparams(("arbitrary",), 40),
    )(x, gain, target)


def _row_tile(R, pref=256):
    t = (min(R, pref) // 16) * 16
    while t >= 16:
        if R % t == 0:
            return t
        t -= 16
    return R


def _prefetch_call(body, name, grid, in_specs, out_specs, out_shape, scalar, operands, vmem_mb=32):
    return pl.pallas_call(
        body, name=name, out_shape=out_shape,
        grid_spec=pltpu.PrefetchScalarGridSpec(num_scalar_prefetch=1, grid=grid, in_specs=in_specs,
                                               out_specs=out_specs),
        compiler_params=_cparams(("arbitrary",) * len(grid), vmem_mb),
    )(scalar, *operands)


def _place_own(name, s, chip):
    _, n, R, C = s.shape
    tr = _row_tile(R, 512)

    def body(chip_ref, s_ref, o_ref):
        o_ref[...] = s_ref[...]

    return _prefetch_call(
        body, name, (2, n, R // tr),
        [pl.BlockSpec((None, None, tr, C), lambda h, j, r, cr: (h, j, r, 0))],
        pl.BlockSpec((None, None, None, tr, C), lambda h, j, r, cr: (cr[0], h, j, r, 0)),
        _sds((4,) + s.shape, s.dtype), chip, [s])


def _pair_add(name, g, theirs, core):
    _, _, n, R, C = g.shape
    tr = _row_tile(R)

    def body(c_ref, a_ref, b_ref, o_ref):
        o_ref[...] = (a_ref[...].astype(F32) + b_ref[...].astype(F32)).astype(BF)

    blk = pl.BlockSpec((None, None, tr, C), lambda k, j, r, cr: (k, j, r, 0))
    return _prefetch_call(
        body, name, (4, n, R // tr),
        [pl.BlockSpec((None, None, None, tr, C), lambda k, j, r, cr: (k, cr[0], j, r, 0)), blk], blk,
        _sds(theirs.shape, BF), core, [g, theirs])


def _chip_sum(name, sums, landed, chip):
    _, n, R, C = sums.shape
    tr = _row_tile(R)

    def body(chip_ref, own_ref, l_ref, o_ref):
        s = own_ref[...].astype(F32)
        for j in range(3):
            s = s + l_ref[j].astype(F32)
        o_ref[...] = s

    return _prefetch_call(
        body, name, (n, R // tr),
        [pl.BlockSpec((None, None, tr, C), lambda j, r, cr: (cr[0], j, r, 0)),
         pl.BlockSpec((3, None, tr, C), lambda j, r, cr: (0, j, r, 0))],
        pl.BlockSpec((None, tr, C), lambda j, r, cr: (j, r, 0)),
        _sds((n, R, C), F32), chip, [sums, landed])


def _adamw_halves(name, w, m, v, g_mine, g_theirs, core, tr=128):
    L, _, R, C = w.shape
    tr = _row_tile(R, tr)
    c1 = 1.0 - ADAM_B1 ** ADAM_STEP
    c2 = 1.0 - ADAM_B2 ** ADAM_STEP

    def body(c_ref, w_ref, m_ref, v_ref, gm_ref, gt_ref, g_ref, d_ref, nm_ref, nv_ref):
        gv = jnp.where(pl.program_id(1) == c_ref[0], gm_ref[...], gt_ref[...])
        nm = ADAM_B1 * m_ref[...] + (1.0 - ADAM_B1) * gv
        nv = ADAM_B2 * v_ref[...] + (1.0 - ADAM_B2) * (gv * gv)
        g_ref[...] = gv
        nm_ref[...] = nm
        nv_ref[...] = nv
        d_ref[...] = -ADAM_LR * ((nm / c1) / (jnp.sqrt(nv / c2) + ADAM_EPS) + ADAM_WD * w_ref[...])

    full = pl.BlockSpec((None, None, tr, C), lambda l, h, r, cr: (l, h, r, 0))
    half = pl.BlockSpec((None, tr, C), lambda l, h, r, cr: (l, r, 0))
    return _prefetch_call(
        body, name, (L, 2, R // tr), [full, full, full, half, half], [full] * 4,
        [_sds(w.shape, F32)] * 4, core, [w, m, v, g_mine, g_theirs])


def _adamw(name, w, g, m, v, tr=128):
    R, C = w.shape
    tr = _tile(R, tr) if R % 8 == 0 else R
    c1 = 1.0 - ADAM_B1 ** ADAM_STEP
    c2 = 1.0 - ADAM_B2 ** ADAM_STEP

    def body(w_ref, g_ref, m_ref, v_ref, d_ref, nm_ref, nv_ref):
        gv = g_ref[...]
        nm = ADAM_B1 * m_ref[...] + (1.0 - ADAM_B1) * gv
        nv = ADAM_B2 * v_ref[...] + (1.0 - ADAM_B2) * (gv * gv)
        nm_ref[...] = nm
        nv_ref[...] = nv
        d_ref[...] = -ADAM_LR * ((nm / c1) / (jnp.sqrt(nv / c2) + ADAM_EPS) + ADAM_WD * w_ref[...])

    blk = pl.BlockSpec((tr, C), lambda i: (i, 0))
    return pl.pallas_call(
        body, name=name, grid=(R // tr,), in_specs=[blk] * 4, out_specs=[blk] * 3,
        out_shape=[_sds((R, C), F32)] * 3, compiler_params=_cparams(("arbitrary",), 32),
    )(w, g, m, v)


def _place():
    x, y, c = lax.axis_index("x"), lax.axis_index("y"), lax.axis_index("c")
    chips = [(1 - x, y), (x, 1 - y), (1 - x, 1 - y)]
    return x, y, c, chips


def _gather_comm(shards, placed):
    n = len(shards)

    def copies(cin, cout, send, recv):
        src, dst = cin[:n], cout
        x, y, c, chips = _place()
        me = 2 * x + y

        def remote(i, k, s, d, to):
            return pltpu.make_async_remote_copy(src_ref=s, dst_ref=d, send_sem=send.at[6 * i + k],
                                                recv_sem=recv.at[6 * i + k], device_id=to, device_id_type=MESH)

        first = [remote(i, j, src[i].at[c], dst[i].at[me, c], (*chip, c))
                 for i in range(n) for j, chip in enumerate(chips)]
        return remote, first, dst, (x, y, c, chips)

    def start(cin, cout, send, recv):
        for cp in copies(cin, cout, send, recv)[1]:
            cp.start()

    def finish(cin, cout, send, recv):
        remote, first, dst, (x, y, c, chips) = copies(cin, cout, send, recv)
        sibling = (x, y, 1 - c)
        passed = []
        for i in range(n):
            for j, (px, py) in enumerate(chips):
                slot = dst[i].at[2 * px + py, c]
                remote(i, j, slot, slot, (px, py, c)).wait_recv()
                cp = remote(i, 3 + j, slot, slot, sibling)
                cp.start()
                passed.append(cp)
        for i in range(n):
            for j, (px, py) in enumerate(chips):
                slot = dst[i].at[2 * px + py, 1 - c]
                remote(i, 3 + j, slot, slot, sibling).wait_recv()
        for cp in first + passed:
            cp.wait_send()

    return _Comm(list(shards) + list(placed), [_sds(p.shape, p.dtype) for p in placed],
                 {n + i: i for i in range(n)}, 6 * n, start, finish)


def _pair_comm(grads):
    n = len(grads)

    def copies(cin, cout, send, recv):
        x, y, c, _ = _place()
        return [pltpu.make_async_remote_copy(
            src_ref=cin[i].at[k, 1 - c], dst_ref=cout[i].at[k], send_sem=send.at[4 * i + k],
            recv_sem=recv.at[4 * i + k], device_id=(x, y, 1 - c), device_id_type=MESH)
            for i in range(n) for k in range(4)]

    def start(cin, cout, send, recv):
        for cp in copies(cin, cout, send, recv):
            cp.start()

    def finish(cin, cout, send, recv):
        for cp in copies(cin, cout, send, recv):
            cp.wait()

    return _Comm(grads, [_sds((4,) + a.shape[2:], a.dtype) for a in grads], {}, 4 * n, start, finish)


def _chips_comm(sums):
    n = len(sums)

    def copies(cin, cout, send, recv):
        x, y, c, chips = _place()
        return [pltpu.make_async_remote_copy(
            src_ref=cin[i].at[2 * px + py], dst_ref=cout[i].at[j], send_sem=send.at[3 * i + j],
            recv_sem=recv.at[3 * i + j], device_id=(px, py, c), device_id_type=MESH)
            for i in range(n) for j, (px, py) in enumerate(chips)]

    def start(cin, cout, send, recv):
        for cp in copies(cin, cout, send, recv):
            cp.start()

    def finish(cin, cout, send, recv):
        for cp in copies(cin, cout, send, recv):
            cp.wait()

    return _Comm(sums, [_sds((3,) + a.shape[1:], a.dtype) for a in sums], {}, 3 * n, start, finish)


def _share_comm(halves):
    n = len(halves)

    def copies(cin, cout, send, recv):
        x, y, c, _ = _place()
        return [pltpu.make_async_remote_copy(
            src_ref=cin[i], dst_ref=cout[i], send_sem=send.at[i], recv_sem=recv.at[i],
            device_id=(x, y, 1 - c), device_id_type=MESH) for i in range(n)]

    def start(cin, cout, send, recv):
        for cp in copies(cin, cout, send, recv):
            cp.start()

    def finish(cin, cout, send, recv):
        for cp in copies(cin, cout, send, recv):
            cp.wait()

    return _Comm(halves, [_sds(a.shape, a.dtype) for a in halves], {}, n, start, finish)


def _allreduce_small(v):
    R, C = v.shape

    def body(v_ref, out_ref, land, send, recv):
        x, y, c, _ = _place()
        me = 4 * x + 2 * y + c
        cps = []
        for m in range(1, 8):
            fx, fy, fc = (m >> 2) & 1, (m >> 1) & 1, m & 1
            peer = (x ^ fx, y ^ fy, c ^ fc)
            cp = pltpu.make_async_remote_copy(
                src_ref=v_ref, dst_ref=land.at[me], send_sem=send.at[m - 1], recv_sem=recv.at[m - 1],
                device_id=peer, device_id_type=MESH)
            cp.start()
            cps.append(cp)
        land[me] = v_ref[...]
        for cp in cps:
            cp.wait()
        s = land[0]
        for d in range(1, 8):
            s = s + land[d]
        out_ref[...] = s

    return pl.pallas_call(
        body, name="allreduce_small",
        in_specs=[pl.BlockSpec(memory_space=pltpu.VMEM)], out_specs=pl.BlockSpec(memory_space=pltpu.VMEM),
        out_shape=_sds((R, C), F32),
        scratch_shapes=[pltpu.VMEM((8, R, C), F32), pltpu.SemaphoreType.DMA((7,)), pltpu.SemaphoreType.DMA((7,))],
    )(v)


def _halves(a):
    return a.reshape((2, a.shape[0] // 2) + a.shape[1:])


def _canon(a, lead):
    piece = a.shape[lead:]
    return a.reshape(a.shape[:lead] + (int(math.prod(piece[:-2])),) + piece[-2:])


def _rope_tables(pos):
    half = ROPE // 2
    inv_freq = ROPE_BASE ** (-jnp.arange(half, dtype=F32) / half)
    ang = pos.astype(F32)[:, None] * inv_freq
    cos, sin = jnp.cos(ang), jnp.sin(ang)
    return jnp.tile(cos, (1, 4)), jnp.concatenate([-sin, sin, -sin, sin], axis=1)


def kernel(x, positions, pool_norm, pool_w, pool_scale, kv_in_norm, w_kv_a, kv_latent_norm, w_kv_b, attn_norm, w_q_a, q_latent_norm, w_q_b, w_o, ffn_norm, w_gate, w_up, w_down, final_norm, loss_target, m_pool_norm, m_pool_w, m_pool_scale, m_kv_in_norm, m_w_kv_a, m_kv_latent_norm, m_w_kv_b, m_attn_norm, m_w_q_a, m_q_latent_norm, m_w_q_b, m_w_o, m_ffn_norm, m_w_gate, m_w_up, m_w_down, m_final_norm, v_pool_norm, v_pool_w, v_pool_scale, v_kv_in_norm, v_w_kv_a, v_kv_latent_norm, v_w_kv_b, v_attn_norm, v_w_q_a, v_q_latent_norm, v_w_q_b, v_w_o, v_ffn_norm, v_w_gate, v_w_up, v_w_down, v_final_norm):
    T, D = x.shape[1], x.shape[2]
    H = N_HEADS
    KL = kv_latent_norm.shape[0]
    QL = q_latent_norm.shape[1]
    pg = D // 4
    x0, tgt = x[0], loss_target[0]
    cos, sin = _rope_tables(positions[0])
    chip = 2 * lax.axis_index("x") + lax.axis_index("y")

    n_kva = w_kv_a.shape[1]
    wkva_s = jnp.pad(w_kv_a, ((0, 0), (0, KL + 128 - n_kva)))
    hs = w_q_b.shape[2] // (NOPE + ROPE)
    wqb_s = jnp.pad(w_q_b[0].reshape(QL, hs, NOPE + ROPE), ((0, 0), (0, 0), (0, HEAD_PAD - NOPE - ROPE)))
    wqb_s = wqb_s.reshape(QL, hs * HEAD_PAD)
    chip_s = chip.astype(jnp.int32).reshape(1)
    core_s = lax.axis_index("c").astype(jnp.int32).reshape(1)

    def halved(a):
        return _canon(_halves(a.astype(BF)), 1)

    def gather_group(tag, shards):
        placed = [_place_own(f"place_own{tag}_{i}", s, chip_s) for i, s in enumerate(shards)]
        return _gather_comm(shards, placed)

    def whole(a, *shape):
        return a.reshape(shape)

    n_ff = w_gate.shape[2]
    group0 = gather_group("0", [halved(pool_w[0]), _canon(jnp.stack([pool_norm, pool_scale]), 1),
                                halved(w_gate[0]), halved(w_up[0])])
    group1 = gather_group("1", [halved(w_down[0]), halved(wkva_s), halved(w_kv_b), halved(w_q_a[0]),
                                halved(wqb_s), halved(w_o[0])])
    group2 = gather_group("2", [halved(w_gate[1]), halved(w_up[1]), halved(w_down[1])])
    PW, PV, WG0, WU0 = _comm_call("gather0", group0)
    WG0, WU0 = whole(WG0, 4, D, n_ff), whole(WU0, 4, D, n_ff)
    pv = jnp.transpose(PV.reshape(4, 2, pg), (1, 0, 2)).reshape(2, D)
    pn_full, ps_full = pv[0:1], pv[1:2]

    diff = _pool_fwd(x0, pn_full)
    pre, x1 = _pool_mix(diff, PW, ps_full, x0)
    (h1,) = _norm_fwd("norm_ffn0", x1, ffn_norm[0:1])
    (g0, u0, a0), (WD0, WKVA, WKVB, WQA, WQB, WO) = _ffn_up("ffn_up0", h1, WG0, WU0, comm=group1)
    WD0 = whole(WD0, 4, n_ff, D)
    WKVA = whole(WKVA, D, KL + 128)
    WKVB = whole(WKVB, 4, KL, -1)
    WQA = whole(WQA, D, QL)
    WQB = whole(WQB, 4, QL, -1)
    WO = whole(WO, H * VDIM, D)
    x2 = _ffn_down("ffn_down0", a0, WD0, x1)
    hk, ha = _norm_fwd("norm_attn", x2, jnp.stack([kv_in_norm, attn_norm[0]]))
    kvp = _mm_plain("kv_a", hk, WKVA, F32)
    ckv, kpe = _kv_post(kvp, kv_latent_norm[None], cos, sin)
    kvup = _mm_cols("kv_b", ckv, WKVB, BF)
    qa = _mm_plain("q_a", ha, WQA, F32)
    (ql,) = _norm_fwd("norm_q", qa, q_latent_norm)
    qp = _q_up(ql, WQB, cos, sin)
    (o, lse), (WG1, WU1, WD1) = _flash_fwd(qp, kvup, kpe, comm=group2)
    WG1, WU1, WD1 = whole(WG1, 4, D, n_ff), whole(WU1, 4, D, n_ff), whole(WD1, 4, n_ff, D)
    x3 = _mm_plain("attn_out", o, WO, F32, res=x2)
    (h3,) = _norm_fwd("norm_ffn1", x3, ffn_norm[1:2])
    (g1, u1, a1), _ = _ffn_up("ffn_up1", h3, WG1, WU1)
    x4 = _ffn_down("ffn_down1", a1, WD1, x3)
    loss_part, dx4, d_final, dx4b = _final_loss(x4, final_norm[None], tgt)

    def pair_stage(tag, grads):
        full = [_canon(a.reshape((4, 2, a.shape[1] // 2) + a.shape[2:]), 2) for a in grads]
        theirs = _comm_call("reduce_pair" + tag, _pair_comm(full))
        return [_pair_add(f"pair_add{tag}_{i}", a, b, core_s) for i, (a, b) in enumerate(zip(full, theirs))]

    dWD1 = _mm_tn_rows("ffn_dwd1", a1, dx4b, 4, BF)
    dg1, du1 = _ffn_bwd_act("ffn_bwd_act1", dx4b, WD1, g1, u1)
    dWG1 = _mm_tn_cols("ffn_dwg1", h3, dg1, 4, BF)
    dWU1 = _mm_tn_cols("ffn_dwu1", h3, du1, 4, BF)
    sums1 = pair_stage("1", [dWG1, dWU1, dWD1])
    dh3, _ = _ffn_dh("ffn_dh1", dg1, WG1, du1, WU1)
    dx3, d_ffn1, dx3b = _norm_bwd("norm_ffn1_bwd", x3, ffn_norm[1:2], [dh3], dx4)

    do = _mm_nt_plain("attn_out_dx", dx3b, WO, BF)
    dWO = _mm_tn_plain("attn_out_dw", o, dx3b, BF)
    tb = _tile(T, 512)
    lse_rows = lse.reshape(H, T // tb, 1, tb)
    delta_rows = _attn_delta(o, do).reshape(H, T // tb, 1, tb)
    (dkvup, dkpe_h, dq_raw), landed1 = _flash_bwd(qp, kvup, kpe, do, lse_rows, delta_rows,
                                                  comm=_chips_comm(sums1))
    dqp = _dq_finish(dq_raw, cos, sin)
    dql = _mm_nt_cols("q_b_dx", dqp, WQB, F32)
    dWQB = _mm_tn_cols("q_b_dw", ql, dqp, 4, BF)
    _, d_qln, dqa = _norm_bwd("norm_q_bwd", qa, q_latent_norm, [dql], None)
    dha = _mm_nt_plain("q_a_dx", dqa, WQA, F32)
    dWQA = _mm_tn_plain("q_a_dw", ha, dqa, BF)
    dckv = _mm_nt_cols("kv_b_dx", dkvup, WKVB, F32)
    dWKVB = _mm_tn_cols("kv_b_dw", ckv, dkvup, 4, BF)
    dkvp, d_kvln = _kv_post_bwd(kvp, kv_latent_norm[None], dckv, dkpe_h, cos, sin)
    dhk = _mm_nt_plain("kv_a_dx", dkvp, WKVA, F32)
    dWKVA = _mm_tn_plain("kv_a_dw", hk, dkvp, BF)
    dx2, d_n2, dx2b = _norm_bwd("norm_attn_bwd", x2, jnp.stack([kv_in_norm, attn_norm[0]]), [dhk, dha], dx3)

    dWD0 = _mm_tn_rows("ffn_dwd0", a0, dx2b, 4, BF)
    dg0, du0 = _ffn_bwd_act("ffn_bwd_act0", dx2b, WD0, g0, u0)
    dWG0 = _mm_tn_cols("ffn_dwg0", h1, dg0, 4, BF)
    dWU0 = _mm_tn_cols("ffn_dwu0", h1, du0, 4, BF)
    sums0 = pair_stage("0", [dWG0, dWU0, dWD0])
    dh1, landed0 = _ffn_dh("ffn_dh0", dg0, WG0, du0, WU0, comm=_chips_comm(sums0))
    dx1, d_ffn0, _ = _norm_bwd("norm_ffn0_bwd", x1, ffn_norm[0:1], [dh1], dx2)

    dpre, d_ps = _pool_dpre(dx1, pre, ps_full)
    ddiff = _pool_ddiff(dpre, PW)
    dPW = _pool_dw(diff, dpre)
    dx0, d_pn = _pool_bwd(x0, pn_full, ddiff, dx1)

    sums_a = pair_stage("a", [dPW, dWKVA.reshape(4, D // 4, KL + 128), dWKVB, dWQA.reshape(4, D // 4, QL), dWQB,
                              dWO.reshape(4, H * VDIM // 4, D)])
    landed_a = _comm_call("reduce_chips", _chips_comm(sums_a))
    sums = sums_a + sums0 + sums1
    landed = list(landed_a) + list(landed0) + list(landed1)
    halves = [_chip_sum(f"chip_sum{i}", a, b, chip_s) for i, (a, b) in enumerate(zip(sums, landed))]
    other = _comm_call("share_halves", _share_comm(halves))

    def rows(a):
        return a.reshape((1, -1, a.shape[-1]))

    def unpad_kva(a):
        return rows(a)[:, :, :n_kva]

    def unpad_qb(a):
        return rows(a).reshape(1, -1, hs, HEAD_PAD)[:, :, :, :NOPE + ROPE].reshape(1, -1, hs * (NOPE + ROPE))

    def layers(a0_, a1_):
        return jnp.concatenate([rows(a0_), rows(a1_)], axis=0)

    def by_name(hv):
        return {"pool_w": rows(hv[0]), "w_kv_a": unpad_kva(hv[1]), "w_kv_b": rows(hv[2]), "w_q_a": rows(hv[3]),
                "w_q_b": unpad_qb(hv[4]), "w_o": rows(hv[5]), "w_gate": layers(hv[6], hv[9]),
                "w_up": layers(hv[7], hv[10]), "w_down": layers(hv[8], hv[11])}

    big = ["pool_w", "w_kv_a", "w_kv_b", "w_q_a", "w_q_b", "w_o", "w_gate", "w_up", "w_down"]
    g_mine, g_other = by_name(halves), by_name(other)

    lat = jnp.concatenate([d_kvln[0], d_qln[0], jnp.zeros((D - KL - QL,), F32)])
    lrow = jnp.pad(loss_part[0], (0, D - 1))
    small = jnp.stack([d_n2[0], d_n2[1], d_ffn0[0], d_ffn1[0], d_final[0], lat, d_pn[0], d_ps[0], lrow]
                      + [jnp.zeros((D,), F32)] * 7)
    red = _allreduce_small(small)
    loss = red[8, 0]
    g_kv_in, g_attn, g_final = red[0], red[1:2], red[4]
    g_ffn = red[2:4]
    g_kvln, g_qln = red[5, :KL], red[5:6, KL:KL + QL]
    g_pn = lax.dynamic_slice(red[6:7], (0, chip * pg), (1, pg))
    g_ps = lax.dynamic_slice(red[7:8], (0, chip * pg), (1, pg))

    grads = {"pool_norm": g_pn, "pool_scale": g_ps, "kv_in_norm": g_kv_in, "kv_latent_norm": g_kvln,
             "attn_norm": g_attn, "q_latent_norm": g_qln, "ffn_norm": g_ffn, "final_norm": g_final}
    weights = dict(pool_norm=pool_norm, pool_w=pool_w, pool_scale=pool_scale, kv_in_norm=kv_in_norm, w_kv_a=w_kv_a,
                   kv_latent_norm=kv_latent_norm, w_kv_b=w_kv_b, attn_norm=attn_norm, w_q_a=w_q_a,
                   q_latent_norm=q_latent_norm, w_q_b=w_q_b, w_o=w_o, ffn_norm=ffn_norm, w_gate=w_gate, w_up=w_up,
                   w_down=w_down, final_norm=final_norm)
    ms = dict(pool_norm=m_pool_norm, pool_w=m_pool_w, pool_scale=m_pool_scale, kv_in_norm=m_kv_in_norm,
              w_kv_a=m_w_kv_a, kv_latent_norm=m_kv_latent_norm, w_kv_b=m_w_kv_b, attn_norm=m_attn_norm,
              w_q_a=m_w_q_a, q_latent_norm=m_q_latent_norm, w_q_b=m_w_q_b, w_o=m_w_o, ffn_norm=m_ffn_norm,
              w_gate=m_w_gate, w_up=m_w_up, w_down=m_w_down, final_norm=m_final_norm)
    vs = dict(pool_norm=v_pool_norm, pool_w=v_pool_w, pool_scale=v_pool_scale, kv_in_norm=v_kv_in_norm,
              w_kv_a=v_w_kv_a, kv_latent_norm=v_kv_latent_norm, w_kv_b=v_w_kv_b, attn_norm=v_attn_norm,
              w_q_a=v_w_q_a, q_latent_norm=v_q_latent_norm, w_q_b=v_w_q_b, w_o=v_w_o, ffn_norm=v_ffn_norm,
              w_gate=v_w_gate, w_up=v_w_up, w_down=v_w_down, final_norm=v_final_norm)
    names = list(weights)

    def as2d(a):
        return a.reshape((-1, a.shape[-1]))

    delta_w, new_m, new_v = {}, {}, {}
    for nm in big:
        n_layers = g_mine[nm].shape[0]

        def two_halves(a):
            a = as2d(a)
            return a.reshape(n_layers, 2, a.shape[0] // (2 * n_layers), a.shape[1])

        g, d, m2, v2 = _adamw_halves("adamw_" + nm, two_halves(weights[nm]), two_halves(ms[nm]),
                                     two_halves(vs[nm]), g_mine[nm], g_other[nm], core_s)
        shp = weights[nm].shape
        grads[nm], delta_w[nm], new_m[nm], new_v[nm] = g.reshape(shp), d.reshape(shp), m2.reshape(shp), v2.reshape(shp)
    groups = [["kv_in_norm", "attn_norm", "ffn_norm", "final_norm"], ["kv_latent_norm", "q_latent_norm"],
              ["pool_norm", "pool_scale"]]
    for gi, grp in enumerate(groups):
        cat = lambda t: jnp.concatenate([as2d(t[nm]) for nm in grp], axis=0)
        d, m2, v2 = _adamw(f"adamw_vec{gi}", cat(weights), cat(grads), cat(ms), cat(vs))
        r0 = 0
        for nm in grp:
            shp = weights[nm].shape
            r = as2d(weights[nm]).shape[0]
            delta_w[nm], new_m[nm], new_v[nm] = (d[r0:r0 + r].reshape(shp), m2[r0:r0 + r].reshape(shp),
                                                 v2[r0:r0 + r].reshape(shp))
            r0 += r

    return (loss, dx0[None], *[grads[nm].reshape(weights[nm].shape) for nm in names],
            *[delta_w[nm] for nm in names], *[new_m[nm] for nm in names], *[new_v[nm] for nm in names])
```

```python
import functools
import math

import jax
import jax.numpy as jnp
from jax import lax
from jax.experimental import pallas as pl
from jax.experimental.pallas import tpu as pltpu

BF = jnp.bfloat16
F32 = jnp.float32
MESH = pl.DeviceIdType.MESH

N_HEADS = 16
NOPE = 128
ROPE = 64
VDIM = 128
HEAD_PAD = 256
ROPE_BASE = 10000.0
ATTN_SCALE = 1.0 / math.sqrt(NOPE + ROPE)
POOL_WINDOWS = (2, 4, 8, 16)
HALO = 16
NORM_EPS = 1e-6
ADAM_LR, ADAM_B1, ADAM_B2, ADAM_EPS, ADAM_WD, ADAM_STEP = 0.001, 0.9, 0.999, 1e-08, 0.01, 10
NEG = -1e30

V7X_VMEM_BYTES = 64 * 1024 * 1024
VMEM_CEILING = V7X_VMEM_BYTES - 8 * 1024 * 1024

NN = (((1,), (0,)), ((), ()))
NT = (((1,), (1,)), ((), ()))
TN = (((0,), (0,)), ((), ()))


def _cparams(sem, vmem_mb):
    return pltpu.CompilerParams(dimension_semantics=sem,
                                vmem_limit_bytes=min(vmem_mb * 1024 * 1024, VMEM_CEILING))


def _tile(n, pref):
    if n <= pref:
        return n
    t = (pref // 128) * 128
    while t > 128 and n % t:
        t -= 128
    assert n % t == 0, (n, pref)
    return t


def _sds(shape, dtype):
    return jax.ShapeDtypeStruct(shape, dtype)


ANY = pl.BlockSpec(memory_space=pl.ANY)


class _Comm:
    def __init__(self, ins, out_shapes, alias, n_sems, start, finish):
        self.ins, self.out_shapes, self.alias, self.n_sems = list(ins), list(out_shapes), dict(alias), n_sems
        self.start, self.finish = start, finish


def _call(name, body, grid, in_specs, out_specs, out_shape, scratch, vmem_mb, operands, comm=None):
    in_specs, out_specs, out_shape, scratch = list(in_specs), list(out_specs), list(out_shape), list(scratch)
    params = _cparams(("arbitrary",) * len(grid), vmem_mb)
    if comm is None:
        res = pl.pallas_call(body, name=name, grid=grid, in_specs=in_specs, out_specs=out_specs,
                             out_shape=out_shape, scratch_shapes=scratch, compiler_params=params)(*operands)
        return list(res), []
    n_in, n_out, n_scr = len(in_specs), len(out_specs), len(scratch)
    c_in, c_out = len(comm.ins), len(comm.out_shapes)

    def hosted(*refs):
        ins, cin = refs[:n_in], refs[n_in:n_in + c_in]
        o0 = n_in + c_in
        outs, cout = refs[o0:o0 + n_out], refs[o0 + n_out:o0 + n_out + c_out]
        s0 = o0 + n_out + c_out
        scr, (send, recv) = refs[s0:s0 + n_scr], refs[s0 + n_scr:]
        ids = [pl.program_id(d) for d in range(len(grid))]
        first = functools.reduce(jnp.logical_and, [i == 0 for i in ids])
        last = functools.reduce(jnp.logical_and, [i == g - 1 for i, g in zip(ids, grid)])

        @pl.when(first)
        def _():
            comm.start(cin, cout, send, recv)

        body(*ins, *outs, *scr)

        @pl.when(last)
        def _():
            comm.finish(cin, cout, send, recv)

    res = pl.pallas_call(
        hosted, name=name, grid=grid, in_specs=in_specs + [ANY] * c_in, out_specs=out_specs + [ANY] * c_out,
        out_shape=out_shape + comm.out_shapes,
        scratch_shapes=scratch + [pltpu.SemaphoreType.DMA((comm.n_sems,)), pltpu.SemaphoreType.DMA((comm.n_sems,))],
        input_output_aliases={n_in + k: n_out + v for k, v in comm.alias.items()}, compiler_params=params,
    )(*operands, *comm.ins)
    return list(res[:n_out]), list(res[n_out:])


def _comm_call(name, comm):
    c_in, c_out = len(comm.ins), len(comm.out_shapes)

    def body(*refs):
        cin, cout = refs[:c_in], refs[c_in:c_in + c_out]
        send, recv = refs[c_in + c_out:]
        comm.start(cin, cout, send, recv)
        comm.finish(cin, cout, send, recv)

    return pl.pallas_call(
        body, name=name, in_specs=[ANY] * c_in, out_specs=[ANY] * c_out, out_shape=comm.out_shapes,
        scratch_shapes=[pltpu.SemaphoreType.DMA((comm.n_sems,)), pltpu.SemaphoreType.DMA((comm.n_sems,))],
        input_output_aliases=dict(comm.alias),
    )(*comm.ins)


def _matmul(name, pairs, pair_specs, dims, grid, nk, extra, extra_specs, out_shapes, out_specs,
            acc_shape, epilogue, vmem_mb, comm=None):
    n_p, n_e, n_o = len(pairs) // 2, len(extra), len(out_shapes)

    def body(*refs):
        ab = refs[:2 * n_p]
        ex = refs[2 * n_p:2 * n_p + n_e]
        outs = refs[2 * n_p + n_e:2 * n_p + n_e + n_o]

        def partial_sum():
            tot = None
            for p in range(n_p):
                a = ab[2 * p][...]
                b = ab[2 * p + 1][...]
                if a.ndim > 2:
                    a = a.reshape(-1, a.shape[-1])
                if b.ndim > 2:
                    b = b.reshape(-1, b.shape[-1])
                d = lax.dot_general(a.astype(BF), b.astype(BF), dims, preferred_element_type=F32)
                tot = d if tot is None else tot + d
            return tot

        if nk == 1:
            epilogue(partial_sum(), ex, outs)
        else:
            acc = refs[-1]
            kk = pl.program_id(2)

            @pl.when(kk == 0)
            def _():
                acc[...] = partial_sum()

            @pl.when(kk > 0)
            def _():
                acc[...] += partial_sum()

            @pl.when(kk == nk - 1)
            def _():
                epilogue(acc[...], ex, outs)

    scratch = [] if nk == 1 else [pltpu.VMEM(acc_shape, F32)]
    res, comm_res = _call(name, body, grid, list(pair_specs) + list(extra_specs), out_specs, out_shapes,
                          scratch, vmem_mb, list(pairs) + list(extra), comm)
    return res if comm is None else (res, comm_res)


def _epi_store(acc, ex, outs):
    outs[0][...] = acc.reshape(outs[0].shape).astype(outs[0].dtype)


def _epi_residual(acc, ex, outs):
    outs[0][...] = (acc + ex[0][...]).astype(outs[0].dtype)


def _swap_halves(x):
    lane = lax.broadcasted_iota(jnp.int32, x.shape, 1)
    return jnp.where((lane % 64) < 32, pltpu.roll(x, 96, 1), pltpu.roll(x, 32, 1))


def _mm_plain(name, a, b, out_dtype, res=None, tm=512, tn=1024):
    M, K = a.shape
    N = b.shape[1]
    tm, tn = _tile(M, tm), _tile(N, tn)
    extra, extra_specs, epi = [], [], _epi_store
    if res is not None:
        extra, extra_specs, epi = [res], [pl.BlockSpec((tm, tn), lambda i, j, k: (i, j))], _epi_residual
    return _matmul(
        name, [a, b],
        [pl.BlockSpec((tm, K), lambda i, j, k: (i, 0)), pl.BlockSpec((K, tn), lambda i, j, k: (0, j))],
        NN, (M // tm, N // tn, 1), 1, extra, extra_specs,
        [_sds((M, N), out_dtype)], [pl.BlockSpec((tm, tn), lambda i, j, k: (i, j))],
        None, epi, 40)[0]


def _mm_cols(name, a, b3, out_dtype, epilogue=_epi_store, extra=(), extra_specs=(), tm=512):
    M, K = a.shape
    G, _, n = b3.shape
    tm = _tile(M, tm)
    return _matmul(
        name, [a, b3],
        [pl.BlockSpec((tm, K), lambda i, j, k: (i, 0)), pl.BlockSpec((None, K, n), lambda i, j, k: (j, 0, 0))],
        NN, (M // tm, G, 1), 1, list(extra), list(extra_specs),
        [_sds((M, G * n), out_dtype)], [pl.BlockSpec((tm, n), lambda i, j, k: (i, j))],
        None, epilogue, 40)[0]


def _mm_nt_plain(name, a, b, out_dtype, tm=512, tn=1024):
    M, K = a.shape
    N = b.shape[0]
    tm, tn = _tile(M, tm), _tile(N, tn)
    return _matmul(
        name, [a, b],
        [pl.BlockSpec((tm, K), lambda i, j, k: (i, 0)), pl.BlockSpec((tn, K), lambda i, j, k: (j, 0))],
        NT, (M // tm, N // tn, 1), 1, [], [],
        [_sds((M, N), out_dtype)], [pl.BlockSpec((tm, tn), lambda i, j, k: (i, j))],
        None, _epi_store, 44)[0]


def _mm_nt_cols(name, a, b3, out_dtype, tm=512):
    M = a.shape[0]
    G, K, n = b3.shape
    tm = _tile(M, tm)
    return _matmul(
        name, [a, b3],
        [pl.BlockSpec((tm, n), lambda i, j, k: (i, k)), pl.BlockSpec((None, K, n), lambda i, j, k: (k, 0, 0))],
        NT, (M // tm, 1, G), G, [], [],
        [_sds((M, K), out_dtype)], [pl.BlockSpec((tm, K), lambda i, j, k: (i, 0))],
        (tm, K), _epi_store, 40)[0]


def _mm_tn_plain(name, a, b, out_dtype, tt=512, tn=1024):
    T, K = a.shape
    N = b.shape[1]
    tt, tn = _tile(T, tt), _tile(N, tn)
    return _matmul(
        name, [a, b],
        [pl.BlockSpec((tt, K), lambda i, j, k: (k, 0)), pl.BlockSpec((tt, tn), lambda i, j, k: (k, j))],
        TN, (1, N // tn, T // tt), T // tt, [], [],
        [_sds((K, N), out_dtype)], [pl.BlockSpec((K, tn), lambda i, j, k: (0, j))],
        (K, tn), _epi_store, 48)[0]


def _mm_tn_cols(name, a, b, G, out_dtype, tt=1024):
    T, K = a.shape
    n = b.shape[1] // G
    tt = _tile(T, tt)
    return _matmul(
        name, [a, b],
        [pl.BlockSpec((tt, K), lambda i, j, k: (k, 0)), pl.BlockSpec((tt, n), lambda i, j, k: (k, j))],
        TN, (1, G, T // tt), T // tt, [], [],
        [_sds((G, K, n), out_dtype)], [pl.BlockSpec((None, K, n), lambda i, j, k: (j, 0, 0))],
        (K, n), _epi_store, 48)[0]


def _norm_fwd(name, x, gains, tb=512):
    T, D = x.shape
    G = gains.shape[0]
    tb = _tile(T, tb)

    def body(x_ref, g_ref, *outs):
        xv = x_ref[...]
        xh = xv * lax.rsqrt(jnp.mean(xv * xv, axis=-1, keepdims=True) + NORM_EPS)
        for g in range(G):
            outs[g][...] = (xh * g_ref[g:g + 1, :]).astype(BF)

    return pl.pallas_call(
        body, name=name, grid=(T // tb,),
        in_specs=[pl.BlockSpec((tb, D), lambda i: (i, 0)), pl.BlockSpec((G, D), lambda i: (0, 0))],
        out_specs=[pl.BlockSpec((tb, D), lambda i: (i, 0))] * G,
        out_shape=[_sds((T, D), BF)] * G,
        compiler_params=_cparams(("arbitrary",), 40),
    )(x, gains)


def _norm_bwd(name, x, gains, dhs, dres, tb=256):
    T, D = x.shape
    G = gains.shape[0]
    tb = _tile(T, tb)
    has_res = dres is not None

    def body(*refs):
        x_ref, g_ref = refs[0], refs[1]
        dh_refs = refs[2:2 + G]
        res_ref = refs[2 + G] if has_res else None
        dx_ref, dg_ref, dxb_ref = refs[-3], refs[-2], refs[-1]
        i = pl.program_id(0)
        xv = x_ref[...]
        r = lax.rsqrt(jnp.mean(xv * xv, axis=-1, keepdims=True) + NORM_EPS)
        xh = xv * r
        dx = res_ref[...] if has_res else jnp.zeros_like(xv)
        rows = []
        for g in range(G):
            dh = dh_refs[g][...].astype(F32)
            dy = dh * g_ref[g:g + 1, :]
            dx = dx + r * (dy - xh * jnp.mean(dy * xh, axis=-1, keepdims=True))
            rows.append(jnp.sum(dh * xh, axis=0, keepdims=True))
        dx_ref[...] = dx
        dxb_ref[...] = dx.astype(BF)

        @pl.when(i == 0)
        def _():
            for g in range(G):
                dg_ref[g:g + 1, :] = rows[g]

        @pl.when(i > 0)
        def _():
            for g in range(G):
                dg_ref[g:g + 1, :] += rows[g]

    blk = pl.BlockSpec((tb, D), lambda i: (i, 0))
    ins = [x, gains] + list(dhs) + ([dres] if has_res else [])
    in_specs = [blk, pl.BlockSpec((G, D), lambda i: (0, 0))] + [blk] * (G + (1 if has_res else 0))
    return pl.pallas_call(
        body, name=name, grid=(T // tb,), in_specs=in_specs,
        out_specs=[blk, pl.BlockSpec((G, D), lambda i: (0, 0)), blk],
        out_shape=[_sds((T, D), F32), _sds((G, D), F32), _sds((T, D), BF)],
        compiler_params=_cparams(("arbitrary",), 48),
    )(*ins)


def _pool_fwd(x, gain, tb=256):
    T, D = x.shape
    tb = _tile(T, tb)
    pg = D // len(POOL_WINDOWS)
    per = tb // HALO

    def body(x_ref, xp_ref, g_ref, diff_ref):
        i = pl.program_id(0)
        xx = jnp.concatenate([xp_ref[...], x_ref[...]], axis=0)
        h = xx * lax.rsqrt(jnp.mean(xx * xx, axis=-1, keepdims=True) + NORM_EPS) * g_ref[...]
        row = lax.broadcasted_iota(jnp.int32, (HALO + tb, 1), 0)
        h = jnp.where((row >= HALO) | (i > 0), h, 0.0)
        t = i * tb + row[HALO:] - HALO
        for g, w in enumerate(POOL_WINDOWS):
            hg = h[:, g * pg:(g + 1) * pg]
            s, k = hg, 1
            while k < w:
                s = s + pltpu.roll(s, k, 0)
                k *= 2
            cnt = jnp.minimum(t + 1, w).astype(F32)
            diff_ref[:, g * pg:(g + 1) * pg] = (s[HALO:] / cnt - hg[HALO:]).astype(BF)

    return pl.pallas_call(
        body, name="pool_fwd", grid=(T // tb,),
        in_specs=[pl.BlockSpec((tb, D), lambda i: (i, 0)),
                  pl.BlockSpec((HALO, D), lambda i: (jnp.maximum(i * per - 1, 0), 0)),
                  pl.BlockSpec((1, D), lambda i: (0, 0))],
        out_specs=pl.BlockSpec((tb, D), lambda i: (i, 0)),
        out_shape=_sds((T, D), BF),
        compiler_params=_cparams(("arbitrary",), 40),
    )(x, x, gain)


def _pool_bwd(x, gain, ddiff, dres, tb=256):
    T, D = x.shape
    tb = _tile(T, tb)
    pg = D // len(POOL_WINDOWS)
    per = tb // HALO
    nblk = T // HALO

    def body(x_ref, g_ref, dd_ref, ddn_ref, res_ref, dx_ref, dg_ref):
        i = pl.program_id(0)
        last = i == T // tb - 1
        dd = jnp.concatenate([dd_ref[...], ddn_ref[...]], axis=0)
        row = lax.broadcasted_iota(jnp.int32, (tb + HALO, 1), 0)
        dd = jnp.where((row < tb) | jnp.logical_not(last), dd, 0.0)
        t = i * tb + row
        parts = []
        for g, w in enumerate(POOL_WINDOWS):
            dg_ = dd[:, g * pg:(g + 1) * pg]
            e = dg_ / jnp.minimum(t + 1, w).astype(F32)
            s, k = e, 1
            while k < w:
                s = s + pltpu.roll(s, tb + HALO - k, 0)
                k *= 2
            parts.append(s[:tb] - dg_[:tb])
        dh = jnp.concatenate(parts, axis=1)
        xv = x_ref[...]
        r = lax.rsqrt(jnp.mean(xv * xv, axis=-1, keepdims=True) + NORM_EPS)
        xh = xv * r
        dy = dh * g_ref[...]
        dx_ref[...] = res_ref[...] + r * (dy - xh * jnp.mean(dy * xh, axis=-1, keepdims=True))
        part = jnp.sum(dh * xh, axis=0, keepdims=True)

        @pl.when(i == 0)
        def _():
            dg_ref[...] = part

        @pl.when(i > 0)
        def _():
            dg_ref[...] += part

    blk = pl.BlockSpec((tb, D), lambda i: (i, 0))
    return pl.pallas_call(
        body, name="pool_bwd", grid=(T // tb,),
        in_specs=[blk, pl.BlockSpec((1, D), lambda i: (0, 0)), blk,
                  pl.BlockSpec((HALO, D), lambda i: (jnp.minimum((i + 1) * per, nblk - 1), 0)), blk],
        out_specs=[blk, pl.BlockSpec((1, D), lambda i: (0, 0))],
        out_shape=[_sds((T, D), F32), _sds((1, D), F32)],
        compiler_params=_cparams(("arbitrary",), 48),
    )(x, gain, ddiff, ddiff, dres)


def _pool_w_spec(pg):
    return pl.BlockSpec((4, None, None, pg // 4, pg), lambda i, j, k: (0, j // 2, j % 2, 0, 0))


def _pool_mix(diff, pw, scale, x, tm=512):
    T, D = x.shape
    pg = D // 4
    tm = _tile(T, tm)

    def epi(acc, ex, outs):
        outs[0][...] = acc
        outs[1][...] = ex[1][...] + acc * ex[0][...]

    blk = pl.BlockSpec((tm, pg), lambda i, j, k: (i, j))
    return _matmul(
        "pool_mix", [diff, pw], [blk, _pool_w_spec(pg)], NN, (T // tm, 4, 1), 1,
        [scale, x], [pl.BlockSpec((1, pg), lambda i, j, k: (0, j)), blk],
        [_sds((T, D), F32), _sds((T, D), F32)], [blk, blk], None, epi, 32)


def _pool_dpre(dx, pre, scale, tb=512):
    T, D = dx.shape
    tb = _tile(T, tb)

    def body(dx_ref, pre_ref, s_ref, dpre_ref, ds_ref):
        i = pl.program_id(0)
        d = dx_ref[...]
        dpre_ref[...] = (d * s_ref[...]).astype(BF)
        part = jnp.sum(d * pre_ref[...], axis=0, keepdims=True)

        @pl.when(i == 0)
        def _():
            ds_ref[...] = part

        @pl.when(i > 0)
        def _():
            ds_ref[...] += part

    blk = pl.BlockSpec((tb, D), lambda i: (i, 0))
    vec = pl.BlockSpec((1, D), lambda i: (0, 0))
    return pl.pallas_call(
        body, name="pool_dpre", grid=(T // tb,), in_specs=[blk, blk, vec], out_specs=[blk, vec],
        out_shape=[_sds((T, D), BF), _sds((1, D), F32)],
        compiler_params=_cparams(("arbitrary",), 40),
    )(dx, pre, scale)


def _pool_ddiff(dpre, pw, tm=512):
    T, D = dpre.shape
    pg = D // 4
    tm = _tile(T, tm)
    blk = pl.BlockSpec((tm, pg), lambda i, j, k: (i, j))
    return _matmul("pool_ddiff", [dpre, pw], [blk, _pool_w_spec(pg)], NT, (T // tm, 4, 1), 1, [], [],
                   [_sds((T, D), F32)], [blk], None, _epi_store, 32)[0]


def _pool_dw(diff, dpre, tt=512):
    T, D = diff.shape
    pg = D // 4
    tt = _tile(T, tt)
    blk = pl.BlockSpec((tt, pg), lambda i, j, k: (k, j))
    return _matmul("pool_dw", [diff, dpre], [blk, blk], TN, (1, 4, T // tt), T // tt, [], [],
                   [_sds((4, 2, 2, pg // 4, pg), BF)], [_pool_w_spec(pg)], (pg, pg), _epi_store, 32)[0]


def _ffn_up(name, h, wg, wu, tm=512, comm=None):
    T, D = h.shape
    n = wg.shape[-1]
    F = 4 * n
    tm = _tile(T, tm)

    def body(h_ref, wg_ref, wu_ref, g_ref, u_ref, a_ref):
        hv = h_ref[...]
        g = jnp.dot(hv, wg_ref[...], preferred_element_type=F32)
        u = jnp.dot(hv, wu_ref[...], preferred_element_type=F32)
        g_ref[...] = g.astype(BF)
        u_ref[...] = u.astype(BF)
        a_ref[...] = (g * jax.nn.sigmoid(g) * u).astype(BF)

    w_spec = pl.BlockSpec((None, D, n), lambda j, i: (j, 0, 0))
    o_spec = pl.BlockSpec((tm, n), lambda j, i: (i, j))
    return _call(name, body, (4, T // tm), [pl.BlockSpec((tm, D), lambda j, i: (i, 0)), w_spec, w_spec],
                 [o_spec] * 3, [_sds((T, F), BF)] * 3, [], 56, [h, wg, wu], comm)


def _ffn_down(name, a, wd, x, tm=1024, tn=1024):
    T, D = x.shape
    n = wd.shape[1]
    tm, tn = _tile(T, tm), _tile(D, tn)
    return _matmul(
        name, [a, wd],
        [pl.BlockSpec((tm, n), lambda i, j, k: (i, k)),
         pl.BlockSpec((None, n, tn), lambda i, j, k: (k, 0, j))],
        NN, (T // tm, D // tn, 4), 4, [x], [pl.BlockSpec((tm, tn), lambda i, j, k: (i, j))],
        [_sds((T, D), F32)], [pl.BlockSpec((tm, tn), lambda i, j, k: (i, j))],
        (tm, tn), _epi_residual, 40)[0]


def _ffn_bwd_act(name, dx, wd, g, u, tm=512):
    T, D = dx.shape
    n = wd.shape[1]
    tm = _tile(T, tm)

    def body(dx_ref, w_ref, g_ref, u_ref, dg_ref, du_ref):
        da = lax.dot_general(dx_ref[...], w_ref[...], NT, preferred_element_type=F32)
        gv = g_ref[...].astype(F32)
        uv = u_ref[...].astype(F32)
        sig = jax.nn.sigmoid(gv)
        dg_ref[...] = (da * uv * (sig * (1.0 + gv * (1.0 - sig)))).astype(BF)
        du_ref[...] = (da * (gv * sig)).astype(BF)

    blk = pl.BlockSpec((tm, n), lambda i, j: (i, j))
    return pl.pallas_call(
        body, name=name, grid=(T // tm, 4),
        in_specs=[pl.BlockSpec((tm, D), lambda i, j: (i, 0)), pl.BlockSpec((None, n, D), lambda i, j: (j, 0, 0)),
                  blk, blk],
        out_specs=[blk, blk], out_shape=[_sds((T, 4 * n), BF)] * 2,
        compiler_params=_cparams(("arbitrary", "arbitrary"), 56),
    )(dx, wd, g, u)


def _ffn_dh(name, dg, wg, du, wu, tm=512, comm=None):
    T = dg.shape[0]
    D, n = wg.shape[1], wg.shape[2]
    tm = _tile(T, tm)
    a_spec = pl.BlockSpec((tm, n), lambda i, j, k: (i, k))
    w_spec = pl.BlockSpec((None, D, n), lambda i, j, k: (k, 0, 0))
    r = _matmul(
        name, [dg, wg, du, wu], [a_spec, w_spec, a_spec, w_spec], NT, (T // tm, 1, 4), 4, [], [],
        [_sds((T, D), F32)], [pl.BlockSpec((tm, D), lambda i, j, k: (i, 0))], (tm, D), _epi_store, 56, comm)
    return (r[0], []) if comm is None else (r[0][0], r[1])


def _mm_tn_rows(name, a, b, G, out_dtype, tt=1024, tn=1024):
    T, N = b.shape
    n = a.shape[1] // G
    tt, tn = _tile(T, tt), _tile(N, tn)
    nj = N // tn
    return _matmul(
        name, [a, b],
        [pl.BlockSpec((tt, n), lambda i, j, k: (k, j // nj)), pl.BlockSpec((tt, tn), lambda i, j, k: (k, j % nj))],
        TN, (1, G * nj, T // tt), T // tt, [], [],
        [_sds((G, n, N), out_dtype)], [pl.BlockSpec((None, n, tn), lambda i, j, k: (j // nj, 0, j % nj))],
        (n, tn), _epi_store, 48)[0]


def _kv_post(kvp, gain, cos, sin, tb=512):
    T, W = kvp.shape
    KL = W - 128
    tb = _tile(T, tb)

    def body(kv_ref, g_ref, c_ref, s_ref, ckv_ref, kpe_ref):
        lat = kv_ref[:, :KL]
        ckv_ref[...] = (lat * lax.rsqrt(jnp.mean(lat * lat, axis=-1, keepdims=True) + NORM_EPS)
                        * g_ref[...]).astype(BF)
        pe = kv_ref[:, KL:]
        kpe_ref[...] = (pe * c_ref[...] + _swap_halves(pe) * s_ref[...]).astype(BF)

    tab = pl.BlockSpec((tb, 128), lambda i: (i, 0))
    return pl.pallas_call(
        body, name="kv_post", grid=(T // tb,),
        in_specs=[pl.BlockSpec((tb, W), lambda i: (i, 0)), pl.BlockSpec((1, KL), lambda i: (0, 0)), tab, tab],
        out_specs=[pl.BlockSpec((tb, KL), lambda i: (i, 0)), tab],
        out_shape=[_sds((T, KL), BF), _sds((T, 128), BF)],
        compiler_params=_cparams(("arbitrary",), 32),
    )(kvp, gain, cos, sin)


def _kv_post_bwd(kvp, gain, dckv, dkpe_heads, cos, sin, tb=256):
    T, W = kvp.shape
    KL = W - 128
    H = dkpe_heads.shape[0]
    tb = _tile(T, tb)

    def body(kv_ref, g_ref, dc_ref, dk_ref, c_ref, s_ref, out_ref, dg_ref):
        i = pl.program_id(0)
        lat = kv_ref[:, :KL]
        r = lax.rsqrt(jnp.mean(lat * lat, axis=-1, keepdims=True) + NORM_EPS)
        xh = lat * r
        dh = dc_ref[...]
        dy = dh * g_ref[...]
        out_ref[:, :KL] = (r * (dy - xh * jnp.mean(dy * xh, axis=-1, keepdims=True))).astype(BF)
        d = dk_ref[0]
        for h in range(1, H):
            d = d + dk_ref[h]
        out_ref[:, KL:] = (d * c_ref[...] - _swap_halves(d) * s_ref[...]).astype(BF)
        part = jnp.sum(dh * xh, axis=0, keepdims=True)

        @pl.when(i == 0)
        def _():
            dg_ref[...] = part

        @pl.when(i > 0)
        def _():
            dg_ref[...] += part

    tab = pl.BlockSpec((tb, 128), lambda i: (i, 0))
    vec = pl.BlockSpec((1, KL), lambda i: (0, 0))
    return pl.pallas_call(
        body, name="kv_post_bwd", grid=(T // tb,),
        in_specs=[pl.BlockSpec((tb, W), lambda i: (i, 0)), vec, pl.BlockSpec((tb, KL), lambda i: (i, 0)),
                  pl.BlockSpec((H, tb, 128), lambda i: (0, i, 0)), tab, tab],
        out_specs=[pl.BlockSpec((tb, W), lambda i: (i, 0)), vec],
        out_shape=[_sds((T, W), BF), _sds((1, KL), F32)],
        compiler_params=_cparams(("arbitrary",), 32),
    )(kvp, gain, dckv, dkpe_heads, cos, sin)


def _q_up(ql, wqb, cos, sin, tm=512):
    n = wqb.shape[2]
    tm = _tile(ql.shape[0], tm)

    def epi(acc, ex, outs):
        c, s = ex[0][...] * LOG2_SCALE, ex[1][...] * LOG2_SCALE
        for j in range(n // HEAD_PAD):
            a0 = j * HEAD_PAD
            outs[0][:, a0:a0 + NOPE] = (acc[:, a0:a0 + NOPE] * LOG2_SCALE).astype(BF)
            pe = acc[:, a0 + NOPE:a0 + HEAD_PAD]
            outs[0][:, a0 + NOPE:a0 + HEAD_PAD] = (pe * c + _swap_halves(pe) * s).astype(BF)

    tab = pl.BlockSpec((tm, 128), lambda i, j, k: (i, 0))
    return _mm_cols("q_up", ql, wqb, BF, epilogue=epi, extra=[cos, sin], extra_specs=[tab, tab], tm=tm)


def _causal_mask(tb):
    r = lax.broadcasted_iota(jnp.int32, (tb, tb), 0)
    c = lax.broadcasted_iota(jnp.int32, (tb, tb), 1)
    return r, c


HP = 2
LOG2_SCALE = ATTN_SCALE * math.log2(math.e)


def _fill_keys(k_scr, kv_ref, kpe_ref):
    for hh in range(HP):
        k_scr[hh, :, :NOPE] = kv_ref[:, hh * HEAD_PAD:hh * HEAD_PAD + NOPE]
        k_scr[hh, :, NOPE:] = kpe_ref[...]


def _flash_fwd(qp, kvup, kpe, tb=512, comm=None):
    T = qp.shape[0]
    H = qp.shape[1] // HEAD_PAD
    tb = _tile(T, tb)

    def body(q_ref, kv_ref, kpe_ref, o_ref, lse_ref, k_scr, m_scr, l_scr, acc_scr):
        iq = pl.program_id(1)

        @pl.when(iq == 0)
        def _():
            _fill_keys(k_scr, kv_ref, kpe_ref)

        qs = [q_ref[:, hh * HEAD_PAD:(hh + 1) * HEAD_PAD] for hh in range(HP)]

        for hh in range(HP):
            m_scr[hh] = jnp.full((tb, 128), NEG, F32)
            l_scr[hh] = jnp.zeros((tb, 128), F32)
            acc_scr[hh] = jnp.zeros((tb, VDIM), F32)

        def blk(ik, masked):
            off = pl.multiple_of(ik * tb, tb)
            ss = []
            for hh in range(HP):
                k = k_scr[hh, pl.ds(off, tb), :]
                s = lax.dot_general(qs[hh], k, NT, preferred_element_type=F32)
                if masked:
                    r, c = _causal_mask(tb)
                    s = jnp.where(c <= r, s, NEG)
                ss.append(s)
            ps = []
            for hh in range(HP):
                m = m_scr[hh]
                m2 = jnp.maximum(m, jnp.max(ss[hh], axis=-1, keepdims=True))
                p = jnp.exp2(ss[hh] - jnp.concatenate([m2] * (tb // 128), axis=1))
                a = jnp.exp2(m - m2)
                l_scr[hh] = a * l_scr[hh] + jnp.sum(p, axis=-1, keepdims=True)
                m_scr[hh] = m2
                ps.append((a, p.astype(BF)))
            for hh in range(HP):
                a, p = ps[hh]
                v = kv_ref[pl.ds(off, tb), hh * HEAD_PAD + NOPE:(hh + 1) * HEAD_PAD]
                acc_scr[hh] = a * acc_scr[hh] + lax.dot_general(p, v, NN, preferred_element_type=F32)

        def trip(ik, carry):
            blk(ik, False)
            return carry

        lax.fori_loop(0, iq, trip, 0)
        blk(iq, True)
        for hh in range(HP):
            l = l_scr[hh]
            o_ref[:, hh * VDIM:(hh + 1) * VDIM] = (acc_scr[hh] / l).astype(BF)
            lse_ref[hh] = (m_scr[hh] + jnp.log2(l))[:, :1]

    return _call(
        "flash_fwd", body, (H // HP, T // tb),
        [pl.BlockSpec((tb, HP * HEAD_PAD), lambda h, i: (i, h)),
         pl.BlockSpec((T, HP * HEAD_PAD), lambda h, i: (0, h)),
         pl.BlockSpec((T, 128), lambda h, i: (0, 0))],
        [pl.BlockSpec((tb, HP * VDIM), lambda h, i: (i, h)),
         pl.BlockSpec((HP, tb, 1), lambda h, i: (h, i, 0))],
        [_sds((T, H * VDIM), BF), _sds((H, T, 1), F32)],
        [pltpu.VMEM((HP, T, HEAD_PAD), BF), pltpu.VMEM((HP, tb, 128), F32), pltpu.VMEM((HP, tb, 128), F32),
         pltpu.VMEM((HP, tb, VDIM), F32)], 56, [qp, kvup, kpe], comm)


def _attn_delta(o, do, tb=512):
    T = o.shape[0]
    H = o.shape[1] // VDIM
    tb = _tile(T, tb)

    def body(o_ref, do_ref, d_ref):
        for h in range(H):
            cols = slice(h * VDIM, (h + 1) * VDIM)
            d_ref[h] = jnp.sum(o_ref[:, cols].astype(F32) * do_ref[:, cols].astype(F32), axis=-1, keepdims=True)

    blk = pl.BlockSpec((tb, H * VDIM), lambda i: (i, 0))
    return pl.pallas_call(
        body, name="attn_delta", grid=(T // tb,), in_specs=[blk, blk],
        out_specs=pl.BlockSpec((H, tb, 1), lambda i: (0, i, 0)), out_shape=_sds((H, T, 1), F32),
        compiler_params=_cparams(("arbitrary",), 32),
    )(o, do)


def _flash_bwd(qp, kvup, kpe, do, lse_rows, delta_rows, tb=512, comm=None):
    T = qp.shape[0]
    H = qp.shape[1] // HEAD_PAD
    tb = _tile(T, tb)
    nq = T // tb

    def body(kv_ref, kpe_ref, q_ref, do_ref, lse_ref, dl_ref, dkv_ref, dkpe_ref, dq_ref, dk_scr, dv_scr):
        ik = pl.program_id(1)

        @pl.when(ik == 0)
        def _():
            dq_ref[...] = jnp.zeros_like(dq_ref)

        k = jnp.concatenate([kv_ref[:, :NOPE], kpe_ref[...]], axis=1)
        v = kv_ref[:, NOPE:]
        dk_scr[...] = jnp.zeros_like(dk_scr)
        dv_scr[...] = jnp.zeros_like(dv_scr)

        def blk(iq, masked):
            off = pl.multiple_of(iq * tb, tb)
            q = q_ref[pl.ds(off, tb), :]
            d = do_ref[pl.ds(off, tb), :]
            st = lax.dot_general(k, q, NT, preferred_element_type=F32)
            if masked:
                r, c = _causal_mask(tb)
                st = jnp.where(r <= c, st, NEG)
            dpt = lax.dot_general(v, d, NT, preferred_element_type=F32)
            pt = jnp.exp2(st - lse_ref[iq])
            dst = (pt * (dpt - dl_ref[iq])).astype(BF)
            dv_scr[...] += lax.dot_general(pt.astype(BF), d, NN, preferred_element_type=F32)
            dk_scr[...] += lax.dot_general(dst, q, NN, preferred_element_type=F32)
            dq_ref[pl.ds(off, tb), :] += lax.dot_general(dst, k, TN, preferred_element_type=F32)

        def trip(iq, carry):
            blk(iq, False)
            return carry

        blk(ik, True)
        lax.fori_loop(ik + 1, nq, trip, 0)
        dk = dk_scr[...] * (ATTN_SCALE / LOG2_SCALE)
        dkv_ref[:, :NOPE] = dk[:, :NOPE].astype(BF)
        dkv_ref[:, NOPE:] = dv_scr[...].astype(BF)
        dkpe_ref[...] = dk[:, NOPE:]

    rows = pl.BlockSpec((None, nq, 1, tb), lambda h, i: (h, 0, 0, 0))
    return _call(
        "flash_bwd", body, (H, nq),
        [pl.BlockSpec((tb, HEAD_PAD), lambda h, i: (i, h)), pl.BlockSpec((tb, 128), lambda h, i: (i, 0)),
         pl.BlockSpec((T, HEAD_PAD), lambda h, i: (0, h)), pl.BlockSpec((T, VDIM), lambda h, i: (0, h)), rows, rows],
        [pl.BlockSpec((tb, HEAD_PAD), lambda h, i: (i, h)), pl.BlockSpec((None, tb, 128), lambda h, i: (h, i, 0)),
         pl.BlockSpec((T, HEAD_PAD), lambda h, i: (0, h))],
        [_sds((T, H * HEAD_PAD), BF), _sds((H, T, 128), F32), _sds((T, H * HEAD_PAD), F32)],
        [pltpu.VMEM((tb, HEAD_PAD), F32), pltpu.VMEM((tb, VDIM), F32)], 56,
        [kvup, kpe, qp, do, lse_rows, delta_rows], comm)


def _dq_finish(dq_raw, cos, sin, tb=256):
    T, W = dq_raw.shape
    tb = _tile(T, tb)

    def body(dq_ref, c_ref, s_ref, o_ref):
        c, s = c_ref[...] * ATTN_SCALE, s_ref[...] * ATTN_SCALE
        for h in range(W // HEAD_PAD):
            a0 = h * HEAD_PAD
            o_ref[:, a0:a0 + NOPE] = (dq_ref[:, a0:a0 + NOPE] * ATTN_SCALE).astype(BF)
            dpe = dq_ref[:, a0 + NOPE:a0 + HEAD_PAD]
            o_ref[:, a0 + NOPE:a0 + HEAD_PAD] = (dpe * c - _swap_halves(dpe) * s).astype(BF)

    blk = pl.BlockSpec((tb, W), lambda i: (i, 0))
    tab = pl.BlockSpec((tb, 128), lambda i: (i, 0))
    return pl.pallas_call(
        body, name="dq_finish", grid=(T // tb,), in_specs=[blk, tab, tab], out_specs=blk,
        out_shape=_sds((T, W), BF), compiler_params=_cparams(("arbitrary",), 40),
    )(dq_raw, cos, sin)


def _final_loss(x, gain, target, tb=256):
    T, D = x.shape
    tb = _tile(T, tb)

    def body(x_ref, g_ref, t_ref, loss_ref, dx_ref, dg_ref, dxb_ref):
        i = pl.program_id(0)
        xv = x_ref[...]
        gv = g_ref[...]
        r = lax.rsqrt(jnp.mean(xv * xv, axis=-1, keepdims=True) + NORM_EPS)
        xh = xv * r
        e = xh * gv - t_ref[...]
        lpart = 0.5 * jnp.sum(jnp.mean(e * e, axis=-1, keepdims=True), axis=0, keepdims=True)
        dy = e / D
        dyg = dy * gv
        dx = r * (dyg - xh * jnp.mean(dyg * xh, axis=-1, keepdims=True))
        dx_ref[...] = dx
        dxb_ref[...] = dx.astype(BF)
        gpart = jnp.sum(dy * xh, axis=0, keepdims=True)

        @pl.when(i == 0)
        def _():
            loss_ref[...] = lpart
            dg_ref[...] = gpart

        @pl.when(i > 0)
        def _():
            loss_ref[...] += lpart
            dg_ref[...] += gpart

    blk = pl.BlockSpec((tb, D), lambda i: (i, 0))
    vec = pl.BlockSpec((1, D), lambda i: (0, 0))
    return pl.pallas_call(
        body, name="final_loss", grid=(T // tb,), in_specs=[blk, vec, blk],
        out_specs=[pl.BlockSpec((1, 1), lambda i: (0, 0)), blk, vec, blk],
        out_shape=[_sds((1, 1), F32), _sds((T, D), F32), _sds((1, D), F32), _sds((T, D), BF)],
        compiler_params=_cparams(("arbitrary",), 40),
    )(x, gain, target)


def _row_tile(R, pref=256):
    t = (min(R, pref) // 16) * 16
    while t >= 16:
        if R % t == 0:
            return t
        t -= 16
    return R


def _prefetch_call(body, name, grid, in_specs, out_specs, out_shape, scalar, operands, vmem_mb=32):
    return pl.pallas_call(
        body, name=name, out_shape=out_shape,
        grid_spec=pltpu.PrefetchScalarGridSpec(num_scalar_prefetch=1, grid=grid, in_specs=in_specs,
                                               out_specs=out_specs),
        compiler_params=_cparams(("arbitrary",) * len(grid), vmem_mb),
    )(scalar, *operands)


def _place_own(name, s, chip):
    _, n, R, C = s.shape
    tr = _row_tile(R, 512)

    def body(chip_ref, s_ref, o_ref):
        o_ref[...] = s_ref[...]

    return _prefetch_call(
        body, name, (2, n, R // tr),
        [pl.BlockSpec((None, None, tr, C), lambda h, j, r, cr: (h, j, r, 0))],
        pl.BlockSpec((None, None, None, tr, C), lambda h, j, r, cr: (cr[0], h, j, r, 0)),
        _sds((4,) + s.shape, s.dtype), chip, [s])


def _pair_add(name, g, theirs, core):
    _, _, n, R, C = g.shape
    tr = _row_tile(R)

    def body(c_ref, a_ref, b_ref, o_ref):
        o_ref[...] = (a_ref[...].astype(F32) + b_ref[...].astype(F32)).astype(BF)

    blk = pl.BlockSpec((None, None, tr, C), lambda k, j, r, cr: (k, j, r, 0))
    return _prefetch_call(
        body, name, (4, n, R // tr),
        [pl.BlockSpec((None, None, None, tr, C), lambda k, j, r, cr: (k, cr[0], j, r, 0)), blk], blk,
        _sds(theirs.shape, BF), core, [g, theirs])


def _chip_sum(name, sums, landed, chip):
    _, n, R, C = sums.shape
    tr = _row_tile(R)

    def body(chip_ref, own_ref, l_ref, o_ref):
        s = own_ref[...].astype(F32)
        for j in range(3):
            s = s + l_ref[j].astype(F32)
        o_ref[...] = s

    return _prefetch_call(
        body, name, (n, R // tr),
        [pl.BlockSpec((None, None, tr, C), lambda j, r, cr: (cr[0], j, r, 0)),
         pl.BlockSpec((3, None, tr, C), lambda j, r, cr: (0, j, r, 0))],
        pl.BlockSpec((None, tr, C), lambda j, r, cr: (j, r, 0)),
        _sds((n, R, C), F32), chip, [sums, landed])


def _adamw_halves(name, w, m, v, g_mine, g_theirs, core, tr=128):
    L, _, R, C = w.shape
    tr = _row_tile(R, tr)
    c1 = 1.0 - ADAM_B1 ** ADAM_STEP
    c2 = 1.0 - ADAM_B2 ** ADAM_STEP

    def body(c_ref, w_ref, m_ref, v_ref, gm_ref, gt_ref, g_ref, d_ref, nm_ref, nv_ref):
        gv = jnp.where(pl.program_id(1) == c_ref[0], gm_ref[...], gt_ref[...])
        nm = ADAM_B1 * m_ref[...] + (1.0 - ADAM_B1) * gv
        nv = ADAM_B2 * v_ref[...] + (1.0 - ADAM_B2) * (gv * gv)
        g_ref[...] = gv
        nm_ref[...] = nm
        nv_ref[...] = nv
        d_ref[...] = -ADAM_LR * ((nm / c1) / (jnp.sqrt(nv / c2) + ADAM_EPS) + ADAM_WD * w_ref[...])

    full = pl.BlockSpec((None, None, tr, C), lambda l, h, r, cr: (l, h, r, 0))
    half = pl.BlockSpec((None, tr, C), lambda l, h, r, cr: (l, r, 0))
    return _prefetch_call(
        body, name, (L, 2, R // tr), [full, full, full, half, half], [full] * 4,
        [_sds(w.shape, F32)] * 4, core, [w, m, v, g_mine, g_theirs])


def _adamw(name, w, g, m, v, tr=128):
    R, C = w.shape
    tr = _tile(R, tr) if R % 8 == 0 else R
    c1 = 1.0 - ADAM_B1 ** ADAM_STEP
    c2 = 1.0 - ADAM_B2 ** ADAM_STEP

    def body(w_ref, g_ref, m_ref, v_ref, d_ref, nm_ref, nv_ref):
        gv = g_ref[...]
        nm = ADAM_B1 * m_ref[...] + (1.0 - ADAM_B1) * gv
        nv = ADAM_B2 * v_ref[...] + (1.0 - ADAM_B2) * (gv * gv)
        nm_ref[...] = nm
        nv_ref[...] = nv
        d_ref[...] = -ADAM_LR * ((nm / c1) / (jnp.sqrt(nv / c2) + ADAM_EPS) + ADAM_WD * w_ref[...])

    blk = pl.BlockSpec((tr, C), lambda i: (i, 0))
    return pl.pallas_call(
        body, name=name, grid=(R // tr,), in_specs=[blk] * 4, out_specs=[blk] * 3,
        out_shape=[_sds((R, C), F32)] * 3, compiler_params=_cparams(("arbitrary",), 32),
    )(w, g, m, v)


def _place():
    x, y, c = lax.axis_index("x"), lax.axis_index("y"), lax.axis_index("c")
    chips = [(1 - x, y), (x, 1 - y), (1 - x, 1 - y)]
    return x, y, c, chips


def _gather_comm(shards, placed):
    n = len(shards)

    def copies(cin, cout, send, recv):
        src, dst = cin[:n], cout
        x, y, c, chips = _place()
        me = 2 * x + y

        def remote(i, k, s, d, to):
            return pltpu.make_async_remote_copy(src_ref=s, dst_ref=d, send_sem=send.at[6 * i + k],
                                                recv_sem=recv.at[6 * i + k], device_id=to, device_id_type=MESH)

        first = [remote(i, j, src[i].at[c], dst[i].at[me, c], (*chip, c))
                 for i in range(n) for j, chip in enumerate(chips)]
        return remote, first, dst, (x, y, c, chips)

    def start(cin, cout, send, recv):
        for cp in copies(cin, cout, send, recv)[1]:
            cp.start()

    def finish(cin, cout, send, recv):
        remote, first, dst, (x, y, c, chips) = copies(cin, cout, send, recv)
        sibling = (x, y, 1 - c)
        passed = []
        for i in range(n):
            for j, (px, py) in enumerate(chips):
                slot = dst[i].at[2 * px + py, c]
                remote(i, j, slot, slot, (px, py, c)).wait_recv()
                cp = remote(i, 3 + j, slot, slot, sibling)
                cp.start()
                passed.append(cp)
        for i in range(n):
            for j, (px, py) in enumerate(chips):
                slot = dst[i].at[2 * px + py, 1 - c]
                remote(i, 3 + j, slot, slot, sibling).wait_recv()
        for cp in first + passed:
            cp.wait_send()

    return _Comm(list(shards) + list(placed), [_sds(p.shape, p.dtype) for p in placed],
                 {n + i: i for i in range(n)}, 6 * n, start, finish)


def _pair_comm(grads):
    n = len(grads)

    def copies(cin, cout, send, recv):
        x, y, c, _ = _place()
        return [pltpu.make_async_remote_copy(
            src_ref=cin[i].at[k, 1 - c], dst_ref=cout[i].at[k], send_sem=send.at[4 * i + k],
            recv_sem=recv.at[4 * i + k], device_id=(x, y, 1 - c), device_id_type=MESH)
            for i in range(n) for k in range(4)]

    def start(cin, cout, send, recv):
        for cp in copies(cin, cout, send, recv):
            cp.start()

    def finish(cin, cout, send, recv):
        for cp in copies(cin, cout, send, recv):
            cp.wait()

    return _Comm(grads, [_sds((4,) + a.shape[2:], a.dtype) for a in grads], {}, 4 * n, start, finish)


def _chips_comm(sums):
    n = len(sums)

    def copies(cin, cout, send, recv):
        x, y, c, chips = _place()
        return [pltpu.make_async_remote_copy(
            src_ref=cin[i].at[2 * px + py], dst_ref=cout[i].at[j], send_sem=send.at[3 * i + j],
            recv_sem=recv.at[3 * i + j], device_id=(px, py, c), device_id_type=MESH)
            for i in range(n) for j, (px, py) in enumerate(chips)]

    def start(cin, cout, send, recv):
        for cp in copies(cin, cout, send, recv):
            cp.start()

    def finish(cin, cout, send, recv):
        for cp in copies(cin, cout, send, recv):
            cp.wait()

    return _Comm(sums, [_sds((3,) + a.shape[1:], a.dtype) for a in sums], {}, 3 * n, start, finish)


def _share_comm(halves):
    n = len(halves)

    def copies(cin, cout, send, recv):
        x, y, c, _ = _place()
        return [pltpu.make_async_remote_copy(
            src_ref=cin[i], dst_ref=cout[i], send_sem=send.at[i], recv_sem=recv.at[i],
            device_id=(x, y, 1 - c), device_id_type=MESH) for i in range(n)]

    def start(cin, cout, send, recv):
        for cp in copies(cin, cout, send, recv):
            cp.start()

    def finish(cin, cout, send, recv):
        for cp in copies(cin, cout, send, recv):
            cp.wait()

    return _Comm(halves, [_sds(a.shape, a.dtype) for a in halves], {}, n, start, finish)


def _allreduce_small(v):
    R, C = v.shape

    def body(v_ref, out_ref, land, send, recv):
        x, y, c, _ = _place()
        me = 4 * x + 2 * y + c
        cps = []
        for m in range(1, 8):
            fx, fy, fc = (m >> 2) & 1, (m >> 1) & 1, m & 1
            peer = (x ^ fx, y ^ fy, c ^ fc)
            cp = pltpu.make_async_remote_copy(
                src_ref=v_ref, dst_ref=land.at[me], send_sem=send.at[m - 1], recv_sem=recv.at[m - 1],
                device_id=peer, device_id_type=MESH)
            cp.start()
            cps.append(cp)
        land[me] = v_ref[...]
        for cp in cps:
            cp.wait()
        s = land[0]
        for d in range(1, 8):
            s = s + land[d]
        out_ref[...] = s

    return pl.pallas_call(
        body, name="allreduce_small",
        in_specs=[pl.BlockSpec(memory_space=pltpu.VMEM)], out_specs=pl.BlockSpec(memory_space=pltpu.VMEM),
        out_shape=_sds((R, C), F32),
        scratch_shapes=[pltpu.VMEM((8, R, C), F32), pltpu.SemaphoreType.DMA((7,)), pltpu.SemaphoreType.DMA((7,))],
    )(v)


def _halves(a):
    return a.reshape((2, a.shape[0] // 2) + a.shape[1:])


def _canon(a, lead):
    piece = a.shape[lead:]
    return a.reshape(a.shape[:lead] + (int(math.prod(piece[:-2])),) + piece[-2:])


def _rope_tables(pos):
    half = ROPE // 2
    inv_freq = ROPE_BASE ** (-jnp.arange(half, dtype=F32) / half)
    ang = pos.astype(F32)[:, None] * inv_freq
    cos, sin = jnp.cos(ang), jnp.sin(ang)
    return jnp.tile(cos, (1, 4)), jnp.concatenate([-sin, sin, -sin, sin], axis=1)


def kernel(x, positions, pool_norm, pool_w, pool_scale, kv_in_norm, w_kv_a, kv_latent_norm, w_kv_b, attn_norm, w_q_a, q_latent_norm, w_q_b, w_o, ffn_norm, w_gate, w_up, w_down, final_norm, loss_target, m_pool_norm, m_pool_w, m_pool_scale, m_kv_in_norm, m_w_kv_a, m_kv_latent_norm, m_w_kv_b, m_attn_norm, m_w_q_a, m_q_latent_norm, m_w_q_b, m_w_o, m_ffn_norm, m_w_gate, m_w_up, m_w_down, m_final_norm, v_pool_norm, v_pool_w, v_pool_scale, v_kv_in_norm, v_w_kv_a, v_kv_latent_norm, v_w_kv_b, v_attn_norm, v_w_q_a, v_q_latent_norm, v_w_q_b, v_w_o, v_ffn_norm, v_w_gate, v_w_up, v_w_down, v_final_norm):
    T, D = x.shape[1], x.shape[2]
    H = N_HEADS
    KL = kv_latent_norm.shape[0]
    QL = q_latent_norm.shape[1]
    pg = D // 4
    x0, tgt = x[0], loss_target[0]
    cos, sin = _rope_tables(positions[0])
    chip = 2 * lax.axis_index("x") + lax.axis_index("y")

    n_kva = w_kv_a.shape[1]
    wkva_s = jnp.pad(w_kv_a, ((0, 0), (0, KL + 128 - n_kva)))
    hs = w_q_b.shape[2] // (NOPE + ROPE)
    wqb_s = jnp.pad(w_q_b[0].reshape(QL, hs, NOPE + ROPE), ((0, 0), (0, 0), (0, HEAD_PAD - NOPE - ROPE)))
    wqb_s = wqb_s.reshape(QL, hs * HEAD_PAD)
    chip_s = chip.astype(jnp.int32).reshape(1)
    core_s = lax.axis_index("c").astype(jnp.int32).reshape(1)

    def halved(a):
        return _canon(_halves(a.astype(BF)), 1)

    def gather_group(tag, shards):
        placed = [_place_own(f"place_own{tag}_{i}", s, chip_s) for i, s in enumerate(shards)]
        return _gather_comm(shards, placed)

    def whole(a, *shape):
        return a.reshape(shape)

    n_ff = w_gate.shape[2]
    group0 = gather_group("0", [halved(pool_w[0]), _canon(jnp.stack([pool_norm, pool_scale]), 1),
                                halved(w_gate[0]), halved(w_up[0])])
    group1 = gather_group("1", [halved(w_down[0]), halved(wkva_s), halved(w_kv_b), halved(w_q_a[0]),
                                halved(wqb_s), halved(w_o[0])])
    group2 = gather_group("2", [halved(w_gate[1]), halved(w_up[1]), halved(w_down[1])])
    PW, PV, WG0, WU0 = _comm_call("gather0", group0)
    WG0, WU0 = whole(WG0, 4, D, n_ff), whole(WU0, 4, D, n_ff)
    pv = jnp.transpose(PV.reshape(4, 2, pg), (1, 0, 2)).reshape(2, D)
    pn_full, ps_full = pv[0:1], pv[1:2]

    diff = _pool_fwd(x0, pn_full)
    pre, x1 = _pool_mix(diff, PW, ps_full, x0)
    (h1,) = _norm_fwd("norm_ffn0", x1, ffn_norm[0:1])
    (g0, u0, a0), (WD0, WKVA, WKVB, WQA, WQB, WO) = _ffn_up("ffn_up0", h1, WG0, WU0, comm=group1)
    WD0 = whole(WD0, 4, n_ff, D)
    WKVA = whole(WKVA, D, KL + 128)
    WKVB = whole(WKVB, 4, KL, -1)
    WQA = whole(WQA, D, QL)
    WQB = whole(WQB, 4, QL, -1)
    WO = whole(WO, H * VDIM, D)
    x2 = _ffn_down("ffn_down0", a0, WD0, x1)
    hk, ha = _norm_fwd("norm_attn", x2, jnp.stack([kv_in_norm, attn_norm[0]]))
    kvp = _mm_plain("kv_a", hk, WKVA, F32)
    ckv, kpe = _kv_post(kvp, kv_latent_norm[None], cos, sin)
    kvup = _mm_cols("kv_b", ckv, WKVB, BF)
    qa = _mm_plain("q_a", ha, WQA, F32)
    (ql,) = _norm_fwd("norm_q", qa, q_latent_norm)
    qp = _q_up(ql, WQB, cos, sin)
    (o, lse), (WG1, WU1, WD1) = _flash_fwd(qp, kvup, kpe, comm=group2)
    WG1, WU1, WD1 = whole(WG1, 4, D, n_ff), whole(WU1, 4, D, n_ff), whole(WD1, 4, n_ff, D)
    x3 = _mm_plain("attn_out", o, WO, F32, res=x2)
    (h3,) = _norm_fwd("norm_ffn1", x3, ffn_norm[1:2])
    (g1, u1, a1), _ = _ffn_up("ffn_up1", h3, WG1, WU1)
    x4 = _ffn_down("ffn_down1", a1, WD1, x3)
    loss_part, dx4, d_final, dx4b = _final_loss(x4, final_norm[None], tgt)

    def pair_stage(tag, grads):
        full = [_canon(a.reshape((4, 2, a.shape[1] // 2) + a.shape[2:]), 2) for a in grads]
        theirs = _comm_call("reduce_pair" + tag, _pair_comm(full))
        return [_pair_add(f"pair_add{tag}_{i}", a, b, core_s) for i, (a, b) in enumerate(zip(full, theirs))]

    dWD1 = _mm_tn_rows("ffn_dwd1", a1, dx4b, 4, BF)
    dg1, du1 = _ffn_bwd_act("ffn_bwd_act1", dx4b, WD1, g1, u1)
    dWG1 = _mm_tn_cols("ffn_dwg1", h3, dg1, 4, BF)
    dWU1 = _mm_tn_cols("ffn_dwu1", h3, du1, 4, BF)
    sums1 = pair_stage("1", [dWG1, dWU1, dWD1])
    dh3, _ = _ffn_dh("ffn_dh1", dg1, WG1, du1, WU1)
    dx3, d_ffn1, dx3b = _norm_bwd("norm_ffn1_bwd", x3, ffn_norm[1:2], [dh3], dx4)

    do = _mm_nt_plain("attn_out_dx", dx3b, WO, BF)
    dWO = _mm_tn_plain("attn_out_dw", o, dx3b, BF)
    tb = _tile(T, 512)
    lse_rows = lse.reshape(H, T // tb, 1, tb)
    delta_rows = _attn_delta(o, do).reshape(H, T // tb, 1, tb)
    (dkvup, dkpe_h, dq_raw), landed1 = _flash_bwd(qp, kvup, kpe, do, lse_rows, delta_rows,
                                                  comm=_chips_comm(sums1))
    dqp = _dq_finish(dq_raw, cos, sin)
    dql = _mm_nt_cols("q_b_dx", dqp, WQB, F32)
    dWQB = _mm_tn_cols("q_b_dw", ql, dqp, 4, BF)
    _, d_qln, dqa = _norm_bwd("norm_q_bwd", qa, q_latent_norm, [dql], None)
    dha = _mm_nt_plain("q_a_dx", dqa, WQA, F32)
    dWQA = _mm_tn_plain("q_a_dw", ha, dqa, BF)
    dckv = _mm_nt_cols("kv_b_dx", dkvup, WKVB, F32)
    dWKVB = _mm_tn_cols("kv_b_dw", ckv, dkvup, 4, BF)
    dkvp, d_kvln = _kv_post_bwd(kvp, kv_latent_norm[None], dckv, dkpe_h, cos, sin)
    dhk = _mm_nt_plain("kv_a_dx", dkvp, WKVA, F32)
    dWKVA = _mm_tn_plain("kv_a_dw", hk, dkvp, BF)
    dx2, d_n2, dx2b = _norm_bwd("norm_attn_bwd", x2, jnp.stack([kv_in_norm, attn_norm[0]]), [dhk, dha], dx3)

    dWD0 = _mm_tn_rows("ffn_dwd0", a0, dx2b, 4, BF)
    dg0, du0 = _ffn_bwd_act("ffn_bwd_act0", dx2b, WD0, g0, u0)
    dWG0 = _mm_tn_cols("ffn_dwg0", h1, dg0, 4, BF)
    dWU0 = _mm_tn_cols("ffn_dwu0", h1, du0, 4, BF)
    sums0 = pair_stage("0", [dWG0, dWU0, dWD0])
    dh1, landed0 = _ffn_dh("ffn_dh0", dg0, WG0, du0, WU0, comm=_chips_comm(sums0))
    dx1, d_ffn0, _ = _norm_bwd("norm_ffn0_bwd", x1, ffn_norm[0:1], [dh1], dx2)

    dpre, d_ps = _pool_dpre(dx1, pre, ps_full)
    ddiff = _pool_ddiff(dpre, PW)
    dPW = _pool_dw(diff, dpre)
    dx0, d_pn = _pool_bwd(x0, pn_full, ddiff, dx1)

    sums_a = pair_stage("a", [dPW, dWKVA.reshape(4, D // 4, KL + 128), dWKVB, dWQA.reshape(4, D // 4, QL), dWQB,
                              dWO.reshape(4, H * VDIM // 4, D)])
    landed_a = _comm_call("reduce_chips", _chips_comm(sums_a))
    sums = sums_a + sums0 + sums1
    landed = list(landed_a) + list(landed0) + list(landed1)
    halves = [_chip_sum(f"chip_sum{i}", a, b, chip_s) for i, (a, b) in enumerate(zip(sums, landed))]
    other = _comm_call("share_halves", _share_comm(halves))

    def rows(a):
        return a.reshape((1, -1, a.shape[-1]))

    def unpad_kva(a):
        return rows(a)[:, :, :n_kva]

    def unpad_qb(a):
        return rows(a).reshape(1, -1, hs, HEAD_PAD)[:, :, :, :NOPE + ROPE].reshape(1, -1, hs * (NOPE + ROPE))

    def layers(a0_, a1_):
        return jnp.concatenate([rows(a0_), rows(a1_)], axis=0)

    def by_name(hv):
        return {"pool_w": rows(hv[0]), "w_kv_a": unpad_kva(hv[1]), "w_kv_b": rows(hv[2]), "w_q_a": rows(hv[3]),
                "w_q_b": unpad_qb(hv[4]), "w_o": rows(hv[5]), "w_gate": layers(hv[6], hv[9]),
                "w_up": layers(hv[7], hv[10]), "w_down": layers(hv[8], hv[11])}

    big = ["pool_w", "w_kv_a", "w_kv_b", "w_q_a", "w_q_b", "w_o", "w_gate", "w_up", "w_down"]
    g_mine, g_other = by_name(halves), by_name(other)

    lat = jnp.concatenate([d_kvln[0], d_qln[0], jnp.zeros((D - KL - QL,), F32)])
    lrow = jnp.pad(loss_part[0], (0, D - 1))
    small = jnp.stack([d_n2[0], d_n2[1], d_ffn0[0], d_ffn1[0], d_final[0], lat, d_pn[0], d_ps[0], lrow]
                      + [jnp.zeros((D,), F32)] * 7)
    red = _allreduce_small(small)
    loss = red[8, 0]
    g_kv_in, g_attn, g_final = red[0], red[1:2], red[4]
    g_ffn = red[2:4]
    g_kvln, g_qln = red[5, :KL], red[5:6, KL:KL + QL]
    g_pn = lax.dynamic_slice(red[6:7], (0, chip * pg), (1, pg))
    g_ps = lax.dynamic_slice(red[7:8], (0, chip * pg), (1, pg))

    grads = {"pool_norm": g_pn, "pool_scale": g_ps, "kv_in_norm": g_kv_in, "kv_latent_norm": g_kvln,
             "attn_norm": g_attn, "q_latent_norm": g_qln, "ffn_norm": g_ffn, "final_norm": g_final}
    weights = dict(pool_norm=pool_norm, pool_w=pool_w, pool_scale=pool_scale, kv_in_norm=kv_in_norm, w_kv_a=w_kv_a,
                   kv_latent_norm=kv_latent_norm, w_kv_b=w_kv_b, attn_norm=attn_norm, w_q_a=w_q_a,
                   q_latent_norm=q_latent_norm, w_q_b=w_q_b, w_o=w_o, ffn_norm=ffn_norm, w_gate=w_gate, w_up=w_up,
                   w_down=w_down, final_norm=final_norm)
    ms = dict(pool_norm=m_pool_norm, pool_w=m_pool_w, pool_scale=m_pool_scale, kv_in_norm=m_kv_in_norm,
              w_kv_a=m_w_kv_a, kv_latent_norm=m_kv_latent_norm, w_kv_b=m_w_kv_b, attn_norm=m_attn_norm,
              w_q_a=m_w_q_a, q_latent_norm=m_q_latent_norm, w_q_b=m_w_q_b, w_o=m_w_o, ffn_norm=m_ffn_norm,
              w_gate=m_w_gate, w_up=m_w_up, w_down=m_w_down, final_norm=m_final_norm)
    vs = dict(pool_norm=v_pool_norm, pool_w=v_pool_w, pool_scale=v_pool_scale, kv_in_norm=v_kv_in_norm,
              w_kv_a=v_w_kv_a, kv_latent_norm=v_kv_latent_norm, w_kv_b=v_w_kv_b, attn_norm=v_attn_norm,
              w_q_a=v_w_q_a, q_latent_norm=v_q_latent_norm, w_q_b=v_w_q_b, w_o=v_w_o, ffn_norm=v_ffn_norm,
              w_gate=v_w_gate, w_up=v_w_up, w_down=v_w_down, final_norm=v_final_norm)
    names = list(weights)

    def as2d(a):
        return a.reshape((-1, a.shape[-1]))

    delta_w, new_m, new_v = {}, {}, {}
    for nm in big:
        n_layers = g_mine[nm].shape[0]

        def two_halves(a):
            a = as2d(a)
            return a.reshape(n_layers, 2, a.shape[0] // (2 * n_layers), a.shape[1])

        g, d, m2, v2 = _adamw_halves("adamw_" + nm, two_halves(weights[nm]), two_halves(ms[nm]),
                                     two_halves(vs[nm]), g_mine[nm], g_other[nm], core_s)
        shp = weights[nm].shape
        grads[nm], delta_w[nm], new_m[nm], new_v[nm] = g.reshape(shp), d.reshape(shp), m2.reshape(shp), v2.reshape(shp)
    groups = [["kv_in_norm", "attn_norm", "ffn_norm", "final_norm"], ["kv_latent_norm", "q_latent_norm"],
              ["pool_norm", "pool_scale"]]
    for gi, grp in enumerate(groups):
        cat = lambda t: jnp.concatenate([as2d(t[nm]) for nm in grp], axis=0)
        d, m2, v2 = _adamw(f"adamw_vec{gi}", cat(weights), cat(grads), cat(ms), cat(vs))
        r0 = 0
        for nm in grp:
            shp = weights[nm].shape
            r = as2d(weights[nm]).shape[0]
            delta_w[nm], new_m[nm], new_v[nm] = (d[r0:r0 + r].reshape(shp), m2[r0:r0 + r].reshape(shp),
                                                 v2[r0:r0 + r].reshape(shp))
            r0 += r

    return (loss, dx0[None], *[grads[nm].reshape(weights[nm].shape) for nm in names],
            *[delta_w[nm] for nm in names], *[new_m[nm] for nm in names], *[new_v[nm] for nm in names])
```

```python
import functools
import math

import jax
import jax.numpy as jnp
from jax import lax
from jax.experimental import pallas as pl
from jax.experimental.pallas import tpu as pltpu

BF = jnp.bfloat16
F32 = jnp.float32
MESH = pl.DeviceIdType.MESH

N_HEADS = 16
NOPE = 128
ROPE = 64
VDIM = 128
HEAD_PAD = 256
ROPE_BASE = 10000.0
ATTN_SCALE = 1.0 / math.sqrt(NOPE + ROPE)
POOL_WINDOWS = (2, 4, 8, 16)
HALO = 16
NORM_EPS = 1e-6
ADAM_LR, ADAM_B1, ADAM_B2, ADAM_EPS, ADAM_WD, ADAM_STEP = 0.001, 0.9, 0.999, 1e-08, 0.01, 10
NEG = -1e30

V7X_VMEM_BYTES = 64 * 1024 * 1024
VMEM_CEILING = V7X_VMEM_BYTES - 8 * 1024 * 1024

NN = (((1,), (0,)), ((), ()))
NT = (((1,), (1,)), ((), ()))
TN = (((0,), (0,)), ((), ()))


def _cparams(sem, vmem_mb):
    return pltpu.CompilerParams(dimension_semantics=sem,
                                vmem_limit_bytes=min(vmem_mb * 1024 * 1024, VMEM_CEILING))


def _tile(n, pref):
    if n <= pref:
        return n
    t = (pref // 128) * 128
    while t > 128 and n % t:
        t -= 128
    assert n % t == 0, (n, pref)
    return t


def _sds(shape, dtype):
    return jax.ShapeDtypeStruct(shape, dtype)


ANY = pl.BlockSpec(memory_space=pl.ANY)


class _Comm:
    def __init__(self, ins, out_shapes, alias, n_sems, start, finish):
        self.ins, self.out_shapes, self.alias, self.n_sems = list(ins), list(out_shapes), dict(alias), n_sems
        self.start, self.finish = start, finish


def _call(name, body, grid, in_specs, out_specs, out_shape, scratch, vmem_mb, operands, comm=None):
    in_specs, out_specs, out_shape, scratch = list(in_specs), list(out_specs), list(out_shape), list(scratch)
    params = _cparams(("arbitrary",) * len(grid), vmem_mb)
    if comm is None:
        res = pl.pallas_call(body, name=name, grid=grid, in_specs=in_specs, out_specs=out_specs,
                             out_shape=out_shape, scratch_shapes=scratch, compiler_params=params)(*operands)
        return list(res), []
    n_in, n_out, n_scr = len(in_specs), len(out_specs), len(scratch)
    c_in, c_out = len(comm.ins), len(comm.out_shapes)

    def hosted(*refs):
        ins, cin = refs[:n_in], refs[n_in:n_in + c_in]
        o0 = n_in + c_in
        outs, cout = refs[o0:o0 + n_out], refs[o0 + n_out:o0 + n_out + c_out]
        s0 = o0 + n_out + c_out
        scr, (send, recv) = refs[s0:s0 + n_scr], refs[s0 + n_scr:]
        ids = [pl.program_id(d) for d in range(len(grid))]
        first = functools.reduce(jnp.logical_and, [i == 0 for i in ids])
        last = functools.reduce(jnp.logical_and, [i == g - 1 for i, g in zip(ids, grid)])

        @pl.when(first)
        def _():
            comm.start(cin, cout, send, recv)

        body(*ins, *outs, *scr)

        @pl.when(last)
        def _():
            comm.finish(cin, cout, send, recv)

    res = pl.pallas_call(
        hosted, name=name, grid=grid, in_specs=in_specs + [ANY] * c_in, out_specs=out_specs + [ANY] * c_out,
        out_shape=out_shape + comm.out_shapes,
        scratch_shapes=scratch + [pltpu.SemaphoreType.DMA((comm.n_sems,)), pltpu.SemaphoreType.DMA((comm.n_sems,))],
        input_output_aliases={n_in + k: n_out + v for k, v in comm.alias.items()}, compiler_params=params,
    )(*operands, *comm.ins)
    return list(res[:n_out]), list(res[n_out:])


def _comm_call(name, comm):
    c_in, c_out = len(comm.ins), len(comm.out_shapes)

    def body(*refs):
        cin, cout = refs[:c_in], refs[c_in:c_in + c_out]
        send, recv = refs[c_in + c_out:]
        comm.start(cin, cout, send, recv)
        comm.finish(cin, cout, send, recv)

    return pl.pallas_call(
        body, name=name, in_specs=[ANY] * c_in, out_specs=[ANY] * c_out, out_shape=comm.out_shapes,
        scratch_shapes=[pltpu.SemaphoreType.DMA((comm.n_sems,)), pltpu.SemaphoreType.DMA((comm.n_sems,))],
        input_output_aliases=dict(comm.alias),
    )(*comm.ins)


def _matmul(name, pairs, pair_specs, dims, grid, nk, extra, extra_specs, out_shapes, out_specs,
            acc_shape, epilogue, vmem_mb, comm=None):
    n_p, n_e, n_o = len(pairs) // 2, len(extra), len(out_shapes)

    def body(*refs):
        ab = refs[:2 * n_p]
        ex = refs[2 * n_p:2 * n_p + n_e]
        outs = refs[2 * n_p + n_e:2 * n_p + n_e + n_o]

        def partial_sum():
            tot = None
            for p in range(n_p):
                a = ab[2 * p][...]
                b = ab[2 * p + 1][...]
                if a.ndim > 2:
                    a = a.reshape(-1, a.shape[-1])
                if b.ndim > 2:
                    b = b.reshape(-1, b.shape[-1])
                d = lax.dot_general(a.astype(BF), b.astype(BF), dims, preferred_element_type=F32)
                tot = d if tot is None else tot + d
            return tot

        if nk == 1:
            epilogue(partial_sum(), ex, outs)
        else:
            acc = refs[-1]
            kk = pl.program_id(2)

            @pl.when(kk == 0)
            def _():
                acc[...] = partial_sum()

            @pl.when(kk > 0)
            def _():
                acc[...] += partial_sum()

            @pl.when(kk == nk - 1)
            def _():
                epilogue(acc[...], ex, outs)

    scratch = [] if nk == 1 else [pltpu.VMEM(acc_shape, F32)]
    res, comm_res = _call(name, body, grid, list(pair_specs) + list(extra_specs), out_specs, out_shapes,
                          scratch, vmem_mb, list(pairs) + list(extra), comm)
    return res if comm is None else (res, comm_res)


def _epi_store(acc, ex, outs):
    outs[0][...] = acc.reshape(outs[0].shape).astype(outs[0].dtype)


def _epi_residual(acc, ex, outs):
    outs[0][...] = (acc + ex[0][...]).astype(outs[0].dtype)


def _swap_halves(x):
    lane = lax.broadcasted_iota(jnp.int32, x.shape, 1)
    return jnp.where((lane % 64) < 32, pltpu.roll(x, 96, 1), pltpu.roll(x, 32, 1))


def _mm_plain(name, a, b, out_dtype, res=None, tm=512, tn=1024):
    M, K = a.shape
    N = b.shape[1]
    tm, tn = _tile(M, tm), _tile(N, tn)
    extra, extra_specs, epi = [], [], _epi_store
    if res is not None:
        extra, extra_specs, epi = [res], [pl.BlockSpec((tm, tn), lambda i, j, k: (i, j))], _epi_residual
    return _matmul(
        name, [a, b],
        [pl.BlockSpec((tm, K), lambda i, j, k: (i, 0)), pl.BlockSpec((K, tn), lambda i, j, k: (0, j))],
        NN, (M // tm, N // tn, 1), 1, extra, extra_specs,
        [_sds((M, N), out_dtype)], [pl.BlockSpec((tm, tn), lambda i, j, k: (i, j))],
        None, epi, 40)[0]


def _mm_cols(name, a, b3, out_dtype, epilogue=_epi_store, extra=(), extra_specs=(), tm=1024):
    M, K = a.shape
    G, _, n = b3.shape
    tm = _tile(M, tm)
    return _matmul(
        name, [a, b3],
        [pl.BlockSpec((tm, K), lambda i, j, k: (i, 0)), pl.BlockSpec((None, K, n), lambda i, j, k: (j, 0, 0))],
        NN, (M // tm, G, 1), 1, list(extra), list(extra_specs),
        [_sds((M, G * n), out_dtype)], [pl.BlockSpec((tm, n), lambda i, j, k: (i, j))],
        None, epilogue, 40)[0]


def _mm_nt_plain(name, a, b, out_dtype, tm=512, tn=1024):
    M, K = a.shape
    N = b.shape[0]
    tm, tn = _tile(M, tm), _tile(N, tn)
    return _matmul(
        name, [a, b],
        [pl.BlockSpec((tm, K), lambda i, j, k: (i, 0)), pl.BlockSpec((tn, K), lambda i, j, k: (j, 0))],
        NT, (M // tm, N // tn, 1), 1, [], [],
        [_sds((M, N), out_dtype)], [pl.BlockSpec((tm, tn), lambda i, j, k: (i, j))],
        None, _epi_store, 44)[0]


def _mm_nt_cols(name, a, b3, out_dtype, tm=2048):
    M = a.shape[0]
    G, K, n = b3.shape
    tm = _tile(M, tm)
    return _matmul(
        name, [a, b3],
        [pl.BlockSpec((tm, n), lambda i, j, k: (i, k)), pl.BlockSpec((None, K, n), lambda i, j, k: (k, 0, 0))],
        NT, (M // tm, 1, G), G, [], [],
        [_sds((M, K), out_dtype)], [pl.BlockSpec((tm, K), lambda i, j, k: (i, 0))],
        (tm, K), _epi_store, 40)[0]


def _mm_tn_plain(name, a, b, out_dtype, tt=512, tn=1024):
    T, K = a.shape
    N = b.shape[1]
    tt, tn = _tile(T, tt), _tile(N, tn)
    return _matmul(
        name, [a, b],
        [pl.BlockSpec((tt, K), lambda i, j, k: (k, 0)), pl.BlockSpec((tt, tn), lambda i, j, k: (k, j))],
        TN, (1, N // tn, T // tt), T // tt, [], [],
        [_sds((K, N), out_dtype)], [pl.BlockSpec((K, tn), lambda i, j, k: (0, j))],
        (K, tn), _epi_store, 48)[0]


def _mm_tn_cols(name, a, b, G, out_dtype, tt=1024):
    T, K = a.shape
    n = b.shape[1] // G
    tt = _tile(T, tt)
    return _matmul(
        name, [a, b],
        [pl.BlockSpec((tt, K), lambda i, j, k: (k, 0)), pl.BlockSpec((tt, n), lambda i, j, k: (k, j))],
        TN, (1, G, T // tt), T // tt, [], [],
        [_sds((G, K, n), out_dtype)], [pl.BlockSpec((None, K, n), lambda i, j, k: (j, 0, 0))],
        (K, n), _epi_store, 48)[0]


def _norm_fwd(name, x, gains, tb=512):
    T, D = x.shape
    G = gains.shape[0]
    tb = _tile(T, tb)

    def body(x_ref, g_ref, *outs):
        xv = x_ref[...]
        xh = xv * lax.rsqrt(jnp.mean(xv * xv, axis=-1, keepdims=True) + NORM_EPS)
        for g in range(G):
            outs[g][...] = (xh * g_ref[g:g + 1, :]).astype(BF)

    return pl.pallas_call(
        body, name=name, grid=(T // tb,),
        in_specs=[pl.BlockSpec((tb, D), lambda i: (i, 0)), pl.BlockSpec((G, D), lambda i: (0, 0))],
        out_specs=[pl.BlockSpec((tb, D), lambda i: (i, 0))] * G,
        out_shape=[_sds((T, D), BF)] * G,
        compiler_params=_cparams(("arbitrary",), 40),
    )(x, gains)


def _norm_bwd(name, x, gains, dhs, dres, tb=256):
    T, D = x.shape
    G = gains.shape[0]
    tb = _tile(T, tb)
    has_res = dres is not None

    def body(*refs):
        x_ref, g_ref = refs[0], refs[1]
        dh_refs = refs[2:2 + G]
        res_ref = refs[2 + G] if has_res else None
        dx_ref, dg_ref, dxb_ref = refs[-3], refs[-2], refs[-1]
        i = pl.program_id(0)
        xv = x_ref[...]
        r = lax.rsqrt(jnp.mean(xv * xv, axis=-1, keepdims=True) + NORM_EPS)
        xh = xv * r
        dx = res_ref[...] if has_res else jnp.zeros_like(xv)
        rows = []
        for g in range(G):
            dh = dh_refs[g][...].astype(F32)
            dy = dh * g_ref[g:g + 1, :]
            dx = dx + r * (dy - xh * jnp.mean(dy * xh, axis=-1, keepdims=True))
            rows.append(jnp.sum(dh * xh, axis=0, keepdims=True))
        dx_ref[...] = dx
        dxb_ref[...] = dx.astype(BF)

        @pl.when(i == 0)
        def _():
            for g in range(G):
                dg_ref[g:g + 1, :] = rows[g]

        @pl.when(i > 0)
        def _():
            for g in range(G):
                dg_ref[g:g + 1, :] += rows[g]

    blk = pl.BlockSpec((tb, D), lambda i: (i, 0))
    ins = [x, gains] + list(dhs) + ([dres] if has_res else [])
    in_specs = [blk, pl.BlockSpec((G, D), lambda i: (0, 0))] + [blk] * (G + (1 if has_res else 0))
    return pl.pallas_call(
        body, name=name, grid=(T // tb,), in_specs=in_specs,
        out_specs=[blk, pl.BlockSpec((G, D), lambda i: (0, 0)), blk],
        out_shape=[_sds((T, D), F32), _sds((G, D), F32), _sds((T, D), BF)],
        compiler_params=_cparams(("arbitrary",), 48),
    )(*ins)


def _pool_fwd(x, gain, tb=256):
    T, D = x.shape
    tb = _tile(T, tb)
    pg = D // len(POOL_WINDOWS)
    per = tb // HALO

    def body(x_ref, xp_ref, g_ref, diff_ref):
        i = pl.program_id(0)
        xx = jnp.concatenate([xp_ref[...], x_ref[...]], axis=0)
        h = xx * lax.rsqrt(jnp.mean(xx * xx, axis=-1, keepdims=True) + NORM_EPS) * g_ref[...]
        row = lax.broadcasted_iota(jnp.int32, (HALO + tb, 1), 0)
        h = jnp.where((row >= HALO) | (i > 0), h, 0.0)
        t = i * tb + row[HALO:] - HALO
        for g, w in enumerate(POOL_WINDOWS):
            hg = h[:, g * pg:(g + 1) * pg]
            s, k = hg, 1
            while k < w:
                s = s + pltpu.roll(s, k, 0)
                k *= 2
            cnt = jnp.minimum(t + 1, w).astype(F32)
            diff_ref[:, g * pg:(g + 1) * pg] = (s[HALO:] / cnt - hg[HALO:]).astype(BF)

    return pl.pallas_call(
        body, name="pool_fwd", grid=(T // tb,),
        in_specs=[pl.BlockSpec((tb, D), lambda i: (i, 0)),
                  pl.BlockSpec((HALO, D), lambda i: (jnp.maximum(i * per - 1, 0), 0)),
                  pl.BlockSpec((1, D), lambda i: (0, 0))],
        out_specs=pl.BlockSpec((tb, D), lambda i: (i, 0)),
        out_shape=_sds((T, D), BF),
        compiler_params=_cparams(("arbitrary",), 40),
    )(x, x, gain)


def _pool_bwd(x, gain, ddiff, dres, tb=256):
    T, D = x.shape
    tb = _tile(T, tb)
    pg = D // len(POOL_WINDOWS)
    per = tb // HALO
    nblk = T // HALO

    def body(x_ref, g_ref, dd_ref, ddn_ref, res_ref, dx_ref, dg_ref):
        i = pl.program_id(0)
        last = i == T // tb - 1
        dd = jnp.concatenate([dd_ref[...], ddn_ref[...]], axis=0)
        row = lax.broadcasted_iota(jnp.int32, (tb + HALO, 1), 0)
        dd = jnp.where((row < tb) | jnp.logical_not(last), dd, 0.0)
        t = i * tb + row
        parts = []
        for g, w in enumerate(POOL_WINDOWS):
            dg_ = dd[:, g * pg:(g + 1) * pg]
            e = dg_ / jnp.minimum(t + 1, w).astype(F32)
            s, k = e, 1
            while k < w:
                s = s + pltpu.roll(s, tb + HALO - k, 0)
                k *= 2
            parts.append(s[:tb] - dg_[:tb])
        dh = jnp.concatenate(parts, axis=1)
        xv = x_ref[...]
        r = lax.rsqrt(jnp.mean(xv * xv, axis=-1, keepdims=True) + NORM_EPS)
        xh = xv * r
        dy = dh * g_ref[...]
        dx_ref[...] = res_ref[...] + r * (dy - xh * jnp.mean(dy * xh, axis=-1, keepdims=True))
        part = jnp.sum(dh * xh, axis=0, keepdims=True)

        @pl.when(i == 0)
        def _():
            dg_ref[...] = part

        @pl.when(i > 0)
        def _():
            dg_ref[...] += part

    blk = pl.BlockSpec((tb, D), lambda i: (i, 0))
    return pl.pallas_call(
        body, name="pool_bwd", grid=(T // tb,),
        in_specs=[blk, pl.BlockSpec((1, D), lambda i: (0, 0)), blk,
                  pl.BlockSpec((HALO, D), lambda i: (jnp.minimum((i + 1) * per, nblk - 1), 0)), blk],
        out_specs=[blk, pl.BlockSpec((1, D), lambda i: (0, 0))],
        out_shape=[_sds((T, D), F32), _sds((1, D), F32)],
        compiler_params=_cparams(("arbitrary",), 48),
    )(x, gain, ddiff, ddiff, dres)


def _pool_w_spec(pg):
    return pl.BlockSpec((4, None, None, pg // 4, pg), lambda i, j, k: (0, j // 2, j % 2, 0, 0))


def _pool_mix(diff, pw, scale, x, tm=512):
    T, D = x.shape
    pg = D // 4
    tm = _tile(T, tm)

    def epi(acc, ex, outs):
        outs[0][...] = acc
        outs[1][...] = ex[1][...] + acc * ex[0][...]

    blk = pl.BlockSpec((tm, pg), lambda i, j, k: (i, j))
    return _matmul(
        "pool_mix", [diff, pw], [blk, _pool_w_spec(pg)], NN, (T // tm, 4, 1), 1,
        [scale, x], [pl.BlockSpec((1, pg), lambda i, j, k: (0, j)), blk],
        [_sds((T, D), F32), _sds((T, D), F32)], [blk, blk], None, epi, 32)


def _pool_dpre(dx, pre, scale, tb=512):
    T, D = dx.shape
    tb = _tile(T, tb)

    def body(dx_ref, pre_ref, s_ref, dpre_ref, ds_ref):
        i = pl.program_id(0)
        d = dx_ref[...]
        dpre_ref[...] = (d * s_ref[...]).astype(BF)
        part = jnp.sum(d * pre_ref[...], axis=0, keepdims=True)

        @pl.when(i == 0)
        def _():
            ds_ref[...] = part

        @pl.when(i > 0)
        def _():
            ds_ref[...] += part

    blk = pl.BlockSpec((tb, D), lambda i: (i, 0))
    vec = pl.BlockSpec((1, D), lambda i: (0, 0))
    return pl.pallas_call(
        body, name="pool_dpre", grid=(T // tb,), in_specs=[blk, blk, vec], out_specs=[blk, vec],
        out_shape=[_sds((T, D), BF), _sds((1, D), F32)],
        compiler_params=_cparams(("arbitrary",), 40),
    )(dx, pre, scale)


def _pool_ddiff(dpre, pw, tm=512):
    T, D = dpre.shape
    pg = D // 4
    tm = _tile(T, tm)
    blk = pl.BlockSpec((tm, pg), lambda i, j, k: (i, j))
    return _matmul("pool_ddiff", [dpre, pw], [blk, _pool_w_spec(pg)], NT, (T // tm, 4, 1), 1, [], [],
                   [_sds((T, D), F32)], [blk], None, _epi_store, 32)[0]


def _pool_dw(diff, dpre, tt=512):
    T, D = diff.shape
    pg = D // 4
    tt = _tile(T, tt)
    blk = pl.BlockSpec((tt, pg), lambda i, j, k: (k, j))
    return _matmul("pool_dw", [diff, dpre], [blk, blk], TN, (1, 4, T // tt), T // tt, [], [],
                   [_sds((4, 2, 2, pg // 4, pg), BF)], [_pool_w_spec(pg)], (pg, pg), _epi_store, 32)[0]


def _ffn_up(name, h, wg, wu, tm=512, comm=None):
    T, D = h.shape
    n = wg.shape[-1]
    F = 4 * n
    tm = _tile(T, tm)

    def body(h_ref, wg_ref, wu_ref, g_ref, u_ref, a_ref):
        hv = h_ref[...]
        g = jnp.dot(hv, wg_ref[...], preferred_element_type=F32)
        u = jnp.dot(hv, wu_ref[...], preferred_element_type=F32)
        g_ref[...] = g.astype(BF)
        u_ref[...] = u.astype(BF)
        a_ref[...] = (g * jax.nn.sigmoid(g) * u).astype(BF)

    w_spec = pl.BlockSpec((None, D, n), lambda j, i: (j, 0, 0))
    o_spec = pl.BlockSpec((tm, n), lambda j, i: (i, j))
    return _call(name, body, (4, T // tm), [pl.BlockSpec((tm, D), lambda j, i: (i, 0)), w_spec, w_spec],
                 [o_spec] * 3, [_sds((T, F), BF)] * 3, [], 56, [h, wg, wu], comm)


def _ffn_down(name, a, wd, x, tm=1024, tn=1024):
    T, D = x.shape
    n = wd.shape[1]
    tm, tn = _tile(T, tm), _tile(D, tn)
    return _matmul(
        name, [a, wd],
        [pl.BlockSpec((tm, n), lambda i, j, k: (i, k)),
         pl.BlockSpec((None, n, tn), lambda i, j, k: (k, 0, j))],
        NN, (T // tm, D // tn, 4), 4, [x], [pl.BlockSpec((tm, tn), lambda i, j, k: (i, j))],
        [_sds((T, D), F32)], [pl.BlockSpec((tm, tn), lambda i, j, k: (i, j))],
        (tm, tn), _epi_residual, 40)[0]


def _ffn_bwd_act(name, dx, wd, g, u, tm=512):
    T, D = dx.shape
    n = wd.shape[1]
    tm = _tile(T, tm)

    def body(dx_ref, w_ref, g_ref, u_ref, dg_ref, du_ref):
        da = lax.dot_general(dx_ref[...], w_ref[...], NT, preferred_element_type=F32)
        gv = g_ref[...].astype(F32)
        uv = u_ref[...].astype(F32)
        sig = jax.nn.sigmoid(gv)
        dg_ref[...] = (da * uv * (sig * (1.0 + gv * (1.0 - sig)))).astype(BF)
        du_ref[...] = (da * (gv * sig)).astype(BF)

    blk = pl.BlockSpec((tm, n), lambda i, j: (i, j))
    return pl.pallas_call(
        body, name=name, grid=(T // tm, 4),
        in_specs=[pl.BlockSpec((tm, D), lambda i, j: (i, 0)), pl.BlockSpec((None, n, D), lambda i, j: (j, 0, 0)),
                  blk, blk],
        out_specs=[blk, blk], out_shape=[_sds((T, 4 * n), BF)] * 2,
        compiler_params=_cparams(("arbitrary", "arbitrary"), 56),
    )(dx, wd, g, u)


def _ffn_dh(name, dg, wg, du, wu, tm=512, comm=None):
    T = dg.shape[0]
    D, n = wg.shape[1], wg.shape[2]
    tm = _tile(T, tm)
    a_spec = pl.BlockSpec((tm, n), lambda i, j, k: (i, k))
    w_spec = pl.BlockSpec((None, D, n), lambda i, j, k: (k, 0, 0))
    r = _matmul(
        name, [dg, wg, du, wu], [a_spec, w_spec, a_spec, w_spec], NT, (T // tm, 1, 4), 4, [], [],
        [_sds((T, D), F32)], [pl.BlockSpec((tm, D), lambda i, j, k: (i, 0))], (tm, D), _epi_store, 56, comm)
    return (r[0], []) if comm is None else (r[0][0], r[1])


def _mm_tn_rows(name, a, b, G, out_dtype, tt=1024, tn=1024):
    T, N = b.shape
    n = a.shape[1] // G
    tt, tn = _tile(T, tt), _tile(N, tn)
    nj = N // tn
    return _matmul(
        name, [a, b],
        [pl.BlockSpec((tt, n), lambda i, j, k: (k, j // nj)), pl.BlockSpec((tt, tn), lambda i, j, k: (k, j % nj))],
        TN, (1, G * nj, T // tt), T // tt, [], [],
        [_sds((G, n, N), out_dtype)], [pl.BlockSpec((None, n, tn), lambda i, j, k: (j // nj, 0, j % nj))],
        (n, tn), _epi_store, 48)[0]


def _kv_post(kvp, gain, cos, sin, tb=512):
    T, W = kvp.shape
    KL = W - 128
    tb = _tile(T, tb)

    def body(kv_ref, g_ref, c_ref, s_ref, ckv_ref, kpe_ref):
        lat = kv_ref[:, :KL]
        ckv_ref[...] = (lat * lax.rsqrt(jnp.mean(lat * lat, axis=-1, keepdims=True) + NORM_EPS)
                        * g_ref[...]).astype(BF)
        pe = kv_ref[:, KL:]
        kpe_ref[...] = (pe * c_ref[...] + _swap_halves(pe) * s_ref[...]).astype(BF)

    tab = pl.BlockSpec((tb, 128), lambda i: (i, 0))
    return pl.pallas_call(
        body, name="kv_post", grid=(T // tb,),
        in_specs=[pl.BlockSpec((tb, W), lambda i: (i, 0)), pl.BlockSpec((1, KL), lambda i: (0, 0)), tab, tab],
        out_specs=[pl.BlockSpec((tb, KL), lambda i: (i, 0)), tab],
        out_shape=[_sds((T, KL), BF), _sds((T, 128), BF)],
        compiler_params=_cparams(("arbitrary",), 32),
    )(kvp, gain, cos, sin)


def _kv_post_bwd(kvp, gain, dckv, dkpe_heads, cos, sin, tb=256):
    T, W = kvp.shape
    KL = W - 128
    H = dkpe_heads.shape[0]
    tb = _tile(T, tb)

    def body(kv_ref, g_ref, dc_ref, dk_ref, c_ref, s_ref, out_ref, dg_ref):
        i = pl.program_id(0)
        lat = kv_ref[:, :KL]
        r = lax.rsqrt(jnp.mean(lat * lat, axis=-1, keepdims=True) + NORM_EPS)
        xh = lat * r
        dh = dc_ref[...]
        dy = dh * g_ref[...]
        out_ref[:, :KL] = (r * (dy - xh * jnp.mean(dy * xh, axis=-1, keepdims=True))).astype(BF)
        d = dk_ref[0]
        for h in range(1, H):
            d = d + dk_ref[h]
        out_ref[:, KL:] = (d * c_ref[...] - _swap_halves(d) * s_ref[...]).astype(BF)
        part = jnp.sum(dh * xh, axis=0, keepdims=True)

        @pl.when(i == 0)
        def _():
            dg_ref[...] = part

        @pl.when(i > 0)
        def _():
            dg_ref[...] += part

    tab = pl.BlockSpec((tb, 128), lambda i: (i, 0))
    vec = pl.BlockSpec((1, KL), lambda i: (0, 0))
    return pl.pallas_call(
        body, name="kv_post_bwd", grid=(T // tb,),
        in_specs=[pl.BlockSpec((tb, W), lambda i: (i, 0)), vec, pl.BlockSpec((tb, KL), lambda i: (i, 0)),
                  pl.BlockSpec((H, tb, 128), lambda i: (0, i, 0)), tab, tab],
        out_specs=[pl.BlockSpec((tb, W), lambda i: (i, 0)), vec],
        out_shape=[_sds((T, W), BF), _sds((1, KL), F32)],
        compiler_params=_cparams(("arbitrary",), 32),
    )(kvp, gain, dckv, dkpe_heads, cos, sin)


def _q_up(ql, wqb, cos, sin, tm=1024):
    n = wqb.shape[2]
    tm = _tile(ql.shape[0], tm)

    def epi(acc, ex, outs):
        c, s = ex[0][...] * LOG2_SCALE, ex[1][...] * LOG2_SCALE
        for j in range(n // HEAD_PAD):
            a0 = j * HEAD_PAD
            outs[0][:, a0:a0 + NOPE] = (acc[:, a0:a0 + NOPE] * LOG2_SCALE).astype(BF)
            pe = acc[:, a0 + NOPE:a0 + HEAD_PAD]
            outs[0][:, a0 + NOPE:a0 + HEAD_PAD] = (pe * c + _swap_halves(pe) * s).astype(BF)

    tab = pl.BlockSpec((tm, 128), lambda i, j, k: (i, 0))
    return _mm_cols("q_up", ql, wqb, BF, epilogue=epi, extra=[cos, sin], extra_specs=[tab, tab], tm=tm)


def _causal_mask(tb):
    r = lax.broadcasted_iota(jnp.int32, (tb, tb), 0)
    c = lax.broadcasted_iota(jnp.int32, (tb, tb), 1)
    return r, c


HP = 2
LOG2_SCALE = ATTN_SCALE * math.log2(math.e)


def _fill_keys(k_scr, kv_ref, kpe_ref):
    for hh in range(HP):
        k_scr[hh, :, :NOPE] = kv_ref[:, hh * HEAD_PAD:hh * HEAD_PAD + NOPE]
        k_scr[hh, :, NOPE:] = kpe_ref[...]


def _flash_fwd(qp, kvup, kpe, tb=512, comm=None):
    T = qp.shape[0]
    H = qp.shape[1] // HEAD_PAD
    tb = _tile(T, tb)

    def body(q_ref, kv_ref, kpe_ref, o_ref, lse_ref, k_scr, m_scr, l_scr, acc_scr, sa_scr, sb_scr):
        iq = pl.program_id(1)

        @pl.when(iq == 0)
        def _():
            _fill_keys(k_scr, kv_ref, kpe_ref)

        qs = [q_ref[:, hh * HEAD_PAD:(hh + 1) * HEAD_PAD] for hh in range(HP)]

        for hh in range(HP):
            m_scr[hh] = jnp.full((tb, 128), NEG, F32)
            l_scr[hh] = jnp.zeros((tb, 128), F32)
            acc_scr[hh] = jnp.zeros((tb, VDIM), F32)

        def scores(ik, s_buf):
            off = pl.multiple_of(ik * tb, tb)
            for hh in range(HP):
                s_buf[hh] = lax.dot_general(qs[hh], k_scr[hh, pl.ds(off, tb), :], NT, preferred_element_type=F32)

        def update(ik, s_buf, masked):
            off = pl.multiple_of(ik * tb, tb)
            ps = []
            for hh in range(HP):
                s = s_buf[hh]
                if masked:
                    r, c = _causal_mask(tb)
                    s = jnp.where(c <= r, s, NEG)
                m = m_scr[hh]
                m2 = jnp.maximum(m, jnp.max(s, axis=-1, keepdims=True))
                p = jnp.exp2(s - jnp.concatenate([m2] * (tb // 128), axis=1))
                a = jnp.exp2(m - m2)
                l_scr[hh] = a * l_scr[hh] + jnp.sum(p, axis=-1, keepdims=True)
                m_scr[hh] = m2
                ps.append((a, p.astype(BF)))
            for hh in range(HP):
                a, p = ps[hh]
                v = kv_ref[pl.ds(off, tb), hh * HEAD_PAD + NOPE:(hh + 1) * HEAD_PAD]
                acc_scr[hh] = a * acc_scr[hh] + lax.dot_general(p, v, NN, preferred_element_type=F32)

        def trip(j, carry):
            scores(2 * j + 1, sb_scr)
            update(2 * j, sa_scr, False)
            scores(2 * j + 2, sa_scr)
            update(2 * j + 1, sb_scr, False)
            return carry

        scores(0, sa_scr)
        lax.fori_loop(0, iq // 2, trip, 0)

        @pl.when(iq % 2 == 0)
        def _():
            update(iq, sa_scr, True)

        @pl.when(iq % 2 == 1)
        def _():
            scores(iq, sb_scr)
            update(iq - 1, sa_scr, False)
            update(iq, sb_scr, True)
        for hh in range(HP):
            l = l_scr[hh]
            o_ref[:, hh * VDIM:(hh + 1) * VDIM] = (acc_scr[hh] / l).astype(BF)
            lse_ref[hh] = (m_scr[hh] + jnp.log2(l))[:, :1]

    return _call(
        "flash_fwd", body, (H // HP, T // tb),
        [pl.BlockSpec((tb, HP * HEAD_PAD), lambda h, i: (i, h)),
         pl.BlockSpec((T, HP * HEAD_PAD), lambda h, i: (0, h)),
         pl.BlockSpec((T, 128), lambda h, i: (0, 0))],
        [pl.BlockSpec((tb, HP * VDIM), lambda h, i: (i, h)),
         pl.BlockSpec((HP, tb, 1), lambda h, i: (h, i, 0))],
        [_sds((T, H * VDIM), BF), _sds((H, T, 1), F32)],
        [pltpu.VMEM((HP, T, HEAD_PAD), BF), pltpu.VMEM((HP, tb, 128), F32), pltpu.VMEM((HP, tb, 128), F32),
         pltpu.VMEM((HP, tb, VDIM), F32), pltpu.VMEM((HP, tb, tb), F32), pltpu.VMEM((HP, tb, tb), F32)],
        56, [qp, kvup, kpe], comm)


def _attn_delta(o, do, tb=512):
    T = o.shape[0]
    H = o.shape[1] // VDIM
    tb = _tile(T, tb)

    def body(o_ref, do_ref, d_ref):
        for h in range(H):
            cols = slice(h * VDIM, (h + 1) * VDIM)
            d_ref[h] = jnp.sum(o_ref[:, cols].astype(F32) * do_ref[:, cols].astype(F32), axis=-1, keepdims=True)

    blk = pl.BlockSpec((tb, H * VDIM), lambda i: (i, 0))
    return pl.pallas_call(
        body, name="attn_delta", grid=(T // tb,), in_specs=[blk, blk],
        out_specs=pl.BlockSpec((H, tb, 1), lambda i: (0, i, 0)), out_shape=_sds((H, T, 1), F32),
        compiler_params=_cparams(("arbitrary",), 32),
    )(o, do)


def _flash_bwd(qp, kvup, kpe, do, lse_rows, delta_rows, tb=512, comm=None):
    T = qp.shape[0]
    H = qp.shape[1] // HEAD_PAD
    tb = _tile(T, tb)
    nq = T // tb

    def body(kv_ref, kpe_ref, q_ref, do_ref, lse_ref, dl_ref, dkv_ref, dkpe_ref, dq_ref, dk_scr, dv_scr,
             sa_scr, pa_scr, sb_scr, pb_scr):
        ik = pl.program_id(1)

        @pl.when(ik == 0)
        def _():
            dq_ref[...] = jnp.zeros_like(dq_ref)

        k = jnp.concatenate([kv_ref[:, :NOPE], kpe_ref[...]], axis=1)
        v = kv_ref[:, NOPE:]
        dk_scr[...] = jnp.zeros_like(dk_scr)
        dv_scr[...] = jnp.zeros_like(dv_scr)

        def scores(iq, bufs):
            off = pl.multiple_of(iq * tb, tb)
            bufs[0][...] = lax.dot_general(k, q_ref[pl.ds(off, tb), :], NT, preferred_element_type=F32)
            bufs[1][...] = lax.dot_general(v, do_ref[pl.ds(off, tb), :], NT, preferred_element_type=F32)

        def update(iq, bufs, masked):
            off = pl.multiple_of(iq * tb, tb)
            q = q_ref[pl.ds(off, tb), :]
            d = do_ref[pl.ds(off, tb), :]
            st = bufs[0][...]
            if masked:
                r, c = _causal_mask(tb)
                st = jnp.where(r <= c, st, NEG)
            pt = jnp.exp2(st - lse_ref[iq])
            dst = (pt * (bufs[1][...] - dl_ref[iq])).astype(BF)
            dv_scr[...] += lax.dot_general(pt.astype(BF), d, NN, preferred_element_type=F32)
            dk_scr[...] += lax.dot_general(dst, q, NN, preferred_element_type=F32)
            dq_ref[pl.ds(off, tb), :] += lax.dot_general(dst, k, TN, preferred_element_type=F32)

        A, B = (sa_scr, pa_scr), (sb_scr, pb_scr)
        n = nq - 1 - ik
        scores(ik, A)

        @pl.when(n == 0)
        def _():
            update(ik, A, True)

        @pl.when(n > 0)
        def _():
            scores(ik + 1, B)
            update(ik, A, True)

            def trip(j, carry):
                b = ik + 1 + 2 * j
                scores(b + 1, A)
                update(b, B, False)
                scores(b + 2, B)
                update(b + 1, A, False)
                return carry

            pairs = (n - 1) // 2
            lax.fori_loop(0, pairs, trip, 0)
            last = ik + 1 + 2 * pairs

            @pl.when(last == nq - 1)
            def _():
                update(last, B, False)

            @pl.when(last < nq - 1)
            def _():
                scores(last + 1, A)
                update(last, B, False)
                update(last + 1, A, False)
        dk = dk_scr[...] * (ATTN_SCALE / LOG2_SCALE)
        dkv_ref[:, :NOPE] = dk[:, :NOPE].astype(BF)
        dkv_ref[:, NOPE:] = dv_scr[...].astype(BF)
        dkpe_ref[...] = dk[:, NOPE:]

    rows = pl.BlockSpec((None, nq, 1, tb), lambda h, i: (h, 0, 0, 0))
    return _call(
        "flash_bwd", body, (H, nq),
        [pl.BlockSpec((tb, HEAD_PAD), lambda h, i: (i, h)), pl.BlockSpec((tb, 128), lambda h, i: (i, 0)),
         pl.BlockSpec((T, HEAD_PAD), lambda h, i: (0, h)), pl.BlockSpec((T, VDIM), lambda h, i: (0, h)), rows, rows],
        [pl.BlockSpec((tb, HEAD_PAD), lambda h, i: (i, h)), pl.BlockSpec((None, tb, 128), lambda h, i: (h, i, 0)),
         pl.BlockSpec((T, HEAD_PAD), lambda h, i: (0, h))],
        [_sds((T, H * HEAD_PAD), BF), _sds((H, T, 128), F32), _sds((T, H * HEAD_PAD), F32)],
        [pltpu.VMEM((tb, HEAD_PAD), F32), pltpu.VMEM((tb, VDIM), F32)] + [pltpu.VMEM((tb, tb), F32)] * 4, 56,
        [kvup, kpe, qp, do, lse_rows, delta_rows], comm)


def _dq_finish(dq_raw, cos, sin, tb=256):
    T, W = dq_raw.shape
    tb = _tile(T, tb)

    def body(dq_ref, c_ref, s_ref, o_ref):
        c, s = c_ref[...] * ATTN_SCALE, s_ref[...] * ATTN_SCALE
        for h in range(W // HEAD_PAD):
            a0 = h * HEAD_PAD
            o_ref[:, a0:a0 + NOPE] = (dq_ref[:, a0:a0 + NOPE] * ATTN_SCALE).astype(BF)
            dpe = dq_ref[:, a0 + NOPE:a0 + HEAD_PAD]
            o_ref[:, a0 + NOPE:a0 + HEAD_PAD] = (dpe * c - _swap_halves(dpe) * s).astype(BF)

    blk = pl.BlockSpec((tb, W), lambda i: (i, 0))
    tab = pl.BlockSpec((tb, 128), lambda i: (i, 0))
    return pl.pallas_call(
        body, name="dq_finish", grid=(T // tb,), in_specs=[blk, tab, tab], out_specs=blk,
        out_shape=_sds((T, W), BF), compiler_params=_cparams(("arbitrary",), 40),
    )(dq_raw, cos, sin)


def _final_loss(x, gain, target, tb=256):
    T, D = x.shape
    tb = _tile(T, tb)

    def body(x_ref, g_ref, t_ref, loss_ref, dx_ref, dg_ref, dxb_ref):
        i = pl.program_id(0)
        xv = x_ref[...]
        gv = g_ref[...]
        r = lax.rsqrt(jnp.mean(xv * xv, axis=-1, keepdims=True) + NORM_EPS)
        xh = xv * r
        e = xh * gv - t_ref[...]
        lpart = 0.5 * jnp.sum(jnp.mean(e * e, axis=-1, keepdims=True), axis=0, keepdims=True)
        dy = e / D
        dyg = dy * gv
        dx = r * (dyg - xh * jnp.mean(dyg * xh, axis=-1, keepdims=True))
        dx_ref[...] = dx
        dxb_ref[...] = dx.astype(BF)
        gpart = jnp.sum(dy * xh, axis=0, keepdims=True)

        @pl.when(i == 0)
        def _():
            loss_ref[...] = lpart
            dg_ref[...] = gpart

        @pl.when(i > 0)
        def _():
            loss_ref[...] += lpart
            dg_ref[...] += gpart

    blk = pl.BlockSpec((tb, D), lambda i: (i, 0))
    vec = pl.BlockSpec((1, D), lambda i: (0, 0))
    return pl.pallas_call(
        body, name="final_loss", grid=(T // tb,), in_specs=[blk, vec, blk],
        out_specs=[pl.BlockSpec((1, 1), lambda i: (0, 0)), blk, vec, blk],
        out_shape=[_sds((1, 1), F32), _sds((T, D), F32), _sds((1, D), F32), _sds((T, D), BF)],
        compiler_params=_cparams(("arbitrary",), 40),
    )(x, gain, target)


def _row_tile(R, pref=256):
    t = (min(R, pref) // 16) * 16
    while t >= 16:
        if R % t == 0:
            return t
        t -= 16
    return R


def _prefetch_call(body, name, grid, in_specs, out_specs, out_shape, scalar, operands, vmem_mb=32):
    return pl.pallas_call(
        body, name=name, out_shape=out_shape,
        grid_spec=pltpu.PrefetchScalarGridSpec(num_scalar_prefetch=1, grid=grid, in_specs=in_specs,
                                               out_specs=out_specs),
        compiler_params=_cparams(("arbitrary",) * len(grid), vmem_mb),
    )(scalar, *operands)


def _place_own(name, s, chip):
    _, n, R, C = s.shape
    tr = _row_tile(R, 512)

    def body(chip_ref, s_ref, o_ref):
        o_ref[...] = s_ref[...]

    return _prefetch_call(
        body, name, (2, n, R // tr),
        [pl.BlockSpec((None, None, tr, C), lambda h, j, r, cr: (h, j, r, 0))],
        pl.BlockSpec((None, None, None, tr, C), lambda h, j, r, cr: (cr[0], h, j, r, 0)),
        _sds((4,) + s.shape, s.dtype), chip, [s])


def _pair_add(name, g, theirs, core):
    _, _, n, R, C = g.shape
    tr = _row_tile(R)

    def body(c_ref, a_ref, b_ref, o_ref):
        o_ref[...] = (a_ref[...].astype(F32) + b_ref[...].astype(F32)).astype(BF)

    blk = pl.BlockSpec((None, None, tr, C), lambda k, j, r, cr: (k, j, r, 0))
    return _prefetch_call(
        body, name, (4, n, R // tr),
        [pl.BlockSpec((None, None, None, tr, C), lambda k, j, r, cr: (k, cr[0], j, r, 0)), blk], blk,
        _sds(theirs.shape, BF), core, [g, theirs])


def _chip_sum(name, sums, landed, chip):
    _, n, R, C = sums.shape
    tr = _row_tile(R)

    def body(chip_ref, own_ref, l_ref, o_ref):
        s = own_ref[...].astype(F32)
        for j in range(3):
            s = s + l_ref[j].astype(F32)
        o_ref[...] = s

    return _prefetch_call(
        body, name, (n, R // tr),
        [pl.BlockSpec((None, None, tr, C), lambda j, r, cr: (cr[0], j, r, 0)),
         pl.BlockSpec((3, None, tr, C), lambda j, r, cr: (0, j, r, 0))],
        pl.BlockSpec((None, tr, C), lambda j, r, cr: (j, r, 0)),
        _sds((n, R, C), F32), chip, [sums, landed])


def _adamw_halves(name, w, m, v, g_mine, g_theirs, core, tr=128):
    L, _, R, C = w.shape
    tr = _row_tile(R, tr)
    c1 = 1.0 - ADAM_B1 ** ADAM_STEP
    c2 = 1.0 - ADAM_B2 ** ADAM_STEP

    def body(c_ref, w_ref, m_ref, v_ref, gm_ref, gt_ref, g_ref, d_ref, nm_ref, nv_ref):
        gv = jnp.where(pl.program_id(1) == c_ref[0], gm_ref[...], gt_ref[...])
        nm = ADAM_B1 * m_ref[...] + (1.0 - ADAM_B1) * gv
        nv = ADAM_B2 * v_ref[...] + (1.0 - ADAM_B2) * (gv * gv)
        g_ref[...] = gv
        nm_ref[...] = nm
        nv_ref[...] = nv
        d_ref[...] = -ADAM_LR * ((nm / c1) / (jnp.sqrt(nv / c2) + ADAM_EPS) + ADAM_WD * w_ref[...])

    full = pl.BlockSpec((None, None, tr, C), lambda l, h, r, cr: (l, h, r, 0))
    half = pl.BlockSpec((None, tr, C), lambda l, h, r, cr: (l, r, 0))
    return _prefetch_call(
        body, name, (L, 2, R // tr), [full, full, full, half, half], [full] * 4,
        [_sds(w.shape, F32)] * 4, core, [w, m, v, g_mine, g_theirs])


def _adamw(name, w, g, m, v, tr=128):
    R, C = w.shape
    tr = _tile(R, tr) if R % 8 == 0 else R
    c1 = 1.0 - ADAM_B1 ** ADAM_STEP
    c2 = 1.0 - ADAM_B2 ** ADAM_STEP

    def body(w_ref, g_ref, m_ref, v_ref, d_ref, nm_ref, nv_ref):
        gv = g_ref[...]
        nm = ADAM_B1 * m_ref[...] + (1.0 - ADAM_B1) * gv
        nv = ADAM_B2 * v_ref[...] + (1.0 - ADAM_B2) * (gv * gv)
        nm_ref[...] = nm
        nv_ref[...] = nv
        d_ref[...] = -ADAM_LR * ((nm / c1) / (jnp.sqrt(nv / c2) + ADAM_EPS) + ADAM_WD * w_ref[...])

    blk = pl.BlockSpec((tr, C), lambda i: (i, 0))
    return pl.pallas_call(
        body, name=name, grid=(R // tr,), in_specs=[blk] * 4, out_specs=[blk] * 3,
        out_shape=[_sds((R, C), F32)] * 3, compiler_params=_cparams(("arbitrary",), 32),
    )(w, g, m, v)


def _place():
    x, y, c = lax.axis_index("x"), lax.axis_index("y"), lax.axis_index("c")
    chips = [(1 - x, y), (x, 1 - y), (1 - x, 1 - y)]
    return x, y, c, chips


def _gather_comm(shards, placed):
    n = len(shards)

    def copies(cin, cout, send, recv):
        src, dst = cin[:n], cout
        x, y, c, chips = _place()
        me = 2 * x + y

        def remote(i, k, s, d, to):
            return pltpu.make_async_remote_copy(src_ref=s, dst_ref=d, send_sem=send.at[6 * i + k],
                                                recv_sem=recv.at[6 * i + k], device_id=to, device_id_type=MESH)

        first = [remote(i, j, src[i].at[c], dst[i].at[me, c], (*chip, c))
                 for i in range(n) for j, chip in enumerate(chips)]
        return remote, first, dst, (x, y, c, chips)

    def start(cin, cout, send, recv):
        for cp in copies(cin, cout, send, recv)[1]:
            cp.start()

    def finish(cin, cout, send, recv):
        remote, first, dst, (x, y, c, chips) = copies(cin, cout, send, recv)
        sibling = (x, y, 1 - c)
        passed = []
        for i in range(n):
            for j, (px, py) in enumerate(chips):
                slot = dst[i].at[2 * px + py, c]
                remote(i, j, slot, slot, (px, py, c)).wait_recv()
                cp = remote(i, 3 + j, slot, slot, sibling)
                cp.start()
                passed.append(cp)
        for i in range(n):
            for j, (px, py) in enumerate(chips):
                slot = dst[i].at[2 * px + py, 1 - c]
                remote(i, 3 + j, slot, slot, sibling).wait_recv()
        for cp in first + passed:
            cp.wait_send()

    return _Comm(list(shards) + list(placed), [_sds(p.shape, p.dtype) for p in placed],
                 {n + i: i for i in range(n)}, 6 * n, start, finish)


def _pair_comm(grads):
    n = len(grads)

    def copies(cin, cout, send, recv):
        x, y, c, _ = _place()
        return [pltpu.make_async_remote_copy(
            src_ref=cin[i].at[k, 1 - c], dst_ref=cout[i].at[k], send_sem=send.at[4 * i + k],
            recv_sem=recv.at[4 * i + k], device_id=(x, y, 1 - c), device_id_type=MESH)
            for i in range(n) for k in range(4)]

    def start(cin, cout, send, recv):
        for cp in copies(cin, cout, send, recv):
            cp.start()

    def finish(cin, cout, send, recv):
        for cp in copies(cin, cout, send, recv):
            cp.wait()

    return _Comm(grads, [_sds((4,) + a.shape[2:], a.dtype) for a in grads], {}, 4 * n, start, finish)


def _chips_comm(sums):
    n = len(sums)

    def copies(cin, cout, send, recv):
        x, y, c, chips = _place()
        return [pltpu.make_async_remote_copy(
            src_ref=cin[i].at[2 * px + py], dst_ref=cout[i].at[j], send_sem=send.at[3 * i + j],
            recv_sem=recv.at[3 * i + j], device_id=(px, py, c), device_id_type=MESH)
            for i in range(n) for j, (px, py) in enumerate(chips)]

    def start(cin, cout, send, recv):
        for cp in copies(cin, cout, send, recv):
            cp.start()

    def finish(cin, cout, send, recv):
        for cp in copies(cin, cout, send, recv):
            cp.wait()

    return _Comm(sums, [_sds((3,) + a.shape[1:], a.dtype) for a in sums], {}, 3 * n, start, finish)


def _share_comm(halves):
    n = len(halves)

    def copies(cin, cout, send, recv):
        x, y, c, _ = _place()
        return [pltpu.make_async_remote_copy(
            src_ref=cin[i], dst_ref=cout[i], send_sem=send.at[i], recv_sem=recv.at[i],
            device_id=(x, y, 1 - c), device_id_type=MESH) for i in range(n)]

    def start(cin, cout, send, recv):
        for cp in copies(cin, cout, send, recv):
            cp.start()

    def finish(cin, cout, send, recv):
        for cp in copies(cin, cout, send, recv):
            cp.wait()

    return _Comm(halves, [_sds(a.shape, a.dtype) for a in halves], {}, n, start, finish)


def _allreduce_small(v):
    R, C = v.shape

    def body(v_ref, out_ref, land, send, recv):
        x, y, c, _ = _place()
        me = 4 * x + 2 * y + c
        cps = []
        for m in range(1, 8):
            fx, fy, fc = (m >> 2) & 1, (m >> 1) & 1, m & 1
            peer = (x ^ fx, y ^ fy, c ^ fc)
            cp = pltpu.make_async_remote_copy(
                src_ref=v_ref, dst_ref=land.at[me], send_sem=send.at[m - 1], recv_sem=recv.at[m - 1],
                device_id=peer, device_id_type=MESH)
            cp.start()
            cps.append(cp)
        land[me] = v_ref[...]
        for cp in cps:
            cp.wait()
        s = land[0]
        for d in range(1, 8):
            s = s + land[d]
        out_ref[...] = s

    return pl.pallas_call(
        body, name="allreduce_small",
        in_specs=[pl.BlockSpec(memory_space=pltpu.VMEM)], out_specs=pl.BlockSpec(memory_space=pltpu.VMEM),
        out_shape=_sds((R, C), F32),
        scratch_shapes=[pltpu.VMEM((8, R, C), F32), pltpu.SemaphoreType.DMA((7,)), pltpu.SemaphoreType.DMA((7,))],
    )(v)


def _halves(a):
    return a.reshape((2, a.shape[0] // 2) + a.shape[1:])


def _canon(a, lead):
    piece = a.shape[lead:]
    return a.reshape(a.shape[:lead] + (int(math.prod(piece[:-2])),) + piece[-2:])


def _rope_tables(pos):
    half = ROPE // 2
    inv_freq = ROPE_BASE ** (-jnp.arange(half, dtype=F32) / half)
    ang = pos.astype(F32)[:, None] * inv_freq
    cos, sin = jnp.cos(ang), jnp.sin(ang)
    return jnp.tile(cos, (1, 4)), jnp.concatenate([-sin, sin, -sin, sin], axis=1)


def kernel(x, positions, pool_norm, pool_w, pool_scale, kv_in_norm, w_kv_a, kv_latent_norm, w_kv_b, attn_norm, w_q_a, q_latent_norm, w_q_b, w_o, ffn_norm, w_gate, w_up, w_down, final_norm, loss_target, m_pool_norm, m_pool_w, m_pool_scale, m_kv_in_norm, m_w_kv_a, m_kv_latent_norm, m_w_kv_b, m_attn_norm, m_w_q_a, m_q_latent_norm, m_w_q_b, m_w_o, m_ffn_norm, m_w_gate, m_w_up, m_w_down, m_final_norm, v_pool_norm, v_pool_w, v_pool_scale, v_kv_in_norm, v_w_kv_a, v_kv_latent_norm, v_w_kv_b, v_attn_norm, v_w_q_a, v_q_latent_norm, v_w_q_b, v_w_o, v_ffn_norm, v_w_gate, v_w_up, v_w_down, v_final_norm):
    T, D = x.shape[1], x.shape[2]
    H = N_HEADS
    KL = kv_latent_norm.shape[0]
    QL = q_latent_norm.shape[1]
    pg = D // 4
    x0, tgt = x[0], loss_target[0]
    cos, sin = _rope_tables(positions[0])
    chip = 2 * lax.axis_index("x") + lax.axis_index("y")

    n_kva = w_kv_a.shape[1]
    wkva_s = jnp.pad(w_kv_a, ((0, 0), (0, KL + 128 - n_kva)))
    hs = w_q_b.shape[2] // (NOPE + ROPE)
    wqb_s = jnp.pad(w_q_b[0].reshape(QL, hs, NOPE + ROPE), ((0, 0), (0, 0), (0, HEAD_PAD - NOPE - ROPE)))
    wqb_s = wqb_s.reshape(QL, hs * HEAD_PAD)
    chip_s = chip.astype(jnp.int32).reshape(1)
    core_s = lax.axis_index("c").astype(jnp.int32).reshape(1)

    def halved(a):
        return _canon(_halves(a.astype(BF)), 1)

    def gather_group(tag, shards):
        placed = [_place_own(f"place_own{tag}_{i}", s, chip_s) for i, s in enumerate(shards)]
        return _gather_comm(shards, placed)

    def whole(a, *shape):
        return a.reshape(shape)

    n_ff = w_gate.shape[2]
    group0 = gather_group("0", [halved(pool_w[0]), _canon(jnp.stack([pool_norm, pool_scale]), 1),
                                halved(w_gate[0]), halved(w_up[0])])
    group1 = gather_group("1", [halved(w_down[0]), halved(wkva_s), halved(w_kv_b), halved(w_q_a[0]),
                                halved(wqb_s), halved(w_o[0])])
    group2 = gather_group("2", [halved(w_gate[1]), halved(w_up[1]), halved(w_down[1])])
    PW, PV, WG0, WU0 = _comm_call("gather0", group0)
    WG0, WU0 = whole(WG0, 4, D, n_ff), whole(WU0, 4, D, n_ff)
    pv = jnp.transpose(PV.reshape(4, 2, pg), (1, 0, 2)).reshape(2, D)
    pn_full, ps_full = pv[0:1], pv[1:2]

    diff = _pool_fwd(x0, pn_full)
    pre, x1 = _pool_mix(diff, PW, ps_full, x0)
    (h1,) = _norm_fwd("norm_ffn0", x1, ffn_norm[0:1])
    (g0, u0, a0), (WD0, WKVA, WKVB, WQA, WQB, WO) = _ffn_up("ffn_up0", h1, WG0, WU0, comm=group1)
    WD0 = whole(WD0, 4, n_ff, D)
    WKVA = whole(WKVA, D, KL + 128)
    WKVB = whole(WKVB, 4, KL, -1)
    WQA = whole(WQA, D, QL)
    WQB = whole(WQB, 4, QL, -1)
    WO = whole(WO, H * VDIM, D)
    x2 = _ffn_down("ffn_down0", a0, WD0, x1)
    hk, ha = _norm_fwd("norm_attn", x2, jnp.stack([kv_in_norm, attn_norm[0]]))
    kvp = _mm_plain("kv_a", hk, WKVA, F32)
    ckv, kpe = _kv_post(kvp, kv_latent_norm[None], cos, sin)
    kvup = _mm_cols("kv_b", ckv, WKVB, BF)
    qa = _mm_plain("q_a", ha, WQA, F32)
    (ql,) = _norm_fwd("norm_q", qa, q_latent_norm)
    qp = _q_up(ql, WQB, cos, sin)
    (o, lse), (WG1, WU1, WD1) = _flash_fwd(qp, kvup, kpe, comm=group2)
    WG1, WU1, WD1 = whole(WG1, 4, D, n_ff), whole(WU1, 4, D, n_ff), whole(WD1, 4, n_ff, D)
    x3 = _mm_plain("attn_out", o, WO, F32, res=x2)
    (h3,) = _norm_fwd("norm_ffn1", x3, ffn_norm[1:2])
    (g1, u1, a1), _ = _ffn_up("ffn_up1", h3, WG1, WU1)
    x4 = _ffn_down("ffn_down1", a1, WD1, x3)
    loss_part, dx4, d_final, dx4b = _final_loss(x4, final_norm[None], tgt)

    def pair_views(grads):
        return [_canon(a.reshape((4, 2, a.shape[1] // 2) + a.shape[2:]), 2) for a in grads]

    def pair_sums(tag, full, theirs):
        return [_pair_add(f"pair_add{tag}_{i}", a, b, core_s) for i, (a, b) in enumerate(zip(full, theirs))]

    def pair_stage(tag, grads):
        full = pair_views(grads)
        return pair_sums(tag, full, _comm_call("reduce_pair" + tag, _pair_comm(full)))

    dWD1 = _mm_tn_rows("ffn_dwd1", a1, dx4b, 4, BF)
    dg1, du1 = _ffn_bwd_act("ffn_bwd_act1", dx4b, WD1, g1, u1)
    dWG1 = _mm_tn_cols("ffn_dwg1", h3, dg1, 4, BF)
    dWU1 = _mm_tn_cols("ffn_dwu1", h3, du1, 4, BF)
    full1 = pair_views([dWG1, dWU1, dWD1])
    dh3, theirs1 = _ffn_dh("ffn_dh1", dg1, WG1, du1, WU1, comm=_pair_comm(full1))
    sums1 = pair_sums("1", full1, theirs1)
    dx3, d_ffn1, dx3b = _norm_bwd("norm_ffn1_bwd", x3, ffn_norm[1:2], [dh3], dx4)

    do = _mm_nt_plain("attn_out_dx", dx3b, WO, BF)
    dWO = _mm_tn_plain("attn_out_dw", o, dx3b, BF)
    tb = _tile(T, 512)
    lse_rows = lse.reshape(H, T // tb, 1, tb)
    delta_rows = _attn_delta(o, do).reshape(H, T // tb, 1, tb)
    (dkvup, dkpe_h, dq_raw), landed1 = _flash_bwd(qp, kvup, kpe, do, lse_rows, delta_rows,
                                                  comm=_chips_comm(sums1))
    dqp = _dq_finish(dq_raw, cos, sin)
    dql = _mm_nt_cols("q_b_dx", dqp, WQB, F32)
    dWQB = _mm_tn_cols("q_b_dw", ql, dqp, 4, BF)
    _, d_qln, dqa = _norm_bwd("norm_q_bwd", qa, q_latent_norm, [dql], None)
    dha = _mm_nt_plain("q_a_dx", dqa, WQA, F32)
    dWQA = _mm_tn_plain("q_a_dw", ha, dqa, BF)
    dckv = _mm_nt_cols("kv_b_dx", dkvup, WKVB, F32)
    dWKVB = _mm_tn_cols("kv_b_dw", ckv, dkvup, 4, BF)
    dkvp, d_kvln = _kv_post_bwd(kvp, kv_latent_norm[None], dckv, dkpe_h, cos, sin)
    dhk = _mm_nt_plain("kv_a_dx", dkvp, WKVA, F32)
    dWKVA = _mm_tn_plain("kv_a_dw", hk, dkvp, BF)
    dx2, d_n2, dx2b = _norm_bwd("norm_attn_bwd", x2, jnp.stack([kv_in_norm, attn_norm[0]]), [dhk, dha], dx3)

    dWD0 = _mm_tn_rows("ffn_dwd0", a0, dx2b, 4, BF)
    dg0, du0 = _ffn_bwd_act("ffn_bwd_act0", dx2b, WD0, g0, u0)
    dWG0 = _mm_tn_cols("ffn_dwg0", h1, dg0, 4, BF)
    dWU0 = _mm_tn_cols("ffn_dwu0", h1, du0, 4, BF)
    sums0 = pair_stage("0", [dWG0, dWU0, dWD0])
    dh1, landed0 = _ffn_dh("ffn_dh0", dg0, WG0, du0, WU0, comm=_chips_comm(sums0))
    dx1, d_ffn0, _ = _norm_bwd("norm_ffn0_bwd", x1, ffn_norm[0:1], [dh1], dx2)

    dpre, d_ps = _pool_dpre(dx1, pre, ps_full)
    ddiff = _pool_ddiff(dpre, PW)
    dPW = _pool_dw(diff, dpre)
    dx0, d_pn = _pool_bwd(x0, pn_full, ddiff, dx1)

    sums_a = pair_stage("a", [dPW, dWKVA.reshape(4, D // 4, KL + 128), dWKVB, dWQA.reshape(4, D // 4, QL), dWQB,
                              dWO.reshape(4, H * VDIM // 4, D)])
    landed_a = _comm_call("reduce_chips", _chips_comm(sums_a))
    sums = sums_a + sums0 + sums1
    landed = list(landed_a) + list(landed0) + list(landed1)
    halves = [_chip_sum(f"chip_sum{i}", a, b, chip_s) for i, (a, b) in enumerate(zip(sums, landed))]
    other = _comm_call("share_halves", _share_comm(halves))

    def rows(a):
        return a.reshape((1, -1, a.shape[-1]))

    def unpad_kva(a):
        return rows(a)[:, :, :n_kva]

    def unpad_qb(a):
        return rows(a).reshape(1, -1, hs, HEAD_PAD)[:, :, :, :NOPE + ROPE].reshape(1, -1, hs * (NOPE + ROPE))

    def layers(a0_, a1_):
        return jnp.concatenate([rows(a0_), rows(a1_)], axis=0)

    def by_name(hv):
        return {"pool_w": rows(hv[0]), "w_kv_a": unpad_kva(hv[1]), "w_kv_b": rows(hv[2]), "w_q_a": rows(hv[3]),
                "w_q_b": unpad_qb(hv[4]), "w_o": rows(hv[5]), "w_gate": layers(hv[6], hv[9]),
                "w_up": layers(hv[7], hv[10]), "w_down": layers(hv[8], hv[11])}

    big = ["pool_w", "w_kv_a", "w_kv_b", "w_q_a", "w_q_b", "w_o", "w_gate", "w_up", "w_down"]
    g_mine, g_other = by_name(halves), by_name(other)

    lat = jnp.concatenate([d_kvln[0], d_qln[0], jnp.zeros((D - KL - QL,), F32)])
    lrow = jnp.pad(loss_part[0], (0, D - 1))
    small = jnp.stack([d_n2[0], d_n2[1], d_ffn0[0], d_ffn1[0], d_final[0], lat, d_pn[0], d_ps[0], lrow]
                      + [jnp.zeros((D,), F32)] * 7)
    red = _allreduce_small(small)
    loss = red[8, 0]
    g_kv_in, g_attn, g_final = red[0], red[1:2], red[4]
    g_ffn = red[2:4]
    g_kvln, g_qln = red[5, :KL], red[5:6, KL:KL + QL]
    g_pn = lax.dynamic_slice(red[6:7], (0, chip * pg), (1, pg))
    g_ps = lax.dynamic_slice(red[7:8], (0, chip * pg), (1, pg))

    grads = {"pool_norm": g_pn, "pool_scale": g_ps, "kv_in_norm": g_kv_in, "kv_latent_norm": g_kvln,
             "attn_norm": g_attn, "q_latent_norm": g_qln, "ffn_norm": g_ffn, "final_norm": g_final}
    weights = dict(pool_norm=pool_norm, pool_w=pool_w, pool_scale=pool_scale, kv_in_norm=kv_in_norm, w_kv_a=w_kv_a,
                   kv_latent_norm=kv_latent_norm, w_kv_b=w_kv_b, attn_norm=attn_norm, w_q_a=w_q_a,
                   q_latent_norm=q_latent_norm, w_q_b=w_q_b, w_o=w_o, ffn_norm=ffn_norm, w_gate=w_gate, w_up=w_up,
                   w_down=w_down, final_norm=final_norm)
    ms = dict(pool_norm=m_pool_norm, pool_w=m_pool_w, pool_scale=m_pool_scale, kv_in_norm=m_kv_in_norm,
              w_kv_a=m_w_kv_a, kv_latent_norm=m_kv_latent_norm, w_kv_b=m_w_kv_b, attn_norm=m_attn_norm,
              w_q_a=m_w_q_a, q_latent_norm=m_q_latent_norm, w_q_b=m_w_q_b, w_o=m_w_o, ffn_norm=m_ffn_norm,
              w_gate=m_w_gate, w_up=m_w_up, w_down=m_w_down, final_norm=m_final_norm)
    vs = dict(pool_norm=v_pool_norm, pool_w=v_pool_w, pool_scale=v_pool_scale, kv_in_norm=v_kv_in_norm,
              w_kv_a=v_w_kv_a, kv_latent_norm=v_kv_latent_norm, w_kv_b=v_w_kv_b, attn_norm=v_attn_norm,
              w_q_a=v_w_q_a, q_latent_norm=v_q_latent_norm, w_q_b=v_w_q_b, w_o=v_w_o, ffn_norm=v_ffn_norm,
              w_gate=v_w_gate, w_up=v_w_up, w_down=v_w_down, final_norm=v_final_norm)
    names = list(weights)

    def as2d(a):
        return a.reshape((-1, a.shape[-1]))

    delta_w, new_m, new_v = {}, {}, {}
    for nm in big:
        n_layers = g_mine[nm].shape[0]

        def two_halves(a):
            a = as2d(a)
            return a.reshape(n_layers, 2, a.shape[0] // (2 * n_layers), a.shape[1])

        g, d, m2, v2 = _adamw_halves("adamw_" + nm, two_halves(weights[nm]), two_halves(ms[nm]),
                                     two_halves(vs[nm]), g_mine[nm], g_other[nm], core_s)
        shp = weights[nm].shape
        grads[nm], delta_w[nm], new_m[nm], new_v[nm] = g.reshape(shp), d.reshape(shp), m2.reshape(shp), v2.reshape(shp)
    groups = [["kv_in_norm", "attn_norm", "ffn_norm", "final_norm"], ["kv_latent_norm", "q_latent_norm"],
              ["pool_norm", "pool_scale"]]
    for gi, grp in enumerate(groups):
        cat = lambda t: jnp.concatenate([as2d(t[nm]) for nm in grp], axis=0)
        d, m2, v2 = _adamw(f"adamw_vec{gi}", cat(weights), cat(grads), cat(ms), cat(vs))
        r0 = 0
        for nm in grp:
            shp = weights[nm].shape
            r = as2d(weights[nm]).shape[0]
            delta_w[nm], new_m[nm], new_v[nm] = (d[r0:r0 + r].reshape(shp), m2[r0:r0 + r].reshape(shp),
                                                 v2[r0:r0 + r].reshape(shp))
            r0 += r

    return (loss, dx0[None], *[grads[nm].reshape(weights[nm].shape) for nm in names],
            *[delta_w[nm] for nm in names], *[new_m[nm] for nm in names], *[new_v[nm] for nm in names])
```

```python
import functools
import math

import jax
import jax.numpy as jnp
from jax import lax
from jax.experimental import pallas as pl
from jax.experimental.pallas import tpu as pltpu

BF = jnp.bfloat16
F32 = jnp.float32
MESH = pl.DeviceIdType.MESH

N_HEADS = 16
NOPE = 128
ROPE = 64
VDIM = 128
HEAD_PAD = 256
ROPE_BASE = 10000.0
ATTN_SCALE = 1.0 / math.sqrt(NOPE + ROPE)
POOL_WINDOWS = (2, 4, 8, 16)
HALO = 16
NORM_EPS = 1e-6
ADAM_LR, ADAM_B1, ADAM_B2, ADAM_EPS, ADAM_WD, ADAM_STEP = 0.001, 0.9, 0.999, 1e-08, 0.01, 10
NEG = -1e30

V7X_VMEM_BYTES = 64 * 1024 * 1024
VMEM_CEILING = V7X_VMEM_BYTES - 8 * 1024 * 1024

NN = (((1,), (0,)), ((), ()))
NT = (((1,), (1,)), ((), ()))
TN = (((0,), (0,)), ((), ()))


def _cparams(sem, vmem_mb):
    return pltpu.CompilerParams(dimension_semantics=sem,
                                vmem_limit_bytes=min(vmem_mb * 1024 * 1024, VMEM_CEILING))


def _tile(n, pref):
    if n <= pref:
        return n
    t = (pref // 128) * 128
    while t > 128 and n % t:
        t -= 128
    assert n % t == 0, (n, pref)
    return t


def _sds(shape, dtype):
    return jax.ShapeDtypeStruct(shape, dtype)


ANY = pl.BlockSpec(memory_space=pl.ANY)


class _Comm:
    def __init__(self, ins, out_shapes, alias, n_sems, start, finish):
        self.ins, self.out_shapes, self.alias, self.n_sems = list(ins), list(out_shapes), dict(alias), n_sems
        self.start, self.finish = start, finish


def _call(name, body, grid, in_specs, out_specs, out_shape, scratch, vmem_mb, operands, comm=None):
    in_specs, out_specs, out_shape, scratch = list(in_specs), list(out_specs), list(out_shape), list(scratch)
    params = _cparams(("arbitrary",) * len(grid), vmem_mb)
    if comm is None:
        res = pl.pallas_call(body, name=name, grid=grid, in_specs=in_specs, out_specs=out_specs,
                             out_shape=out_shape, scratch_shapes=scratch, compiler_params=params)(*operands)
        return list(res), []
    n_in, n_out, n_scr = len(in_specs), len(out_specs), len(scratch)
    c_in, c_out = len(comm.ins), len(comm.out_shapes)

    def hosted(*refs):
        ins, cin = refs[:n_in], refs[n_in:n_in + c_in]
        o0 = n_in + c_in
        outs, cout = refs[o0:o0 + n_out], refs[o0 + n_out:o0 + n_out + c_out]
        s0 = o0 + n_out + c_out
        scr, (send, recv) = refs[s0:s0 + n_scr], refs[s0 + n_scr:]
        ids = [pl.program_id(d) for d in range(len(grid))]
        first = functools.reduce(jnp.logical_and, [i == 0 for i in ids])
        last = functools.reduce(jnp.logical_and, [i == g - 1 for i, g in zip(ids, grid)])

        @pl.when(first)
        def _():
            comm.start(cin, cout, send, recv)

        body(*ins, *outs, *scr)

        @pl.when(last)
        def _():
            comm.finish(cin, cout, send, recv)

    res = pl.pallas_call(
        hosted, name=name, grid=grid, in_specs=in_specs + [ANY] * c_in, out_specs=out_specs + [ANY] * c_out,
        out_shape=out_shape + comm.out_shapes,
        scratch_shapes=scratch + [pltpu.SemaphoreType.DMA((comm.n_sems,)), pltpu.SemaphoreType.DMA((comm.n_sems,))],
        input_output_aliases={n_in + k: n_out + v for k, v in comm.alias.items()}, compiler_params=params,
    )(*operands, *comm.ins)
    return list(res[:n_out]), list(res[n_out:])


def _comm_call(name, comm):
    c_in, c_out = len(comm.ins), len(comm.out_shapes)

    def body(*refs):
        cin, cout = refs[:c_in], refs[c_in:c_in + c_out]
        send, recv = refs[c_in + c_out:]
        comm.start(cin, cout, send, recv)
        comm.finish(cin, cout, send, recv)

    return pl.pallas_call(
        body, name=name, in_specs=[ANY] * c_in, out_specs=[ANY] * c_out, out_shape=comm.out_shapes,
        scratch_shapes=[pltpu.SemaphoreType.DMA((comm.n_sems,)), pltpu.SemaphoreType.DMA((comm.n_sems,))],
        input_output_aliases=dict(comm.alias),
    )(*comm.ins)


def _matmul(name, pairs, pair_specs, dims, grid, nk, extra, extra_specs, out_shapes, out_specs,
            acc_shape, epilogue, vmem_mb, comm=None):
    n_p, n_e, n_o = len(pairs) // 2, len(extra), len(out_shapes)

    def body(*refs):
        ab = refs[:2 * n_p]
        ex = refs[2 * n_p:2 * n_p + n_e]
        outs = refs[2 * n_p + n_e:2 * n_p + n_e + n_o]

        def partial_sum():
            tot = None
            for p in range(n_p):
                a = ab[2 * p][...]
                b = ab[2 * p + 1][...]
                if a.ndim > 2:
                    a = a.reshape(-1, a.shape[-1])
                if b.ndim > 2:
                    b = b.reshape(-1, b.shape[-1])
                d = lax.dot_general(a.astype(BF), b.astype(BF), dims, preferred_element_type=F32)
                tot = d if tot is None else tot + d
            return tot

        if nk == 1:
            epilogue(partial_sum(), ex, outs)
        else:
            acc = refs[-1]
            kk = pl.program_id(2)

            @pl.when(kk == 0)
            def _():
                acc[...] = partial_sum()

            @pl.when(kk > 0)
            def _():
                acc[...] += partial_sum()

            @pl.when(kk == nk - 1)
            def _():
                epilogue(acc[...], ex, outs)

    scratch = [] if nk == 1 else [pltpu.VMEM(acc_shape, F32)]
    res, comm_res = _call(name, body, grid, list(pair_specs) + list(extra_specs), out_specs, out_shapes,
                          scratch, vmem_mb, list(pairs) + list(extra), comm)
    return res if comm is None else (res, comm_res)


def _epi_store(acc, ex, outs):
    outs[0][...] = acc.reshape(outs[0].shape).astype(outs[0].dtype)


def _epi_residual(acc, ex, outs):
    outs[0][...] = (acc + ex[0][...]).astype(outs[0].dtype)


def _swap_halves(x):
    lane = lax.broadcasted_iota(jnp.int32, x.shape, 1)
    return jnp.where((lane % 64) < 32, pltpu.roll(x, 96, 1), pltpu.roll(x, 32, 1))


def _mm_plain(name, a, b, out_dtype, res=None, tm=512, tn=1024):
    M, K = a.shape
    N = b.shape[1]
    tm, tn = _tile(M, tm), _tile(N, tn)
    extra, extra_specs, epi = [], [], _epi_store
    if res is not None:
        extra, extra_specs, epi = [res], [pl.BlockSpec((tm, tn), lambda i, j, k: (i, j))], _epi_residual
    return _matmul(
        name, [a, b],
        [pl.BlockSpec((tm, K), lambda i, j, k: (i, 0)), pl.BlockSpec((K, tn), lambda i, j, k: (0, j))],
        NN, (M // tm, N // tn, 1), 1, extra, extra_specs,
        [_sds((M, N), out_dtype)], [pl.BlockSpec((tm, tn), lambda i, j, k: (i, j))],
        None, epi, 40)[0]


def _mm_cols(name, a, b3, out_dtype, epilogue=_epi_store, extra=(), extra_specs=(), tm=1024):
    M, K = a.shape
    G, _, n = b3.shape
    tm = _tile(M, tm)
    return _matmul(
        name, [a, b3],
        [pl.BlockSpec((tm, K), lambda i, j, k: (i, 0)), pl.BlockSpec((None, K, n), lambda i, j, k: (j, 0, 0))],
        NN, (M // tm, G, 1), 1, list(extra), list(extra_specs),
        [_sds((M, G * n), out_dtype)], [pl.BlockSpec((tm, n), lambda i, j, k: (i, j))],
        None, epilogue, 40)[0]


def _mm_nt_plain(name, a, b, out_dtype, tm=512, tn=1024):
    M, K = a.shape
    N = b.shape[0]
    tm, tn = _tile(M, tm), _tile(N, tn)
    return _matmul(
        name, [a, b],
        [pl.BlockSpec((tm, K), lambda i, j, k: (i, 0)), pl.BlockSpec((tn, K), lambda i, j, k: (j, 0))],
        NT, (M // tm, N // tn, 1), 1, [], [],
        [_sds((M, N), out_dtype)], [pl.BlockSpec((tm, tn), lambda i, j, k: (i, j))],
        None, _epi_store, 44)[0]


def _mm_nt_cols(name, a, b3, out_dtype, tm=2048):
    M = a.shape[0]
    G, K, n = b3.shape
    tm = _tile(M, tm)
    return _matmul(
        name, [a, b3],
        [pl.BlockSpec((tm, n), lambda i, j, k: (i, k)), pl.BlockSpec((None, K, n), lambda i, j, k: (k, 0, 0))],
        NT, (M // tm, 1, G), G, [], [],
        [_sds((M, K), out_dtype)], [pl.BlockSpec((tm, K), lambda i, j, k: (i, 0))],
        (tm, K), _epi_store, 40)[0]


def _mm_tn_plain(name, a, b, out_dtype, tt=512, tn=1024):
    T, K = a.shape
    N = b.shape[1]
    tt, tn = _tile(T, tt), _tile(N, tn)
    return _matmul(
        name, [a, b],
        [pl.BlockSpec((tt, K), lambda i, j, k: (k, 0)), pl.BlockSpec((tt, tn), lambda i, j, k: (k, j))],
        TN, (1, N // tn, T // tt), T // tt, [], [],
        [_sds((K, N), out_dtype)], [pl.BlockSpec((K, tn), lambda i, j, k: (0, j))],
        (K, tn), _epi_store, 48)[0]


def _mm_tn_cols(name, a, b, G, out_dtype, tt=1024):
    T, K = a.shape
    n = b.shape[1] // G
    tt = _tile(T, tt)
    return _matmul(
        name, [a, b],
        [pl.BlockSpec((tt, K), lambda i, j, k: (k, 0)), pl.BlockSpec((tt, n), lambda i, j, k: (k, j))],
        TN, (1, G, T // tt), T // tt, [], [],
        [_sds((G, K, n), out_dtype)], [pl.BlockSpec((None, K, n), lambda i, j, k: (j, 0, 0))],
        (K, n), _epi_store, 48)[0]


def _norm_fwd(name, x, gains, tb=512):
    T, D = x.shape
    G = gains.shape[0]
    tb = _tile(T, tb)

    def body(x_ref, g_ref, *outs):
        xv = x_ref[...]
        xh = xv * lax.rsqrt(jnp.mean(xv * xv, axis=-1, keepdims=True) + NORM_EPS)
        for g in range(G):
            outs[g][...] = (xh * g_ref[g:g + 1, :]).astype(BF)

    return pl.pallas_call(
        body, name=name, grid=(T // tb,),
        in_specs=[pl.BlockSpec((tb, D), lambda i: (i, 0)), pl.BlockSpec((G, D), lambda i: (0, 0))],
        out_specs=[pl.BlockSpec((tb, D), lambda i: (i, 0))] * G,
        out_shape=[_sds((T, D), BF)] * G,
        compiler_params=_cparams(("arbitrary",), 40),
    )(x, gains)


def _norm_bwd(name, x, gains, dhs, dres, tb=256):
    T, D = x.shape
    G = gains.shape[0]
    tb = _tile(T, tb)
    has_res = dres is not None

    def body(*refs):
        x_ref, g_ref = refs[0], refs[1]
        dh_refs = refs[2:2 + G]
        res_ref = refs[2 + G] if has_res else None
        dx_ref, dg_ref, dxb_ref = refs[-3], refs[-2], refs[-1]
        i = pl.program_id(0)
        xv = x_ref[...]
        r = lax.rsqrt(jnp.mean(xv * xv, axis=-1, keepdims=True) + NORM_EPS)
        xh = xv * r
        dx = res_ref[...] if has_res else jnp.zeros_like(xv)
        rows = []
        for g in range(G):
            dh = dh_refs[g][...].astype(F32)
            dy = dh * g_ref[g:g + 1, :]
            dx = dx + r * (dy - xh * jnp.mean(dy * xh, axis=-1, keepdims=True))
            rows.append(jnp.sum(dh * xh, axis=0, keepdims=True))
        dx_ref[...] = dx
        dxb_ref[...] = dx.astype(BF)

        @pl.when(i == 0)
        def _():
            for g in range(G):
                dg_ref[g:g + 1, :] = rows[g]

        @pl.when(i > 0)
        def _():
            for g in range(G):
                dg_ref[g:g + 1, :] += rows[g]

    blk = pl.BlockSpec((tb, D), lambda i: (i, 0))
    ins = [x, gains] + list(dhs) + ([dres] if has_res else [])
    in_specs = [blk, pl.BlockSpec((G, D), lambda i: (0, 0))] + [blk] * (G + (1 if has_res else 0))
    return pl.pallas_call(
        body, name=name, grid=(T // tb,), in_specs=in_specs,
        out_specs=[blk, pl.BlockSpec((G, D), lambda i: (0, 0)), blk],
        out_shape=[_sds((T, D), F32), _sds((G, D), F32), _sds((T, D), BF)],
        compiler_params=_cparams(("arbitrary",), 48),
    )(*ins)


def _pool_fwd(x, gain, tb=256):
    T, D = x.shape
    tb = _tile(T, tb)
    pg = D // len(POOL_WINDOWS)
    per = tb // HALO

    def body(x_ref, xp_ref, g_ref, diff_ref):
        i = pl.program_id(0)
        xx = jnp.concatenate([xp_ref[...], x_ref[...]], axis=0)
        h = xx * lax.rsqrt(jnp.mean(xx * xx, axis=-1, keepdims=True) + NORM_EPS) * g_ref[...]
        row = lax.broadcasted_iota(jnp.int32, (HALO + tb, 1), 0)
        h = jnp.where((row >= HALO) | (i > 0), h, 0.0)
        t = i * tb + row[HALO:] - HALO
        for g, w in enumerate(POOL_WINDOWS):
            hg = h[:, g * pg:(g + 1) * pg]
            s, k = hg, 1
            while k < w:
                s = s + pltpu.roll(s, k, 0)
                k *= 2
            cnt = jnp.minimum(t + 1, w).astype(F32)
            diff_ref[:, g * pg:(g + 1) * pg] = (s[HALO:] / cnt - hg[HALO:]).astype(BF)

    return pl.pallas_call(
        body, name="pool_fwd", grid=(T // tb,),
        in_specs=[pl.BlockSpec((tb, D), lambda i: (i, 0)),
                  pl.BlockSpec((HALO, D), lambda i: (jnp.maximum(i * per - 1, 0), 0)),
                  pl.BlockSpec((1, D), lambda i: (0, 0))],
        out_specs=pl.BlockSpec((tb, D), lambda i: (i, 0)),
        out_shape=_sds((T, D), BF),
        compiler_params=_cparams(("arbitrary",), 40),
    )(x, x, gain)


def _pool_bwd(x, gain, ddiff, dres, tb=256):
    T, D = x.shape
    tb = _tile(T, tb)
    pg = D // len(POOL_WINDOWS)
    per = tb // HALO
    nblk = T // HALO

    def body(x_ref, g_ref, dd_ref, ddn_ref, res_ref, dx_ref, dg_ref):
        i = pl.program_id(0)
        last = i == T // tb - 1
        dd = jnp.concatenate([dd_ref[...], ddn_ref[...]], axis=0)
        row = lax.broadcasted_iota(jnp.int32, (tb + HALO, 1), 0)
        dd = jnp.where((row < tb) | jnp.logical_not(last), dd, 0.0)
        t = i * tb + row
        parts = []
        for g, w in enumerate(POOL_WINDOWS):
            dg_ = dd[:, g * pg:(g + 1) * pg]
            e = dg_ / jnp.minimum(t + 1, w).astype(F32)
            s, k = e, 1
            while k < w:
                s = s + pltpu.roll(s, tb + HALO - k, 0)
                k *= 2
            parts.append(s[:tb] - dg_[:tb])
        dh = jnp.concatenate(parts, axis=1)
        xv = x_ref[...]
        r = lax.rsqrt(jnp.mean(xv * xv, axis=-1, keepdims=True) + NORM_EPS)
        xh = xv * r
        dy = dh * g_ref[...]
        dx_ref[...] = res_ref[...] + r * (dy - xh * jnp.mean(dy * xh, axis=-1, keepdims=True))
        part = jnp.sum(dh * xh, axis=0, keepdims=True)

        @pl.when(i == 0)
        def _():
            dg_ref[...] = part

        @pl.when(i > 0)
        def _():
            dg_ref[...] += part

    blk = pl.BlockSpec((tb, D), lambda i: (i, 0))
    return pl.pallas_call(
        body, name="pool_bwd", grid=(T // tb,),
        in_specs=[blk, pl.BlockSpec((1, D), lambda i: (0, 0)), blk,
                  pl.BlockSpec((HALO, D), lambda i: (jnp.minimum((i + 1) * per, nblk - 1), 0)), blk],
        out_specs=[blk, pl.BlockSpec((1, D), lambda i: (0, 0))],
        out_shape=[_sds((T, D), F32), _sds((1, D), F32)],
        compiler_params=_cparams(("arbitrary",), 48),
    )(x, gain, ddiff, ddiff, dres)


def _pool_w_spec(pg):
    return pl.BlockSpec((4, None, None, pg // 4, pg), lambda i, j, k: (0, j // 2, j % 2, 0, 0))


def _pool_mix(diff, pw, scale, x, tm=512):
    T, D = x.shape
    pg = D // 4
    tm = _tile(T, tm)

    def epi(acc, ex, outs):
        outs[0][...] = acc
        outs[1][...] = ex[1][...] + acc * ex[0][...]

    blk = pl.BlockSpec((tm, pg), lambda i, j, k: (i, j))
    return _matmul(
        "pool_mix", [diff, pw], [blk, _pool_w_spec(pg)], NN, (T // tm, 4, 1), 1,
        [scale, x], [pl.BlockSpec((1, pg), lambda i, j, k: (0, j)), blk],
        [_sds((T, D), F32), _sds((T, D), F32)], [blk, blk], None, epi, 32)


def _pool_dpre(dx, pre, scale, tb=512):
    T, D = dx.shape
    tb = _tile(T, tb)

    def body(dx_ref, pre_ref, s_ref, dpre_ref, ds_ref):
        i = pl.program_id(0)
        d = dx_ref[...]
        dpre_ref[...] = (d * s_ref[...]).astype(BF)
        part = jnp.sum(d * pre_ref[...], axis=0, keepdims=True)

        @pl.when(i == 0)
        def _():
            ds_ref[...] = part

        @pl.when(i > 0)
        def _():
            ds_ref[...] += part

    blk = pl.BlockSpec((tb, D), lambda i: (i, 0))
    vec = pl.BlockSpec((1, D), lambda i: (0, 0))
    return pl.pallas_call(
        body, name="pool_dpre", grid=(T // tb,), in_specs=[blk, blk, vec], out_specs=[blk, vec],
        out_shape=[_sds((T, D), BF), _sds((1, D), F32)],
        compiler_params=_cparams(("arbitrary",), 40),
    )(dx, pre, scale)


def _pool_ddiff(dpre, pw, tm=512):
    T, D = dpre.shape
    pg = D // 4
    tm = _tile(T, tm)
    blk = pl.BlockSpec((tm, pg), lambda i, j, k: (i, j))
    return _matmul("pool_ddiff", [dpre, pw], [blk, _pool_w_spec(pg)], NT, (T // tm, 4, 1), 1, [], [],
                   [_sds((T, D), F32)], [blk], None, _epi_store, 32)[0]


def _pool_dw(diff, dpre, tt=512):
    T, D = diff.shape
    pg = D // 4
    tt = _tile(T, tt)
    blk = pl.BlockSpec((tt, pg), lambda i, j, k: (k, j))
    return _matmul("pool_dw", [diff, dpre], [blk, blk], TN, (1, 4, T // tt), T // tt, [], [],
                   [_sds((4, 2, 2, pg // 4, pg), BF)], [_pool_w_spec(pg)], (pg, pg), _epi_store, 32)[0]


def _ffn_up(name, h, wg, wu, tm=512, comm=None):
    T, D = h.shape
    n = wg.shape[-1]
    F = 4 * n
    tm = _tile(T, tm)

    def body(h_ref, wg_ref, wu_ref, s_ref, ds_ref, a_ref):
        hv = h_ref[...]
        g = jnp.dot(hv, wg_ref[...], preferred_element_type=F32)
        u = jnp.dot(hv, wu_ref[...], preferred_element_type=F32)
        sig = jax.nn.sigmoid(g)
        silu = g * sig
        s_ref[...] = silu.astype(BF)
        ds_ref[...] = (u * (sig * (1.0 + g * (1.0 - sig)))).astype(BF)
        a_ref[...] = (silu * u).astype(BF)

    w_spec = pl.BlockSpec((None, D, n), lambda j, i: (j, 0, 0))
    o_spec = pl.BlockSpec((tm, n), lambda j, i: (i, j))
    return _call(name, body, (4, T // tm), [pl.BlockSpec((tm, D), lambda j, i: (i, 0)), w_spec, w_spec],
                 [o_spec] * 3, [_sds((T, F), BF)] * 3, [], 56, [h, wg, wu], comm)


def _ffn_down(name, a, wd, x, tm=1024, tn=1024):
    T, D = x.shape
    n = wd.shape[1]
    tm, tn = _tile(T, tm), _tile(D, tn)
    return _matmul(
        name, [a, wd],
        [pl.BlockSpec((tm, n), lambda i, j, k: (i, k)),
         pl.BlockSpec((None, n, tn), lambda i, j, k: (k, 0, j))],
        NN, (T // tm, D // tn, 4), 4, [x], [pl.BlockSpec((tm, tn), lambda i, j, k: (i, j))],
        [_sds((T, D), F32)], [pl.BlockSpec((tm, tn), lambda i, j, k: (i, j))],
        (tm, tn), _epi_residual, 40)[0]


def _ffn_bwd_act(name, dx, wd, silu, dsilu_up, tm=512):
    T, D = dx.shape
    n = wd.shape[1]
    tm = _tile(T, tm)

    def body(dx_ref, w_ref, s_ref, ds_ref, dg_ref, du_ref):
        da = lax.dot_general(dx_ref[...], w_ref[...], NT, preferred_element_type=F32)
        dg_ref[...] = (da * ds_ref[...].astype(F32)).astype(BF)
        du_ref[...] = (da * s_ref[...].astype(F32)).astype(BF)

    blk = pl.BlockSpec((tm, n), lambda i, j: (i, j))
    return pl.pallas_call(
        body, name=name, grid=(T // tm, 4),
        in_specs=[pl.BlockSpec((tm, D), lambda i, j: (i, 0)), pl.BlockSpec((None, n, D), lambda i, j: (j, 0, 0)),
                  blk, blk],
        out_specs=[blk, blk], out_shape=[_sds((T, 4 * n), BF)] * 2,
        compiler_params=_cparams(("arbitrary", "arbitrary"), 56),
    )(dx, wd, silu, dsilu_up)


def _ffn_dh(name, dg, wg, du, wu, tm=512, comm=None):
    T = dg.shape[0]
    D, n = wg.shape[1], wg.shape[2]
    tm = _tile(T, tm)
    a_spec = pl.BlockSpec((tm, n), lambda i, j, k: (i, k))
    w_spec = pl.BlockSpec((None, D, n), lambda i, j, k: (k, 0, 0))
    r = _matmul(
        name, [dg, wg, du, wu], [a_spec, w_spec, a_spec, w_spec], NT, (T // tm, 1, 4), 4, [], [],
        [_sds((T, D), F32)], [pl.BlockSpec((tm, D), lambda i, j, k: (i, 0))], (tm, D), _epi_store, 56, comm)
    return (r[0], []) if comm is None else (r[0][0], r[1])


def _mm_tn_rows(name, a, b, G, out_dtype, tt=1024, tn=1024):
    T, N = b.shape
    n = a.shape[1] // G
    tt, tn = _tile(T, tt), _tile(N, tn)
    nj = N // tn
    return _matmul(
        name, [a, b],
        [pl.BlockSpec((tt, n), lambda i, j, k: (k, j // nj)), pl.BlockSpec((tt, tn), lambda i, j, k: (k, j % nj))],
        TN, (1, G * nj, T // tt), T // tt, [], [],
        [_sds((G, n, N), out_dtype)], [pl.BlockSpec((None, n, tn), lambda i, j, k: (j // nj, 0, j % nj))],
        (n, tn), _epi_store, 48)[0]


def _kv_post(kvp, gain, cos, sin, tb=512):
    T, W = kvp.shape
    KL = W - 128
    tb = _tile(T, tb)

    def body(kv_ref, g_ref, c_ref, s_ref, ckv_ref, kpe_ref):
        lat = kv_ref[:, :KL]
        ckv_ref[...] = (lat * lax.rsqrt(jnp.mean(lat * lat, axis=-1, keepdims=True) + NORM_EPS)
                        * g_ref[...]).astype(BF)
        pe = kv_ref[:, KL:]
        kpe_ref[...] = (pe * c_ref[...] + _swap_halves(pe) * s_ref[...]).astype(BF)

    tab = pl.BlockSpec((tb, 128), lambda i: (i, 0))
    return pl.pallas_call(
        body, name="kv_post", grid=(T // tb,),
        in_specs=[pl.BlockSpec((tb, W), lambda i: (i, 0)), pl.BlockSpec((1, KL), lambda i: (0, 0)), tab, tab],
        out_specs=[pl.BlockSpec((tb, KL), lambda i: (i, 0)), tab],
        out_shape=[_sds((T, KL), BF), _sds((T, 128), BF)],
        compiler_params=_cparams(("arbitrary",), 32),
    )(kvp, gain, cos, sin)


def _kv_post_bwd(kvp, gain, dckv, dkpe_heads, cos, sin, tb=256):
    T, W = kvp.shape
    KL = W - 128
    H = dkpe_heads.shape[0]
    tb = _tile(T, tb)

    def body(kv_ref, g_ref, dc_ref, dk_ref, c_ref, s_ref, out_ref, dg_ref):
        i = pl.program_id(0)
        lat = kv_ref[:, :KL]
        r = lax.rsqrt(jnp.mean(lat * lat, axis=-1, keepdims=True) + NORM_EPS)
        xh = lat * r
        dh = dc_ref[...]
        dy = dh * g_ref[...]
        out_ref[:, :KL] = (r * (dy - xh * jnp.mean(dy * xh, axis=-1, keepdims=True))).astype(BF)
        d = dk_ref[0]
        for h in range(1, H):
            d = d + dk_ref[h]
        out_ref[:, KL:] = (d * c_ref[...] - _swap_halves(d) * s_ref[...]).astype(BF)
        part = jnp.sum(dh * xh, axis=0, keepdims=True)

        @pl.when(i == 0)
        def _():
            dg_ref[...] = part

        @pl.when(i > 0)
        def _():
            dg_ref[...] += part

    tab = pl.BlockSpec((tb, 128), lambda i: (i, 0))
    vec = pl.BlockSpec((1, KL), lambda i: (0, 0))
    return pl.pallas_call(
        body, name="kv_post_bwd", grid=(T // tb,),
        in_specs=[pl.BlockSpec((tb, W), lambda i: (i, 0)), vec, pl.BlockSpec((tb, KL), lambda i: (i, 0)),
                  pl.BlockSpec((H, tb, 128), lambda i: (0, i, 0)), tab, tab],
        out_specs=[pl.BlockSpec((tb, W), lambda i: (i, 0)), vec],
        out_shape=[_sds((T, W), BF), _sds((1, KL), F32)],
        compiler_params=_cparams(("arbitrary",), 32),
    )(kvp, gain, dckv, dkpe_heads, cos, sin)


def _q_up(ql, wqb, cos, sin, tm=1024):
    n = wqb.shape[2]
    tm = _tile(ql.shape[0], tm)

    def epi(acc, ex, outs):
        c, s = ex[0][...] * LOG2_SCALE, ex[1][...] * LOG2_SCALE
        for j in range(n // HEAD_PAD):
            a0 = j * HEAD_PAD
            outs[0][:, a0:a0 + NOPE] = (acc[:, a0:a0 + NOPE] * LOG2_SCALE).astype(BF)
            pe = acc[:, a0 + NOPE:a0 + HEAD_PAD]
            outs[0][:, a0 + NOPE:a0 + HEAD_PAD] = (pe * c + _swap_halves(pe) * s).astype(BF)

    tab = pl.BlockSpec((tm, 128), lambda i, j, k: (i, 0))
    return _mm_cols("q_up", ql, wqb, BF, epilogue=epi, extra=[cos, sin], extra_specs=[tab, tab], tm=tm)


def _causal_mask(tb):
    r = lax.broadcasted_iota(jnp.int32, (tb, tb), 0)
    c = lax.broadcasted_iota(jnp.int32, (tb, tb), 1)
    return r, c


HP = 2
LOG2_SCALE = ATTN_SCALE * math.log2(math.e)


def _fill_keys(k_scr, kv_ref, kpe_ref):
    for hh in range(HP):
        k_scr[hh, :, :NOPE] = kv_ref[:, hh * HEAD_PAD:hh * HEAD_PAD + NOPE]
        k_scr[hh, :, NOPE:] = kpe_ref[...]


def _flash_fwd(qp, kvup, kpe, tb=512, comm=None):
    T = qp.shape[0]
    H = qp.shape[1] // HEAD_PAD
    tb = _tile(T, tb)

    def body(q_ref, kv_ref, kpe_ref, o_ref, lse_ref, k_scr, m_scr, l_scr, acc_scr, sa_scr, sb_scr):
        iq = pl.program_id(1)

        @pl.when(iq == 0)
        def _():
            _fill_keys(k_scr, kv_ref, kpe_ref)

        qs = [q_ref[:, hh * HEAD_PAD:(hh + 1) * HEAD_PAD] for hh in range(HP)]

        for hh in range(HP):
            m_scr[hh] = jnp.full((tb, 128), NEG, F32)
            l_scr[hh] = jnp.zeros((tb, 128), F32)
            acc_scr[hh] = jnp.zeros((tb, VDIM), F32)

        def scores(ik, s_buf):
            off = pl.multiple_of(ik * tb, tb)
            for hh in range(HP):
                s_buf[hh] = lax.dot_general(qs[hh], k_scr[hh, pl.ds(off, tb), :], NT, preferred_element_type=F32)

        def update(ik, s_buf, masked):
            off = pl.multiple_of(ik * tb, tb)
            ps = []
            for hh in range(HP):
                s = s_buf[hh]
                if masked:
                    r, c = _causal_mask(tb)
                    s = jnp.where(c <= r, s, NEG)
                m = m_scr[hh]
                m2 = jnp.maximum(m, jnp.max(s, axis=-1, keepdims=True))
                p = jnp.exp2(s - jnp.concatenate([m2] * (tb // 128), axis=1))
                a = jnp.exp2(m - m2)
                l_scr[hh] = a * l_scr[hh] + jnp.sum(p, axis=-1, keepdims=True)
                m_scr[hh] = m2
                ps.append((a, p.astype(BF)))
            for hh in range(HP):
                a, p = ps[hh]
                v = kv_ref[pl.ds(off, tb), hh * HEAD_PAD + NOPE:(hh + 1) * HEAD_PAD]
                acc_scr[hh] = a * acc_scr[hh] + lax.dot_general(p, v, NN, preferred_element_type=F32)

        def trip(j, carry):
            scores(2 * j + 1, sb_scr)
            update(2 * j, sa_scr, False)
            scores(2 * j + 2, sa_scr)
            update(2 * j + 1, sb_scr, False)
            return carry

        scores(0, sa_scr)
        lax.fori_loop(0, iq // 2, trip, 0)

        @pl.when(iq % 2 == 0)
        def _():
            update(iq, sa_scr, True)

        @pl.when(iq % 2 == 1)
        def _():
            scores(iq, sb_scr)
            update(iq - 1, sa_scr, False)
            update(iq, sb_scr, True)
        for hh in range(HP):
            l = l_scr[hh]
            o_ref[:, hh * VDIM:(hh + 1) * VDIM] = (acc_scr[hh] / l).astype(BF)
            lse_ref[hh] = (m_scr[hh] + jnp.log2(l))[:, :1]

    return _call(
        "flash_fwd", body, (H // HP, T // tb),
        [pl.BlockSpec((tb, HP * HEAD_PAD), lambda h, i: (i, h)),
         pl.BlockSpec((T, HP * HEAD_PAD), lambda h, i: (0, h)),
         pl.BlockSpec((T, 128), lambda h, i: (0, 0))],
        [pl.BlockSpec((tb, HP * VDIM), lambda h, i: (i, h)),
         pl.BlockSpec((HP, tb, 1), lambda h, i: (h, i, 0))],
        [_sds((T, H * VDIM), BF), _sds((H, T, 1), F32)],
        [pltpu.VMEM((HP, T, HEAD_PAD), BF), pltpu.VMEM((HP, tb, 128), F32), pltpu.VMEM((HP, tb, 128), F32),
         pltpu.VMEM((HP, tb, VDIM), F32), pltpu.VMEM((HP, tb, tb), F32), pltpu.VMEM((HP, tb, tb), F32)],
        56, [qp, kvup, kpe], comm)


def _attn_delta(o, do, tb=512):
    T = o.shape[0]
    H = o.shape[1] // VDIM
    tb = _tile(T, tb)

    def body(o_ref, do_ref, d_ref):
        for h in range(H):
            cols = slice(h * VDIM, (h + 1) * VDIM)
            d_ref[h] = jnp.sum(o_ref[:, cols].astype(F32) * do_ref[:, cols].astype(F32), axis=-1, keepdims=True)

    blk = pl.BlockSpec((tb, H * VDIM), lambda i: (i, 0))
    return pl.pallas_call(
        body, name="attn_delta", grid=(T // tb,), in_specs=[blk, blk],
        out_specs=pl.BlockSpec((H, tb, 1), lambda i: (0, i, 0)), out_shape=_sds((H, T, 1), F32),
        compiler_params=_cparams(("arbitrary",), 32),
    )(o, do)


def _flash_bwd(qp, kvup, kpe, do, lse_rows, delta_rows, tb=512, comm=None):
    T = qp.shape[0]
    H = qp.shape[1] // HEAD_PAD
    tb = _tile(T, tb)
    nq = T // tb

    def body(kv_ref, kpe_ref, q_ref, do_ref, lse_ref, dl_ref, dkv_ref, dkpe_ref, dq_ref, dk_scr, dv_scr,
             sa_scr, pa_scr, sb_scr, pb_scr):
        ik = pl.program_id(1)

        @pl.when(ik == 0)
        def _():
            dq_ref[...] = jnp.zeros_like(dq_ref)

        k = jnp.concatenate([kv_ref[:, :NOPE], kpe_ref[...]], axis=1)
        v = kv_ref[:, NOPE:]
        dk_scr[...] = jnp.zeros_like(dk_scr)
        dv_scr[...] = jnp.zeros_like(dv_scr)

        def scores(iq, bufs):
            off = pl.multiple_of(iq * tb, tb)
            bufs[0][...] = lax.dot_general(k, q_ref[pl.ds(off, tb), :], NT, preferred_element_type=F32)
            bufs[1][...] = lax.dot_general(v, do_ref[pl.ds(off, tb), :], NT, preferred_element_type=F32)

        def update(iq, bufs, masked):
            off = pl.multiple_of(iq * tb, tb)
            q = q_ref[pl.ds(off, tb), :]
            d = do_ref[pl.ds(off, tb), :]
            st = bufs[0][...]
            if masked:
                r, c = _causal_mask(tb)
                st = jnp.where(r <= c, st, NEG)
            pt = jnp.exp2(st - lse_ref[iq])
            dst = (pt * (bufs[1][...] - dl_ref[iq])).astype(BF)
            dv_scr[...] += lax.dot_general(pt.astype(BF), d, NN, preferred_element_type=F32)
            dk_scr[...] += lax.dot_general(dst, q, NN, preferred_element_type=F32)
            dq_ref[pl.ds(off, tb), :] += lax.dot_general(dst, k, TN, preferred_element_type=F32)

        A, B = (sa_scr, pa_scr), (sb_scr, pb_scr)
        n = nq - 1 - ik
        scores(ik, A)

        @pl.when(n == 0)
        def _():
            update(ik, A, True)

        @pl.when(n > 0)
        def _():
            scores(ik + 1, B)
            update(ik, A, True)

            def trip(j, carry):
                b = ik + 1 + 2 * j
                scores(b + 1, A)
                update(b, B, False)
                scores(b + 2, B)
                update(b + 1, A, False)
                return carry

            pairs = (n - 1) // 2
            lax.fori_loop(0, pairs, trip, 0)
            last = ik + 1 + 2 * pairs

            @pl.when(last == nq - 1)
            def _():
                update(last, B, False)

            @pl.when(last < nq - 1)
            def _():
                scores(last + 1, A)
                update(last, B, False)
                update(last + 1, A, False)
        dk = dk_scr[...] * (ATTN_SCALE / LOG2_SCALE)
        dkv_ref[:, :NOPE] = dk[:, :NOPE].astype(BF)
        dkv_ref[:, NOPE:] = dv_scr[...].astype(BF)
        dkpe_ref[...] = dk[:, NOPE:]

    rows = pl.BlockSpec((None, nq, 1, tb), lambda h, i: (h, 0, 0, 0))
    return _call(
        "flash_bwd", body, (H, nq),
        [pl.BlockSpec((tb, HEAD_PAD), lambda h, i: (i, h)), pl.BlockSpec((tb, 128), lambda h, i: (i, 0)),
         pl.BlockSpec((T, HEAD_PAD), lambda h, i: (0, h)), pl.BlockSpec((T, VDIM), lambda h, i: (0, h)), rows, rows],
        [pl.BlockSpec((tb, HEAD_PAD), lambda h, i: (i, h)), pl.BlockSpec((None, tb, 128), lambda h, i: (h, i, 0)),
         pl.BlockSpec((T, HEAD_PAD), lambda h, i: (0, h))],
        [_sds((T, H * HEAD_PAD), BF), _sds((H, T, 128), F32), _sds((T, H * HEAD_PAD), F32)],
        [pltpu.VMEM((tb, HEAD_PAD), F32), pltpu.VMEM((tb, VDIM), F32)] + [pltpu.VMEM((tb, tb), F32)] * 4, 56,
        [kvup, kpe, qp, do, lse_rows, delta_rows], comm)


def _dq_finish(dq_raw, cos, sin, tb=256):
    T, W = dq_raw.shape
    tb = _tile(T, tb)

    def body(dq_ref, c_ref, s_ref, o_ref):
        c, s = c_ref[...] * ATTN_SCALE, s_ref[...] * ATTN_SCALE
        for h in range(W // HEAD_PAD):
            a0 = h * HEAD_PAD
            o_ref[:, a0:a0 + NOPE] = (dq_ref[:, a0:a0 + NOPE] * ATTN_SCALE).astype(BF)
            dpe = dq_ref[:, a0 + NOPE:a0 + HEAD_PAD]
            o_ref[:, a0 + NOPE:a0 + HEAD_PAD] = (dpe * c - _swap_halves(dpe) * s).astype(BF)

    blk = pl.BlockSpec((tb, W), lambda i: (i, 0))
    tab = pl.BlockSpec((tb, 128), lambda i: (i, 0))
    return pl.pallas_call(
        body, name="dq_finish", grid=(T // tb,), in_specs=[blk, tab, tab], out_specs=blk,
        out_shape=_sds((T, W), BF), compiler_params=_cparams(("arbitrary",), 40),
    )(dq_raw, cos, sin)


def _final_loss(x, gain, target, tb=256):
    T, D = x.shape
    tb = _tile(T, tb)

    def body(x_ref, g_ref, t_ref, loss_ref, dx_ref, dg_ref, dxb_ref):
        i = pl.program_id(0)
        xv = x_ref[...]
        gv = g_ref[...]
        r = lax.rsqrt(jnp.mean(xv * xv, axis=-1, keepdims=True) + NORM_EPS)
        xh = xv * r
        e = xh * gv - t_ref[...]
        lpart = 0.5 * jnp.sum(jnp.mean(e * e, axis=-1, keepdims=True), axis=0, keepdims=True)
        dy = e / D
        dyg = dy * gv
        dx = r * (dyg - xh * jnp.mean(dyg * xh, axis=-1, keepdims=True))
        dx_ref[...] = dx
        dxb_ref[...] = dx.astype(BF)
        gpart = jnp.sum(dy * xh, axis=0, keepdims=True)

        @pl.when(i == 0)
        def _():
            loss_ref[...] = lpart
            dg_ref[...] = gpart

        @pl.when(i > 0)
        def _():
            loss_ref[...] += lpart
            dg_ref[...] += gpart

    blk = pl.BlockSpec((tb, D), lambda i: (i, 0))
    vec = pl.BlockSpec((1, D), lambda i: (0, 0))
    return pl.pallas_call(
        body, name="final_loss", grid=(T // tb,), in_specs=[blk, vec, blk],
        out_specs=[pl.BlockSpec((1, 1), lambda i: (0, 0)), blk, vec, blk],
        out_shape=[_sds((1, 1), F32), _sds((T, D), F32), _sds((1, D), F32), _sds((T, D), BF)],
        compiler_params=_cparams(("arbitrary",), 40),
    )(x, gain, target)


def _row_tile(R, pref=256):
    t = (min(R, pref) // 16) * 16
    while t >= 16:
        if R % t == 0:
            return t
        t -= 16
    return R


def _prefetch_call(body, name, grid, in_specs, out_specs, out_shape, scalar, operands, vmem_mb=32):
    return pl.pallas_call(
        body, name=name, out_shape=out_shape,
        grid_spec=pltpu.PrefetchScalarGridSpec(num_scalar_prefetch=1, grid=grid, in_specs=in_specs,
                                               out_specs=out_specs),
        compiler_params=_cparams(("arbitrary",) * len(grid), vmem_mb),
    )(scalar, *operands)


def _place_own(name, s, chip):
    _, n, R, C = s.shape
    tr = _row_tile(R, 512)

    def body(chip_ref, s_ref, o_ref):
        o_ref[...] = s_ref[...]

    return _prefetch_call(
        body, name, (2, n, R // tr),
        [pl.BlockSpec((None, None, tr, C), lambda h, j, r, cr: (h, j, r, 0))],
        pl.BlockSpec((None, None, None, tr, C), lambda h, j, r, cr: (cr[0], h, j, r, 0)),
        _sds((4,) + s.shape, s.dtype), chip, [s])


def _pair_add(name, g, theirs, core):
    _, _, n, R, C = g.shape
    tr = _row_tile(R)

    def body(c_ref, a_ref, b_ref, o_ref):
        o_ref[...] = (a_ref[...].astype(F32) + b_ref[...].astype(F32)).astype(BF)

    blk = pl.BlockSpec((None, None, tr, C), lambda k, j, r, cr: (k, j, r, 0))
    return _prefetch_call(
        body, name, (4, n, R // tr),
        [pl.BlockSpec((None, None, None, tr, C), lambda k, j, r, cr: (k, cr[0], j, r, 0)), blk], blk,
        _sds(theirs.shape, BF), core, [g, theirs])


def _chip_sum(name, sums, landed, chip):
    _, n, R, C = sums.shape
    tr = _row_tile(R)

    def body(chip_ref, own_ref, l_ref, o_ref):
        s = own_ref[...].astype(F32)
        for j in range(3):
            s = s + l_ref[j].astype(F32)
        o_ref[...] = s

    return _prefetch_call(
        body, name, (n, R // tr),
        [pl.BlockSpec((None, None, tr, C), lambda j, r, cr: (cr[0], j, r, 0)),
         pl.BlockSpec((3, None, tr, C), lambda j, r, cr: (0, j, r, 0))],
        pl.BlockSpec((None, tr, C), lambda j, r, cr: (j, r, 0)),
        _sds((n, R, C), F32), chip, [sums, landed])


def _adamw_halves(name, w, m, v, g_mine, g_theirs, core, tr=128):
    L, _, R, C = w.shape
    tr = _row_tile(R, tr)
    c1 = 1.0 - ADAM_B1 ** ADAM_STEP
    c2 = 1.0 - ADAM_B2 ** ADAM_STEP

    def body(c_ref, w_ref, m_ref, v_ref, gm_ref, gt_ref, g_ref, d_ref, nm_ref, nv_ref):
        gv = jnp.where(pl.program_id(1) == c_ref[0], gm_ref[...], gt_ref[...])
        nm = ADAM_B1 * m_ref[...] + (1.0 - ADAM_B1) * gv
        nv = ADAM_B2 * v_ref[...] + (1.0 - ADAM_B2) * (gv * gv)
        g_ref[...] = gv
        nm_ref[...] = nm
        nv_ref[...] = nv
        d_ref[...] = -ADAM_LR * ((nm / c1) / (jnp.sqrt(nv / c2) + ADAM_EPS) + ADAM_WD * w_ref[...])

    full = pl.BlockSpec((None, None, tr, C), lambda l, h, r, cr: (l, h, r, 0))
    half = pl.BlockSpec((None, tr, C), lambda l, h, r, cr: (l, r, 0))
    return _prefetch_call(
        body, name, (L, 2, R // tr), [full, full, full, half, half], [full] * 4,
        [_sds(w.shape, F32)] * 4, core, [w, m, v, g_mine, g_theirs])


def _adamw(name, w, g, m, v, tr=128):
    R, C = w.shape
    tr = _tile(R, tr) if R % 8 == 0 else R
    c1 = 1.0 - ADAM_B1 ** ADAM_STEP
    c2 = 1.0 - ADAM_B2 ** ADAM_STEP

    def body(w_ref, g_ref, m_ref, v_ref, d_ref, nm_ref, nv_ref):
        gv = g_ref[...]
        nm = ADAM_B1 * m_ref[...] + (1.0 - ADAM_B1) * gv
        nv = ADAM_B2 * v_ref[...] + (1.0 - ADAM_B2) * (gv * gv)
        nm_ref[...] = nm
        nv_ref[...] = nv
        d_ref[...] = -ADAM_LR * ((nm / c1) / (jnp.sqrt(nv / c2) + ADAM_EPS) + ADAM_WD * w_ref[...])

    blk = pl.BlockSpec((tr, C), lambda i: (i, 0))
    return pl.pallas_call(
        body, name=name, grid=(R // tr,), in_specs=[blk] * 4, out_specs=[blk] * 3,
        out_shape=[_sds((R, C), F32)] * 3, compiler_params=_cparams(("arbitrary",), 32),
    )(w, g, m, v)


def _place():
    x, y, c = lax.axis_index("x"), lax.axis_index("y"), lax.axis_index("c")
    chips = [(1 - x, y), (x, 1 - y), (1 - x, 1 - y)]
    return x, y, c, chips


def _gather_comm(shards, placed):
    n = len(shards)

    def copies(cin, cout, send, recv):
        src, dst = cin[:n], cout
        x, y, c, chips = _place()
        me = 2 * x + y

        def remote(i, k, s, d, to):
            return pltpu.make_async_remote_copy(src_ref=s, dst_ref=d, send_sem=send.at[6 * i + k],
                                                recv_sem=recv.at[6 * i + k], device_id=to, device_id_type=MESH)

        first = [remote(i, j, src[i].at[c], dst[i].at[me, c], (*chip, c))
                 for i in range(n) for j, chip in enumerate(chips)]
        return remote, first, dst, (x, y, c, chips)

    def start(cin, cout, send, recv):
        for cp in copies(cin, cout, send, recv)[1]:
            cp.start()

    def finish(cin, cout, send, recv):
        remote, first, dst, (x, y, c, chips) = copies(cin, cout, send, recv)
        sibling = (x, y, 1 - c)
        passed = []
        for i in range(n):
            for j, (px, py) in enumerate(chips):
                slot = dst[i].at[2 * px + py, c]
                remote(i, j, slot, slot, (px, py, c)).wait_recv()
                cp = remote(i, 3 + j, slot, slot, sibling)
                cp.start()
                passed.append(cp)
        for i in range(n):
            for j, (px, py) in enumerate(chips):
                slot = dst[i].at[2 * px + py, 1 - c]
                remote(i, 3 + j, slot, slot, sibling).wait_recv()
        for cp in first + passed:
            cp.wait_send()

    return _Comm(list(shards) + list(placed), [_sds(p.shape, p.dtype) for p in placed],
                 {n + i: i for i in range(n)}, 6 * n, start, finish)


def _pair_comm(grads):
    n = len(grads)

    def copies(cin, cout, send, recv):
        x, y, c, _ = _place()
        return [pltpu.make_async_remote_copy(
            src_ref=cin[i].at[k, 1 - c], dst_ref=cout[i].at[k], send_sem=send.at[4 * i + k],
            recv_sem=recv.at[4 * i + k], device_id=(x, y, 1 - c), device_id_type=MESH)
            for i in range(n) for k in range(4)]

    def start(cin, cout, send, recv):
        for cp in copies(cin, cout, send, recv):
            cp.start()

    def finish(cin, cout, send, recv):
        for cp in copies(cin, cout, send, recv):
            cp.wait()

    return _Comm(grads, [_sds((4,) + a.shape[2:], a.dtype) for a in grads], {}, 4 * n, start, finish)


def _chips_comm(sums):
    n = len(sums)

    def copies(cin, cout, send, recv):
        x, y, c, chips = _place()
        return [pltpu.make_async_remote_copy(
            src_ref=cin[i].at[2 * px + py], dst_ref=cout[i].at[j], send_sem=send.at[3 * i + j],
            recv_sem=recv.at[3 * i + j], device_id=(px, py, c), device_id_type=MESH)
            for i in range(n) for j, (px, py) in enumerate(chips)]

    def start(cin, cout, send, recv):
        for cp in copies(cin, cout, send, recv):
            cp.start()

    def finish(cin, cout, send, recv):
        for cp in copies(cin, cout, send, recv):
            cp.wait()

    return _Comm(sums, [_sds((3,) + a.shape[1:], a.dtype) for a in sums], {}, 3 * n, start, finish)


def _share_comm(halves):
    n = len(halves)

    def copies(cin, cout, send, recv):
        x, y, c, _ = _place()
        return [pltpu.make_async_remote_copy(
            src_ref=cin[i], dst_ref=cout[i], send_sem=send.at[i], recv_sem=recv.at[i],
            device_id=(x, y, 1 - c), device_id_type=MESH) for i in range(n)]

    def start(cin, cout, send, recv):
        for cp in copies(cin, cout, send, recv):
            cp.start()

    def finish(cin, cout, send, recv):
        for cp in copies(cin, cout, send, recv):
            cp.wait()

    return _Comm(halves, [_sds(a.shape, a.dtype) for a in halves], {}, n, start, finish)


def _allreduce_small(v):
    R, C = v.shape

    def body(v_ref, out_ref, land, send, recv):
        x, y, c, _ = _place()
        me = 4 * x + 2 * y + c
        cps = []
        for m in range(1, 8):
            fx, fy, fc = (m >> 2) & 1, (m >> 1) & 1, m & 1
            peer = (x ^ fx, y ^ fy, c ^ fc)
            cp = pltpu.make_async_remote_copy(
                src_ref=v_ref, dst_ref=land.at[me], send_sem=send.at[m - 1], recv_sem=recv.at[m - 1],
                device_id=peer, device_id_type=MESH)
            cp.start()
            cps.append(cp)
        land[me] = v_ref[...]
        for cp in cps:
            cp.wait()
        s = land[0]
        for d in range(1, 8):
            s = s + land[d]
        out_ref[...] = s

    return pl.pallas_call(
        body, name="allreduce_small",
        in_specs=[pl.BlockSpec(memory_space=pltpu.VMEM)], out_specs=pl.BlockSpec(memory_space=pltpu.VMEM),
        out_shape=_sds((R, C), F32),
        scratch_shapes=[pltpu.VMEM((8, R, C), F32), pltpu.SemaphoreType.DMA((7,)), pltpu.SemaphoreType.DMA((7,))],
    )(v)


def _halves(a):
    return a.reshape((2, a.shape[0] // 2) + a.shape[1:])


def _canon(a, lead):
    piece = a.shape[lead:]
    return a.reshape(a.shape[:lead] + (int(math.prod(piece[:-2])),) + piece[-2:])


def _rope_tables(pos):
    half = ROPE // 2
    inv_freq = ROPE_BASE ** (-jnp.arange(half, dtype=F32) / half)
    ang = pos.astype(F32)[:, None] * inv_freq
    cos, sin = jnp.cos(ang), jnp.sin(ang)
    return jnp.tile(cos, (1, 4)), jnp.concatenate([-sin, sin, -sin, sin], axis=1)


def kernel(x, positions, pool_norm, pool_w, pool_scale, kv_in_norm, w_kv_a, kv_latent_norm, w_kv_b, attn_norm, w_q_a, q_latent_norm, w_q_b, w_o, ffn_norm, w_gate, w_up, w_down, final_norm, loss_target, m_pool_norm, m_pool_w, m_pool_scale, m_kv_in_norm, m_w_kv_a, m_kv_latent_norm, m_w_kv_b, m_attn_norm, m_w_q_a, m_q_latent_norm, m_w_q_b, m_w_o, m_ffn_norm, m_w_gate, m_w_up, m_w_down, m_final_norm, v_pool_norm, v_pool_w, v_pool_scale, v_kv_in_norm, v_w_kv_a, v_kv_latent_norm, v_w_kv_b, v_attn_norm, v_w_q_a, v_q_latent_norm, v_w_q_b, v_w_o, v_ffn_norm, v_w_gate, v_w_up, v_w_down, v_final_norm):
    T, D = x.shape[1], x.shape[2]
    H = N_HEADS
    KL = kv_latent_norm.shape[0]
    QL = q_latent_norm.shape[1]
    pg = D // 4
    x0, tgt = x[0], loss_target[0]
    cos, sin = _rope_tables(positions[0])
    chip = 2 * lax.axis_index("x") + lax.axis_index("y")

    n_kva = w_kv_a.shape[1]
    wkva_s = jnp.pad(w_kv_a, ((0, 0), (0, KL + 128 - n_kva)))
    hs = w_q_b.shape[2] // (NOPE + ROPE)
    wqb_s = jnp.pad(w_q_b[0].reshape(QL, hs, NOPE + ROPE), ((0, 0), (0, 0), (0, HEAD_PAD - NOPE - ROPE)))
    wqb_s = wqb_s.reshape(QL, hs * HEAD_PAD)
    chip_s = chip.astype(jnp.int32).reshape(1)
    core_s = lax.axis_index("c").astype(jnp.int32).reshape(1)

    def halved(a):
        return _canon(_halves(a.astype(BF)), 1)

    def gather_group(tag, shards):
        placed = [_place_own(f"place_own{tag}_{i}", s, chip_s) for i, s in enumerate(shards)]
        return _gather_comm(shards, placed)

    def whole(a, *shape):
        return a.reshape(shape)

    n_ff = w_gate.shape[2]
    group0 = gather_group("0", [halved(pool_w[0]), _canon(jnp.stack([pool_norm, pool_scale]), 1),
                                halved(w_gate[0]), halved(w_up[0])])
    group1 = gather_group("1", [halved(w_down[0]), halved(wkva_s), halved(w_kv_b), halved(w_q_a[0]),
                                halved(wqb_s), halved(w_o[0])])
    group2 = gather_group("2", [halved(w_gate[1]), halved(w_up[1]), halved(w_down[1])])
    PW, PV, WG0, WU0 = _comm_call("gather0", group0)
    WG0, WU0 = whole(WG0, 4, D, n_ff), whole(WU0, 4, D, n_ff)
    pv = jnp.transpose(PV.reshape(4, 2, pg), (1, 0, 2)).reshape(2, D)
    pn_full, ps_full = pv[0:1], pv[1:2]

    diff = _pool_fwd(x0, pn_full)
    pre, x1 = _pool_mix(diff, PW, ps_full, x0)
    (h1,) = _norm_fwd("norm_ffn0", x1, ffn_norm[0:1])
    (silu0, dsu0, a0), (WD0, WKVA, WKVB, WQA, WQB, WO) = _ffn_up("ffn_up0", h1, WG0, WU0, comm=group1)
    WD0 = whole(WD0, 4, n_ff, D)
    WKVA = whole(WKVA, D, KL + 128)
    WKVB = whole(WKVB, 4, KL, -1)
    WQA = whole(WQA, D, QL)
    WQB = whole(WQB, 4, QL, -1)
    WO = whole(WO, H * VDIM, D)
    x2 = _ffn_down("ffn_down0", a0, WD0, x1)
    hk, ha = _norm_fwd("norm_attn", x2, jnp.stack([kv_in_norm, attn_norm[0]]))
    kvp = _mm_plain("kv_a", hk, WKVA, F32)
    ckv, kpe = _kv_post(kvp, kv_latent_norm[None], cos, sin)
    kvup = _mm_cols("kv_b", ckv, WKVB, BF)
    qa = _mm_plain("q_a", ha, WQA, F32)
    (ql,) = _norm_fwd("norm_q", qa, q_latent_norm)
    qp = _q_up(ql, WQB, cos, sin)
    (o, lse), (WG1, WU1, WD1) = _flash_fwd(qp, kvup, kpe, comm=group2)
    WG1, WU1, WD1 = whole(WG1, 4, D, n_ff), whole(WU1, 4, D, n_ff), whole(WD1, 4, n_ff, D)
    x3 = _mm_plain("attn_out", o, WO, F32, res=x2)
    (h3,) = _norm_fwd("norm_ffn1", x3, ffn_norm[1:2])
    (silu1, dsu1, a1), _ = _ffn_up("ffn_up1", h3, WG1, WU1)
    x4 = _ffn_down("ffn_down1", a1, WD1, x3)
    loss_part, dx4, d_final, dx4b = _final_loss(x4, final_norm[None], tgt)

    def pair_views(grads):
        return [_canon(a.reshape((4, 2, a.shape[1] // 2) + a.shape[2:]), 2) for a in grads]

    def pair_sums(tag, full, theirs):
        return [_pair_add(f"pair_add{tag}_{i}", a, b, core_s) for i, (a, b) in enumerate(zip(full, theirs))]

    def pair_stage(tag, grads):
        full = pair_views(grads)
        return pair_sums(tag, full, _comm_call("reduce_pair" + tag, _pair_comm(full)))

    dWD1 = _mm_tn_rows("ffn_dwd1", a1, dx4b, 4, BF)
    dg1, du1 = _ffn_bwd_act("ffn_bwd_act1", dx4b, WD1, silu1, dsu1)
    dWG1 = _mm_tn_cols("ffn_dwg1", h3, dg1, 4, BF)
    dWU1 = _mm_tn_cols("ffn_dwu1", h3, du1, 4, BF)
    full1 = pair_views([dWG1, dWU1, dWD1])
    dh3, theirs1 = _ffn_dh("ffn_dh1", dg1, WG1, du1, WU1, comm=_pair_comm(full1))
    sums1 = pair_sums("1", full1, theirs1)
    dx3, d_ffn1, dx3b = _norm_bwd("norm_ffn1_bwd", x3, ffn_norm[1:2], [dh3], dx4)

    do = _mm_nt_plain("attn_out_dx", dx3b, WO, BF)
    dWO = _mm_tn_plain("attn_out_dw", o, dx3b, BF)
    tb = _tile(T, 512)
    lse_rows = lse.reshape(H, T // tb, 1, tb)
    delta_rows = _attn_delta(o, do).reshape(H, T // tb, 1, tb)
    (dkvup, dkpe_h, dq_raw), landed1 = _flash_bwd(qp, kvup, kpe, do, lse_rows, delta_rows,
                                                  comm=_chips_comm(sums1))
    dqp = _dq_finish(dq_raw, cos, sin)
    dql = _mm_nt_cols("q_b_dx", dqp, WQB, F32)
    dWQB = _mm_tn_cols("q_b_dw", ql, dqp, 4, BF)
    _, d_qln, dqa = _norm_bwd("norm_q_bwd", qa, q_latent_norm, [dql], None)
    dha = _mm_nt_plain("q_a_dx", dqa, WQA, F32)
    dWQA = _mm_tn_plain("q_a_dw", ha, dqa, BF)
    dckv = _mm_nt_cols("kv_b_dx", dkvup, WKVB, F32)
    dWKVB = _mm_tn_cols("kv_b_dw", ckv, dkvup, 4, BF)
    dkvp, d_kvln = _kv_post_bwd(kvp, kv_latent_norm[None], dckv, dkpe_h, cos, sin)
    dhk = _mm_nt_plain("kv_a_dx", dkvp, WKVA, F32)
    dWKVA = _mm_tn_plain("kv_a_dw", hk, dkvp, BF)
    dx2, d_n2, dx2b = _norm_bwd("norm_attn_bwd", x2, jnp.stack([kv_in_norm, attn_norm[0]]), [dhk, dha], dx3)

    dWD0 = _mm_tn_rows("ffn_dwd0", a0, dx2b, 4, BF)
    dg0, du0 = _ffn_bwd_act("ffn_bwd_act0", dx2b, WD0, silu0, dsu0)
    dWG0 = _mm_tn_cols("ffn_dwg0", h1, dg0, 4, BF)
    dWU0 = _mm_tn_cols("ffn_dwu0", h1, du0, 4, BF)
    sums0 = pair_stage("0", [dWG0, dWU0, dWD0])
    dh1, landed0 = _ffn_dh("ffn_dh0", dg0, WG0, du0, WU0, comm=_chips_comm(sums0))
    dx1, d_ffn0, _ = _norm_bwd("norm_ffn0_bwd", x1, ffn_norm[0:1], [dh1], dx2)

    dpre, d_ps = _pool_dpre(dx1, pre, ps_full)
    ddiff = _pool_ddiff(dpre, PW)
    dPW = _pool_dw(diff, dpre)
    dx0, d_pn = _pool_bwd(x0, pn_full, ddiff, dx1)

    sums_a = pair_stage("a", [dPW, dWKVA.reshape(4, D // 4, KL + 128), dWKVB, dWQA.reshape(4, D // 4, QL), dWQB,
                              dWO.reshape(4, H * VDIM // 4, D)])
    landed_a = _comm_call("reduce_chips", _chips_comm(sums_a))
    sums = sums_a + sums0 + sums1
    landed = list(landed_a) + list(landed0) + list(landed1)
    halves = [_chip_sum(f"chip_sum{i}", a, b, chip_s) for i, (a, b) in enumerate(zip(sums, landed))]
    other = _comm_call("share_halves", _share_comm(halves))

    def rows(a):
        return a.reshape((1, -1, a.shape[-1]))

    def unpad_kva(a):
        return rows(a)[:, :, :n_kva]

    def unpad_qb(a):
        return rows(a).reshape(1, -1, hs, HEAD_PAD)[:, :, :, :NOPE + ROPE].reshape(1, -1, hs * (NOPE + ROPE))

    def layers(a0_, a1_):
        return jnp.concatenate([rows(a0_), rows(a1_)], axis=0)

    def by_name(hv):
        return {"pool_w": rows(hv[0]), "w_kv_a": unpad_kva(hv[1]), "w_kv_b": rows(hv[2]), "w_q_a": rows(hv[3]),
                "w_q_b": unpad_qb(hv[4]), "w_o": rows(hv[5]), "w_gate": layers(hv[6], hv[9]),
                "w_up": layers(hv[7], hv[10]), "w_down": layers(hv[8], hv[11])}

    big = ["pool_w", "w_kv_a", "w_kv_b", "w_q_a", "w_q_b", "w_o", "w_gate", "w_up", "w_down"]
    g_mine, g_other = by_name(halves), by_name(other)

    lat = jnp.concatenate([d_kvln[0], d_qln[0], jnp.zeros((D - KL - QL,), F32)])
    lrow = jnp.pad(loss_part[0], (0, D - 1))
    small = jnp.stack([d_n2[0], d_n2[1], d_ffn0[0], d_ffn1[0], d_final[0], lat, d_pn[0], d_ps[0], lrow]
                      + [jnp.zeros((D,), F32)] * 7)
    red = _allreduce_small(small)
    loss = red[8, 0]
    g_kv_in, g_attn, g_final = red[0], red[1:2], red[4]
    g_ffn = red[2:4]
    g_kvln, g_qln = red[5, :KL], red[5:6, KL:KL + QL]
    g_pn = lax.dynamic_slice(red[6:7], (0, chip * pg), (1, pg))
    g_ps = lax.dynamic_slice(red[7:8], (0, chip * pg), (1, pg))

    grads = {"pool_norm": g_pn, "pool_scale": g_ps, "kv_in_norm": g_kv_in, "kv_latent_norm": g_kvln,
             "attn_norm": g_attn, "q_latent_norm": g_qln, "ffn_norm": g_ffn, "final_norm": g_final}
    weights = dict(pool_norm=pool_norm, pool_w=pool_w, pool_scale=pool_scale, kv_in_norm=kv_in_norm, w_kv_a=w_kv_a,
                   kv_latent_norm=kv_latent_norm, w_kv_b=w_kv_b, attn_norm=attn_norm, w_q_a=w_q_a,
                   q_latent_norm=q_latent_norm, w_q_b=w_q_b, w_o=w_o, ffn_norm=ffn_norm, w_gate=w_gate, w_up=w_up,
                   w_down=w_down, final_norm=final_norm)
    ms = dict(pool_norm=m_pool_norm, pool_w=m_pool_w, pool_scale=m_pool_scale, kv_in_norm=m_kv_in_norm,
              w_kv_a=m_w_kv_a, kv_latent_norm=m_kv_latent_norm, w_kv_b=m_w_kv_b, attn_norm=m_attn_norm,
              w_q_a=m_w_q_a, q_latent_norm=m_q_latent_norm, w_q_b=m_w_q_b, w_o=m_w_o, ffn_norm=m_ffn_norm,
              w_gate=m_w_gate, w_up=m_w_up, w_down=m_w_down, final_norm=m_final_norm)
    vs = dict(pool_norm=v_pool_norm, pool_w=v_pool_w, pool_scale=v_pool_scale, kv_in_norm=v_kv_in_norm,
              w_kv_a=v_w_kv_a, kv_latent_norm=v_kv_latent_norm, w_kv_b=v_w_kv_b, attn_norm=v_attn_norm,
              w_q_a=v_w_q_a, q_latent_norm=v_q_latent_norm, w_q_b=v_w_q_b, w_o=v_w_o, ffn_norm=v_ffn_norm,
              w_gate=v_w_gate, w_up=v_w_up, w_down=v_w_down, final_norm=v_final_norm)
    names = list(weights)

    def as2d(a):
        return a.reshape((-1, a.shape[-1]))

    delta_w, new_m, new_v = {}, {}, {}
    for nm in big:
        n_layers = g_mine[nm].shape[0]

        def two_halves(a):
            a = as2d(a)
            return a.reshape(n_layers, 2, a.shape[0] // (2 * n_layers), a.shape[1])

        g, d, m2, v2 = _adamw_halves("adamw_" + nm, two_halves(weights[nm]), two_halves(ms[nm]),
                                     two_halves(vs[nm]), g_mine[nm], g_other[nm], core_s)
        shp = weights[nm].shape
        grads[nm], delta_w[nm], new_m[nm], new_v[nm] = g.reshape(shp), d.reshape(shp), m2.reshape(shp), v2.reshape(shp)
    groups = [["kv_in_norm", "attn_norm", "ffn_norm", "final_norm"], ["kv_latent_norm", "q_latent_norm"],
              ["pool_norm", "pool_scale"]]
    for gi, grp in enumerate(groups):
        cat = lambda t: jnp.concatenate([as2d(t[nm]) for nm in grp], axis=0)
        d, m2, v2 = _adamw(f"adamw_vec{gi}", cat(weights), cat(grads), cat(ms), cat(vs))
        r0 = 0
        for nm in grp:
            shp = weights[nm].shape
            r = as2d(weights[nm]).shape[0]
            delta_w[nm], new_m[nm], new_v[nm] = (d[r0:r0 + r].reshape(shp), m2[r0:r0 + r].reshape(shp),
                                                 v2[r0:r0 + r].reshape(shp))
            r0 += r

    return (loss, dx0[None], *[grads[nm].reshape(weights[nm].shape) for nm in names],
            *[delta_w[nm] for nm in names], *[new_m[nm] for nm in names], *[new_v[nm] for nm in names])
```

```python
import functools
import math

import jax
import jax.numpy as jnp
from jax import lax
from jax.experimental import pallas as pl
from jax.experimental.pallas import tpu as pltpu

BF = jnp.bfloat16
F32 = jnp.float32
MESH = pl.DeviceIdType.MESH

N_HEADS = 16
NOPE = 128
ROPE = 64
VDIM = 128
HEAD_PAD = 256
ROPE_BASE = 10000.0
ATTN_SCALE = 1.0 / math.sqrt(NOPE + ROPE)
POOL_WINDOWS = (2, 4, 8, 16)
HALO = 16
NORM_EPS = 1e-6
ADAM_LR, ADAM_B1, ADAM_B2, ADAM_EPS, ADAM_WD, ADAM_STEP = 0.001, 0.9, 0.999, 1e-08, 0.01, 10
NEG = -1e30

V7X_VMEM_BYTES = 64 * 1024 * 1024
VMEM_CEILING = V7X_VMEM_BYTES - 8 * 1024 * 1024

NN = (((1,), (0,)), ((), ()))
NT = (((1,), (1,)), ((), ()))
TN = (((0,), (0,)), ((), ()))


def _cparams(sem, vmem_mb):
    return pltpu.CompilerParams(dimension_semantics=sem,
                                vmem_limit_bytes=min(vmem_mb * 1024 * 1024, VMEM_CEILING))


def _tile(n, pref):
    if n <= pref:
        return n
    t = (pref // 128) * 128
    while t > 128 and n % t:
        t -= 128
    assert n % t == 0, (n, pref)
    return t


def _sds(shape, dtype):
    return jax.ShapeDtypeStruct(shape, dtype)


ANY = pl.BlockSpec(memory_space=pl.ANY)


class _Comm:
    def __init__(self, ins, out_shapes, alias, n_sems, start, finish):
        self.ins, self.out_shapes, self.alias, self.n_sems = list(ins), list(out_shapes), dict(alias), n_sems
        self.start, self.finish = start, finish


def _call(name, body, grid, in_specs, out_specs, out_shape, scratch, vmem_mb, operands, comm=None):
    in_specs, out_specs, out_shape, scratch = list(in_specs), list(out_specs), list(out_shape), list(scratch)
    params = _cparams(("arbitrary",) * len(grid), vmem_mb)
    if comm is None:
        res = pl.pallas_call(body, name=name, grid=grid, in_specs=in_specs, out_specs=out_specs,
                             out_shape=out_shape, scratch_shapes=scratch, compiler_params=params)(*operands)
        return list(res), []
    n_in, n_out, n_scr = len(in_specs), len(out_specs), len(scratch)
    c_in, c_out = len(comm.ins), len(comm.out_shapes)

    def hosted(*refs):
        ins, cin = refs[:n_in], refs[n_in:n_in + c_in]
        o0 = n_in + c_in
        outs, cout = refs[o0:o0 + n_out], refs[o0 + n_out:o0 + n_out + c_out]
        s0 = o0 + n_out + c_out
        scr, (send, recv) = refs[s0:s0 + n_scr], refs[s0 + n_scr:]
        ids = [pl.program_id(d) for d in range(len(grid))]
        first = functools.reduce(jnp.logical_and, [i == 0 for i in ids])
        last = functools.reduce(jnp.logical_and, [i == g - 1 for i, g in zip(ids, grid)])

        @pl.when(first)
        def _():
            comm.start(cin, cout, send, recv)

        body(*ins, *outs, *scr)

        @pl.when(last)
        def _():
            comm.finish(cin, cout, send, recv)

    res = pl.pallas_call(
        hosted, name=name, grid=grid, in_specs=in_specs + [ANY] * c_in, out_specs=out_specs + [ANY] * c_out,
        out_shape=out_shape + comm.out_shapes,
        scratch_shapes=scratch + [pltpu.SemaphoreType.DMA((comm.n_sems,)), pltpu.SemaphoreType.DMA((comm.n_sems,))],
        input_output_aliases={n_in + k: n_out + v for k, v in comm.alias.items()}, compiler_params=params,
    )(*operands, *comm.ins)
    return list(res[:n_out]), list(res[n_out:])


def _comm_call(name, comm):
    c_in, c_out = len(comm.ins), len(comm.out_shapes)

    def body(*refs):
        cin, cout = refs[:c_in], refs[c_in:c_in + c_out]
        send, recv = refs[c_in + c_out:]
        comm.start(cin, cout, send, recv)
        comm.finish(cin, cout, send, recv)

    return pl.pallas_call(
        body, name=name, in_specs=[ANY] * c_in, out_specs=[ANY] * c_out, out_shape=comm.out_shapes,
        scratch_shapes=[pltpu.SemaphoreType.DMA((comm.n_sems,)), pltpu.SemaphoreType.DMA((comm.n_sems,))],
        input_output_aliases=dict(comm.alias),
    )(*comm.ins)


def _matmul(name, pairs, pair_specs, dims, grid, nk, extra, extra_specs, out_shapes, out_specs,
            acc_shape, epilogue, vmem_mb, comm=None):
    n_p, n_e, n_o = len(pairs) // 2, len(extra), len(out_shapes)

    def body(*refs):
        ab = refs[:2 * n_p]
        ex = refs[2 * n_p:2 * n_p + n_e]
        outs = refs[2 * n_p + n_e:2 * n_p + n_e + n_o]

        def partial_sum():
            tot = None
            for p in range(n_p):
                a = ab[2 * p][...]
                b = ab[2 * p + 1][...]
                if a.ndim > 2:
                    a = a.reshape(-1, a.shape[-1])
                if b.ndim > 2:
                    b = b.reshape(-1, b.shape[-1])
                d = lax.dot_general(a.astype(BF), b.astype(BF), dims, preferred_element_type=F32)
                tot = d if tot is None else tot + d
            return tot

        if nk == 1:
            epilogue(partial_sum(), ex, outs)
        else:
            acc = refs[-1]
            kk = pl.program_id(2)

            @pl.when(kk == 0)
            def _():
                acc[...] = partial_sum()

            @pl.when(kk > 0)
            def _():
                acc[...] += partial_sum()

            @pl.when(kk == nk - 1)
            def _():
                epilogue(acc[...], ex, outs)

    scratch = [] if nk == 1 else [pltpu.VMEM(acc_shape, F32)]
    res, comm_res = _call(name, body, grid, list(pair_specs) + list(extra_specs), out_specs, out_shapes,
                          scratch, vmem_mb, list(pairs) + list(extra), comm)
    return res if comm is None else (res, comm_res)


def _epi_store(acc, ex, outs):
    outs[0][...] = acc.reshape(outs[0].shape).astype(outs[0].dtype)


def _epi_residual(acc, ex, outs):
    outs[0][...] = (acc + ex[0][...]).astype(outs[0].dtype)


def _swap_halves(x):
    lane = lax.broadcasted_iota(jnp.int32, x.shape, 1)
    return jnp.where((lane % 64) < 32, pltpu.roll(x, 96, 1), pltpu.roll(x, 32, 1))


def _mm_plain(name, a, b, out_dtype, res=None, tm=512, tn=1024):
    M, K = a.shape
    N = b.shape[1]
    tm, tn = _tile(M, tm), _tile(N, tn)
    extra, extra_specs, epi = [], [], _epi_store
    if res is not None:
        extra, extra_specs, epi = [res], [pl.BlockSpec((tm, tn), lambda i, j, k: (i, j))], _epi_residual
    return _matmul(
        name, [a, b],
        [pl.BlockSpec((tm, K), lambda i, j, k: (i, 0)), pl.BlockSpec((K, tn), lambda i, j, k: (0, j))],
        NN, (M // tm, N // tn, 1), 1, extra, extra_specs,
        [_sds((M, N), out_dtype)], [pl.BlockSpec((tm, tn), lambda i, j, k: (i, j))],
        None, epi, 40)[0]


def _mm_cols(name, a, b3, out_dtype, epilogue=_epi_store, extra=(), extra_specs=(), tm=1024):
    M, K = a.shape
    G, _, n = b3.shape
    tm = _tile(M, tm)
    return _matmul(
        name, [a, b3],
        [pl.BlockSpec((tm, K), lambda i, j, k: (i, 0)), pl.BlockSpec((None, K, n), lambda i, j, k: (j, 0, 0))],
        NN, (M // tm, G, 1), 1, list(extra), list(extra_specs),
        [_sds((M, G * n), out_dtype)], [pl.BlockSpec((tm, n), lambda i, j, k: (i, j))],
        None, epilogue, 40)[0]


def _mm_nt_plain(name, a, b, out_dtype, tm=512, tn=1024):
    M, K = a.shape
    N = b.shape[0]
    tm, tn = _tile(M, tm), _tile(N, tn)
    return _matmul(
        name, [a, b],
        [pl.BlockSpec((tm, K), lambda i, j, k: (i, 0)), pl.BlockSpec((tn, K), lambda i, j, k: (j, 0))],
        NT, (M // tm, N // tn, 1), 1, [], [],
        [_sds((M, N), out_dtype)], [pl.BlockSpec((tm, tn), lambda i, j, k: (i, j))],
        None, _epi_store, 44)[0]


def _mm_nt_cols(name, a, b3, out_dtype, tm=2048):
    M = a.shape[0]
    G, K, n = b3.shape
    tm = _tile(M, tm)
    return _matmul(
        name, [a, b3],
        [pl.BlockSpec((tm, n), lambda i, j, k: (i, k)), pl.BlockSpec((None, K, n), lambda i, j, k: (k, 0, 0))],
        NT, (M // tm, 1, G), G, [], [],
        [_sds((M, K), out_dtype)], [pl.BlockSpec((tm, K), lambda i, j, k: (i, 0))],
        (tm, K), _epi_store, 40)[0]


def _mm_tn_plain(name, a, b, out_dtype, tt=512, tn=1024):
    T, K = a.shape
    N = b.shape[1]
    tt, tn = _tile(T, tt), _tile(N, tn)
    return _matmul(
        name, [a, b],
        [pl.BlockSpec((tt, K), lambda i, j, k: (k, 0)), pl.BlockSpec((tt, tn), lambda i, j, k: (k, j))],
        TN, (1, N // tn, T // tt), T // tt, [], [],
        [_sds((K, N), out_dtype)], [pl.BlockSpec((K, tn), lambda i, j, k: (0, j))],
        (K, tn), _epi_store, 48)[0]


def _mm_tn_cols(name, a, b, G, out_dtype, tt=1024, comm=None):
    T, K = a.shape
    n = b.shape[1] // G
    tt = _tile(T, tt)
    r = _matmul(
        name, [a, b],
        [pl.BlockSpec((tt, K), lambda i, j, k: (k, 0)), pl.BlockSpec((tt, n), lambda i, j, k: (k, j))],
        TN, (1, G, T // tt), T // tt, [], [],
        [_sds((G, K, n), out_dtype)], [pl.BlockSpec((None, K, n), lambda i, j, k: (j, 0, 0))],
        (K, n), _epi_store, 48, comm)
    return r[0] if comm is None else (r[0][0], r[1])


def _norm_fwd(name, x, gains, tb=512):
    T, D = x.shape
    G = gains.shape[0]
    tb = _tile(T, tb)

    def body(x_ref, g_ref, *outs):
        xv = x_ref[...]
        xh = xv * lax.rsqrt(jnp.mean(xv * xv, axis=-1, keepdims=True) + NORM_EPS)
        for g in range(G):
            outs[g][...] = (xh * g_ref[g:g + 1, :]).astype(BF)

    return pl.pallas_call(
        body, name=name, grid=(T // tb,),
        in_specs=[pl.BlockSpec((tb, D), lambda i: (i, 0)), pl.BlockSpec((G, D), lambda i: (0, 0))],
        out_specs=[pl.BlockSpec((tb, D), lambda i: (i, 0))] * G,
        out_shape=[_sds((T, D), BF)] * G,
        compiler_params=_cparams(("arbitrary",), 40),
    )(x, gains)


def _norm_bwd(name, x, gains, dhs, dres, tb=256):
    T, D = x.shape
    G = gains.shape[0]
    tb = _tile(T, tb)
    has_res = dres is not None

    def body(*refs):
        x_ref, g_ref = refs[0], refs[1]
        dh_refs = refs[2:2 + G]
        res_ref = refs[2 + G] if has_res else None
        dx_ref, dg_ref, dxb_ref = refs[-3], refs[-2], refs[-1]
        i = pl.program_id(0)
        xv = x_ref[...]
        r = lax.rsqrt(jnp.mean(xv * xv, axis=-1, keepdims=True) + NORM_EPS)
        xh = xv * r
        dx = res_ref[...] if has_res else jnp.zeros_like(xv)
        rows = []
        for g in range(G):
            dh = dh_refs[g][...].astype(F32)
            dy = dh * g_ref[g:g + 1, :]
            dx = dx + r * (dy - xh * jnp.mean(dy * xh, axis=-1, keepdims=True))
            rows.append(jnp.sum(dh * xh, axis=0, keepdims=True))
        dx_ref[...] = dx
        dxb_ref[...] = dx.astype(BF)

        @pl.when(i == 0)
        def _():
            for g in range(G):
                dg_ref[g:g + 1, :] = rows[g]

        @pl.when(i > 0)
        def _():
            for g in range(G):
                dg_ref[g:g + 1, :] += rows[g]

    blk = pl.BlockSpec((tb, D), lambda i: (i, 0))
    ins = [x, gains] + list(dhs) + ([dres] if has_res else [])
    in_specs = [blk, pl.BlockSpec((G, D), lambda i: (0, 0))] + [blk] * (G + (1 if has_res else 0))
    return pl.pallas_call(
        body, name=name, grid=(T // tb,), in_specs=in_specs,
        out_specs=[blk, pl.BlockSpec((G, D), lambda i: (0, 0)), blk],
        out_shape=[_sds((T, D), F32), _sds((G, D), F32), _sds((T, D), BF)],
        compiler_params=_cparams(("arbitrary",), 48),
    )(*ins)


def _pool_fwd(x, gain, tb=256):
    T, D = x.shape
    tb = _tile(T, tb)
    pg = D // len(POOL_WINDOWS)
    per = tb // HALO

    def body(x_ref, xp_ref, g_ref, diff_ref):
        i = pl.program_id(0)
        xx = jnp.concatenate([xp_ref[...], x_ref[...]], axis=0)
        h = xx * lax.rsqrt(jnp.mean(xx * xx, axis=-1, keepdims=True) + NORM_EPS) * g_ref[...]
        row = lax.broadcasted_iota(jnp.int32, (HALO + tb, 1), 0)
        h = jnp.where((row >= HALO) | (i > 0), h, 0.0)
        t = i * tb + row[HALO:] - HALO
        for g, w in enumerate(POOL_WINDOWS):
            hg = h[:, g * pg:(g + 1) * pg]
            s, k = hg, 1
            while k < w:
                s = s + pltpu.roll(s, k, 0)
                k *= 2
            cnt = jnp.minimum(t + 1, w).astype(F32)
            diff_ref[:, g * pg:(g + 1) * pg] = (s[HALO:] / cnt - hg[HALO:]).astype(BF)

    return pl.pallas_call(
        body, name="pool_fwd", grid=(T // tb,),
        in_specs=[pl.BlockSpec((tb, D), lambda i: (i, 0)),
                  pl.BlockSpec((HALO, D), lambda i: (jnp.maximum(i * per - 1, 0), 0)),
                  pl.BlockSpec((1, D), lambda i: (0, 0))],
        out_specs=pl.BlockSpec((tb, D), lambda i: (i, 0)),
        out_shape=_sds((T, D), BF),
        compiler_params=_cparams(("arbitrary",), 40),
    )(x, x, gain)


def _pool_bwd(x, gain, ddiff, dres, tb=256):
    T, D = x.shape
    tb = _tile(T, tb)
    pg = D // len(POOL_WINDOWS)
    per = tb // HALO
    nblk = T // HALO

    def body(x_ref, g_ref, dd_ref, ddn_ref, res_ref, dx_ref, dg_ref):
        i = pl.program_id(0)
        last = i == T // tb - 1
        dd = jnp.concatenate([dd_ref[...], ddn_ref[...]], axis=0)
        row = lax.broadcasted_iota(jnp.int32, (tb + HALO, 1), 0)
        dd = jnp.where((row < tb) | jnp.logical_not(last), dd, 0.0)
        t = i * tb + row
        parts = []
        for g, w in enumerate(POOL_WINDOWS):
            dg_ = dd[:, g * pg:(g + 1) * pg]
            e = dg_ / jnp.minimum(t + 1, w).astype(F32)
            s, k = e, 1
            while k < w:
                s = s + pltpu.roll(s, tb + HALO - k, 0)
                k *= 2
            parts.append(s[:tb] - dg_[:tb])
        dh = jnp.concatenate(parts, axis=1)
        xv = x_ref[...]
        r = lax.rsqrt(jnp.mean(xv * xv, axis=-1, keepdims=True) + NORM_EPS)
        xh = xv * r
        dy = dh * g_ref[...]
        dx_ref[...] = res_ref[...] + r * (dy - xh * jnp.mean(dy * xh, axis=-1, keepdims=True))
        part = jnp.sum(dh * xh, axis=0, keepdims=True)

        @pl.when(i == 0)
        def _():
            dg_ref[...] = part

        @pl.when(i > 0)
        def _():
            dg_ref[...] += part

    blk = pl.BlockSpec((tb, D), lambda i: (i, 0))
    return pl.pallas_call(
        body, name="pool_bwd", grid=(T // tb,),
        in_specs=[blk, pl.BlockSpec((1, D), lambda i: (0, 0)), blk,
                  pl.BlockSpec((HALO, D), lambda i: (jnp.minimum((i + 1) * per, nblk - 1), 0)), blk],
        out_specs=[blk, pl.BlockSpec((1, D), lambda i: (0, 0))],
        out_shape=[_sds((T, D), F32), _sds((1, D), F32)],
        compiler_params=_cparams(("arbitrary",), 48),
    )(x, gain, ddiff, ddiff, dres)


def _pool_w_spec(pg):
    return pl.BlockSpec((4, None, None, pg // 4, pg), lambda i, j, k: (0, j // 2, j % 2, 0, 0))


def _pool_mix(diff, pw, scale, x, tm=512):
    T, D = x.shape
    pg = D // 4
    tm = _tile(T, tm)

    def epi(acc, ex, outs):
        outs[0][...] = acc
        outs[1][...] = ex[1][...] + acc * ex[0][...]

    blk = pl.BlockSpec((tm, pg), lambda i, j, k: (i, j))
    return _matmul(
        "pool_mix", [diff, pw], [blk, _pool_w_spec(pg)], NN, (T // tm, 4, 1), 1,
        [scale, x], [pl.BlockSpec((1, pg), lambda i, j, k: (0, j)), blk],
        [_sds((T, D), F32), _sds((T, D), F32)], [blk, blk], None, epi, 32)


def _pool_dpre(dx, pre, scale, tb=512):
    T, D = dx.shape
    tb = _tile(T, tb)

    def body(dx_ref, pre_ref, s_ref, dpre_ref, ds_ref):
        i = pl.program_id(0)
        d = dx_ref[...]
        dpre_ref[...] = (d * s_ref[...]).astype(BF)
        part = jnp.sum(d * pre_ref[...], axis=0, keepdims=True)

        @pl.when(i == 0)
        def _():
            ds_ref[...] = part

        @pl.when(i > 0)
        def _():
            ds_ref[...] += part

    blk = pl.BlockSpec((tb, D), lambda i: (i, 0))
    vec = pl.BlockSpec((1, D), lambda i: (0, 0))
    return pl.pallas_call(
        body, name="pool_dpre", grid=(T // tb,), in_specs=[blk, blk, vec], out_specs=[blk, vec],
        out_shape=[_sds((T, D), BF), _sds((1, D), F32)],
        compiler_params=_cparams(("arbitrary",), 40),
    )(dx, pre, scale)


def _pool_ddiff(dpre, pw, tm=512):
    T, D = dpre.shape
    pg = D // 4
    tm = _tile(T, tm)
    blk = pl.BlockSpec((tm, pg), lambda i, j, k: (i, j))
    return _matmul("pool_ddiff", [dpre, pw], [blk, _pool_w_spec(pg)], NT, (T // tm, 4, 1), 1, [], [],
                   [_sds((T, D), F32)], [blk], None, _epi_store, 32)[0]


def _pool_dw(diff, dpre, tt=512):
    T, D = diff.shape
    pg = D // 4
    tt = _tile(T, tt)
    blk = pl.BlockSpec((tt, pg), lambda i, j, k: (k, j))
    return _matmul("pool_dw", [diff, dpre], [blk, blk], TN, (1, 4, T // tt), T // tt, [], [],
                   [_sds((4, 2, 2, pg // 4, pg), BF)], [_pool_w_spec(pg)], (pg, pg), _epi_store, 32)[0]


def _ffn_up(name, h, wg, wu, tm=512, comm=None):
    T, D = h.shape
    n = wg.shape[-1]
    F = 4 * n
    tm = _tile(T, tm)

    def body(h_ref, wg_ref, wu_ref, s_ref, ds_ref, a_ref):
        hv = h_ref[...]
        g = jnp.dot(hv, wg_ref[...], preferred_element_type=F32)
        u = jnp.dot(hv, wu_ref[...], preferred_element_type=F32)
        sig = jax.nn.sigmoid(g)
        silu = g * sig
        s_ref[...] = silu.astype(BF)
        ds_ref[...] = (u * (sig * (1.0 + g * (1.0 - sig)))).astype(BF)
        a_ref[...] = (silu * u).astype(BF)

    w_spec = pl.BlockSpec((None, D, n), lambda j, i: (j, 0, 0))
    o_spec = pl.BlockSpec((tm, n), lambda j, i: (i, j))
    return _call(name, body, (4, T // tm), [pl.BlockSpec((tm, D), lambda j, i: (i, 0)), w_spec, w_spec],
                 [o_spec] * 3, [_sds((T, F), BF)] * 3, [], 56, [h, wg, wu], comm)


def _ffn_down(name, a, wd, x, tm=1024, tn=1024):
    T, D = x.shape
    n = wd.shape[1]
    tm, tn = _tile(T, tm), _tile(D, tn)
    return _matmul(
        name, [a, wd],
        [pl.BlockSpec((tm, n), lambda i, j, k: (i, k)),
         pl.BlockSpec((None, n, tn), lambda i, j, k: (k, 0, j))],
        NN, (T // tm, D // tn, 4), 4, [x], [pl.BlockSpec((tm, tn), lambda i, j, k: (i, j))],
        [_sds((T, D), F32)], [pl.BlockSpec((tm, tn), lambda i, j, k: (i, j))],
        (tm, tn), _epi_residual, 40)[0]


def _ffn_bwd_act(name, dx, wd, silu, dsilu_up, tm=1024):
    T, D = dx.shape
    n = wd.shape[1]
    tm = _tile(T, tm)

    def body(dx_ref, w_ref, s_ref, ds_ref, dg_ref, du_ref):
        da = lax.dot_general(dx_ref[...], w_ref[...], NT, preferred_element_type=F32)
        dg_ref[...] = (da * ds_ref[...].astype(F32)).astype(BF)
        du_ref[...] = (da * s_ref[...].astype(F32)).astype(BF)

    blk = pl.BlockSpec((tm, n), lambda i, j: (i, j))
    return pl.pallas_call(
        body, name=name, grid=(T // tm, 4),
        in_specs=[pl.BlockSpec((tm, D), lambda i, j: (i, 0)), pl.BlockSpec((None, n, D), lambda i, j: (j, 0, 0)),
                  blk, blk],
        out_specs=[blk, blk], out_shape=[_sds((T, 4 * n), BF)] * 2,
        compiler_params=_cparams(("arbitrary", "arbitrary"), 56),
    )(dx, wd, silu, dsilu_up)


def _ffn_dh(name, dg, wg, du, wu, tm=512, comm=None):
    T = dg.shape[0]
    D, n = wg.shape[1], wg.shape[2]
    tm = _tile(T, tm)
    a_spec = pl.BlockSpec((tm, n), lambda i, j, k: (i, k))
    w_spec = pl.BlockSpec((None, D, n), lambda i, j, k: (k, 0, 0))
    r = _matmul(
        name, [dg, wg, du, wu], [a_spec, w_spec, a_spec, w_spec], NT, (T // tm, 1, 4), 4, [], [],
        [_sds((T, D), F32)], [pl.BlockSpec((tm, D), lambda i, j, k: (i, 0))], (tm, D), _epi_store, 56, comm)
    return (r[0], []) if comm is None else (r[0][0], r[1])


def _mm_tn_rows(name, a, b, G, out_dtype, tt=1024, tn=1024, comm=None):
    T, N = b.shape
    n = a.shape[1] // G
    tt, tn = _tile(T, tt), _tile(N, tn)
    nj = N // tn
    r = _matmul(
        name, [a, b],
        [pl.BlockSpec((tt, n), lambda i, j, k: (k, j // nj)), pl.BlockSpec((tt, tn), lambda i, j, k: (k, j % nj))],
        TN, (1, G * nj, T // tt), T // tt, [], [],
        [_sds((G, n, N), out_dtype)], [pl.BlockSpec((None, n, tn), lambda i, j, k: (j // nj, 0, j % nj))],
        (n, tn), _epi_store, 48, comm)
    return r[0] if comm is None else (r[0][0], r[1])


def _kv_post(kvp, gain, cos, sin, tb=512):
    T, W = kvp.shape
    KL = W - 128
    tb = _tile(T, tb)

    def body(kv_ref, g_ref, c_ref, s_ref, ckv_ref, kpe_ref):
        lat = kv_ref[:, :KL]
        ckv_ref[...] = (lat * lax.rsqrt(jnp.mean(lat * lat, axis=-1, keepdims=True) + NORM_EPS)
                        * g_ref[...]).astype(BF)
        pe = kv_ref[:, KL:]
        kpe_ref[...] = (pe * c_ref[...] + _swap_halves(pe) * s_ref[...]).astype(BF)

    tab = pl.BlockSpec((tb, 128), lambda i: (i, 0))
    return pl.pallas_call(
        body, name="kv_post", grid=(T // tb,),
        in_specs=[pl.BlockSpec((tb, W), lambda i: (i, 0)), pl.BlockSpec((1, KL), lambda i: (0, 0)), tab, tab],
        out_specs=[pl.BlockSpec((tb, KL), lambda i: (i, 0)), tab],
        out_shape=[_sds((T, KL), BF), _sds((T, 128), BF)],
        compiler_params=_cparams(("arbitrary",), 32),
    )(kvp, gain, cos, sin)


def _kv_post_bwd(kvp, gain, dckv, dkpe_heads, cos, sin, tb=256):
    T, W = kvp.shape
    KL = W - 128
    H = dkpe_heads.shape[0]
    tb = _tile(T, tb)

    def body(kv_ref, g_ref, dc_ref, dk_ref, c_ref, s_ref, out_ref, dg_ref):
        i = pl.program_id(0)
        lat = kv_ref[:, :KL]
        r = lax.rsqrt(jnp.mean(lat * lat, axis=-1, keepdims=True) + NORM_EPS)
        xh = lat * r
        dh = dc_ref[...]
        dy = dh * g_ref[...]
        out_ref[:, :KL] = (r * (dy - xh * jnp.mean(dy * xh, axis=-1, keepdims=True))).astype(BF)
        d = dk_ref[0]
        for h in range(1, H):
            d = d + dk_ref[h]
        out_ref[:, KL:] = (d * c_ref[...] - _swap_halves(d) * s_ref[...]).astype(BF)
        part = jnp.sum(dh * xh, axis=0, keepdims=True)

        @pl.when(i == 0)
        def _():
            dg_ref[...] = part

        @pl.when(i > 0)
        def _():
            dg_ref[...] += part

    tab = pl.BlockSpec((tb, 128), lambda i: (i, 0))
    vec = pl.BlockSpec((1, KL), lambda i: (0, 0))
    return pl.pallas_call(
        body, name="kv_post_bwd", grid=(T // tb,),
        in_specs=[pl.BlockSpec((tb, W), lambda i: (i, 0)), vec, pl.BlockSpec((tb, KL), lambda i: (i, 0)),
                  pl.BlockSpec((H, tb, 128), lambda i: (0, i, 0)), tab, tab],
        out_specs=[pl.BlockSpec((tb, W), lambda i: (i, 0)), vec],
        out_shape=[_sds((T, W), BF), _sds((1, KL), F32)],
        compiler_params=_cparams(("arbitrary",), 32),
    )(kvp, gain, dckv, dkpe_heads, cos, sin)


def _q_up(ql, wqb, cos, sin, tm=1024):
    n = wqb.shape[2]
    tm = _tile(ql.shape[0], tm)

    def epi(acc, ex, outs):
        c, s = ex[0][...] * LOG2_SCALE, ex[1][...] * LOG2_SCALE
        for j in range(n // HEAD_PAD):
            a0 = j * HEAD_PAD
            outs[0][:, a0:a0 + NOPE] = (acc[:, a0:a0 + NOPE] * LOG2_SCALE).astype(BF)
            pe = acc[:, a0 + NOPE:a0 + HEAD_PAD]
            outs[0][:, a0 + NOPE:a0 + HEAD_PAD] = (pe * c + _swap_halves(pe) * s).astype(BF)

    tab = pl.BlockSpec((tm, 128), lambda i, j, k: (i, 0))
    return _mm_cols("q_up", ql, wqb, BF, epilogue=epi, extra=[cos, sin], extra_specs=[tab, tab], tm=tm)


def _causal_mask(tb):
    r = lax.broadcasted_iota(jnp.int32, (tb, tb), 0)
    c = lax.broadcasted_iota(jnp.int32, (tb, tb), 1)
    return r, c


HP = 2
LOG2_SCALE = ATTN_SCALE * math.log2(math.e)


def _fill_keys(k_scr, kv_ref, kpe_ref):
    for hh in range(HP):
        k_scr[hh, :, :NOPE] = kv_ref[:, hh * HEAD_PAD:hh * HEAD_PAD + NOPE]
        k_scr[hh, :, NOPE:] = kpe_ref[...]


def _flash_fwd(qp, kvup, kpe, tb=512, comm=None):
    T = qp.shape[0]
    H = qp.shape[1] // HEAD_PAD
    tb = _tile(T, tb)

    def body(q_ref, kv_ref, kpe_ref, o_ref, lse_ref, k_scr, m_scr, l_scr, acc_scr, sa_scr, sb_scr):
        iq = pl.program_id(1)

        @pl.when(iq == 0)
        def _():
            _fill_keys(k_scr, kv_ref, kpe_ref)

        qs = [q_ref[:, hh * HEAD_PAD:(hh + 1) * HEAD_PAD] for hh in range(HP)]

        for hh in range(HP):
            m_scr[hh] = jnp.full((tb, 128), NEG, F32)
            l_scr[hh] = jnp.zeros((tb, 128), F32)
            acc_scr[hh] = jnp.zeros((tb, VDIM), F32)

        def scores(ik, s_buf):
            off = pl.multiple_of(ik * tb, tb)
            for hh in range(HP):
                s_buf[hh] = lax.dot_general(qs[hh], k_scr[hh, pl.ds(off, tb), :], NT, preferred_element_type=F32)

        def update(ik, s_buf, masked):
            off = pl.multiple_of(ik * tb, tb)
            ps = []
            for hh in range(HP):
                s = s_buf[hh]
                if masked:
                    r, c = _causal_mask(tb)
                    s = jnp.where(c <= r, s, NEG)
                m = m_scr[hh]
                m2 = jnp.maximum(m, jnp.max(s, axis=-1, keepdims=True))
                p = jnp.exp2(s - jnp.concatenate([m2] * (tb // 128), axis=1))
                a = jnp.exp2(m - m2)
                l_scr[hh] = a * l_scr[hh] + jnp.sum(p, axis=-1, keepdims=True)
                m_scr[hh] = m2
                ps.append((a, p.astype(BF)))
            for hh in range(HP):
                a, p = ps[hh]
                v = kv_ref[pl.ds(off, tb), hh * HEAD_PAD + NOPE:(hh + 1) * HEAD_PAD]
                acc_scr[hh] = a * acc_scr[hh] + lax.dot_general(p, v, NN, preferred_element_type=F32)

        def trip(j, carry):
            scores(2 * j + 1, sb_scr)
            update(2 * j, sa_scr, False)
            scores(2 * j + 2, sa_scr)
            update(2 * j + 1, sb_scr, False)
            return carry

        scores(0, sa_scr)
        lax.fori_loop(0, iq // 2, trip, 0)

        @pl.when(iq % 2 == 0)
        def _():
            update(iq, sa_scr, True)

        @pl.when(iq % 2 == 1)
        def _():
            scores(iq, sb_scr)
            update(iq - 1, sa_scr, False)
            update(iq, sb_scr, True)
        for hh in range(HP):
            l = l_scr[hh]
            o_ref[:, hh * VDIM:(hh + 1) * VDIM] = (acc_scr[hh] / l).astype(BF)
            lse_ref[hh] = (m_scr[hh] + jnp.log2(l))[:, :1]

    return _call(
        "flash_fwd", body, (H // HP, T // tb),
        [pl.BlockSpec((tb, HP * HEAD_PAD), lambda h, i: (i, h)),
         pl.BlockSpec((T, HP * HEAD_PAD), lambda h, i: (0, h)),
         pl.BlockSpec((T, 128), lambda h, i: (0, 0))],
        [pl.BlockSpec((tb, HP * VDIM), lambda h, i: (i, h)),
         pl.BlockSpec((HP, tb, 1), lambda h, i: (h, i, 0))],
        [_sds((T, H * VDIM), BF), _sds((H, T, 1), F32)],
        [pltpu.VMEM((HP, T, HEAD_PAD), BF), pltpu.VMEM((HP, tb, 128), F32), pltpu.VMEM((HP, tb, 128), F32),
         pltpu.VMEM((HP, tb, VDIM), F32), pltpu.VMEM((HP, tb, tb), F32), pltpu.VMEM((HP, tb, tb), F32)],
        56, [qp, kvup, kpe], comm)


def _attn_delta(o, do, tb=512):
    T = o.shape[0]
    H = o.shape[1] // VDIM
    tb = _tile(T, tb)

    def body(o_ref, do_ref, d_ref):
        for h in range(H):
            cols = slice(h * VDIM, (h + 1) * VDIM)
            d_ref[h] = jnp.sum(o_ref[:, cols].astype(F32) * do_ref[:, cols].astype(F32), axis=-1, keepdims=True)

    blk = pl.BlockSpec((tb, H * VDIM), lambda i: (i, 0))
    return pl.pallas_call(
        body, name="attn_delta", grid=(T // tb,), in_specs=[blk, blk],
        out_specs=pl.BlockSpec((H, tb, 1), lambda i: (0, i, 0)), out_shape=_sds((H, T, 1), F32),
        compiler_params=_cparams(("arbitrary",), 32),
    )(o, do)


def _flash_bwd(qp, kvup, kpe, do, lse_rows, delta_rows, tb=512, comm=None):
    T = qp.shape[0]
    H = qp.shape[1] // HEAD_PAD
    tb = _tile(T, tb)
    nq = T // tb

    def body(kv_ref, kpe_ref, q_ref, do_ref, lse_ref, dl_ref, dkv_ref, dkpe_ref, dq_ref, dk_scr, dv_scr,
             sa_scr, pa_scr, sb_scr, pb_scr):
        ik = pl.program_id(1)

        @pl.when(ik == 0)
        def _():
            dq_ref[...] = jnp.zeros_like(dq_ref)

        k = jnp.concatenate([kv_ref[:, :NOPE], kpe_ref[...]], axis=1)
        v = kv_ref[:, NOPE:]
        dk_scr[...] = jnp.zeros_like(dk_scr)
        dv_scr[...] = jnp.zeros_like(dv_scr)

        def scores(iq, bufs):
            off = pl.multiple_of(iq * tb, tb)
            bufs[0][...] = lax.dot_general(k, q_ref[pl.ds(off, tb), :], NT, preferred_element_type=F32)
            bufs[1][...] = lax.dot_general(v, do_ref[pl.ds(off, tb), :], NT, preferred_element_type=F32)

        def update(iq, bufs, masked):
            off = pl.multiple_of(iq * tb, tb)
            q = q_ref[pl.ds(off, tb), :]
            d = do_ref[pl.ds(off, tb), :]
            st = bufs[0][...]
            if masked:
                r, c = _causal_mask(tb)
                st = jnp.where(r <= c, st, NEG)
            pt = jnp.exp2(st - lse_ref[iq])
            dst = (pt * (bufs[1][...] - dl_ref[iq])).astype(BF)
            dv_scr[...] += lax.dot_general(pt.astype(BF), d, NN, preferred_element_type=F32)
            dk_scr[...] += lax.dot_general(dst, q, NN, preferred_element_type=F32)
            dq_ref[pl.ds(off, tb), :] += lax.dot_general(dst, k, TN, preferred_element_type=F32)

        A, B = (sa_scr, pa_scr), (sb_scr, pb_scr)
        n = nq - 1 - ik
        scores(ik, A)

        @pl.when(n == 0)
        def _():
            update(ik, A, True)

        @pl.when(n > 0)
        def _():
            scores(ik + 1, B)
            update(ik, A, True)

            def trip(j, carry):
                b = ik + 1 + 2 * j
                scores(b + 1, A)
                update(b, B, False)
                scores(b + 2, B)
                update(b + 1, A, False)
                return carry

            pairs = (n - 1) // 2
            lax.fori_loop(0, pairs, trip, 0)
            last = ik + 1 + 2 * pairs

            @pl.when(last == nq - 1)
            def _():
                update(last, B, False)

            @pl.when(last < nq - 1)
            def _():
                scores(last + 1, A)
                update(last, B, False)
                update(last + 1, A, False)
        dk = dk_scr[...] * (ATTN_SCALE / LOG2_SCALE)
        dkv_ref[:, :NOPE] = dk[:, :NOPE].astype(BF)
        dkv_ref[:, NOPE:] = dv_scr[...].astype(BF)
        dkpe_ref[...] = dk[:, NOPE:]

    rows = pl.BlockSpec((None, nq, 1, tb), lambda h, i: (h, 0, 0, 0))
    return _call(
        "flash_bwd", body, (H, nq),
        [pl.BlockSpec((tb, HEAD_PAD), lambda h, i: (i, h)), pl.BlockSpec((tb, 128), lambda h, i: (i, 0)),
         pl.BlockSpec((T, HEAD_PAD), lambda h, i: (0, h)), pl.BlockSpec((T, VDIM), lambda h, i: (0, h)), rows, rows],
        [pl.BlockSpec((tb, HEAD_PAD), lambda h, i: (i, h)), pl.BlockSpec((None, tb, 128), lambda h, i: (h, i, 0)),
         pl.BlockSpec((T, HEAD_PAD), lambda h, i: (0, h))],
        [_sds((T, H * HEAD_PAD), BF), _sds((H, T, 128), F32), _sds((T, H * HEAD_PAD), F32)],
        [pltpu.VMEM((tb, HEAD_PAD), F32), pltpu.VMEM((tb, VDIM), F32)] + [pltpu.VMEM((tb, tb), F32)] * 4, 56,
        [kvup, kpe, qp, do, lse_rows, delta_rows], comm)


def _dq_finish(dq_raw, cos, sin, tb=256):
    T, W = dq_raw.shape
    tb = _tile(T, tb)

    def body(dq_ref, c_ref, s_ref, o_ref):
        c, s = c_ref[...] * ATTN_SCALE, s_ref[...] * ATTN_SCALE
        for h in range(W // HEAD_PAD):
            a0 = h * HEAD_PAD
            o_ref[:, a0:a0 + NOPE] = (dq_ref[:, a0:a0 + NOPE] * ATTN_SCALE).astype(BF)
            dpe = dq_ref[:, a0 + NOPE:a0 + HEAD_PAD]
            o_ref[:, a0 + NOPE:a0 + HEAD_PAD] = (dpe * c - _swap_halves(dpe) * s).astype(BF)

    blk = pl.BlockSpec((tb, W), lambda i: (i, 0))
    tab = pl.BlockSpec((tb, 128), lambda i: (i, 0))
    return pl.pallas_call(
        body, name="dq_finish", grid=(T // tb,), in_specs=[blk, tab, tab], out_specs=blk,
        out_shape=_sds((T, W), BF), compiler_params=_cparams(("arbitrary",), 40),
    )(dq_raw, cos, sin)


def _final_loss(x, gain, target, tb=256):
    T, D = x.shape
    tb = _tile(T, tb)

    def body(x_ref, g_ref, t_ref, loss_ref, dx_ref, dg_ref, dxb_ref):
        i = pl.program_id(0)
        xv = x_ref[...]
        gv = g_ref[...]
        r = lax.rsqrt(jnp.mean(xv * xv, axis=-1, keepdims=True) + NORM_EPS)
        xh = xv * r
        e = xh * gv - t_ref[...]
        lpart = 0.5 * jnp.sum(jnp.mean(e * e, axis=-1, keepdims=True), axis=0, keepdims=True)
        dy = e / D
        dyg = dy * gv
        dx = r * (dyg - xh * jnp.mean(dyg * xh, axis=-1, keepdims=True))
        dx_ref[...] = dx
        dxb_ref[...] = dx.astype(BF)
        gpart = jnp.sum(dy * xh, axis=0, keepdims=True)

        @pl.when(i == 0)
        def _():
            loss_ref[...] = lpart
            dg_ref[...] = gpart

        @pl.when(i > 0)
        def _():
            loss_ref[...] += lpart
            dg_ref[...] += gpart

    blk = pl.BlockSpec((tb, D), lambda i: (i, 0))
    vec = pl.BlockSpec((1, D), lambda i: (0, 0))
    return pl.pallas_call(
        body, name="final_loss", grid=(T // tb,), in_specs=[blk, vec, blk],
        out_specs=[pl.BlockSpec((1, 1), lambda i: (0, 0)), blk, vec, blk],
        out_shape=[_sds((1, 1), F32), _sds((T, D), F32), _sds((1, D), F32), _sds((T, D), BF)],
        compiler_params=_cparams(("arbitrary",), 40),
    )(x, gain, target)


def _row_tile(R, pref=256):
    t = (min(R, pref) // 16) * 16
    while t >= 16:
        if R % t == 0:
            return t
        t -= 16
    return R


def _prefetch_call(body, name, grid, in_specs, out_specs, out_shape, scalar, operands, vmem_mb=32):
    return pl.pallas_call(
        body, name=name, out_shape=out_shape,
        grid_spec=pltpu.PrefetchScalarGridSpec(num_scalar_prefetch=1, grid=grid, in_specs=in_specs,
                                               out_specs=out_specs),
        compiler_params=_cparams(("arbitrary",) * len(grid), vmem_mb),
    )(scalar, *operands)


def _place_own(name, s, chip):
    _, n, R, C = s.shape
    tr = _row_tile(R, 512)

    def body(chip_ref, s_ref, o_ref):
        o_ref[...] = s_ref[...]

    return _prefetch_call(
        body, name, (2, n, R // tr),
        [pl.BlockSpec((None, None, tr, C), lambda h, j, r, cr: (h, j, r, 0))],
        pl.BlockSpec((None, None, None, tr, C), lambda h, j, r, cr: (cr[0], h, j, r, 0)),
        _sds((4,) + s.shape, s.dtype), chip, [s])


def _pair_add(name, g, theirs, core):
    _, _, n, R, C = g.shape
    tr = _row_tile(R)

    def body(c_ref, a_ref, b_ref, o_ref):
        o_ref[...] = (a_ref[...].astype(F32) + b_ref[...].astype(F32)).astype(BF)

    blk = pl.BlockSpec((None, None, tr, C), lambda k, j, r, cr: (k, j, r, 0))
    return _prefetch_call(
        body, name, (4, n, R // tr),
        [pl.BlockSpec((None, None, None, tr, C), lambda k, j, r, cr: (k, cr[0], j, r, 0)), blk], blk,
        _sds(theirs.shape, BF), core, [g, theirs])


def _chip_sum(name, sums, landed, chip):
    _, n, R, C = sums.shape
    tr = _row_tile(R)

    def body(chip_ref, own_ref, l_ref, o_ref):
        s = own_ref[...].astype(F32)
        for j in range(3):
            s = s + l_ref[j].astype(F32)
        o_ref[...] = s

    return _prefetch_call(
        body, name, (n, R // tr),
        [pl.BlockSpec((None, None, tr, C), lambda j, r, cr: (cr[0], j, r, 0)),
         pl.BlockSpec((3, None, tr, C), lambda j, r, cr: (0, j, r, 0))],
        pl.BlockSpec((None, tr, C), lambda j, r, cr: (j, r, 0)),
        _sds((n, R, C), F32), chip, [sums, landed])


def _adamw_halves(name, w, m, v, g_mine, g_theirs, core, tr=128):
    L, _, R, C = w.shape
    tr = _row_tile(R, tr)
    c1 = 1.0 - ADAM_B1 ** ADAM_STEP
    c2 = 1.0 - ADAM_B2 ** ADAM_STEP

    def body(c_ref, w_ref, m_ref, v_ref, gm_ref, gt_ref, g_ref, d_ref, nm_ref, nv_ref):
        gv = jnp.where(pl.program_id(1) == c_ref[0], gm_ref[...], gt_ref[...])
        nm = ADAM_B1 * m_ref[...] + (1.0 - ADAM_B1) * gv
        nv = ADAM_B2 * v_ref[...] + (1.0 - ADAM_B2) * (gv * gv)
        g_ref[...] = gv
        nm_ref[...] = nm
        nv_ref[...] = nv
        d_ref[...] = -ADAM_LR * ((nm / c1) / (jnp.sqrt(nv / c2) + ADAM_EPS) + ADAM_WD * w_ref[...])

    full = pl.BlockSpec((None, None, tr, C), lambda l, h, r, cr: (l, h, r, 0))
    half = pl.BlockSpec((None, tr, C), lambda l, h, r, cr: (l, r, 0))
    return _prefetch_call(
        body, name, (L, 2, R // tr), [full, full, full, half, half], [full] * 4,
        [_sds(w.shape, F32)] * 4, core, [w, m, v, g_mine, g_theirs])


def _adamw(name, w, g, m, v, tr=128):
    R, C = w.shape
    tr = _tile(R, tr) if R % 8 == 0 else R
    c1 = 1.0 - ADAM_B1 ** ADAM_STEP
    c2 = 1.0 - ADAM_B2 ** ADAM_STEP

    def body(w_ref, g_ref, m_ref, v_ref, d_ref, nm_ref, nv_ref):
        gv = g_ref[...]
        nm = ADAM_B1 * m_ref[...] + (1.0 - ADAM_B1) * gv
        nv = ADAM_B2 * v_ref[...] + (1.0 - ADAM_B2) * (gv * gv)
        nm_ref[...] = nm
        nv_ref[...] = nv
        d_ref[...] = -ADAM_LR * ((nm / c1) / (jnp.sqrt(nv / c2) + ADAM_EPS) + ADAM_WD * w_ref[...])

    blk = pl.BlockSpec((tr, C), lambda i: (i, 0))
    return pl.pallas_call(
        body, name=name, grid=(R // tr,), in_specs=[blk] * 4, out_specs=[blk] * 3,
        out_shape=[_sds((R, C), F32)] * 3, compiler_params=_cparams(("arbitrary",), 32),
    )(w, g, m, v)


def _place():
    x, y, c = lax.axis_index("x"), lax.axis_index("y"), lax.axis_index("c")
    chips = [(1 - x, y), (x, 1 - y), (1 - x, 1 - y)]
    return x, y, c, chips


def _gather_comm(shards, placed):
    n = len(shards)

    def copies(cin, cout, send, recv):
        src, dst = cin[:n], cout
        x, y, c, chips = _place()
        me = 2 * x + y

        def remote(i, k, s, d, to):
            return pltpu.make_async_remote_copy(src_ref=s, dst_ref=d, send_sem=send.at[6 * i + k],
                                                recv_sem=recv.at[6 * i + k], device_id=to, device_id_type=MESH)

        first = [remote(i, j, src[i].at[c], dst[i].at[me, c], (*chip, c))
                 for i in range(n) for j, chip in enumerate(chips)]
        return remote, first, dst, (x, y, c, chips)

    def start(cin, cout, send, recv):
        for cp in copies(cin, cout, send, recv)[1]:
            cp.start()

    def finish(cin, cout, send, recv):
        remote, first, dst, (x, y, c, chips) = copies(cin, cout, send, recv)
        sibling = (x, y, 1 - c)
        passed = []
        for i in range(n):
            for j, (px, py) in enumerate(chips):
                slot = dst[i].at[2 * px + py, c]
                remote(i, j, slot, slot, (px, py, c)).wait_recv()
                cp = remote(i, 3 + j, slot, slot, sibling)
                cp.start()
                passed.append(cp)
        for i in range(n):
            for j, (px, py) in enumerate(chips):
                slot = dst[i].at[2 * px + py, 1 - c]
                remote(i, 3 + j, slot, slot, sibling).wait_recv()
        for cp in first + passed:
            cp.wait_send()

    return _Comm(list(shards) + list(placed), [_sds(p.shape, p.dtype) for p in placed],
                 {n + i: i for i in range(n)}, 6 * n, start, finish)


def _pair_comm(grads):
    n = len(grads)

    def copies(cin, cout, send, recv):
        x, y, c, _ = _place()
        return [pltpu.make_async_remote_copy(
            src_ref=cin[i].at[k, 1 - c], dst_ref=cout[i].at[k], send_sem=send.at[4 * i + k],
            recv_sem=recv.at[4 * i + k], device_id=(x, y, 1 - c), device_id_type=MESH)
            for i in range(n) for k in range(4)]

    def start(cin, cout, send, recv):
        for cp in copies(cin, cout, send, recv):
            cp.start()

    def finish(cin, cout, send, recv):
        for cp in copies(cin, cout, send, recv):
            cp.wait()

    return _Comm(grads, [_sds((4,) + a.shape[2:], a.dtype) for a in grads], {}, 4 * n, start, finish)


def _chips_comm(sums):
    n = len(sums)

    def copies(cin, cout, send, recv):
        x, y, c, chips = _place()
        return [pltpu.make_async_remote_copy(
            src_ref=cin[i].at[2 * px + py], dst_ref=cout[i].at[j], send_sem=send.at[3 * i + j],
            recv_sem=recv.at[3 * i + j], device_id=(px, py, c), device_id_type=MESH)
            for i in range(n) for j, (px, py) in enumerate(chips)]

    def start(cin, cout, send, recv):
        for cp in copies(cin, cout, send, recv):
            cp.start()

    def finish(cin, cout, send, recv):
        for cp in copies(cin, cout, send, recv):
            cp.wait()

    return _Comm(sums, [_sds((3,) + a.shape[1:], a.dtype) for a in sums], {}, 3 * n, start, finish)


def _share_comm(halves):
    n = len(halves)

    def copies(cin, cout, send, recv):
        x, y, c, _ = _place()
        return [pltpu.make_async_remote_copy(
            src_ref=cin[i], dst_ref=cout[i], send_sem=send.at[i], recv_sem=recv.at[i],
            device_id=(x, y, 1 - c), device_id_type=MESH) for i in range(n)]

    def start(cin, cout, send, recv):
        for cp in copies(cin, cout, send, recv):
            cp.start()

    def finish(cin, cout, send, recv):
        for cp in copies(cin, cout, send, recv):
            cp.wait()

    return _Comm(halves, [_sds(a.shape, a.dtype) for a in halves], {}, n, start, finish)


def _allreduce_small(v):
    R, C = v.shape

    def body(v_ref, out_ref, land, send, recv):
        x, y, c, _ = _place()
        me = 4 * x + 2 * y + c
        cps = []
        for m in range(1, 8):
            fx, fy, fc = (m >> 2) & 1, (m >> 1) & 1, m & 1
            peer = (x ^ fx, y ^ fy, c ^ fc)
            cp = pltpu.make_async_remote_copy(
                src_ref=v_ref, dst_ref=land.at[me], send_sem=send.at[m - 1], recv_sem=recv.at[m - 1],
                device_id=peer, device_id_type=MESH)
            cp.start()
            cps.append(cp)
        land[me] = v_ref[...]
        for cp in cps:
            cp.wait()
        s = land[0]
        for d in range(1, 8):
            s = s + land[d]
        out_ref[...] = s

    return pl.pallas_call(
        body, name="allreduce_small",
        in_specs=[pl.BlockSpec(memory_space=pltpu.VMEM)], out_specs=pl.BlockSpec(memory_space=pltpu.VMEM),
        out_shape=_sds((R, C), F32),
        scratch_shapes=[pltpu.VMEM((8, R, C), F32), pltpu.SemaphoreType.DMA((7,)), pltpu.SemaphoreType.DMA((7,))],
    )(v)


def _halves(a):
    return a.reshape((2, a.shape[0] // 2) + a.shape[1:])


def _canon(a, lead):
    piece = a.shape[lead:]
    return a.reshape(a.shape[:lead] + (int(math.prod(piece[:-2])),) + piece[-2:])


def _rope_tables(pos):
    half = ROPE // 2
    inv_freq = ROPE_BASE ** (-jnp.arange(half, dtype=F32) / half)
    ang = pos.astype(F32)[:, None] * inv_freq
    cos, sin = jnp.cos(ang), jnp.sin(ang)
    return jnp.tile(cos, (1, 4)), jnp.concatenate([-sin, sin, -sin, sin], axis=1)


def kernel(x, positions, pool_norm, pool_w, pool_scale, kv_in_norm, w_kv_a, kv_latent_norm, w_kv_b, attn_norm, w_q_a, q_latent_norm, w_q_b, w_o, ffn_norm, w_gate, w_up, w_down, final_norm, loss_target, m_pool_norm, m_pool_w, m_pool_scale, m_kv_in_norm, m_w_kv_a, m_kv_latent_norm, m_w_kv_b, m_attn_norm, m_w_q_a, m_q_latent_norm, m_w_q_b, m_w_o, m_ffn_norm, m_w_gate, m_w_up, m_w_down, m_final_norm, v_pool_norm, v_pool_w, v_pool_scale, v_kv_in_norm, v_w_kv_a, v_kv_latent_norm, v_w_kv_b, v_attn_norm, v_w_q_a, v_q_latent_norm, v_w_q_b, v_w_o, v_ffn_norm, v_w_gate, v_w_up, v_w_down, v_final_norm):
    T, D = x.shape[1], x.shape[2]
    H = N_HEADS
    KL = kv_latent_norm.shape[0]
    QL = q_latent_norm.shape[1]
    pg = D // 4
    x0, tgt = x[0], loss_target[0]
    cos, sin = _rope_tables(positions[0])
    chip = 2 * lax.axis_index("x") + lax.axis_index("y")

    n_kva = w_kv_a.shape[1]
    wkva_s = jnp.pad(w_kv_a, ((0, 0), (0, KL + 128 - n_kva)))
    hs = w_q_b.shape[2] // (NOPE + ROPE)
    wqb_s = jnp.pad(w_q_b[0].reshape(QL, hs, NOPE + ROPE), ((0, 0), (0, 0), (0, HEAD_PAD - NOPE - ROPE)))
    wqb_s = wqb_s.reshape(QL, hs * HEAD_PAD)
    chip_s = chip.astype(jnp.int32).reshape(1)
    core_s = lax.axis_index("c").astype(jnp.int32).reshape(1)

    def halved(a):
        return _canon(_halves(a.astype(BF)), 1)

    def gather_group(tag, shards):
        placed = [_place_own(f"place_own{tag}_{i}", s, chip_s) for i, s in enumerate(shards)]
        return _gather_comm(shards, placed)

    def whole(a, *shape):
        return a.reshape(shape)

    n_ff = w_gate.shape[2]
    group0 = gather_group("0", [halved(pool_w[0]), _canon(jnp.stack([pool_norm, pool_scale]), 1),
                                halved(w_gate[0]), halved(w_up[0])])
    group1 = gather_group("1", [halved(w_down[0]), halved(wkva_s), halved(w_kv_b), halved(w_q_a[0]),
                                halved(wqb_s), halved(w_o[0])])
    group2 = gather_group("2", [halved(w_gate[1]), halved(w_up[1]), halved(w_down[1])])
    PW, PV, WG0, WU0 = _comm_call("gather0", group0)
    WG0, WU0 = whole(WG0, 4, D, n_ff), whole(WU0, 4, D, n_ff)
    pv = jnp.transpose(PV.reshape(4, 2, pg), (1, 0, 2)).reshape(2, D)
    pn_full, ps_full = pv[0:1], pv[1:2]

    diff = _pool_fwd(x0, pn_full)
    pre, x1 = _pool_mix(diff, PW, ps_full, x0)
    (h1,) = _norm_fwd("norm_ffn0", x1, ffn_norm[0:1])
    (silu0, dsu0, a0), (WD0, WKVA, WKVB, WQA, WQB, WO) = _ffn_up("ffn_up0", h1, WG0, WU0, comm=group1)
    WD0 = whole(WD0, 4, n_ff, D)
    WKVA = whole(WKVA, D, KL + 128)
    WKVB = whole(WKVB, 4, KL, -1)
    WQA = whole(WQA, D, QL)
    WQB = whole(WQB, 4, QL, -1)
    WO = whole(WO, H * VDIM, D)
    x2 = _ffn_down("ffn_down0", a0, WD0, x1)
    hk, ha = _norm_fwd("norm_attn", x2, jnp.stack([kv_in_norm, attn_norm[0]]))
    kvp = _mm_plain("kv_a", hk, WKVA, F32)
    ckv, kpe = _kv_post(kvp, kv_latent_norm[None], cos, sin)
    kvup = _mm_cols("kv_b", ckv, WKVB, BF)
    qa = _mm_plain("q_a", ha, WQA, F32)
    (ql,) = _norm_fwd("norm_q", qa, q_latent_norm)
    qp = _q_up(ql, WQB, cos, sin)
    (o, lse), (WG1, WU1, WD1) = _flash_fwd(qp, kvup, kpe, comm=group2)
    WG1, WU1, WD1 = whole(WG1, 4, D, n_ff), whole(WU1, 4, D, n_ff), whole(WD1, 4, n_ff, D)
    x3 = _mm_plain("attn_out", o, WO, F32, res=x2)
    (h3,) = _norm_fwd("norm_ffn1", x3, ffn_norm[1:2])
    (silu1, dsu1, a1), _ = _ffn_up("ffn_up1", h3, WG1, WU1)
    x4 = _ffn_down("ffn_down1", a1, WD1, x3)
    loss_part, dx4, d_final, dx4b = _final_loss(x4, final_norm[None], tgt)

    def pair_views(grads):
        return [_canon(a.reshape((4, 2, a.shape[1] // 2) + a.shape[2:]), 2) for a in grads]

    def pair_sums(tag, full, theirs):
        return [_pair_add(f"pair_add{tag}_{i}", a, b, core_s) for i, (a, b) in enumerate(zip(full, theirs))]

    def pair_stage(tag, grads):
        full = pair_views(grads)
        return pair_sums(tag, full, _comm_call("reduce_pair" + tag, _pair_comm(full)))

    dWD1 = _mm_tn_rows("ffn_dwd1", a1, dx4b, 4, BF)
    dg1, du1 = _ffn_bwd_act("ffn_bwd_act1", dx4b, WD1, silu1, dsu1)
    dWG1 = _mm_tn_cols("ffn_dwg1", h3, dg1, 4, BF)
    dWU1 = _mm_tn_cols("ffn_dwu1", h3, du1, 4, BF)
    full1 = pair_views([dWG1, dWU1, dWD1])
    dh3, theirs1 = _ffn_dh("ffn_dh1", dg1, WG1, du1, WU1, comm=_pair_comm(full1))
    sums1 = pair_sums("1", full1, theirs1)
    dx3, d_ffn1, dx3b = _norm_bwd("norm_ffn1_bwd", x3, ffn_norm[1:2], [dh3], dx4)

    do = _mm_nt_plain("attn_out_dx", dx3b, WO, BF)
    dWO = _mm_tn_plain("attn_out_dw", o, dx3b, BF)
    tb = _tile(T, 512)
    lse_rows = lse.reshape(H, T // tb, 1, tb)
    delta_rows = _attn_delta(o, do).reshape(H, T // tb, 1, tb)
    (dkvup, dkpe_h, dq_raw), landed1 = _flash_bwd(qp, kvup, kpe, do, lse_rows, delta_rows,
                                                  comm=_chips_comm(sums1))
    dqp = _dq_finish(dq_raw, cos, sin)
    dql = _mm_nt_cols("q_b_dx", dqp, WQB, F32)
    dWQB = _mm_tn_cols("q_b_dw", ql, dqp, 4, BF)
    _, d_qln, dqa = _norm_bwd("norm_q_bwd", qa, q_latent_norm, [dql], None)
    dha = _mm_nt_plain("q_a_dx", dqa, WQA, F32)
    dWQA = _mm_tn_plain("q_a_dw", ha, dqa, BF)
    dckv = _mm_nt_cols("kv_b_dx", dkvup, WKVB, F32)
    dWKVB = _mm_tn_cols("kv_b_dw", ckv, dkvup, 4, BF)
    dkvp, d_kvln = _kv_post_bwd(kvp, kv_latent_norm[None], dckv, dkpe_h, cos, sin)
    dhk = _mm_nt_plain("kv_a_dx", dkvp, WKVA, F32)
    dWKVA = _mm_tn_plain("kv_a_dw", hk, dkvp, BF)
    dx2, d_n2, dx2b = _norm_bwd("norm_attn_bwd", x2, jnp.stack([kv_in_norm, attn_norm[0]]), [dhk, dha], dx3)

    def chip_sums(tag, sums, landed):
        return [_chip_sum(f"chip_sum{tag}_{i}", a, b, chip_s) for i, (a, b) in enumerate(zip(sums, landed))]

    sums_a = pair_stage("a", [dWKVA.reshape(4, D // 4, KL + 128), dWKVB, dWQA.reshape(4, D // 4, QL), dWQB,
                              dWO.reshape(4, H * VDIM // 4, D)])
    dWD0, landed_a = _mm_tn_rows("ffn_dwd0", a0, dx2b, 4, BF, comm=_chips_comm(sums_a))
    halves_a1 = chip_sums("a", sums_a, landed_a) + chip_sums("1", sums1, landed1)
    dg0, du0 = _ffn_bwd_act("ffn_bwd_act0", dx2b, WD0, silu0, dsu0)
    dWG0 = _mm_tn_cols("ffn_dwg0", h1, dg0, 4, BF)
    dWU0, other_a1 = _mm_tn_cols("ffn_dwu0", h1, du0, 4, BF, comm=_share_comm(halves_a1))
    sums0 = pair_stage("0", [dWG0, dWU0, dWD0])
    dh1, landed0 = _ffn_dh("ffn_dh0", dg0, WG0, du0, WU0, comm=_chips_comm(sums0))
    dx1, d_ffn0, _ = _norm_bwd("norm_ffn0_bwd", x1, ffn_norm[0:1], [dh1], dx2)

    dpre, d_ps = _pool_dpre(dx1, pre, ps_full)
    ddiff = _pool_ddiff(dpre, PW)
    dPW = _pool_dw(diff, dpre)
    dx0, d_pn = _pool_bwd(x0, pn_full, ddiff, dx1)

    sums_p = pair_stage("p", [dPW])
    landed_p = _comm_call("reduce_chips", _chips_comm(sums_p))
    halves_p0 = chip_sums("p", sums_p, landed_p) + chip_sums("0", sums0, landed0)
    other_p0 = _comm_call("share_halves", _share_comm(halves_p0))
    halves = halves_p0[:1] + halves_a1[:5] + halves_p0[1:] + halves_a1[5:]
    other = list(other_p0[:1]) + list(other_a1[:5]) + list(other_p0[1:]) + list(other_a1[5:])

    def rows(a):
        return a.reshape((1, -1, a.shape[-1]))

    def unpad_kva(a):
        return rows(a)[:, :, :n_kva]

    def unpad_qb(a):
        return rows(a).reshape(1, -1, hs, HEAD_PAD)[:, :, :, :NOPE + ROPE].reshape(1, -1, hs * (NOPE + ROPE))

    def layers(a0_, a1_):
        return jnp.concatenate([rows(a0_), rows(a1_)], axis=0)

    def by_name(hv):
        return {"pool_w": rows(hv[0]), "w_kv_a": unpad_kva(hv[1]), "w_kv_b": rows(hv[2]), "w_q_a": rows(hv[3]),
                "w_q_b": unpad_qb(hv[4]), "w_o": rows(hv[5]), "w_gate": layers(hv[6], hv[9]),
                "w_up": layers(hv[7], hv[10]), "w_down": layers(hv[8], hv[11])}

    big = ["pool_w", "w_kv_a", "w_kv_b", "w_q_a", "w_q_b", "w_o", "w_gate", "w_up", "w_down"]
    g_mine, g_other = by_name(halves), by_name(other)

    lat = jnp.concatenate([d_kvln[0], d_qln[0], jnp.zeros((D - KL - QL,), F32)])
    lrow = jnp.pad(loss_part[0], (0, D - 1))
    small = jnp.stack([d_n2[0], d_n2[1], d_ffn0[0], d_ffn1[0], d_final[0], lat, d_pn[0], d_ps[0], lrow]
                      + [jnp.zeros((D,), F32)] * 7)
    red = _allreduce_small(small)
    loss = red[8, 0]
    g_kv_in, g_attn, g_final = red[0], red[1:2], red[4]
    g_ffn = red[2:4]
    g_kvln, g_qln = red[5, :KL], red[5:6, KL:KL + QL]
    g_pn = lax.dynamic_slice(red[6:7], (0, chip * pg), (1, pg))
    g_ps = lax.dynamic_slice(red[7:8], (0, chip * pg), (1, pg))

    grads = {"pool_norm": g_pn, "pool_scale": g_ps, "kv_in_norm": g_kv_in, "kv_latent_norm": g_kvln,
             "attn_norm": g_attn, "q_latent_norm": g_qln, "ffn_norm": g_ffn, "final_norm": g_final}
    weights = dict(pool_norm=pool_norm, pool_w=pool_w, pool_scale=pool_scale, kv_in_norm=kv_in_norm, w_kv_a=w_kv_a,
                   kv_latent_norm=kv_latent_norm, w_kv_b=w_kv_b, attn_norm=attn_norm, w_q_a=w_q_a,
                   q_latent_norm=q_latent_norm, w_q_b=w_q_b, w_o=w_o, ffn_norm=ffn_norm, w_gate=w_gate, w_up=w_up,
                   w_down=w_down, final_norm=final_norm)
    ms = dict(pool_norm=m_pool_norm, pool_w=m_pool_w, pool_scale=m_pool_scale, kv_in_norm=m_kv_in_norm,
              w_kv_a=m_w_kv_a, kv_latent_norm=m_kv_latent_norm, w_kv_b=m_w_kv_b, attn_norm=m_attn_norm,
              w_q_a=m_w_q_a, q_latent_norm=m_q_latent_norm, w_q_b=m_w_q_b, w_o=m_w_o, ffn_norm=m_ffn_norm,
              w_gate=m_w_gate, w_up=m_w_up, w_down=m_w_down, final_norm=m_final_norm)
    vs = dict(pool_norm=v_pool_norm, pool_w=v_pool_w, pool_scale=v_pool_scale, kv_in_norm=v_kv_in_norm,
              w_kv_a=v_w_kv_a, kv_latent_norm=v_kv_latent_norm, w_kv_b=v_w_kv_b, attn_norm=v_attn_norm,
              w_q_a=v_w_q_a, q_latent_norm=v_q_latent_norm, w_q_b=v_w_q_b, w_o=v_w_o, ffn_norm=v_ffn_norm,
              w_gate=v_w_gate, w_up=v_w_up, w_down=v_w_down, final_norm=v_final_norm)
    names = list(weights)

    def as2d(a):
        return a.reshape((-1, a.shape[-1]))

    delta_w, new_m, new_v = {}, {}, {}
    for nm in big:
        n_layers = g_mine[nm].shape[0]

        def two_halves(a):
            a = as2d(a)
            return a.reshape(n_layers, 2, a.shape[0] // (2 * n_layers), a.shape[1])

        g, d, m2, v2 = _adamw_halves("adamw_" + nm, two_halves(weights[nm]), two_halves(ms[nm]),
                                     two_halves(vs[nm]), g_mine[nm], g_other[nm], core_s)
        shp = weights[nm].shape
        grads[nm], delta_w[nm], new_m[nm], new_v[nm] = g.reshape(shp), d.reshape(shp), m2.reshape(shp), v2.reshape(shp)
    groups = [["kv_in_norm", "attn_norm", "ffn_norm", "final_norm"], ["kv_latent_norm", "q_latent_norm"],
              ["pool_norm", "pool_scale"]]
    for gi, grp in enumerate(groups):
        cat = lambda t: jnp.concatenate([as2d(t[nm]) for nm in grp], axis=0)
        d, m2, v2 = _adamw(f"adamw_vec{gi}", cat(weights), cat(grads), cat(ms), cat(vs))
        r0 = 0
        for nm in grp:
            shp = weights[nm].shape
            r = as2d(weights[nm]).shape[0]
            delta_w[nm], new_m[nm], new_v[nm] = (d[r0:r0 + r].reshape(shp), m2[r0:r0 + r].reshape(shp),
                                                 v2[r0:r0 + r].reshape(shp))
            r0 += r

    return (loss, dx0[None], *[grads[nm].reshape(weights[nm].shape) for nm in names],
            *[delta_w[nm] for nm in names], *[new_m[nm] for nm in names], *[new_v[nm] for nm in names])
```

```python
import functools
import math

import jax
import jax.numpy as jnp
from jax import lax
from jax.experimental import pallas as pl
from jax.experimental.pallas import tpu as pltpu

BF = jnp.bfloat16
F32 = jnp.float32
MESH = pl.DeviceIdType.MESH

N_HEADS = 16
NOPE = 128
ROPE = 64
VDIM = 128
HEAD_PAD = 256
ROPE_BASE = 10000.0
ATTN_SCALE = 1.0 / math.sqrt(NOPE + ROPE)
POOL_WINDOWS = (2, 4, 8, 16)
HALO = 16
NORM_EPS = 1e-6
ADAM_LR, ADAM_B1, ADAM_B2, ADAM_EPS, ADAM_WD, ADAM_STEP = 0.001, 0.9, 0.999, 1e-08, 0.01, 10
NEG = -1e30

V7X_VMEM_BYTES = 64 * 1024 * 1024
VMEM_CEILING = V7X_VMEM_BYTES - 8 * 1024 * 1024

NN = (((1,), (0,)), ((), ()))
NT = (((1,), (1,)), ((), ()))
TN = (((0,), (0,)), ((), ()))


def _cparams(sem, vmem_mb):
    return pltpu.CompilerParams(dimension_semantics=sem,
                                vmem_limit_bytes=min(vmem_mb * 1024 * 1024, VMEM_CEILING))


def _tile(n, pref):
    if n <= pref:
        return n
    t = (pref // 128) * 128
    while t > 128 and n % t:
        t -= 128
    assert n % t == 0, (n, pref)
    return t


def _sds(shape, dtype):
    return jax.ShapeDtypeStruct(shape, dtype)


ANY = pl.BlockSpec(memory_space=pl.ANY)


class _Comm:
    def __init__(self, ins, out_shapes, alias, n_sems, start, finish):
        self.ins, self.out_shapes, self.alias, self.n_sems = list(ins), list(out_shapes), dict(alias), n_sems
        self.start, self.finish = start, finish


def _call(name, body, grid, in_specs, out_specs, out_shape, scratch, vmem_mb, operands, comm=None):
    in_specs, out_specs, out_shape, scratch = list(in_specs), list(out_specs), list(out_shape), list(scratch)
    params = _cparams(("arbitrary",) * len(grid), vmem_mb)
    if comm is None:
        res = pl.pallas_call(body, name=name, grid=grid, in_specs=in_specs, out_specs=out_specs,
                             out_shape=out_shape, scratch_shapes=scratch, compiler_params=params)(*operands)
        return list(res), []
    n_in, n_out, n_scr = len(in_specs), len(out_specs), len(scratch)
    c_in, c_out = len(comm.ins), len(comm.out_shapes)

    def hosted(*refs):
        ins, cin = refs[:n_in], refs[n_in:n_in + c_in]
        o0 = n_in + c_in
        outs, cout = refs[o0:o0 + n_out], refs[o0 + n_out:o0 + n_out + c_out]
        s0 = o0 + n_out + c_out
        scr, (send, recv) = refs[s0:s0 + n_scr], refs[s0 + n_scr:]
        ids = [pl.program_id(d) for d in range(len(grid))]
        first = functools.reduce(jnp.logical_and, [i == 0 for i in ids])
        last = functools.reduce(jnp.logical_and, [i == g - 1 for i, g in zip(ids, grid)])

        @pl.when(first)
        def _():
            comm.start(cin, cout, send, recv)

        body(*ins, *outs, *scr)

        @pl.when(last)
        def _():
            comm.finish(cin, cout, send, recv)

    res = pl.pallas_call(
        hosted, name=name, grid=grid, in_specs=in_specs + [ANY] * c_in, out_specs=out_specs + [ANY] * c_out,
        out_shape=out_shape + comm.out_shapes,
        scratch_shapes=scratch + [pltpu.SemaphoreType.DMA((comm.n_sems,)), pltpu.SemaphoreType.DMA((comm.n_sems,))],
        input_output_aliases={n_in + k: n_out + v for k, v in comm.alias.items()}, compiler_params=params,
    )(*operands, *comm.ins)
    return list(res[:n_out]), list(res[n_out:])


def _comm_call(name, comm):
    c_in, c_out = len(comm.ins), len(comm.out_shapes)

    def body(*refs):
        cin, cout = refs[:c_in], refs[c_in:c_in + c_out]
        send, recv = refs[c_in + c_out:]
        comm.start(cin, cout, send, recv)
        comm.finish(cin, cout, send, recv)

    return pl.pallas_call(
        body, name=name, in_specs=[ANY] * c_in, out_specs=[ANY] * c_out, out_shape=comm.out_shapes,
        scratch_shapes=[pltpu.SemaphoreType.DMA((comm.n_sems,)), pltpu.SemaphoreType.DMA((comm.n_sems,))],
        input_output_aliases=dict(comm.alias),
    )(*comm.ins)


def _matmul(name, pairs, pair_specs, dims, grid, nk, extra, extra_specs, out_shapes, out_specs,
            acc_shape, epilogue, vmem_mb, comm=None):
    n_p, n_e, n_o = len(pairs) // 2, len(extra), len(out_shapes)

    def body(*refs):
        ab = refs[:2 * n_p]
        ex = refs[2 * n_p:2 * n_p + n_e]
        outs = refs[2 * n_p + n_e:2 * n_p + n_e + n_o]

        def partial_sum():
            tot = None
            for p in range(n_p):
                a = ab[2 * p][...]
                b = ab[2 * p + 1][...]
                if a.ndim > 2:
                    a = a.reshape(-1, a.shape[-1])
                if b.ndim > 2:
                    b = b.reshape(-1, b.shape[-1])
                d = lax.dot_general(a.astype(BF), b.astype(BF), dims, preferred_element_type=F32)
                tot = d if tot is None else tot + d
            return tot

        if nk == 1:
            epilogue(partial_sum(), ex, outs)
        else:
            acc = refs[-1]
            kk = pl.program_id(2)

            @pl.when(kk == 0)
            def _():
                acc[...] = partial_sum()

            @pl.when(kk > 0)
            def _():
                acc[...] += partial_sum()

            @pl.when(kk == nk - 1)
            def _():
                epilogue(acc[...], ex, outs)

    scratch = [] if nk == 1 else [pltpu.VMEM(acc_shape, F32)]
    res, comm_res = _call(name, body, grid, list(pair_specs) + list(extra_specs), out_specs, out_shapes,
                          scratch, vmem_mb, list(pairs) + list(extra), comm)
    return res if comm is None else (res, comm_res)


def _epi_store(acc, ex, outs):
    outs[0][...] = acc.reshape(outs[0].shape).astype(outs[0].dtype)


def _epi_residual(acc, ex, outs):
    outs[0][...] = (acc + ex[0][...]).astype(outs[0].dtype)


def _swap_halves(x):
    lane = lax.broadcasted_iota(jnp.int32, x.shape, 1)
    return jnp.where((lane % 64) < 32, pltpu.roll(x, 96, 1), pltpu.roll(x, 32, 1))


def _mm_plain(name, a, b, out_dtype, res=None, tm=1024, tn=1024):
    M, K = a.shape
    N = b.shape[1]
    tm, tn = _tile(M, tm), _tile(N, tn)
    extra, extra_specs, epi = [], [], _epi_store
    if res is not None:
        extra, extra_specs, epi = [res], [pl.BlockSpec((tm, tn), lambda i, j, k: (i, j))], _epi_residual
    return _matmul(
        name, [a, b],
        [pl.BlockSpec((tm, K), lambda i, j, k: (i, 0)), pl.BlockSpec((K, tn), lambda i, j, k: (0, j))],
        NN, (M // tm, N // tn, 1), 1, extra, extra_specs,
        [_sds((M, N), out_dtype)], [pl.BlockSpec((tm, tn), lambda i, j, k: (i, j))],
        None, epi, 48)[0]


def _mm_cols(name, a, b3, out_dtype, epilogue=_epi_store, extra=(), extra_specs=(), tm=1024):
    M, K = a.shape
    G, _, n = b3.shape
    tm = _tile(M, tm)
    return _matmul(
        name, [a, b3],
        [pl.BlockSpec((tm, K), lambda i, j, k: (i, 0)), pl.BlockSpec((None, K, n), lambda i, j, k: (j, 0, 0))],
        NN, (M // tm, G, 1), 1, list(extra), list(extra_specs),
        [_sds((M, G * n), out_dtype)], [pl.BlockSpec((tm, n), lambda i, j, k: (i, j))],
        None, epilogue, 40)[0]


def _mm_nt_plain(name, a, b, out_dtype, tm=1024, tn=1024):
    M, K = a.shape
    N = b.shape[0]
    tm, tn = _tile(M, tm), _tile(N, tn)
    return _matmul(
        name, [a, b],
        [pl.BlockSpec((tm, K), lambda i, j, k: (i, 0)), pl.BlockSpec((tn, K), lambda i, j, k: (j, 0))],
        NT, (M // tm, N // tn, 1), 1, [], [],
        [_sds((M, N), out_dtype)], [pl.BlockSpec((tm, tn), lambda i, j, k: (i, j))],
        None, _epi_store, 44)[0]


def _mm_nt_cols(name, a, b3, out_dtype, tm=2048):
    M = a.shape[0]
    G, K, n = b3.shape
    tm = _tile(M, tm)
    return _matmul(
        name, [a, b3],
        [pl.BlockSpec((tm, n), lambda i, j, k: (i, k)), pl.BlockSpec((None, K, n), lambda i, j, k: (k, 0, 0))],
        NT, (M // tm, 1, G), G, [], [],
        [_sds((M, K), out_dtype)], [pl.BlockSpec((tm, K), lambda i, j, k: (i, 0))],
        (tm, K), _epi_store, 40)[0]


def _mm_tn_plain(name, a, b, out_dtype, tt=1024, tn=1024):
    T, K = a.shape
    N = b.shape[1]
    tt, tn = _tile(T, tt), _tile(N, tn)
    return _matmul(
        name, [a, b],
        [pl.BlockSpec((tt, K), lambda i, j, k: (k, 0)), pl.BlockSpec((tt, tn), lambda i, j, k: (k, j))],
        TN, (1, N // tn, T // tt), T // tt, [], [],
        [_sds((K, N), out_dtype)], [pl.BlockSpec((K, tn), lambda i, j, k: (0, j))],
        (K, tn), _epi_store, 48)[0]


def _mm_tn_cols(name, a, b, G, out_dtype, tt=1024, comm=None):
    T, K = a.shape
    n = b.shape[1] // G
    tt = _tile(T, tt)
    r = _matmul(
        name, [a, b],
        [pl.BlockSpec((tt, K), lambda i, j, k: (k, 0)), pl.BlockSpec((tt, n), lambda i, j, k: (k, j))],
        TN, (1, G, T // tt), T // tt, [], [],
        [_sds((G, K, n), out_dtype)], [pl.BlockSpec((None, K, n), lambda i, j, k: (j, 0, 0))],
        (K, n), _epi_store, 48, comm)
    return r[0] if comm is None else (r[0][0], r[1])


def _norm_fwd(name, x, gains, tb=512):
    T, D = x.shape
    G = gains.shape[0]
    tb = _tile(T, tb)

    def body(x_ref, g_ref, *outs):
        xv = x_ref[...]
        xh = xv * lax.rsqrt(jnp.mean(xv * xv, axis=-1, keepdims=True) + NORM_EPS)
        for g in range(G):
            outs[g][...] = (xh * g_ref[g:g + 1, :]).astype(BF)

    return pl.pallas_call(
        body, name=name, grid=(T // tb,),
        in_specs=[pl.BlockSpec((tb, D), lambda i: (i, 0)), pl.BlockSpec((G, D), lambda i: (0, 0))],
        out_specs=[pl.BlockSpec((tb, D), lambda i: (i, 0))] * G,
        out_shape=[_sds((T, D), BF)] * G,
        compiler_params=_cparams(("arbitrary",), 40),
    )(x, gains)


def _norm_bwd(name, x, gains, dhs, dres, tb=256):
    T, D = x.shape
    G = gains.shape[0]
    tb = _tile(T, tb)
    has_res = dres is not None

    def body(*refs):
        x_ref, g_ref = refs[0], refs[1]
        dh_refs = refs[2:2 + G]
        res_ref = refs[2 + G] if has_res else None
        dx_ref, dg_ref, dxb_ref = refs[-3], refs[-2], refs[-1]
        i = pl.program_id(0)
        xv = x_ref[...]
        r = lax.rsqrt(jnp.mean(xv * xv, axis=-1, keepdims=True) + NORM_EPS)
        xh = xv * r
        dx = res_ref[...] if has_res else jnp.zeros_like(xv)
        rows = []
        for g in range(G):
            dh = dh_refs[g][...].astype(F32)
            dy = dh * g_ref[g:g + 1, :]
            dx = dx + r * (dy - xh * jnp.mean(dy * xh, axis=-1, keepdims=True))
            rows.append(jnp.sum(dh * xh, axis=0, keepdims=True))
        dx_ref[...] = dx
        dxb_ref[...] = dx.astype(BF)

        @pl.when(i == 0)
        def _():
            for g in range(G):
                dg_ref[g:g + 1, :] = rows[g]

        @pl.when(i > 0)
        def _():
            for g in range(G):
                dg_ref[g:g + 1, :] += rows[g]

    blk = pl.BlockSpec((tb, D), lambda i: (i, 0))
    ins = [x, gains] + list(dhs) + ([dres] if has_res else [])
    in_specs = [blk, pl.BlockSpec((G, D), lambda i: (0, 0))] + [blk] * (G + (1 if has_res else 0))
    return pl.pallas_call(
        body, name=name, grid=(T // tb,), in_specs=in_specs,
        out_specs=[blk, pl.BlockSpec((G, D), lambda i: (0, 0)), blk],
        out_shape=[_sds((T, D), F32), _sds((G, D), F32), _sds((T, D), BF)],
        compiler_params=_cparams(("arbitrary",), 48),
    )(*ins)


def _pool_fwd(x, gain, tb=256):
    T, D = x.shape
    tb = _tile(T, tb)
    pg = D // len(POOL_WINDOWS)
    per = tb // HALO

    def body(x_ref, xp_ref, g_ref, diff_ref):
        i = pl.program_id(0)
        xx = jnp.concatenate([xp_ref[...], x_ref[...]], axis=0)
        h = xx * lax.rsqrt(jnp.mean(xx * xx, axis=-1, keepdims=True) + NORM_EPS) * g_ref[...]
        row = lax.broadcasted_iota(jnp.int32, (HALO + tb, 1), 0)
        h = jnp.where((row >= HALO) | (i > 0), h, 0.0)
        t = i * tb + row[HALO:] - HALO
        for g, w in enumerate(POOL_WINDOWS):
            hg = h[:, g * pg:(g + 1) * pg]
            s, k = hg, 1
            while k < w:
                s = s + pltpu.roll(s, k, 0)
                k *= 2
            cnt = jnp.minimum(t + 1, w).astype(F32)
            diff_ref[:, g * pg:(g + 1) * pg] = (s[HALO:] / cnt - hg[HALO:]).astype(BF)

    return pl.pallas_call(
        body, name="pool_fwd", grid=(T // tb,),
        in_specs=[pl.BlockSpec((tb, D), lambda i: (i, 0)),
                  pl.BlockSpec((HALO, D), lambda i: (jnp.maximum(i * per - 1, 0), 0)),
                  pl.BlockSpec((1, D), lambda i: (0, 0))],
        out_specs=pl.BlockSpec((tb, D), lambda i: (i, 0)),
        out_shape=_sds((T, D), BF),
        compiler_params=_cparams(("arbitrary",), 40),
    )(x, x, gain)


def _pool_bwd(x, gain, ddiff, dres, tb=256):
    T, D = x.shape
    tb = _tile(T, tb)
    pg = D // len(POOL_WINDOWS)
    per = tb // HALO
    nblk = T // HALO

    def body(x_ref, g_ref, dd_ref, ddn_ref, res_ref, dx_ref, dg_ref):
        i = pl.program_id(0)
        last = i == T // tb - 1
        dd = jnp.concatenate([dd_ref[...], ddn_ref[...]], axis=0)
        row = lax.broadcasted_iota(jnp.int32, (tb + HALO, 1), 0)
        dd = jnp.where((row < tb) | jnp.logical_not(last), dd, 0.0)
        t = i * tb + row
        parts = []
        for g, w in enumerate(POOL_WINDOWS):
            dg_ = dd[:, g * pg:(g + 1) * pg]
            e = dg_ / jnp.minimum(t + 1, w).astype(F32)
            s, k = e, 1
            while k < w:
                s = s + pltpu.roll(s, tb + HALO - k, 0)
                k *= 2
            parts.append(s[:tb] - dg_[:tb])
        dh = jnp.concatenate(parts, axis=1)
        xv = x_ref[...]
        r = lax.rsqrt(jnp.mean(xv * xv, axis=-1, keepdims=True) + NORM_EPS)
        xh = xv * r
        dy = dh * g_ref[...]
        dx_ref[...] = res_ref[...] + r * (dy - xh * jnp.mean(dy * xh, axis=-1, keepdims=True))
        part = jnp.sum(dh * xh, axis=0, keepdims=True)

        @pl.when(i == 0)
        def _():
            dg_ref[...] = part

        @pl.when(i > 0)
        def _():
            dg_ref[...] += part

    blk = pl.BlockSpec((tb, D), lambda i: (i, 0))
    return pl.pallas_call(
        body, name="pool_bwd", grid=(T // tb,),
        in_specs=[blk, pl.BlockSpec((1, D), lambda i: (0, 0)), blk,
                  pl.BlockSpec((HALO, D), lambda i: (jnp.minimum((i + 1) * per, nblk - 1), 0)), blk],
        out_specs=[blk, pl.BlockSpec((1, D), lambda i: (0, 0))],
        out_shape=[_sds((T, D), F32), _sds((1, D), F32)],
        compiler_params=_cparams(("arbitrary",), 48),
    )(x, gain, ddiff, ddiff, dres)


def _pool_w_spec(pg):
    return pl.BlockSpec((4, None, None, pg // 4, pg), lambda i, j, k: (0, j // 2, j % 2, 0, 0))


def _pool_mix(diff, pw, scale, x, tm=1024):
    T, D = x.shape
    pg = D // 4
    tm = _tile(T, tm)

    def epi(acc, ex, outs):
        outs[0][...] = acc
        outs[1][...] = ex[1][...] + acc * ex[0][...]

    blk = pl.BlockSpec((tm, pg), lambda i, j, k: (i, j))
    return _matmul(
        "pool_mix", [diff, pw], [blk, _pool_w_spec(pg)], NN, (T // tm, 4, 1), 1,
        [scale, x], [pl.BlockSpec((1, pg), lambda i, j, k: (0, j)), blk],
        [_sds((T, D), F32), _sds((T, D), F32)], [blk, blk], None, epi, 32)


def _pool_dpre(dx, pre, scale, tb=512):
    T, D = dx.shape
    tb = _tile(T, tb)

    def body(dx_ref, pre_ref, s_ref, dpre_ref, ds_ref):
        i = pl.program_id(0)
        d = dx_ref[...]
        dpre_ref[...] = (d * s_ref[...]).astype(BF)
        part = jnp.sum(d * pre_ref[...], axis=0, keepdims=True)

        @pl.when(i == 0)
        def _():
            ds_ref[...] = part

        @pl.when(i > 0)
        def _():
            ds_ref[...] += part

    blk = pl.BlockSpec((tb, D), lambda i: (i, 0))
    vec = pl.BlockSpec((1, D), lambda i: (0, 0))
    return pl.pallas_call(
        body, name="pool_dpre", grid=(T // tb,), in_specs=[blk, blk, vec], out_specs=[blk, vec],
        out_shape=[_sds((T, D), BF), _sds((1, D), F32)],
        compiler_params=_cparams(("arbitrary",), 40),
    )(dx, pre, scale)


def _pool_ddiff(dpre, pw, tm=1024):
    T, D = dpre.shape
    pg = D // 4
    tm = _tile(T, tm)
    blk = pl.BlockSpec((tm, pg), lambda i, j, k: (i, j))
    return _matmul("pool_ddiff", [dpre, pw], [blk, _pool_w_spec(pg)], NT, (T // tm, 4, 1), 1, [], [],
                   [_sds((T, D), F32)], [blk], None, _epi_store, 32)[0]


def _pool_dw(diff, dpre, tt=1024):
    T, D = diff.shape
    pg = D // 4
    tt = _tile(T, tt)
    blk = pl.BlockSpec((tt, pg), lambda i, j, k: (k, j))
    return _matmul("pool_dw", [diff, dpre], [blk, blk], TN, (1, 4, T // tt), T // tt, [], [],
                   [_sds((4, 2, 2, pg // 4, pg), BF)], [_pool_w_spec(pg)], (pg, pg), _epi_store, 32)[0]


def _ffn_up(name, h, wg, wu, tm=512, comm=None):
    T, D = h.shape
    n = wg.shape[-1]
    F = 4 * n
    tm = _tile(T, tm)

    def body(h_ref, wg_ref, wu_ref, s_ref, ds_ref, a_ref):
        hv = h_ref[...]
        g = jnp.dot(hv, wg_ref[...], preferred_element_type=F32)
        u = jnp.dot(hv, wu_ref[...], preferred_element_type=F32)
        sig = jax.nn.sigmoid(g)
        silu = g * sig
        s_ref[...] = silu.astype(BF)
        ds_ref[...] = (u * (sig * (1.0 + g * (1.0 - sig)))).astype(BF)
        a_ref[...] = (silu * u).astype(BF)

    w_spec = pl.BlockSpec((None, D, n), lambda j, i: (j, 0, 0))
    o_spec = pl.BlockSpec((tm, n), lambda j, i: (i, j))
    return _call(name, body, (4, T // tm), [pl.BlockSpec((tm, D), lambda j, i: (i, 0)), w_spec, w_spec],
                 [o_spec] * 3, [_sds((T, F), BF)] * 3, [], 56, [h, wg, wu], comm)


def _ffn_down(name, a, wd, x, tm=1024, tn=1024):
    T, D = x.shape
    n = wd.shape[1]
    tm, tn = _tile(T, tm), _tile(D, tn)
    return _matmul(
        name, [a, wd],
        [pl.BlockSpec((tm, n), lambda i, j, k: (i, k)),
         pl.BlockSpec((None, n, tn), lambda i, j, k: (k, 0, j))],
        NN, (T // tm, D // tn, 4), 4, [x], [pl.BlockSpec((tm, tn), lambda i, j, k: (i, j))],
        [_sds((T, D), F32)], [pl.BlockSpec((tm, tn), lambda i, j, k: (i, j))],
        (tm, tn), _epi_residual, 40)[0]


def _ffn_bwd_act(name, dx, wd, silu, dsilu_up, tm=1024):
    T, D = dx.shape
    n = wd.shape[1]
    tm = _tile(T, tm)

    def body(dx_ref, w_ref, s_ref, ds_ref, dg_ref, du_ref):
        da = lax.dot_general(dx_ref[...], w_ref[...], NT, preferred_element_type=F32)
        dg_ref[...] = (da * ds_ref[...].astype(F32)).astype(BF)
        du_ref[...] = (da * s_ref[...].astype(F32)).astype(BF)

    blk = pl.BlockSpec((tm, n), lambda i, j: (i, j))
    return pl.pallas_call(
        body, name=name, grid=(T // tm, 4),
        in_specs=[pl.BlockSpec((tm, D), lambda i, j: (i, 0)), pl.BlockSpec((None, n, D), lambda i, j: (j, 0, 0)),
                  blk, blk],
        out_specs=[blk, blk], out_shape=[_sds((T, 4 * n), BF)] * 2,
        compiler_params=_cparams(("arbitrary", "arbitrary"), 56),
    )(dx, wd, silu, dsilu_up)


def _ffn_dh(name, dg, wg, du, wu, tm=512, comm=None):
    T = dg.shape[0]
    D, n = wg.shape[1], wg.shape[2]
    tm = _tile(T, tm)
    a_spec = pl.BlockSpec((tm, n), lambda i, j, k: (i, k))
    w_spec = pl.BlockSpec((None, D, n), lambda i, j, k: (k, 0, 0))
    r = _matmul(
        name, [dg, wg, du, wu], [a_spec, w_spec, a_spec, w_spec], NT, (T // tm, 1, 4), 4, [], [],
        [_sds((T, D), F32)], [pl.BlockSpec((tm, D), lambda i, j, k: (i, 0))], (tm, D), _epi_store, 56, comm)
    return (r[0], []) if comm is None else (r[0][0], r[1])


def _mm_tn_rows(name, a, b, G, out_dtype, tt=1024, tn=1024, comm=None):
    T, N = b.shape
    n = a.shape[1] // G
    tt, tn = _tile(T, tt), _tile(N, tn)
    nj = N // tn
    r = _matmul(
        name, [a, b],
        [pl.BlockSpec((tt, n), lambda i, j, k: (k, j // nj)), pl.BlockSpec((tt, tn), lambda i, j, k: (k, j % nj))],
        TN, (1, G * nj, T // tt), T // tt, [], [],
        [_sds((G, n, N), out_dtype)], [pl.BlockSpec((None, n, tn), lambda i, j, k: (j // nj, 0, j % nj))],
        (n, tn), _epi_store, 48, comm)
    return r[0] if comm is None else (r[0][0], r[1])


def _kv_post(kvp, gain, cos, sin, tb=512):
    T, W = kvp.shape
    KL = W - 128
    tb = _tile(T, tb)

    def body(kv_ref, g_ref, c_ref, s_ref, ckv_ref, kpe_ref):
        lat = kv_ref[:, :KL]
        ckv_ref[...] = (lat * lax.rsqrt(jnp.mean(lat * lat, axis=-1, keepdims=True) + NORM_EPS)
                        * g_ref[...]).astype(BF)
        pe = kv_ref[:, KL:]
        kpe_ref[...] = (pe * c_ref[...] + _swap_halves(pe) * s_ref[...]).astype(BF)

    tab = pl.BlockSpec((tb, 128), lambda i: (i, 0))
    return pl.pallas_call(
        body, name="kv_post", grid=(T // tb,),
        in_specs=[pl.BlockSpec((tb, W), lambda i: (i, 0)), pl.BlockSpec((1, KL), lambda i: (0, 0)), tab, tab],
        out_specs=[pl.BlockSpec((tb, KL), lambda i: (i, 0)), tab],
        out_shape=[_sds((T, KL), BF), _sds((T, 128), BF)],
        compiler_params=_cparams(("arbitrary",), 32),
    )(kvp, gain, cos, sin)


def _kv_post_bwd(kvp, gain, dckv, dkpe_heads, cos, sin, tb=256):
    T, W = kvp.shape
    KL = W - 128
    H = dkpe_heads.shape[0]
    tb = _tile(T, tb)

    def body(kv_ref, g_ref, dc_ref, dk_ref, c_ref, s_ref, out_ref, dg_ref):
        i = pl.program_id(0)
        lat = kv_ref[:, :KL]
        r = lax.rsqrt(jnp.mean(lat * lat, axis=-1, keepdims=True) + NORM_EPS)
        xh = lat * r
        dh = dc_ref[...]
        dy = dh * g_ref[...]
        out_ref[:, :KL] = (r * (dy - xh * jnp.mean(dy * xh, axis=-1, keepdims=True))).astype(BF)
        d = dk_ref[0]
        for h in range(1, H):
            d = d + dk_ref[h]
        out_ref[:, KL:] = (d * c_ref[...] - _swap_halves(d) * s_ref[...]).astype(BF)
        part = jnp.sum(dh * xh, axis=0, keepdims=True)

        @pl.when(i == 0)
        def _():
            dg_ref[...] = part

        @pl.when(i > 0)
        def _():
            dg_ref[...] += part

    tab = pl.BlockSpec((tb, 128), lambda i: (i, 0))
    vec = pl.BlockSpec((1, KL), lambda i: (0, 0))
    return pl.pallas_call(
        body, name="kv_post_bwd", grid=(T // tb,),
        in_specs=[pl.BlockSpec((tb, W), lambda i: (i, 0)), vec, pl.BlockSpec((tb, KL), lambda i: (i, 0)),
                  pl.BlockSpec((H, tb, 128), lambda i: (0, i, 0)), tab, tab],
        out_specs=[pl.BlockSpec((tb, W), lambda i: (i, 0)), vec],
        out_shape=[_sds((T, W), BF), _sds((1, KL), F32)],
        compiler_params=_cparams(("arbitrary",), 32),
    )(kvp, gain, dckv, dkpe_heads, cos, sin)


def _q_up(ql, wqb, cos, sin, tm=1024):
    n = wqb.shape[2]
    tm = _tile(ql.shape[0], tm)

    def epi(acc, ex, outs):
        c, s = ex[0][...] * LOG2_SCALE, ex[1][...] * LOG2_SCALE
        for j in range(n // HEAD_PAD):
            a0 = j * HEAD_PAD
            outs[0][:, a0:a0 + NOPE] = (acc[:, a0:a0 + NOPE] * LOG2_SCALE).astype(BF)
            pe = acc[:, a0 + NOPE:a0 + HEAD_PAD]
            outs[0][:, a0 + NOPE:a0 + HEAD_PAD] = (pe * c + _swap_halves(pe) * s).astype(BF)

    tab = pl.BlockSpec((tm, 128), lambda i, j, k: (i, 0))
    return _mm_cols("q_up", ql, wqb, BF, epilogue=epi, extra=[cos, sin], extra_specs=[tab, tab], tm=tm)


def _causal_mask(tb):
    r = lax.broadcasted_iota(jnp.int32, (tb, tb), 0)
    c = lax.broadcasted_iota(jnp.int32, (tb, tb), 1)
    return r, c


HP = 2
LOG2_SCALE = ATTN_SCALE * math.log2(math.e)


def _fill_keys(k_scr, kv_ref, kpe_ref):
    for hh in range(HP):
        k_scr[hh, :, :NOPE] = kv_ref[:, hh * HEAD_PAD:hh * HEAD_PAD + NOPE]
        k_scr[hh, :, NOPE:] = kpe_ref[...]


def _flash_fwd(qp, kvup, kpe, tb=512, comm=None):
    T = qp.shape[0]
    H = qp.shape[1] // HEAD_PAD
    tb = _tile(T, tb)

    def body(q_ref, kv_ref, kpe_ref, o_ref, lse_ref, k_scr, m_scr, l_scr, acc_scr, sa_scr, sb_scr):
        iq = pl.program_id(1)

        @pl.when(iq == 0)
        def _():
            _fill_keys(k_scr, kv_ref, kpe_ref)

        qs = [q_ref[:, hh * HEAD_PAD:(hh + 1) * HEAD_PAD] for hh in range(HP)]

        for hh in range(HP):
            m_scr[hh] = jnp.full((tb, 128), NEG, F32)
            l_scr[hh] = jnp.zeros((tb, 128), F32)
            acc_scr[hh] = jnp.zeros((tb, VDIM), F32)

        def scores(ik, s_buf):
            off = pl.multiple_of(ik * tb, tb)
            for hh in range(HP):
                s_buf[hh] = lax.dot_general(qs[hh], k_scr[hh, pl.ds(off, tb), :], NT, preferred_element_type=F32)

        def update(ik, s_buf, masked):
            off = pl.multiple_of(ik * tb, tb)
            ps = []
            for hh in range(HP):
                s = s_buf[hh]
                if masked:
                    r, c = _causal_mask(tb)
                    s = jnp.where(c <= r, s, NEG)
                m = m_scr[hh]
                m2 = jnp.maximum(m, jnp.max(s, axis=-1, keepdims=True))
                p = jnp.exp2(s - jnp.concatenate([m2] * (tb // 128), axis=1))
                a = jnp.exp2(m - m2)
                l_scr[hh] = a * l_scr[hh] + jnp.sum(p, axis=-1, keepdims=True)
                m_scr[hh] = m2
                ps.append((a, p.astype(BF)))
            for hh in range(HP):
                a, p = ps[hh]
                v = kv_ref[pl.ds(off, tb), hh * HEAD_PAD + NOPE:(hh + 1) * HEAD_PAD]
                acc_scr[hh] = a * acc_scr[hh] + lax.dot_general(p, v, NN, preferred_element_type=F32)

        def trip(j, carry):
            scores(2 * j + 1, sb_scr)
            update(2 * j, sa_scr, False)
            scores(2 * j + 2, sa_scr)
            update(2 * j + 1, sb_scr, False)
            return carry

        scores(0, sa_scr)
        lax.fori_loop(0, iq // 2, trip, 0)

        @pl.when(iq % 2 == 0)
        def _():
            update(iq, sa_scr, True)

        @pl.when(iq % 2 == 1)
        def _():
            scores(iq, sb_scr)
            update(iq - 1, sa_scr, False)
            update(iq, sb_scr, True)
        for hh in range(HP):
            l = l_scr[hh]
            o_ref[:, hh * VDIM:(hh + 1) * VDIM] = (acc_scr[hh] / l).astype(BF)
            lse_ref[hh] = (m_scr[hh] + jnp.log2(l))[:, :1]

    return _call(
        "flash_fwd", body, (H // HP, T // tb),
        [pl.BlockSpec((tb, HP * HEAD_PAD), lambda h, i: (i, h)),
         pl.BlockSpec((T, HP * HEAD_PAD), lambda h, i: (0, h)),
         pl.BlockSpec((T, 128), lambda h, i: (0, 0))],
        [pl.BlockSpec((tb, HP * VDIM), lambda h, i: (i, h)),
         pl.BlockSpec((HP, tb, 1), lambda h, i: (h, i, 0))],
        [_sds((T, H * VDIM), BF), _sds((H, T, 1), F32)],
        [pltpu.VMEM((HP, T, HEAD_PAD), BF), pltpu.VMEM((HP, tb, 128), F32), pltpu.VMEM((HP, tb, 128), F32),
         pltpu.VMEM((HP, tb, VDIM), F32), pltpu.VMEM((HP, tb, tb), F32), pltpu.VMEM((HP, tb, tb), F32)],
        56, [qp, kvup, kpe], comm)


def _attn_delta(o, do, tb=512):
    T = o.shape[0]
    H = o.shape[1] // VDIM
    tb = _tile(T, tb)

    def body(o_ref, do_ref, d_ref):
        for h in range(H):
            cols = slice(h * VDIM, (h + 1) * VDIM)
            d_ref[h] = jnp.sum(o_ref[:, cols].astype(F32) * do_ref[:, cols].astype(F32), axis=-1, keepdims=True)

    blk = pl.BlockSpec((tb, H * VDIM), lambda i: (i, 0))
    return pl.pallas_call(
        body, name="attn_delta", grid=(T // tb,), in_specs=[blk, blk],
        out_specs=pl.BlockSpec((H, tb, 1), lambda i: (0, i, 0)), out_shape=_sds((H, T, 1), F32),
        compiler_params=_cparams(("arbitrary",), 32),
    )(o, do)


def _flash_bwd(qp, kvup, kpe, do, lse_rows, delta_rows, tb=512, comm=None):
    T = qp.shape[0]
    H = qp.shape[1] // HEAD_PAD
    tb = _tile(T, tb)
    nq = T // tb

    def body(kv_ref, kpe_ref, q_ref, do_ref, lse_ref, dl_ref, dkv_ref, dkpe_ref, dq_ref, dk_scr, dv_scr,
             sa_scr, pa_scr, sb_scr, pb_scr):
        ik = pl.program_id(1)

        @pl.when(ik == 0)
        def _():
            dq_ref[...] = jnp.zeros_like(dq_ref)

        k = jnp.concatenate([kv_ref[:, :NOPE], kpe_ref[...]], axis=1)
        v = kv_ref[:, NOPE:]
        dk_scr[...] = jnp.zeros_like(dk_scr)
        dv_scr[...] = jnp.zeros_like(dv_scr)

        def scores(iq, bufs):
            off = pl.multiple_of(iq * tb, tb)
            bufs[0][...] = lax.dot_general(k, q_ref[pl.ds(off, tb), :], NT, preferred_element_type=F32)
            bufs[1][...] = lax.dot_general(v, do_ref[pl.ds(off, tb), :], NT, preferred_element_type=F32)

        def update(iq, bufs, masked):
            off = pl.multiple_of(iq * tb, tb)
            q = q_ref[pl.ds(off, tb), :]
            d = do_ref[pl.ds(off, tb), :]
            st = bufs[0][...]
            if masked:
                r, c = _causal_mask(tb)
                st = jnp.where(r <= c, st, NEG)
            pt = jnp.exp2(st - lse_ref[iq])
            dst = (pt * (bufs[1][...] - dl_ref[iq])).astype(BF)
            dv_scr[...] += lax.dot_general(pt.astype(BF), d, NN, preferred_element_type=F32)
            dk_scr[...] += lax.dot_general(dst, q, NN, preferred_element_type=F32)
            dq_ref[pl.ds(off, tb), :] += lax.dot_general(dst, k, TN, preferred_element_type=F32)

        A, B = (sa_scr, pa_scr), (sb_scr, pb_scr)
        n = nq - 1 - ik
        scores(ik, A)

        @pl.when(n == 0)
        def _():
            update(ik, A, True)

        @pl.when(n > 0)
        def _():
            scores(ik + 1, B)
            update(ik, A, True)

            def trip(j, carry):
                b = ik + 1 + 2 * j
                scores(b + 1, A)
                update(b, B, False)
                scores(b + 2, B)
                update(b + 1, A, False)
                return carry

            pairs = (n - 1) // 2
            lax.fori_loop(0, pairs, trip, 0)
            last = ik + 1 + 2 * pairs

            @pl.when(last == nq - 1)
            def _():
                update(last, B, False)

            @pl.when(last < nq - 1)
            def _():
                scores(last + 1, A)
                update(last, B, False)
                update(last + 1, A, False)
        dk = dk_scr[...] * (ATTN_SCALE / LOG2_SCALE)
        dkv_ref[:, :NOPE] = dk[:, :NOPE].astype(BF)
        dkv_ref[:, NOPE:] = dv_scr[...].astype(BF)
        dkpe_ref[...] = dk[:, NOPE:]

    rows = pl.BlockSpec((None, nq, 1, tb), lambda h, i: (h, 0, 0, 0))
    return _call(
        "flash_bwd", body, (H, nq),
        [pl.BlockSpec((tb, HEAD_PAD), lambda h, i: (i, h)), pl.BlockSpec((tb, 128), lambda h, i: (i, 0)),
         pl.BlockSpec((T, HEAD_PAD), lambda h, i: (0, h)), pl.BlockSpec((T, VDIM), lambda h, i: (0, h)), rows, rows],
        [pl.BlockSpec((tb, HEAD_PAD), lambda h, i: (i, h)), pl.BlockSpec((None, tb, 128), lambda h, i: (h, i, 0)),
         pl.BlockSpec((T, HEAD_PAD), lambda h, i: (0, h))],
        [_sds((T, H * HEAD_PAD), BF), _sds((H, T, 128), F32), _sds((T, H * HEAD_PAD), F32)],
        [pltpu.VMEM((tb, HEAD_PAD), F32), pltpu.VMEM((tb, VDIM), F32)] + [pltpu.VMEM((tb, tb), F32)] * 4, 56,
        [kvup, kpe, qp, do, lse_rows, delta_rows], comm)


def _dq_finish(dq_raw, cos, sin, tb=256):
    T, W = dq_raw.shape
    tb = _tile(T, tb)

    def body(dq_ref, c_ref, s_ref, o_ref):
        c, s = c_ref[...] * ATTN_SCALE, s_ref[...] * ATTN_SCALE
        for h in range(W // HEAD_PAD):
            a0 = h * HEAD_PAD
            o_ref[:, a0:a0 + NOPE] = (dq_ref[:, a0:a0 + NOPE] * ATTN_SCALE).astype(BF)
            dpe = dq_ref[:, a0 + NOPE:a0 + HEAD_PAD]
            o_ref[:, a0 + NOPE:a0 + HEAD_PAD] = (dpe * c - _swap_halves(dpe) * s).astype(BF)

    blk = pl.BlockSpec((tb, W), lambda i: (i, 0))
    tab = pl.BlockSpec((tb, 128), lambda i: (i, 0))
    return pl.pallas_call(
        body, name="dq_finish", grid=(T // tb,), in_specs=[blk, tab, tab], out_specs=blk,
        out_shape=_sds((T, W), BF), compiler_params=_cparams(("arbitrary",), 40),
    )(dq_raw, cos, sin)


def _final_loss(x, gain, target, tb=512):
    T, D = x.shape
    tb = _tile(T, tb)

    def body(x_ref, g_ref, t_ref, loss_ref, dx_ref, dg_ref, dxb_ref):
        i = pl.program_id(0)
        xv = x_ref[...]
        gv = g_ref[...]
        r = lax.rsqrt(jnp.mean(xv * xv, axis=-1, keepdims=True) + NORM_EPS)
        xh = xv * r
        e = xh * gv - t_ref[...]
        lpart = 0.5 * jnp.sum(jnp.mean(e * e, axis=-1, keepdims=True), axis=0, keepdims=True)
        dy = e / D
        dyg = dy * gv
        dx = r * (dyg - xh * jnp.mean(dyg * xh, axis=-1, keepdims=True))
        dx_ref[...] = dx
        dxb_ref[...] = dx.astype(BF)
        gpart = jnp.sum(dy * xh, axis=0, keepdims=True)

        @pl.when(i == 0)
        def _():
            loss_ref[...] = lpart
            dg_ref[...] = gpart

        @pl.when(i > 0)
        def _():
            loss_ref[...] += lpart
            dg_ref[...] += gpart

    blk = pl.BlockSpec((tb, D), lambda i: (i, 0))
    vec = pl.BlockSpec((1, D), lambda i: (0, 0))
    return pl.pallas_call(
        body, name="final_loss", grid=(T // tb,), in_specs=[blk, vec, blk],
        out_specs=[pl.BlockSpec((1, 1), lambda i: (0, 0)), blk, vec, blk],
        out_shape=[_sds((1, 1), F32), _sds((T, D), F32), _sds((1, D), F32), _sds((T, D), BF)],
        compiler_params=_cparams(("arbitrary",), 56),
    )(x, gain, target)


def _row_tile(R, pref=256):
    t = (min(R, pref) // 16) * 16
    while t >= 16:
        if R % t == 0:
            return t
        t -= 16
    return R


def _prefetch_call(body, name, grid, in_specs, out_specs, out_shape, scalar, operands, vmem_mb=32):
    return pl.pallas_call(
        body, name=name, out_shape=out_shape,
        grid_spec=pltpu.PrefetchScalarGridSpec(num_scalar_prefetch=1, grid=grid, in_specs=in_specs,
                                               out_specs=out_specs),
        compiler_params=_cparams(("arbitrary",) * len(grid), vmem_mb),
    )(scalar, *operands)


def _place_own(name, s, chip):
    _, n, R, C = s.shape
    tr = _row_tile(R, 512)

    def body(chip_ref, s_ref, o_ref):
        o_ref[...] = s_ref[...]

    return _prefetch_call(
        body, name, (2, n, R // tr),
        [pl.BlockSpec((None, None, tr, C), lambda h, j, r, cr: (h, j, r, 0))],
        pl.BlockSpec((None, None, None, tr, C), lambda h, j, r, cr: (cr[0], h, j, r, 0)),
        _sds((4,) + s.shape, s.dtype), chip, [s])


def _pair_add(name, g, theirs, core):
    _, _, n, R, C = g.shape
    tr = _row_tile(R)

    def body(c_ref, a_ref, b_ref, o_ref):
        o_ref[...] = (a_ref[...].astype(F32) + b_ref[...].astype(F32)).astype(BF)

    blk = pl.BlockSpec((None, None, tr, C), lambda k, j, r, cr: (k, j, r, 0))
    return _prefetch_call(
        body, name, (4, n, R // tr),
        [pl.BlockSpec((None, None, None, tr, C), lambda k, j, r, cr: (k, cr[0], j, r, 0)), blk], blk,
        _sds(theirs.shape, BF), core, [g, theirs])


def _chip_sum(name, sums, landed, chip):
    _, n, R, C = sums.shape
    tr = _row_tile(R)

    def body(chip_ref, own_ref, l_ref, o_ref):
        s = own_ref[...].astype(F32)
        for j in range(3):
            s = s + l_ref[j].astype(F32)
        o_ref[...] = s

    return _prefetch_call(
        body, name, (n, R // tr),
        [pl.BlockSpec((None, None, tr, C), lambda j, r, cr: (cr[0], j, r, 0)),
         pl.BlockSpec((3, None, tr, C), lambda j, r, cr: (0, j, r, 0))],
        pl.BlockSpec((None, tr, C), lambda j, r, cr: (j, r, 0)),
        _sds((n, R, C), F32), chip, [sums, landed])


def _adamw_halves(name, w, m, v, g_mine, g_theirs, core, tr=128):
    L, _, R, C = w.shape
    tr = _row_tile(R, tr)
    c1 = 1.0 - ADAM_B1 ** ADAM_STEP
    c2 = 1.0 - ADAM_B2 ** ADAM_STEP

    def body(c_ref, w_ref, m_ref, v_ref, gm_ref, gt_ref, g_ref, d_ref, nm_ref, nv_ref):
        gv = jnp.where(pl.program_id(1) == c_ref[0], gm_ref[...], gt_ref[...])
        nm = ADAM_B1 * m_ref[...] + (1.0 - ADAM_B1) * gv
        nv = ADAM_B2 * v_ref[...] + (1.0 - ADAM_B2) * (gv * gv)
        g_ref[...] = gv
        nm_ref[...] = nm
        nv_ref[...] = nv
        d_ref[...] = -ADAM_LR * ((nm / c1) / (jnp.sqrt(nv / c2) + ADAM_EPS) + ADAM_WD * w_ref[...])

    full = pl.BlockSpec((None, None, tr, C), lambda l, h, r, cr: (l, h, r, 0))
    half = pl.BlockSpec((None, tr, C), lambda l, h, r, cr: (l, r, 0))
    return _prefetch_call(
        body, name, (L, 2, R // tr), [full, full, full, half, half], [full] * 4,
        [_sds(w.shape, F32)] * 4, core, [w, m, v, g_mine, g_theirs])


def _adamw(name, w, g, m, v, tr=128):
    R, C = w.shape
    tr = _tile(R, tr) if R % 8 == 0 else R
    c1 = 1.0 - ADAM_B1 ** ADAM_STEP
    c2 = 1.0 - ADAM_B2 ** ADAM_STEP

    def body(w_ref, g_ref, m_ref, v_ref, d_ref, nm_ref, nv_ref):
        gv = g_ref[...]
        nm = ADAM_B1 * m_ref[...] + (1.0 - ADAM_B1) * gv
        nv = ADAM_B2 * v_ref[...] + (1.0 - ADAM_B2) * (gv * gv)
        nm_ref[...] = nm
        nv_ref[...] = nv
        d_ref[...] = -ADAM_LR * ((nm / c1) / (jnp.sqrt(nv / c2) + ADAM_EPS) + ADAM_WD * w_ref[...])

    blk = pl.BlockSpec((tr, C), lambda i: (i, 0))
    return pl.pallas_call(
        body, name=name, grid=(R // tr,), in_specs=[blk] * 4, out_specs=[blk] * 3,
        out_shape=[_sds((R, C), F32)] * 3, compiler_params=_cparams(("arbitrary",), 32),
    )(w, g, m, v)


def _place():
    x, y, c = lax.axis_index("x"), lax.axis_index("y"), lax.axis_index("c")
    chips = [(1 - x, y), (x, 1 - y), (1 - x, 1 - y)]
    return x, y, c, chips


def _gather_comm(shards, placed):
    n = len(shards)

    def copies(cin, cout, send, recv):
        src, dst = cin[:n], cout
        x, y, c, chips = _place()
        me = 2 * x + y

        def remote(i, k, s, d, to):
            return pltpu.make_async_remote_copy(src_ref=s, dst_ref=d, send_sem=send.at[6 * i + k],
                                                recv_sem=recv.at[6 * i + k], device_id=to, device_id_type=MESH)

        first = [remote(i, j, src[i].at[c], dst[i].at[me, c], (*chip, c))
                 for i in range(n) for j, chip in enumerate(chips)]
        return remote, first, dst, (x, y, c, chips)

    def start(cin, cout, send, recv):
        for cp in copies(cin, cout, send, recv)[1]:
            cp.start()

    def finish(cin, cout, send, recv):
        remote, first, dst, (x, y, c, chips) = copies(cin, cout, send, recv)
        sibling = (x, y, 1 - c)
        passed = []
        for i in range(n):
            for j, (px, py) in enumerate(chips):
                slot = dst[i].at[2 * px + py, c]
                remote(i, j, slot, slot, (px, py, c)).wait_recv()
                cp = remote(i, 3 + j, slot, slot, sibling)
                cp.start()
                passed.append(cp)
        for i in range(n):
            for j, (px, py) in enumerate(chips):
                slot = dst[i].at[2 * px + py, 1 - c]
                remote(i, 3 + j, slot, slot, sibling).wait_recv()
        for cp in first + passed:
            cp.wait_send()

    return _Comm(list(shards) + list(placed), [_sds(p.shape, p.dtype) for p in placed],
                 {n + i: i for i in range(n)}, 6 * n, start, finish)


def _pair_comm(grads):
    n = len(grads)

    def copies(cin, cout, send, recv):
        x, y, c, _ = _place()
        return [pltpu.make_async_remote_copy(
            src_ref=cin[i].at[k, 1 - c], dst_ref=cout[i].at[k], send_sem=send.at[4 * i + k],
            recv_sem=recv.at[4 * i + k], device_id=(x, y, 1 - c), device_id_type=MESH)
            for i in range(n) for k in range(4)]

    def start(cin, cout, send, recv):
        for cp in copies(cin, cout, send, recv):
            cp.start()

    def finish(cin, cout, send, recv):
        for cp in copies(cin, cout, send, recv):
            cp.wait()

    return _Comm(grads, [_sds((4,) + a.shape[2:], a.dtype) for a in grads], {}, 4 * n, start, finish)


def _chips_comm(sums):
    n = len(sums)

    def copies(cin, cout, send, recv):
        x, y, c, chips = _place()
        return [pltpu.make_async_remote_copy(
            src_ref=cin[i].at[2 * px + py], dst_ref=cout[i].at[j], send_sem=send.at[3 * i + j],
            recv_sem=recv.at[3 * i + j], device_id=(px, py, c), device_id_type=MESH)
            for i in range(n) for j, (px, py) in enumerate(chips)]

    def start(cin, cout, send, recv):
        for cp in copies(cin, cout, send, recv):
            cp.start()

    def finish(cin, cout, send, recv):
        for cp in copies(cin, cout, send, recv):
            cp.wait()

    return _Comm(sums, [_sds((3,) + a.shape[1:], a.dtype) for a in sums], {}, 3 * n, start, finish)


def _share_comm(halves):
    n = len(halves)

    def copies(cin, cout, send, recv):
        x, y, c, _ = _place()
        return [pltpu.make_async_remote_copy(
            src_ref=cin[i], dst_ref=cout[i], send_sem=send.at[i], recv_sem=recv.at[i],
            device_id=(x, y, 1 - c), device_id_type=MESH) for i in range(n)]

    def start(cin, cout, send, recv):
        for cp in copies(cin, cout, send, recv):
            cp.start()

    def finish(cin, cout, send, recv):
        for cp in copies(cin, cout, send, recv):
            cp.wait()

    return _Comm(halves, [_sds(a.shape, a.dtype) for a in halves], {}, n, start, finish)


def _allreduce_small(v):
    R, C = v.shape

    def body(v_ref, out_ref, land, send, recv):
        x, y, c, _ = _place()
        me = 4 * x + 2 * y + c
        cps = []
        for m in range(1, 8):
            fx, fy, fc = (m >> 2) & 1, (m >> 1) & 1, m & 1
            peer = (x ^ fx, y ^ fy, c ^ fc)
            cp = pltpu.make_async_remote_copy(
                src_ref=v_ref, dst_ref=land.at[me], send_sem=send.at[m - 1], recv_sem=recv.at[m - 1],
                device_id=peer, device_id_type=MESH)
            cp.start()
            cps.append(cp)
        land[me] = v_ref[...]
        for cp in cps:
            cp.wait()
        s = land[0]
        for d in range(1, 8):
            s = s + land[d]
        out_ref[...] = s

    return pl.pallas_call(
        body, name="allreduce_small",
        in_specs=[pl.BlockSpec(memory_space=pltpu.VMEM)], out_specs=pl.BlockSpec(memory_space=pltpu.VMEM),
        out_shape=_sds((R, C), F32),
        scratch_shapes=[pltpu.VMEM((8, R, C), F32), pltpu.SemaphoreType.DMA((7,)), pltpu.SemaphoreType.DMA((7,))],
    )(v)


def _halves(a):
    return a.reshape((2, a.shape[0] // 2) + a.shape[1:])


def _canon(a, lead):
    piece = a.shape[lead:]
    return a.reshape(a.shape[:lead] + (int(math.prod(piece[:-2])),) + piece[-2:])


def _rope_tables(pos):
    half = ROPE // 2
    inv_freq = ROPE_BASE ** (-jnp.arange(half, dtype=F32) / half)
    ang = pos.astype(F32)[:, None] * inv_freq
    cos, sin = jnp.cos(ang), jnp.sin(ang)
    return jnp.tile(cos, (1, 4)), jnp.concatenate([-sin, sin, -sin, sin], axis=1)


def kernel(x, positions, pool_norm, pool_w, pool_scale, kv_in_norm, w_kv_a, kv_latent_norm, w_kv_b, attn_norm, w_q_a, q_latent_norm, w_q_b, w_o, ffn_norm, w_gate, w_up, w_down, final_norm, loss_target, m_pool_norm, m_pool_w, m_pool_scale, m_kv_in_norm, m_w_kv_a, m_kv_latent_norm, m_w_kv_b, m_attn_norm, m_w_q_a, m_q_latent_norm, m_w_q_b, m_w_o, m_ffn_norm, m_w_gate, m_w_up, m_w_down, m_final_norm, v_pool_norm, v_pool_w, v_pool_scale, v_kv_in_norm, v_w_kv_a, v_kv_latent_norm, v_w_kv_b, v_attn_norm, v_w_q_a, v_q_latent_norm, v_w_q_b, v_w_o, v_ffn_norm, v_w_gate, v_w_up, v_w_down, v_final_norm):
    T, D = x.shape[1], x.shape[2]
    H = N_HEADS
    KL = kv_latent_norm.shape[0]
    QL = q_latent_norm.shape[1]
    pg = D // 4
    x0, tgt = x[0], loss_target[0]
    cos, sin = _rope_tables(positions[0])
    chip = 2 * lax.axis_index("x") + lax.axis_index("y")

    n_kva = w_kv_a.shape[1]
    wkva_s = jnp.pad(w_kv_a, ((0, 0), (0, KL + 128 - n_kva)))
    hs = w_q_b.shape[2] // (NOPE + ROPE)
    wqb_s = jnp.pad(w_q_b[0].reshape(QL, hs, NOPE + ROPE), ((0, 0), (0, 0), (0, HEAD_PAD - NOPE - ROPE)))
    wqb_s = wqb_s.reshape(QL, hs * HEAD_PAD)
    chip_s = chip.astype(jnp.int32).reshape(1)
    core_s = lax.axis_index("c").astype(jnp.int32).reshape(1)

    def halved(a):
        return _canon(_halves(a.astype(BF)), 1)

    def gather_group(tag, shards):
        placed = [_place_own(f"place_own{tag}_{i}", s, chip_s) for i, s in enumerate(shards)]
        return _gather_comm(shards, placed)

    def whole(a, *shape):
        return a.reshape(shape)

    n_ff = w_gate.shape[2]
    group0 = gather_group("0", [halved(pool_w[0]), _canon(jnp.stack([pool_norm, pool_scale]), 1),
                                halved(w_gate[0]), halved(w_up[0])])
    group1 = gather_group("1", [halved(w_down[0]), halved(wkva_s), halved(w_kv_b), halved(w_q_a[0]),
                                halved(wqb_s), halved(w_o[0])])
    group2 = gather_group("2", [halved(w_gate[1]), halved(w_up[1]), halved(w_down[1])])
    PW, PV, WG0, WU0 = _comm_call("gather0", group0)
    WG0, WU0 = whole(WG0, 4, D, n_ff), whole(WU0, 4, D, n_ff)
    pv = jnp.transpose(PV.reshape(4, 2, pg), (1, 0, 2)).reshape(2, D)
    pn_full, ps_full = pv[0:1], pv[1:2]

    diff = _pool_fwd(x0, pn_full)
    pre, x1 = _pool_mix(diff, PW, ps_full, x0)
    (h1,) = _norm_fwd("norm_ffn0", x1, ffn_norm[0:1])
    (silu0, dsu0, a0), (WD0, WKVA, WKVB, WQA, WQB, WO) = _ffn_up("ffn_up0", h1, WG0, WU0, comm=group1)
    WD0 = whole(WD0, 4, n_ff, D)
    WKVA = whole(WKVA, D, KL + 128)
    WKVB = whole(WKVB, 4, KL, -1)
    WQA = whole(WQA, D, QL)
    WQB = whole(WQB, 4, QL, -1)
    WO = whole(WO, H * VDIM, D)
    x2 = _ffn_down("ffn_down0", a0, WD0, x1)
    hk, ha = _norm_fwd("norm_attn", x2, jnp.stack([kv_in_norm, attn_norm[0]]))
    kvp = _mm_plain("kv_a", hk, WKVA, F32)
    ckv, kpe = _kv_post(kvp, kv_latent_norm[None], cos, sin)
    kvup = _mm_cols("kv_b", ckv, WKVB, BF)
    qa = _mm_plain("q_a", ha, WQA, F32)
    (ql,) = _norm_fwd("norm_q", qa, q_latent_norm)
    qp = _q_up(ql, WQB, cos, sin)
    (o, lse), (WG1, WU1, WD1) = _flash_fwd(qp, kvup, kpe, comm=group2)
    WG1, WU1, WD1 = whole(WG1, 4, D, n_ff), whole(WU1, 4, D, n_ff), whole(WD1, 4, n_ff, D)
    x3 = _mm_plain("attn_out", o, WO, F32, res=x2)
    (h3,) = _norm_fwd("norm_ffn1", x3, ffn_norm[1:2])
    (silu1, dsu1, a1), _ = _ffn_up("ffn_up1", h3, WG1, WU1)
    x4 = _ffn_down("ffn_down1", a1, WD1, x3)
    loss_part, dx4, d_final, dx4b = _final_loss(x4, final_norm[None], tgt)

    def pair_views(grads):
        return [_canon(a.reshape((4, 2, a.shape[1] // 2) + a.shape[2:]), 2) for a in grads]

    def pair_sums(tag, full, theirs):
        return [_pair_add(f"pair_add{tag}_{i}", a, b, core_s) for i, (a, b) in enumerate(zip(full, theirs))]

    def pair_stage(tag, grads):
        full = pair_views(grads)
        return pair_sums(tag, full, _comm_call("reduce_pair" + tag, _pair_comm(full)))

    dWD1 = _mm_tn_rows("ffn_dwd1", a1, dx4b, 4, BF)
    dg1, du1 = _ffn_bwd_act("ffn_bwd_act1", dx4b, WD1, silu1, dsu1)
    dWG1 = _mm_tn_cols("ffn_dwg1", h3, dg1, 4, BF)
    dWU1 = _mm_tn_cols("ffn_dwu1", h3, du1, 4, BF)
    full1 = pair_views([dWG1, dWU1, dWD1])
    dh3, theirs1 = _ffn_dh("ffn_dh1", dg1, WG1, du1, WU1, comm=_pair_comm(full1))
    sums1 = pair_sums("1", full1, theirs1)
    dx3, d_ffn1, dx3b = _norm_bwd("norm_ffn1_bwd", x3, ffn_norm[1:2], [dh3], dx4)

    do = _mm_nt_plain("attn_out_dx", dx3b, WO, BF)
    dWO = _mm_tn_plain("attn_out_dw", o, dx3b, BF)
    tb = _tile(T, 512)
    lse_rows = lse.reshape(H, T // tb, 1, tb)
    delta_rows = _attn_delta(o, do).reshape(H, T // tb, 1, tb)
    (dkvup, dkpe_h, dq_raw), landed1 = _flash_bwd(qp, kvup, kpe, do, lse_rows, delta_rows,
                                                  comm=_chips_comm(sums1))
    dqp = _dq_finish(dq_raw, cos, sin)
    dql = _mm_nt_cols("q_b_dx", dqp, WQB, F32)
    dWQB = _mm_tn_cols("q_b_dw", ql, dqp, 4, BF)
    _, d_qln, dqa = _norm_bwd("norm_q_bwd", qa, q_latent_norm, [dql], None)
    dha = _mm_nt_plain("q_a_dx", dqa, WQA, F32)
    dWQA = _mm_tn_plain("q_a_dw", ha, dqa, BF)
    dckv = _mm_nt_cols("kv_b_dx", dkvup, WKVB, F32)
    dWKVB = _mm_tn_cols("kv_b_dw", ckv, dkvup, 4, BF)
    dkvp, d_kvln = _kv_post_bwd(kvp, kv_latent_norm[None], dckv, dkpe_h, cos, sin)
    dhk = _mm_nt_plain("kv_a_dx", dkvp, WKVA, F32)
    dWKVA = _mm_tn_plain("kv_a_dw", hk, dkvp, BF)
    dx2, d_n2, dx2b = _norm_bwd("norm_attn_bwd", x2, jnp.stack([kv_in_norm, attn_norm[0]]), [dhk, dha], dx3)

    def chip_sums(tag, sums, landed):
        return [_chip_sum(f"chip_sum{tag}_{i}", a, b, chip_s) for i, (a, b) in enumerate(zip(sums, landed))]

    sums_a = pair_stage("a", [dWKVA.reshape(4, D // 4, KL + 128), dWKVB, dWQA.reshape(4, D // 4, QL), dWQB,
                              dWO.reshape(4, H * VDIM // 4, D)])
    dWD0, landed_a = _mm_tn_rows("ffn_dwd0", a0, dx2b, 4, BF, comm=_chips_comm(sums_a))
    halves_a1 = chip_sums("a", sums_a, landed_a) + chip_sums("1", sums1, landed1)
    dg0, du0 = _ffn_bwd_act("ffn_bwd_act0", dx2b, WD0, silu0, dsu0)
    dWG0 = _mm_tn_cols("ffn_dwg0", h1, dg0, 4, BF)
    dWU0, other_a1 = _mm_tn_cols("ffn_dwu0", h1, du0, 4, BF, comm=_share_comm(halves_a1))
    sums0 = pair_stage("0", [dWG0, dWU0, dWD0])
    dh1, landed0 = _ffn_dh("ffn_dh0", dg0, WG0, du0, WU0, comm=_chips_comm(sums0))
    dx1, d_ffn0, _ = _norm_bwd("norm_ffn0_bwd", x1, ffn_norm[0:1], [dh1], dx2)

    dpre, d_ps = _pool_dpre(dx1, pre, ps_full)
    ddiff = _pool_ddiff(dpre, PW)
    dPW = _pool_dw(diff, dpre)
    dx0, d_pn = _pool_bwd(x0, pn_full, ddiff, dx1)

    sums_p = pair_stage("p", [dPW])
    landed_p = _comm_call("reduce_chips", _chips_comm(sums_p))
    halves_p0 = chip_sums("p", sums_p, landed_p) + chip_sums("0", sums0, landed0)
    other_p0 = _comm_call("share_halves", _share_comm(halves_p0))
    halves = halves_p0[:1] + halves_a1[:5] + halves_p0[1:] + halves_a1[5:]
    other = list(other_p0[:1]) + list(other_a1[:5]) + list(other_p0[1:]) + list(other_a1[5:])

    def rows(a):
        return a.reshape((1, -1, a.shape[-1]))

    def unpad_kva(a):
        return rows(a)[:, :, :n_kva]

    def unpad_qb(a):
        return rows(a).reshape(1, -1, hs, HEAD_PAD)[:, :, :, :NOPE + ROPE].reshape(1, -1, hs * (NOPE + ROPE))

    def layers(a0_, a1_):
        return jnp.concatenate([rows(a0_), rows(a1_)], axis=0)

    def by_name(hv):
        return {"pool_w": rows(hv[0]), "w_kv_a": unpad_kva(hv[1]), "w_kv_b": rows(hv[2]), "w_q_a": rows(hv[3]),
                "w_q_b": unpad_qb(hv[4]), "w_o": rows(hv[5]), "w_gate": layers(hv[6], hv[9]),
                "w_up": layers(hv[7], hv[10]), "w_down": layers(hv[8], hv[11])}

    big = ["pool_w", "w_kv_a", "w_kv_b", "w_q_a", "w_q_b", "w_o", "w_gate", "w_up", "w_down"]
    g_mine, g_other = by_name(halves), by_name(other)

    lat = jnp.concatenate([d_kvln[0], d_qln[0], jnp.zeros((D - KL - QL,), F32)])
    lrow = jnp.pad(loss_part[0], (0, D - 1))
    small = jnp.stack([d_n2[0], d_n2[1], d_ffn0[0], d_ffn1[0], d_final[0], lat, d_pn[0], d_ps[0], lrow]
                      + [jnp.zeros((D,), F32)] * 7)
    red = _allreduce_small(small)
    loss = red[8, 0]
    g_kv_in, g_attn, g_final = red[0], red[1:2], red[4]
    g_ffn = red[2:4]
    g_kvln, g_qln = red[5, :KL], red[5:6, KL:KL + QL]
    g_pn = lax.dynamic_slice(red[6:7], (0, chip * pg), (1, pg))
    g_ps = lax.dynamic_slice(red[7:8], (0, chip * pg), (1, pg))

    grads = {"pool_norm": g_pn, "pool_scale": g_ps, "kv_in_norm": g_kv_in, "kv_latent_norm": g_kvln,
             "attn_norm": g_attn, "q_latent_norm": g_qln, "ffn_norm": g_ffn, "final_norm": g_final}
    weights = dict(pool_norm=pool_norm, pool_w=pool_w, pool_scale=pool_scale, kv_in_norm=kv_in_norm, w_kv_a=w_kv_a,
                   kv_latent_norm=kv_latent_norm, w_kv_b=w_kv_b, attn_norm=attn_norm, w_q_a=w_q_a,
                   q_latent_norm=q_latent_norm, w_q_b=w_q_b, w_o=w_o, ffn_norm=ffn_norm, w_gate=w_gate, w_up=w_up,
                   w_down=w_down, final_norm=final_norm)
    ms = dict(pool_norm=m_pool_norm, pool_w=m_pool_w, pool_scale=m_pool_scale, kv_in_norm=m_kv_in_norm,
              w_kv_a=m_w_kv_a, kv_latent_norm=m_kv_latent_norm, w_kv_b=m_w_kv_b, attn_norm=m_attn_norm,
              w_q_a=m_w_q_a, q_latent_norm=m_q_latent_norm, w_q_b=m_w_q_b, w_o=m_w_o, ffn_norm=m_ffn_norm,
              w_gate=m_w_gate, w_up=m_w_up, w_down=m_w_down, final_norm=m_final_norm)
    vs = dict(pool_norm=v_pool_norm, pool_w=v_pool_w, pool_scale=v_pool_scale, kv_in_norm=v_kv_in_norm,
              w_kv_a=v_w_kv_a, kv_latent_norm=v_kv_latent_norm, w_kv_b=v_w_kv_b, attn_norm=v_attn_norm,
              w_q_a=v_w_q_a, q_latent_norm=v_q_latent_norm, w_q_b=v_w_q_b, w_o=v_w_o, ffn_norm=v_ffn_norm,
              w_gate=v_w_gate, w_up=v_w_up, w_down=v_w_down, final_norm=v_final_norm)
    names = list(weights)

    def as2d(a):
        return a.reshape((-1, a.shape[-1]))

    delta_w, new_m, new_v = {}, {}, {}
    for nm in big:
        n_layers = g_mine[nm].shape[0]

        def two_halves(a):
            a = as2d(a)
            return a.reshape(n_layers, 2, a.shape[0] // (2 * n_layers), a.shape[1])

        g, d, m2, v2 = _adamw_halves("adamw_" + nm, two_halves(weights[nm]), two_halves(ms[nm]),
                                     two_halves(vs[nm]), g_mine[nm], g_other[nm], core_s)
        shp = weights[nm].shape
        grads[nm], delta_w[nm], new_m[nm], new_v[nm] = g.reshape(shp), d.reshape(shp), m2.reshape(shp), v2.reshape(shp)
    groups = [["kv_in_norm", "attn_norm", "ffn_norm", "final_norm"], ["kv_latent_norm", "q_latent_norm"],
              ["pool_norm", "pool_scale"]]
    for gi, grp in enumerate(groups):
        cat = lambda t: jnp.concatenate([as2d(t[nm]) for nm in grp], axis=0)
        d, m2, v2 = _adamw(f"adamw_vec{gi}", cat(weights), cat(grads), cat(ms), cat(vs))
        r0 = 0
        for nm in grp:
            shp = weights[nm].shape
            r = as2d(weights[nm]).shape[0]
            delta_w[nm], new_m[nm], new_v[nm] = (d[r0:r0 + r].reshape(shp), m2[r0:r0 + r].reshape(shp),
                                                 v2[r0:r0 + r].reshape(shp))
            r0 += r

    return (loss, dx0[None], *[grads[nm].reshape(weights[nm].shape) for nm in names],
            *[delta_w[nm] for nm in names], *[new_m[nm] for nm in names], *[new_v[nm] for nm in names])
```

```python
import functools
import math

import jax
import jax.numpy as jnp
from jax import lax
from jax.experimental import pallas as pl
from jax.experimental.pallas import tpu as pltpu

BF = jnp.bfloat16
F32 = jnp.float32
MESH = pl.DeviceIdType.MESH

N_HEADS = 16
NOPE = 128
ROPE = 64
VDIM = 128
HEAD_PAD = 256
ROPE_BASE = 10000.0
ATTN_SCALE = 1.0 / math.sqrt(NOPE + ROPE)
POOL_WINDOWS = (2, 4, 8, 16)
HALO = 16
NORM_EPS = 1e-6
ADAM_LR, ADAM_B1, ADAM_B2, ADAM_EPS, ADAM_WD, ADAM_STEP = 0.001, 0.9, 0.999, 1e-08, 0.01, 10
NEG = -1e30

V7X_VMEM_BYTES = 64 * 1024 * 1024
VMEM_CEILING = V7X_VMEM_BYTES - 8 * 1024 * 1024

NN = (((1,), (0,)), ((), ()))
NT = (((1,), (1,)), ((), ()))
TN = (((0,), (0,)), ((), ()))


def _cparams(sem, vmem_mb):
    return pltpu.CompilerParams(dimension_semantics=sem,
                                vmem_limit_bytes=min(vmem_mb * 1024 * 1024, VMEM_CEILING))


def _tile(n, pref):
    if n <= pref:
        return n
    t = (pref // 128) * 128
    while t > 128 and n % t:
        t -= 128
    assert n % t == 0, (n, pref)
    return t


def _sds(shape, dtype):
    return jax.ShapeDtypeStruct(shape, dtype)


ANY = pl.BlockSpec(memory_space=pl.ANY)


class _Comm:
    def __init__(self, ins, out_shapes, alias, n_sems, start, finish):
        self.ins, self.out_shapes, self.alias, self.n_sems = list(ins), list(out_shapes), dict(alias), n_sems
        self.start, self.finish = start, finish


def _call(name, body, grid, in_specs, out_specs, out_shape, scratch, vmem_mb, operands, comm=None):
    in_specs, out_specs, out_shape, scratch = list(in_specs), list(out_specs), list(out_shape), list(scratch)
    params = _cparams(("arbitrary",) * len(grid), vmem_mb)
    if comm is None:
        res = pl.pallas_call(body, name=name, grid=grid, in_specs=in_specs, out_specs=out_specs,
                             out_shape=out_shape, scratch_shapes=scratch, compiler_params=params)(*operands)
        return list(res), []
    n_in, n_out, n_scr = len(in_specs), len(out_specs), len(scratch)
    c_in, c_out = len(comm.ins), len(comm.out_shapes)

    def hosted(*refs):
        ins, cin = refs[:n_in], refs[n_in:n_in + c_in]
        o0 = n_in + c_in
        outs, cout = refs[o0:o0 + n_out], refs[o0 + n_out:o0 + n_out + c_out]
        s0 = o0 + n_out + c_out
        scr, (send, recv) = refs[s0:s0 + n_scr], refs[s0 + n_scr:]
        ids = [pl.program_id(d) for d in range(len(grid))]
        first = functools.reduce(jnp.logical_and, [i == 0 for i in ids])
        last = functools.reduce(jnp.logical_and, [i == g - 1 for i, g in zip(ids, grid)])

        @pl.when(first)
        def _():
            comm.start(cin, cout, send, recv)

        body(*ins, *outs, *scr)

        @pl.when(last)
        def _():
            comm.finish(cin, cout, send, recv)

    res = pl.pallas_call(
        hosted, name=name, grid=grid, in_specs=in_specs + [ANY] * c_in, out_specs=out_specs + [ANY] * c_out,
        out_shape=out_shape + comm.out_shapes,
        scratch_shapes=scratch + [pltpu.SemaphoreType.DMA((comm.n_sems,)), pltpu.SemaphoreType.DMA((comm.n_sems,))],
        input_output_aliases={n_in + k: n_out + v for k, v in comm.alias.items()}, compiler_params=params,
    )(*operands, *comm.ins)
    return list(res[:n_out]), list(res[n_out:])


def _comm_call(name, comm):
    c_in, c_out = len(comm.ins), len(comm.out_shapes)

    def body(*refs):
        cin, cout = refs[:c_in], refs[c_in:c_in + c_out]
        send, recv = refs[c_in + c_out:]
        comm.start(cin, cout, send, recv)
        comm.finish(cin, cout, send, recv)

    return pl.pallas_call(
        body, name=name, in_specs=[ANY] * c_in, out_specs=[ANY] * c_out, out_shape=comm.out_shapes,
        scratch_shapes=[pltpu.SemaphoreType.DMA((comm.n_sems,)), pltpu.SemaphoreType.DMA((comm.n_sems,))],
        input_output_aliases=dict(comm.alias),
    )(*comm.ins)


def _matmul(name, pairs, pair_specs, dims, grid, nk, extra, extra_specs, out_shapes, out_specs,
            acc_shape, epilogue, vmem_mb, comm=None):
    n_p, n_e, n_o = len(pairs) // 2, len(extra), len(out_shapes)

    def body(*refs):
        ab = refs[:2 * n_p]
        ex = refs[2 * n_p:2 * n_p + n_e]
        outs = refs[2 * n_p + n_e:2 * n_p + n_e + n_o]

        def partial_sum():
            tot = None
            for p in range(n_p):
                a = ab[2 * p][...]
                b = ab[2 * p + 1][...]
                if a.ndim > 2:
                    a = a.reshape(-1, a.shape[-1])
                if b.ndim > 2:
                    b = b.reshape(-1, b.shape[-1])
                d = lax.dot_general(a.astype(BF), b.astype(BF), dims, preferred_element_type=F32)
                tot = d if tot is None else tot + d
            return tot

        if nk == 1:
            epilogue(partial_sum(), ex, outs)
        else:
            acc = refs[-1]
            kk = pl.program_id(2)

            @pl.when(kk == 0)
            def _():
                acc[...] = partial_sum()

            @pl.when(kk > 0)
            def _():
                acc[...] += partial_sum()

            @pl.when(kk == nk - 1)
            def _():
                epilogue(acc[...], ex, outs)

    scratch = [] if nk == 1 else [pltpu.VMEM(acc_shape, F32)]
    res, comm_res = _call(name, body, grid, list(pair_specs) + list(extra_specs), out_specs, out_shapes,
                          scratch, vmem_mb, list(pairs) + list(extra), comm)
    return res if comm is None else (res, comm_res)


def _epi_store(acc, ex, outs):
    outs[0][...] = acc.reshape(outs[0].shape).astype(outs[0].dtype)


def _epi_residual(acc, ex, outs):
    outs[0][...] = (acc + ex[0][...]).astype(outs[0].dtype)


def _swap_halves(x):
    lane = lax.broadcasted_iota(jnp.int32, x.shape, 1)
    return jnp.where((lane % 64) < 32, pltpu.roll(x, 96, 1), pltpu.roll(x, 32, 1))


def _mm_plain(name, a, b, out_dtype, res=None, tm=1024, tn=1024):
    M, K = a.shape
    N = b.shape[1]
    tm, tn = _tile(M, tm), _tile(N, tn)
    extra, extra_specs, epi = [], [], _epi_store
    if res is not None:
        extra, extra_specs, epi = [res], [pl.BlockSpec((tm, tn), lambda i, j, k: (i, j))], _epi_residual
    return _matmul(
        name, [a, b],
        [pl.BlockSpec((tm, K), lambda i, j, k: (i, 0)), pl.BlockSpec((K, tn), lambda i, j, k: (0, j))],
        NN, (M // tm, N // tn, 1), 1, extra, extra_specs,
        [_sds((M, N), out_dtype)], [pl.BlockSpec((tm, tn), lambda i, j, k: (i, j))],
        None, epi, 48)[0]


def _mm_cols(name, a, b3, out_dtype, epilogue=_epi_store, extra=(), extra_specs=(), tm=1024):
    M, K = a.shape
    G, _, n = b3.shape
    tm = _tile(M, tm)
    return _matmul(
        name, [a, b3],
        [pl.BlockSpec((tm, K), lambda i, j, k: (i, 0)), pl.BlockSpec((None, K, n), lambda i, j, k: (j, 0, 0))],
        NN, (M // tm, G, 1), 1, list(extra), list(extra_specs),
        [_sds((M, G * n), out_dtype)], [pl.BlockSpec((tm, n), lambda i, j, k: (i, j))],
        None, epilogue, 40)[0]


def _mm_nt_plain(name, a, b, out_dtype, tm=1024, tn=1024):
    M, K = a.shape
    N = b.shape[0]
    tm, tn = _tile(M, tm), _tile(N, tn)
    return _matmul(
        name, [a, b],
        [pl.BlockSpec((tm, K), lambda i, j, k: (i, 0)), pl.BlockSpec((tn, K), lambda i, j, k: (j, 0))],
        NT, (M // tm, N // tn, 1), 1, [], [],
        [_sds((M, N), out_dtype)], [pl.BlockSpec((tm, tn), lambda i, j, k: (i, j))],
        None, _epi_store, 44)[0]


def _mm_nt_cols(name, a, b3, out_dtype, tm=2048):
    M = a.shape[0]
    G, K, n = b3.shape
    tm = _tile(M, tm)
    return _matmul(
        name, [a, b3],
        [pl.BlockSpec((tm, n), lambda i, j, k: (i, k)), pl.BlockSpec((None, K, n), lambda i, j, k: (k, 0, 0))],
        NT, (M // tm, 1, G), G, [], [],
        [_sds((M, K), out_dtype)], [pl.BlockSpec((tm, K), lambda i, j, k: (i, 0))],
        (tm, K), _epi_store, 40)[0]


def _mm_tn_plain(name, a, b, out_dtype, tt=1024, tn=1024):
    T, K = a.shape
    N = b.shape[1]
    tt, tn = _tile(T, tt), _tile(N, tn)
    return _matmul(
        name, [a, b],
        [pl.BlockSpec((tt, K), lambda i, j, k: (k, 0)), pl.BlockSpec((tt, tn), lambda i, j, k: (k, j))],
        TN, (1, N // tn, T // tt), T // tt, [], [],
        [_sds((K, N), out_dtype)], [pl.BlockSpec((K, tn), lambda i, j, k: (0, j))],
        (K, tn), _epi_store, 48)[0]


def _mm_tn_cols(name, a, b, G, out_dtype, tt=1024, comm=None):
    T, K = a.shape
    n = b.shape[1] // G
    tt = _tile(T, tt)
    r = _matmul(
        name, [a, b],
        [pl.BlockSpec((tt, K), lambda i, j, k: (k, 0)), pl.BlockSpec((tt, n), lambda i, j, k: (k, j))],
        TN, (1, G, T // tt), T // tt, [], [],
        [_sds((G, K, n), out_dtype)], [pl.BlockSpec((None, K, n), lambda i, j, k: (j, 0, 0))],
        (K, n), _epi_store, 48, comm)
    return r[0] if comm is None else (r[0][0], r[1])


def _norm_fwd(name, x, gains, tb=512):
    T, D = x.shape
    G = gains.shape[0]
    tb = _tile(T, tb)

    def body(x_ref, g_ref, *outs):
        xv = x_ref[...]
        xh = xv * lax.rsqrt(jnp.mean(xv * xv, axis=-1, keepdims=True) + NORM_EPS)
        for g in range(G):
            outs[g][...] = (xh * g_ref[g:g + 1, :]).astype(BF)

    return pl.pallas_call(
        body, name=name, grid=(T // tb,),
        in_specs=[pl.BlockSpec((tb, D), lambda i: (i, 0)), pl.BlockSpec((G, D), lambda i: (0, 0))],
        out_specs=[pl.BlockSpec((tb, D), lambda i: (i, 0))] * G,
        out_shape=[_sds((T, D), BF)] * G,
        compiler_params=_cparams(("arbitrary",), 40),
    )(x, gains)


def _norm_bwd(name, x, gains, dhs, dres, tb=256):
    T, D = x.shape
    G = gains.shape[0]
    tb = _tile(T, tb)
    has_res = dres is not None

    def body(*refs):
        x_ref, g_ref = refs[0], refs[1]
        dh_refs = refs[2:2 + G]
        res_ref = refs[2 + G] if has_res else None
        dx_ref, dg_ref, dxb_ref = refs[-3], refs[-2], refs[-1]
        i = pl.program_id(0)
        xv = x_ref[...]
        r = lax.rsqrt(jnp.mean(xv * xv, axis=-1, keepdims=True) + NORM_EPS)
        xh = xv * r
        dx = res_ref[...] if has_res else jnp.zeros_like(xv)
        rows = []
        for g in range(G):
            dh = dh_refs[g][...].astype(F32)
            dy = dh * g_ref[g:g + 1, :]
            dx = dx + r * (dy - xh * jnp.mean(dy * xh, axis=-1, keepdims=True))
            rows.append(jnp.sum(dh * xh, axis=0, keepdims=True))
        dx_ref[...] = dx
        dxb_ref[...] = dx.astype(BF)

        @pl.when(i == 0)
        def _():
            for g in range(G):
                dg_ref[g:g + 1, :] = rows[g]

        @pl.when(i > 0)
        def _():
            for g in range(G):
                dg_ref[g:g + 1, :] += rows[g]

    blk = pl.BlockSpec((tb, D), lambda i: (i, 0))
    ins = [x, gains] + list(dhs) + ([dres] if has_res else [])
    in_specs = [blk, pl.BlockSpec((G, D), lambda i: (0, 0))] + [blk] * (G + (1 if has_res else 0))
    return pl.pallas_call(
        body, name=name, grid=(T // tb,), in_specs=in_specs,
        out_specs=[blk, pl.BlockSpec((G, D), lambda i: (0, 0)), blk],
        out_shape=[_sds((T, D), F32), _sds((G, D), F32), _sds((T, D), BF)],
        compiler_params=_cparams(("arbitrary",), 48),
    )(*ins)


def _pool_fwd(x, gain, tb=256):
    T, D = x.shape
    tb = _tile(T, tb)
    pg = D // len(POOL_WINDOWS)
    per = tb // HALO

    def body(x_ref, xp_ref, g_ref, diff_ref):
        i = pl.program_id(0)
        xx = jnp.concatenate([xp_ref[...], x_ref[...]], axis=0)
        h = xx * lax.rsqrt(jnp.mean(xx * xx, axis=-1, keepdims=True) + NORM_EPS) * g_ref[...]
        row = lax.broadcasted_iota(jnp.int32, (HALO + tb, 1), 0)
        h = jnp.where((row >= HALO) | (i > 0), h, 0.0)
        t = i * tb + row[HALO:] - HALO
        for g, w in enumerate(POOL_WINDOWS):
            hg = h[:, g * pg:(g + 1) * pg]
            s, k = hg, 1
            while k < w:
                s = s + pltpu.roll(s, k, 0)
                k *= 2
            cnt = jnp.minimum(t + 1, w).astype(F32)
            diff_ref[:, g * pg:(g + 1) * pg] = (s[HALO:] / cnt - hg[HALO:]).astype(BF)

    return pl.pallas_call(
        body, name="pool_fwd", grid=(T // tb,),
        in_specs=[pl.BlockSpec((tb, D), lambda i: (i, 0)),
                  pl.BlockSpec((HALO, D), lambda i: (jnp.maximum(i * per - 1, 0), 0)),
                  pl.BlockSpec((1, D), lambda i: (0, 0))],
        out_specs=pl.BlockSpec((tb, D), lambda i: (i, 0)),
        out_shape=_sds((T, D), BF),
        compiler_params=_cparams(("arbitrary",), 40),
    )(x, x, gain)


def _pool_bwd(x, gain, ddiff, dres, tb=256):
    T, D = x.shape
    tb = _tile(T, tb)
    pg = D // len(POOL_WINDOWS)
    per = tb // HALO
    nblk = T // HALO

    def body(x_ref, g_ref, dd_ref, ddn_ref, res_ref, dx_ref, dg_ref):
        i = pl.program_id(0)
        last = i == T // tb - 1
        dd = jnp.concatenate([dd_ref[...], ddn_ref[...]], axis=0)
        row = lax.broadcasted_iota(jnp.int32, (tb + HALO, 1), 0)
        dd = jnp.where((row < tb) | jnp.logical_not(last), dd, 0.0)
        t = i * tb + row
        parts = []
        for g, w in enumerate(POOL_WINDOWS):
            dg_ = dd[:, g * pg:(g + 1) * pg]
            e = dg_ / jnp.minimum(t + 1, w).astype(F32)
            s, k = e, 1
            while k < w:
                s = s + pltpu.roll(s, tb + HALO - k, 0)
                k *= 2
            parts.append(s[:tb] - dg_[:tb])
        dh = jnp.concatenate(parts, axis=1)
        xv = x_ref[...]
        r = lax.rsqrt(jnp.mean(xv * xv, axis=-1, keepdims=True) + NORM_EPS)
        xh = xv * r
        dy = dh * g_ref[...]
        dx_ref[...] = res_ref[...] + r * (dy - xh * jnp.mean(dy * xh, axis=-1, keepdims=True))
        part = jnp.sum(dh * xh, axis=0, keepdims=True)

        @pl.when(i == 0)
        def _():
            dg_ref[...] = part

        @pl.when(i > 0)
        def _():
            dg_ref[...] += part

    blk = pl.BlockSpec((tb, D), lambda i: (i, 0))
    return pl.pallas_call(
        body, name="pool_bwd", grid=(T // tb,),
        in_specs=[blk, pl.BlockSpec((1, D), lambda i: (0, 0)), blk,
                  pl.BlockSpec((HALO, D), lambda i: (jnp.minimum((i + 1) * per, nblk - 1), 0)), blk],
        out_specs=[blk, pl.BlockSpec((1, D), lambda i: (0, 0))],
        out_shape=[_sds((T, D), F32), _sds((1, D), F32)],
        compiler_params=_cparams(("arbitrary",), 48),
    )(x, gain, ddiff, ddiff, dres)


def _pool_w_spec(pg):
    return pl.BlockSpec((4, None, None, pg // 4, pg), lambda i, j, k: (0, j // 2, j % 2, 0, 0))


def _pool_mix(diff, pw, scale, x, tm=1024):
    T, D = x.shape
    pg = D // 4
    tm = _tile(T, tm)

    def epi(acc, ex, outs):
        outs[0][...] = acc
        outs[1][...] = ex[1][...] + acc * ex[0][...]

    blk = pl.BlockSpec((tm, pg), lambda i, j, k: (i, j))
    return _matmul(
        "pool_mix", [diff, pw], [blk, _pool_w_spec(pg)], NN, (T // tm, 4, 1), 1,
        [scale, x], [pl.BlockSpec((1, pg), lambda i, j, k: (0, j)), blk],
        [_sds((T, D), F32), _sds((T, D), F32)], [blk, blk], None, epi, 32)


def _pool_dpre(dx, pre, scale, tb=512):
    T, D = dx.shape
    tb = _tile(T, tb)

    def body(dx_ref, pre_ref, s_ref, dpre_ref, ds_ref):
        i = pl.program_id(0)
        d = dx_ref[...]
        dpre_ref[...] = (d * s_ref[...]).astype(BF)
        part = jnp.sum(d * pre_ref[...], axis=0, keepdims=True)

        @pl.when(i == 0)
        def _():
            ds_ref[...] = part

        @pl.when(i > 0)
        def _():
            ds_ref[...] += part

    blk = pl.BlockSpec((tb, D), lambda i: (i, 0))
    vec = pl.BlockSpec((1, D), lambda i: (0, 0))
    return pl.pallas_call(
        body, name="pool_dpre", grid=(T // tb,), in_specs=[blk, blk, vec], out_specs=[blk, vec],
        out_shape=[_sds((T, D), BF), _sds((1, D), F32)],
        compiler_params=_cparams(("arbitrary",), 40),
    )(dx, pre, scale)


def _pool_ddiff(dpre, pw, tm=1024):
    T, D = dpre.shape
    pg = D // 4
    tm = _tile(T, tm)
    blk = pl.BlockSpec((tm, pg), lambda i, j, k: (i, j))
    return _matmul("pool_ddiff", [dpre, pw], [blk, _pool_w_spec(pg)], NT, (T // tm, 4, 1), 1, [], [],
                   [_sds((T, D), F32)], [blk], None, _epi_store, 32)[0]


def _pool_dw(diff, dpre, tt=1024):
    T, D = diff.shape
    pg = D // 4
    tt = _tile(T, tt)
    blk = pl.BlockSpec((tt, pg), lambda i, j, k: (k, j))
    return _matmul("pool_dw", [diff, dpre], [blk, blk], TN, (1, 4, T // tt), T // tt, [], [],
                   [_sds((4, 2, 2, pg // 4, pg), BF)], [_pool_w_spec(pg)], (pg, pg), _epi_store, 32)[0]


def _ffn_up(name, h, wg, wu, tm=512, comm=None):
    T, D = h.shape
    n = wg.shape[-1]
    F = 4 * n
    tm = _tile(T, tm)

    def body(h_ref, wg_ref, wu_ref, s_ref, ds_ref, a_ref):
        hv = h_ref[...]
        g = jnp.dot(hv, wg_ref[...], preferred_element_type=F32)
        u = jnp.dot(hv, wu_ref[...], preferred_element_type=F32)
        sig = jax.nn.sigmoid(g)
        silu = g * sig
        s_ref[...] = silu.astype(BF)
        ds_ref[...] = (u * (sig * (1.0 + g * (1.0 - sig)))).astype(BF)
        a_ref[...] = (silu * u).astype(BF)

    w_spec = pl.BlockSpec((None, D, n), lambda j, i: (j, 0, 0))
    o_spec = pl.BlockSpec((tm, n), lambda j, i: (i, j))
    return _call(name, body, (4, T // tm), [pl.BlockSpec((tm, D), lambda j, i: (i, 0)), w_spec, w_spec],
                 [o_spec] * 3, [_sds((T, F), BF)] * 3, [], 56, [h, wg, wu], comm)


def _ffn_down(name, a, wd, x, tm=1024, tn=1024):
    T, D = x.shape
    n = wd.shape[1]
    tm, tn = _tile(T, tm), _tile(D, tn)
    return _matmul(
        name, [a, wd],
        [pl.BlockSpec((tm, n), lambda i, j, k: (i, k)),
         pl.BlockSpec((None, n, tn), lambda i, j, k: (k, 0, j))],
        NN, (T // tm, D // tn, 4), 4, [x], [pl.BlockSpec((tm, tn), lambda i, j, k: (i, j))],
        [_sds((T, D), F32)], [pl.BlockSpec((tm, tn), lambda i, j, k: (i, j))],
        (tm, tn), _epi_residual, 40)[0]


def _ffn_bwd_act(name, dx, wd, silu, dsilu_up, tm=1024):
    T, D = dx.shape
    n = wd.shape[1]
    tm = _tile(T, tm)

    def body(dx_ref, w_ref, s_ref, ds_ref, dg_ref, du_ref):
        da = lax.dot_general(dx_ref[...], w_ref[...], NT, preferred_element_type=F32)
        dg_ref[...] = (da * ds_ref[...].astype(F32)).astype(BF)
        du_ref[...] = (da * s_ref[...].astype(F32)).astype(BF)

    blk = pl.BlockSpec((tm, n), lambda i, j: (i, j))
    return pl.pallas_call(
        body, name=name, grid=(T // tm, 4),
        in_specs=[pl.BlockSpec((tm, D), lambda i, j: (i, 0)), pl.BlockSpec((None, n, D), lambda i, j: (j, 0, 0)),
                  blk, blk],
        out_specs=[blk, blk], out_shape=[_sds((T, 4 * n), BF)] * 2,
        compiler_params=_cparams(("arbitrary", "arbitrary"), 56),
    )(dx, wd, silu, dsilu_up)


def _ffn_dh(name, dg, wg, du, wu, tm=512, comm=None):
    T = dg.shape[0]
    D, n = wg.shape[1], wg.shape[2]
    tm = _tile(T, tm)
    a_spec = pl.BlockSpec((tm, n), lambda i, j, k: (i, k))
    w_spec = pl.BlockSpec((None, D, n), lambda i, j, k: (k, 0, 0))
    r = _matmul(
        name, [dg, wg, du, wu], [a_spec, w_spec, a_spec, w_spec], NT, (T // tm, 1, 4), 4, [], [],
        [_sds((T, D), F32)], [pl.BlockSpec((tm, D), lambda i, j, k: (i, 0))], (tm, D), _epi_store, 56, comm)
    return (r[0], []) if comm is None else (r[0][0], r[1])


def _mm_tn_rows(name, a, b, G, out_dtype, tt=1024, tn=1024, comm=None):
    T, N = b.shape
    n = a.shape[1] // G
    tt, tn = _tile(T, tt), _tile(N, tn)
    nj = N // tn
    r = _matmul(
        name, [a, b],
        [pl.BlockSpec((tt, n), lambda i, j, k: (k, j // nj)), pl.BlockSpec((tt, tn), lambda i, j, k: (k, j % nj))],
        TN, (1, G * nj, T // tt), T // tt, [], [],
        [_sds((G, n, N), out_dtype)], [pl.BlockSpec((None, n, tn), lambda i, j, k: (j // nj, 0, j % nj))],
        (n, tn), _epi_store, 48, comm)
    return r[0] if comm is None else (r[0][0], r[1])


def _kv_post(kvp, gain, cos, sin, tb=512):
    T, W = kvp.shape
    KL = W - 128
    tb = _tile(T, tb)

    def body(kv_ref, g_ref, c_ref, s_ref, ckv_ref, kpe_ref):
        lat = kv_ref[:, :KL]
        ckv_ref[...] = (lat * lax.rsqrt(jnp.mean(lat * lat, axis=-1, keepdims=True) + NORM_EPS)
                        * g_ref[...]).astype(BF)
        pe = kv_ref[:, KL:]
        kpe_ref[...] = (pe * c_ref[...] + _swap_halves(pe) * s_ref[...]).astype(BF)

    tab = pl.BlockSpec((tb, 128), lambda i: (i, 0))
    return pl.pallas_call(
        body, name="kv_post", grid=(T // tb,),
        in_specs=[pl.BlockSpec((tb, W), lambda i: (i, 0)), pl.BlockSpec((1, KL), lambda i: (0, 0)), tab, tab],
        out_specs=[pl.BlockSpec((tb, KL), lambda i: (i, 0)), tab],
        out_shape=[_sds((T, KL), BF), _sds((T, 128), BF)],
        compiler_params=_cparams(("arbitrary",), 32),
    )(kvp, gain, cos, sin)


def _kv_post_bwd(kvp, gain, dckv, dkpe_heads, cos, sin, tb=256):
    T, W = kvp.shape
    KL = W - 128
    H = dkpe_heads.shape[0]
    tb = _tile(T, tb)

    def body(kv_ref, g_ref, dc_ref, dk_ref, c_ref, s_ref, out_ref, dg_ref):
        i = pl.program_id(0)
        lat = kv_ref[:, :KL]
        r = lax.rsqrt(jnp.mean(lat * lat, axis=-1, keepdims=True) + NORM_EPS)
        xh = lat * r
        dh = dc_ref[...]
        dy = dh * g_ref[...]
        out_ref[:, :KL] = (r * (dy - xh * jnp.mean(dy * xh, axis=-1, keepdims=True))).astype(BF)
        d = dk_ref[0]
        for h in range(1, H):
            d = d + dk_ref[h]
        out_ref[:, KL:] = (d * c_ref[...] - _swap_halves(d) * s_ref[...]).astype(BF)
        part = jnp.sum(dh * xh, axis=0, keepdims=True)

        @pl.when(i == 0)
        def _():
            dg_ref[...] = part

        @pl.when(i > 0)
        def _():
            dg_ref[...] += part

    tab = pl.BlockSpec((tb, 128), lambda i: (i, 0))
    vec = pl.BlockSpec((1, KL), lambda i: (0, 0))
    return pl.pallas_call(
        body, name="kv_post_bwd", grid=(T // tb,),
        in_specs=[pl.BlockSpec((tb, W), lambda i: (i, 0)), vec, pl.BlockSpec((tb, KL), lambda i: (i, 0)),
                  pl.BlockSpec((H, tb, 128), lambda i: (0, i, 0)), tab, tab],
        out_specs=[pl.BlockSpec((tb, W), lambda i: (i, 0)), vec],
        out_shape=[_sds((T, W), BF), _sds((1, KL), F32)],
        compiler_params=_cparams(("arbitrary",), 32),
    )(kvp, gain, dckv, dkpe_heads, cos, sin)


def _q_up(ql, wqb, cos, sin, tm=1024):
    n = wqb.shape[2]
    tm = _tile(ql.shape[0], tm)

    def epi(acc, ex, outs):
        c, s = ex[0][...] * LOG2_SCALE, ex[1][...] * LOG2_SCALE
        for j in range(n // HEAD_PAD):
            a0 = j * HEAD_PAD
            outs[0][:, a0:a0 + NOPE] = (acc[:, a0:a0 + NOPE] * LOG2_SCALE).astype(BF)
            pe = acc[:, a0 + NOPE:a0 + HEAD_PAD]
            outs[0][:, a0 + NOPE:a0 + HEAD_PAD] = (pe * c + _swap_halves(pe) * s).astype(BF)

    tab = pl.BlockSpec((tm, 128), lambda i, j, k: (i, 0))
    return _mm_cols("q_up", ql, wqb, BF, epilogue=epi, extra=[cos, sin], extra_specs=[tab, tab], tm=tm)


def _causal_mask(tb):
    r = lax.broadcasted_iota(jnp.int32, (tb, tb), 0)
    c = lax.broadcasted_iota(jnp.int32, (tb, tb), 1)
    return r, c


HP = 2
LOG2_SCALE = ATTN_SCALE * math.log2(math.e)


def _fill_keys(k_scr, kv_ref, kpe_ref):
    for hh in range(HP):
        k_scr[hh, :, :NOPE] = kv_ref[:, hh * HEAD_PAD:hh * HEAD_PAD + NOPE]
        k_scr[hh, :, NOPE:] = kpe_ref[...]


def _flash_fwd(qp, kvup, kpe, tb=512, comm=None):
    T = qp.shape[0]
    H = qp.shape[1] // HEAD_PAD
    tb = _tile(T, tb)

    def body(q_ref, kv_ref, kpe_ref, o_ref, lse_ref, k_scr, m_scr, l_scr, acc_scr, sa_scr, sb_scr):
        iq = pl.program_id(1)

        @pl.when(iq == 0)
        def _():
            _fill_keys(k_scr, kv_ref, kpe_ref)

        qs = [q_ref[:, hh * HEAD_PAD:(hh + 1) * HEAD_PAD] for hh in range(HP)]

        for hh in range(HP):
            m_scr[hh] = jnp.full((tb, 128), NEG, F32)
            l_scr[hh] = jnp.zeros((tb, 128), F32)
            acc_scr[hh] = jnp.zeros((tb, VDIM), F32)

        def scores(ik, s_buf):
            off = pl.multiple_of(ik * tb, tb)
            for hh in range(HP):
                s_buf[hh] = lax.dot_general(qs[hh], k_scr[hh, pl.ds(off, tb), :], NT, preferred_element_type=F32)

        def update(ik, s_buf, masked):
            off = pl.multiple_of(ik * tb, tb)
            ps = []
            for hh in range(HP):
                s = s_buf[hh]
                if masked:
                    r, c = _causal_mask(tb)
                    s = jnp.where(c <= r, s, NEG)
                m = m_scr[hh]
                m2 = jnp.maximum(m, jnp.max(s, axis=-1, keepdims=True))
                p = jnp.exp2(s - jnp.concatenate([m2] * (tb // 128), axis=1))
                a = jnp.exp2(m - m2)
                l_scr[hh] = a * l_scr[hh] + jnp.sum(p, axis=-1, keepdims=True)
                m_scr[hh] = m2
                ps.append((a, p.astype(BF)))
            for hh in range(HP):
                a, p = ps[hh]
                v = kv_ref[pl.ds(off, tb), hh * HEAD_PAD + NOPE:(hh + 1) * HEAD_PAD]
                acc_scr[hh] = a * acc_scr[hh] + lax.dot_general(p, v, NN, preferred_element_type=F32)

        def trip(j, carry):
            scores(2 * j + 1, sb_scr)
            update(2 * j, sa_scr, False)
            scores(2 * j + 2, sa_scr)
            update(2 * j + 1, sb_scr, False)
            return carry

        scores(0, sa_scr)
        lax.fori_loop(0, iq // 2, trip, 0)

        @pl.when(iq % 2 == 0)
        def _():
            update(iq, sa_scr, True)

        @pl.when(iq % 2 == 1)
        def _():
            scores(iq, sb_scr)
            update(iq - 1, sa_scr, False)
            update(iq, sb_scr, True)
        for hh in range(HP):
            l = l_scr[hh]
            o_ref[:, hh * VDIM:(hh + 1) * VDIM] = (acc_scr[hh] / l).astype(BF)
            lse_ref[hh] = (m_scr[hh] + jnp.log2(l))[:, :1]

    return _call(
        "flash_fwd", body, (H // HP, T // tb),
        [pl.BlockSpec((tb, HP * HEAD_PAD), lambda h, i: (i, h)),
         pl.BlockSpec((T, HP * HEAD_PAD), lambda h, i: (0, h)),
         pl.BlockSpec((T, 128), lambda h, i: (0, 0))],
        [pl.BlockSpec((tb, HP * VDIM), lambda h, i: (i, h)),
         pl.BlockSpec((HP, tb, 1), lambda h, i: (h, i, 0))],
        [_sds((T, H * VDIM), BF), _sds((H, T, 1), F32)],
        [pltpu.VMEM((HP, T, HEAD_PAD), BF), pltpu.VMEM((HP, tb, 128), F32), pltpu.VMEM((HP, tb, 128), F32),
         pltpu.VMEM((HP, tb, VDIM), F32), pltpu.VMEM((HP, tb, tb), F32), pltpu.VMEM((HP, tb, tb), F32)],
        56, [qp, kvup, kpe], comm)


def _attn_out_dx(dx, wo, o, tm=1024, tn=1024):
    M, K = dx.shape
    N = wo.shape[0]
    tm, tn = _tile(M, tm), _tile(N, tn)
    hb = tn // VDIM

    def epi(acc, ex, outs):
        d = acc.astype(BF)
        outs[0][...] = d
        for h in range(hb):
            cols = slice(h * VDIM, (h + 1) * VDIM)
            outs[1][h] = jnp.sum(ex[0][:, cols].astype(F32) * d[:, cols].astype(F32), axis=-1, keepdims=True)

    blk = pl.BlockSpec((tm, tn), lambda i, j, k: (i, j))
    return _matmul(
        "attn_out_dx", [dx, wo],
        [pl.BlockSpec((tm, K), lambda i, j, k: (i, 0)), pl.BlockSpec((tn, K), lambda i, j, k: (j, 0))],
        NT, (M // tm, N // tn, 1), 1, [o], [blk],
        [_sds((M, N), BF), _sds((N // VDIM, M, 1), F32)],
        [blk, pl.BlockSpec((hb, tm, 1), lambda i, j, k: (j, i, 0))], None, epi, 48)


def _flash_bwd(qp, kvup, kpe, do, lse_rows, delta_rows, tb=512, comm=None):
    T = qp.shape[0]
    H = qp.shape[1] // HEAD_PAD
    tb = _tile(T, tb)
    nq = T // tb

    def body(kv_ref, kpe_ref, q_ref, do_ref, lse_ref, dl_ref, dkv_ref, dkpe_ref, dq_ref, dk_scr, dv_scr,
             sa_scr, pa_scr, sb_scr, pb_scr):
        ik = pl.program_id(1)

        @pl.when(ik == 0)
        def _():
            dq_ref[...] = jnp.zeros_like(dq_ref)

        k = jnp.concatenate([kv_ref[:, :NOPE], kpe_ref[...]], axis=1)
        v = kv_ref[:, NOPE:]
        dk_scr[...] = jnp.zeros_like(dk_scr)
        dv_scr[...] = jnp.zeros_like(dv_scr)

        def scores(iq, bufs):
            off = pl.multiple_of(iq * tb, tb)
            bufs[0][...] = lax.dot_general(k, q_ref[pl.ds(off, tb), :], NT, preferred_element_type=F32)
            bufs[1][...] = lax.dot_general(v, do_ref[pl.ds(off, tb), :], NT, preferred_element_type=F32)

        def update(iq, bufs, masked):
            off = pl.multiple_of(iq * tb, tb)
            q = q_ref[pl.ds(off, tb), :]
            d = do_ref[pl.ds(off, tb), :]
            st = bufs[0][...]
            if masked:
                r, c = _causal_mask(tb)
                st = jnp.where(r <= c, st, NEG)
            pt = jnp.exp2(st - lse_ref[iq])
            dst = (pt * (bufs[1][...] - dl_ref[iq])).astype(BF)
            dv_scr[...] += lax.dot_general(pt.astype(BF), d, NN, preferred_element_type=F32)
            dk_scr[...] += lax.dot_general(dst, q, NN, preferred_element_type=F32)
            dq_ref[pl.ds(off, tb), :] += lax.dot_general(dst, k, TN, preferred_element_type=F32)

        A, B = (sa_scr, pa_scr), (sb_scr, pb_scr)
        n = nq - 1 - ik
        scores(ik, A)

        @pl.when(n == 0)
        def _():
            update(ik, A, True)

        @pl.when(n > 0)
        def _():
            scores(ik + 1, B)
            update(ik, A, True)

            def trip(j, carry):
                b = ik + 1 + 2 * j
                scores(b + 1, A)
                update(b, B, False)
                scores(b + 2, B)
                update(b + 1, A, False)
                return carry

            pairs = (n - 1) // 2
            lax.fori_loop(0, pairs, trip, 0)
            last = ik + 1 + 2 * pairs

            @pl.when(last == nq - 1)
            def _():
                update(last, B, False)

            @pl.when(last < nq - 1)
            def _():
                scores(last + 1, A)
                update(last, B, False)
                update(last + 1, A, False)
        dk = dk_scr[...] * (ATTN_SCALE / LOG2_SCALE)
        dkv_ref[:, :NOPE] = dk[:, :NOPE].astype(BF)
        dkv_ref[:, NOPE:] = dv_scr[...].astype(BF)
        dkpe_ref[...] = dk[:, NOPE:]

    rows = pl.BlockSpec((None, nq, 1, tb), lambda h, i: (h, 0, 0, 0))
    return _call(
        "flash_bwd", body, (H, nq),
        [pl.BlockSpec((tb, HEAD_PAD), lambda h, i: (i, h)), pl.BlockSpec((tb, 128), lambda h, i: (i, 0)),
         pl.BlockSpec((T, HEAD_PAD), lambda h, i: (0, h)), pl.BlockSpec((T, VDIM), lambda h, i: (0, h)), rows, rows],
        [pl.BlockSpec((tb, HEAD_PAD), lambda h, i: (i, h)), pl.BlockSpec((None, tb, 128), lambda h, i: (h, i, 0)),
         pl.BlockSpec((T, HEAD_PAD), lambda h, i: (0, h))],
        [_sds((T, H * HEAD_PAD), BF), _sds((H, T, 128), F32), _sds((T, H * HEAD_PAD), F32)],
        [pltpu.VMEM((tb, HEAD_PAD), F32), pltpu.VMEM((tb, VDIM), F32)] + [pltpu.VMEM((tb, tb), F32)] * 4, 56,
        [kvup, kpe, qp, do, lse_rows, delta_rows], comm)


def _dq_finish(dq_raw, cos, sin, tb=256):
    T, W = dq_raw.shape
    tb = _tile(T, tb)

    def body(dq_ref, c_ref, s_ref, o_ref):
        c, s = c_ref[...] * ATTN_SCALE, s_ref[...] * ATTN_SCALE
        for h in range(W // HEAD_PAD):
            a0 = h * HEAD_PAD
            o_ref[:, a0:a0 + NOPE] = (dq_ref[:, a0:a0 + NOPE] * ATTN_SCALE).astype(BF)
            dpe = dq_ref[:, a0 + NOPE:a0 + HEAD_PAD]
            o_ref[:, a0 + NOPE:a0 + HEAD_PAD] = (dpe * c - _swap_halves(dpe) * s).astype(BF)

    blk = pl.BlockSpec((tb, W), lambda i: (i, 0))
    tab = pl.BlockSpec((tb, 128), lambda i: (i, 0))
    return pl.pallas_call(
        body, name="dq_finish", grid=(T // tb,), in_specs=[blk, tab, tab], out_specs=blk,
        out_shape=_sds((T, W), BF), compiler_params=_cparams(("arbitrary",), 40),
    )(dq_raw, cos, sin)


def _final_loss(x, gain, target, tb=512):
    T, D = x.shape
    tb = _tile(T, tb)

    def body(x_ref, g_ref, t_ref, loss_ref, dx_ref, dg_ref, dxb_ref):
        i = pl.program_id(0)
        xv = x_ref[...]
        gv = g_ref[...]
        r = lax.rsqrt(jnp.mean(xv * xv, axis=-1, keepdims=True) + NORM_EPS)
        xh = xv * r
        e = xh * gv - t_ref[...]
        lpart = 0.5 * jnp.sum(jnp.mean(e * e, axis=-1, keepdims=True), axis=0, keepdims=True)
        dy = e / D
        dyg = dy * gv
        dx = r * (dyg - xh * jnp.mean(dyg * xh, axis=-1, keepdims=True))
        dx_ref[...] = dx
        dxb_ref[...] = dx.astype(BF)
        gpart = jnp.sum(dy * xh, axis=0, keepdims=True)

        @pl.when(i == 0)
        def _():
            loss_ref[...] = lpart
            dg_ref[...] = gpart

        @pl.when(i > 0)
        def _():
            loss_ref[...] += lpart
            dg_ref[...] += gpart

    blk = pl.BlockSpec((tb, D), lambda i: (i, 0))
    vec = pl.BlockSpec((1, D), lambda i: (0, 0))
    return pl.pallas_call(
        body, name="final_loss", grid=(T // tb,), in_specs=[blk, vec, blk],
        out_specs=[pl.BlockSpec((1, 1), lambda i: (0, 0)), blk, vec, blk],
        out_shape=[_sds((1, 1), F32), _sds((T, D), F32), _sds((1, D), F32), _sds((T, D), BF)],
        compiler_params=_cparams(("arbitrary",), 56),
    )(x, gain, target)


def _row_tile(R, pref=256):
    t = (min(R, pref) // 16) * 16
    while t >= 16:
        if R % t == 0:
            return t
        t -= 16
    return R


def _prefetch_call(body, name, grid, in_specs, out_specs, out_shape, scalar, operands, vmem_mb=32):
    return pl.pallas_call(
        body, name=name, out_shape=out_shape,
        grid_spec=pltpu.PrefetchScalarGridSpec(num_scalar_prefetch=1, grid=grid, in_specs=in_specs,
                                               out_specs=out_specs),
        compiler_params=_cparams(("arbitrary",) * len(grid), vmem_mb),
    )(scalar, *operands)


def _place_own(name, s, chip):
    _, n, R, C = s.shape
    tr = _row_tile(R, 512)

    def body(chip_ref, s_ref, o_ref):
        o_ref[...] = s_ref[...]

    return _prefetch_call(
        body, name, (2, n, R // tr),
        [pl.BlockSpec((None, None, tr, C), lambda h, j, r, cr: (h, j, r, 0))],
        pl.BlockSpec((None, None, None, tr, C), lambda h, j, r, cr: (cr[0], h, j, r, 0)),
        _sds((4,) + s.shape, s.dtype), chip, [s])


def _pair_add(name, g, theirs, core):
    _, _, n, R, C = g.shape
    tr = _row_tile(R)

    def body(c_ref, a_ref, b_ref, o_ref):
        o_ref[...] = (a_ref[...].astype(F32) + b_ref[...].astype(F32)).astype(BF)

    blk = pl.BlockSpec((None, None, tr, C), lambda k, j, r, cr: (k, j, r, 0))
    return _prefetch_call(
        body, name, (4, n, R // tr),
        [pl.BlockSpec((None, None, None, tr, C), lambda k, j, r, cr: (k, cr[0], j, r, 0)), blk], blk,
        _sds(theirs.shape, BF), core, [g, theirs])


def _chip_sum(name, sums, landed, chip):
    _, n, R, C = sums.shape
    tr = _row_tile(R)

    def body(chip_ref, own_ref, l_ref, o_ref):
        s = own_ref[...].astype(F32)
        for j in range(3):
            s = s + l_ref[j].astype(F32)
        o_ref[...] = s

    return _prefetch_call(
        body, name, (n, R // tr),
        [pl.BlockSpec((None, None, tr, C), lambda j, r, cr: (cr[0], j, r, 0)),
         pl.BlockSpec((3, None, tr, C), lambda j, r, cr: (0, j, r, 0))],
        pl.BlockSpec((None, tr, C), lambda j, r, cr: (j, r, 0)),
        _sds((n, R, C), F32), chip, [sums, landed])


def _adamw_halves(name, w, m, v, g_mine, g_theirs, core, tr=128):
    L, _, R, C = w.shape
    tr = _row_tile(R, tr)
    c1 = 1.0 - ADAM_B1 ** ADAM_STEP
    c2 = 1.0 - ADAM_B2 ** ADAM_STEP

    def body(c_ref, w_ref, m_ref, v_ref, gm_ref, gt_ref, g_ref, d_ref, nm_ref, nv_ref):
        gv = jnp.where(pl.program_id(1) == c_ref[0], gm_ref[...], gt_ref[...])
        nm = ADAM_B1 * m_ref[...] + (1.0 - ADAM_B1) * gv
        nv = ADAM_B2 * v_ref[...] + (1.0 - ADAM_B2) * (gv * gv)
        g_ref[...] = gv
        nm_ref[...] = nm
        nv_ref[...] = nv
        d_ref[...] = -ADAM_LR * ((nm / c1) / (jnp.sqrt(nv / c2) + ADAM_EPS) + ADAM_WD * w_ref[...])

    full = pl.BlockSpec((None, None, tr, C), lambda l, h, r, cr: (l, h, r, 0))
    half = pl.BlockSpec((None, tr, C), lambda l, h, r, cr: (l, r, 0))
    return _prefetch_call(
        body, name, (L, 2, R // tr), [full, full, full, half, half], [full] * 4,
        [_sds(w.shape, F32)] * 4, core, [w, m, v, g_mine, g_theirs])


def _adamw(name, w, g, m, v, tr=128):
    R, C = w.shape
    tr = _tile(R, tr) if R % 8 == 0 else R
    c1 = 1.0 - ADAM_B1 ** ADAM_STEP
    c2 = 1.0 - ADAM_B2 ** ADAM_STEP

    def body(w_ref, g_ref, m_ref, v_ref, d_ref, nm_ref, nv_ref):
        gv = g_ref[...]
        nm = ADAM_B1 * m_ref[...] + (1.0 - ADAM_B1) * gv
        nv = ADAM_B2 * v_ref[...] + (1.0 - ADAM_B2) * (gv * gv)
        nm_ref[...] = nm
        nv_ref[...] = nv
        d_ref[...] = -ADAM_LR * ((nm / c1) / (jnp.sqrt(nv / c2) + ADAM_EPS) + ADAM_WD * w_ref[...])

    blk = pl.BlockSpec((tr, C), lambda i: (i, 0))
    return pl.pallas_call(
        body, name=name, grid=(R // tr,), in_specs=[blk] * 4, out_specs=[blk] * 3,
        out_shape=[_sds((R, C), F32)] * 3, compiler_params=_cparams(("arbitrary",), 32),
    )(w, g, m, v)


def _place():
    x, y, c = lax.axis_index("x"), lax.axis_index("y"), lax.axis_index("c")
    chips = [(1 - x, y), (x, 1 - y), (1 - x, 1 - y)]
    return x, y, c, chips


def _gather_comm(shards, placed):
    n = len(shards)

    def copies(cin, cout, send, recv):
        src, dst = cin[:n], cout
        x, y, c, chips = _place()
        me = 2 * x + y

        def remote(i, k, s, d, to):
            return pltpu.make_async_remote_copy(src_ref=s, dst_ref=d, send_sem=send.at[6 * i + k],
                                                recv_sem=recv.at[6 * i + k], device_id=to, device_id_type=MESH)

        first = [remote(i, j, src[i].at[c], dst[i].at[me, c], (*chip, c))
                 for i in range(n) for j, chip in enumerate(chips)]
        return remote, first, dst, (x, y, c, chips)

    def start(cin, cout, send, recv):
        for cp in copies(cin, cout, send, recv)[1]:
            cp.start()

    def finish(cin, cout, send, recv):
        remote, first, dst, (x, y, c, chips) = copies(cin, cout, send, recv)
        sibling = (x, y, 1 - c)
        passed = []
        for i in range(n):
            for j, (px, py) in enumerate(chips):
                slot = dst[i].at[2 * px + py, c]
                remote(i, j, slot, slot, (px, py, c)).wait_recv()
                cp = remote(i, 3 + j, slot, slot, sibling)
                cp.start()
                passed.append(cp)
        for i in range(n):
            for j, (px, py) in enumerate(chips):
                slot = dst[i].at[2 * px + py, 1 - c]
                remote(i, 3 + j, slot, slot, sibling).wait_recv()
        for cp in first + passed:
            cp.wait_send()

    return _Comm(list(shards) + list(placed), [_sds(p.shape, p.dtype) for p in placed],
                 {n + i: i for i in range(n)}, 6 * n, start, finish)


def _pair_comm(grads):
    n = len(grads)

    def copies(cin, cout, send, recv):
        x, y, c, _ = _place()
        return [pltpu.make_async_remote_copy(
            src_ref=cin[i].at[k, 1 - c], dst_ref=cout[i].at[k], send_sem=send.at[4 * i + k],
            recv_sem=recv.at[4 * i + k], device_id=(x, y, 1 - c), device_id_type=MESH)
            for i in range(n) for k in range(4)]

    def start(cin, cout, send, recv):
        for cp in copies(cin, cout, send, recv):
            cp.start()

    def finish(cin, cout, send, recv):
        for cp in copies(cin, cout, send, recv):
            cp.wait()

    return _Comm(grads, [_sds((4,) + a.shape[2:], a.dtype) for a in grads], {}, 4 * n, start, finish)


def _chips_comm(sums):
    n = len(sums)

    def copies(cin, cout, send, recv):
        x, y, c, chips = _place()
        return [pltpu.make_async_remote_copy(
            src_ref=cin[i].at[2 * px + py], dst_ref=cout[i].at[j], send_sem=send.at[3 * i + j],
            recv_sem=recv.at[3 * i + j], device_id=(px, py, c), device_id_type=MESH)
            for i in range(n) for j, (px, py) in enumerate(chips)]

    def start(cin, cout, send, recv):
        for cp in copies(cin, cout, send, recv):
            cp.start()

    def finish(cin, cout, send, recv):
        for cp in copies(cin, cout, send, recv):
            cp.wait()

    return _Comm(sums, [_sds((3,) + a.shape[1:], a.dtype) for a in sums], {}, 3 * n, start, finish)


def _share_comm(halves):
    n = len(halves)

    def copies(cin, cout, send, recv):
        x, y, c, _ = _place()
        return [pltpu.make_async_remote_copy(
            src_ref=cin[i], dst_ref=cout[i], send_sem=send.at[i], recv_sem=recv.at[i],
            device_id=(x, y, 1 - c), device_id_type=MESH) for i in range(n)]

    def start(cin, cout, send, recv):
        for cp in copies(cin, cout, send, recv):
            cp.start()

    def finish(cin, cout, send, recv):
        for cp in copies(cin, cout, send, recv):
            cp.wait()

    return _Comm(halves, [_sds(a.shape, a.dtype) for a in halves], {}, n, start, finish)


def _allreduce_small(v):
    R, C = v.shape

    def body(v_ref, out_ref, land, send, recv):
        x, y, c, _ = _place()
        me = 4 * x + 2 * y + c
        cps = []
        for m in range(1, 8):
            fx, fy, fc = (m >> 2) & 1, (m >> 1) & 1, m & 1
            peer = (x ^ fx, y ^ fy, c ^ fc)
            cp = pltpu.make_async_remote_copy(
                src_ref=v_ref, dst_ref=land.at[me], send_sem=send.at[m - 1], recv_sem=recv.at[m - 1],
                device_id=peer, device_id_type=MESH)
            cp.start()
            cps.append(cp)
        land[me] = v_ref[...]
        for cp in cps:
            cp.wait()
        s = land[0]
        for d in range(1, 8):
            s = s + land[d]
        out_ref[...] = s

    return pl.pallas_call(
        body, name="allreduce_small",
        in_specs=[pl.BlockSpec(memory_space=pltpu.VMEM)], out_specs=pl.BlockSpec(memory_space=pltpu.VMEM),
        out_shape=_sds((R, C), F32),
        scratch_shapes=[pltpu.VMEM((8, R, C), F32), pltpu.SemaphoreType.DMA((7,)), pltpu.SemaphoreType.DMA((7,))],
    )(v)


def _halves(a):
    return a.reshape((2, a.shape[0] // 2) + a.shape[1:])


def _canon(a, lead):
    piece = a.shape[lead:]
    return a.reshape(a.shape[:lead] + (int(math.prod(piece[:-2])),) + piece[-2:])


def _rope_tables(pos):
    half = ROPE // 2
    inv_freq = ROPE_BASE ** (-jnp.arange(half, dtype=F32) / half)
    ang = pos.astype(F32)[:, None] * inv_freq
    cos, sin = jnp.cos(ang), jnp.sin(ang)
    return jnp.tile(cos, (1, 4)), jnp.concatenate([-sin, sin, -sin, sin], axis=1)


def kernel(x, positions, pool_norm, pool_w, pool_scale, kv_in_norm, w_kv_a, kv_latent_norm, w_kv_b, attn_norm, w_q_a, q_latent_norm, w_q_b, w_o, ffn_norm, w_gate, w_up, w_down, final_norm, loss_target, m_pool_norm, m_pool_w, m_pool_scale, m_kv_in_norm, m_w_kv_a, m_kv_latent_norm, m_w_kv_b, m_attn_norm, m_w_q_a, m_q_latent_norm, m_w_q_b, m_w_o, m_ffn_norm, m_w_gate, m_w_up, m_w_down, m_final_norm, v_pool_norm, v_pool_w, v_pool_scale, v_kv_in_norm, v_w_kv_a, v_kv_latent_norm, v_w_kv_b, v_attn_norm, v_w_q_a, v_q_latent_norm, v_w_q_b, v_w_o, v_ffn_norm, v_w_gate, v_w_up, v_w_down, v_final_norm):
    T, D = x.shape[1], x.shape[2]
    H = N_HEADS
    KL = kv_latent_norm.shape[0]
    QL = q_latent_norm.shape[1]
    pg = D // 4
    x0, tgt = x[0], loss_target[0]
    cos, sin = _rope_tables(positions[0])
    chip = 2 * lax.axis_index("x") + lax.axis_index("y")

    n_kva = w_kv_a.shape[1]
    wkva_s = jnp.pad(w_kv_a, ((0, 0), (0, KL + 128 - n_kva)))
    hs = w_q_b.shape[2] // (NOPE + ROPE)
    wqb_s = jnp.pad(w_q_b[0].reshape(QL, hs, NOPE + ROPE), ((0, 0), (0, 0), (0, HEAD_PAD - NOPE - ROPE)))
    wqb_s = wqb_s.reshape(QL, hs * HEAD_PAD)
    chip_s = chip.astype(jnp.int32).reshape(1)
    core_s = lax.axis_index("c").astype(jnp.int32).reshape(1)

    def halved(a):
        return _canon(_halves(a.astype(BF)), 1)

    def gather_group(tag, shards):
        placed = [_place_own(f"place_own{tag}_{i}", s, chip_s) for i, s in enumerate(shards)]
        return _gather_comm(shards, placed)

    def whole(a, *shape):
        return a.reshape(shape)

    n_ff = w_gate.shape[2]
    group0 = gather_group("0", [halved(pool_w[0]), _canon(jnp.stack([pool_norm, pool_scale]), 1),
                                halved(w_gate[0]), halved(w_up[0])])
    group1 = gather_group("1", [halved(w_down[0]), halved(wkva_s), halved(w_kv_b), halved(w_q_a[0]),
                                halved(wqb_s), halved(w_o[0])])
    group2 = gather_group("2", [halved(w_gate[1]), halved(w_up[1]), halved(w_down[1])])
    PW, PV, WG0, WU0 = _comm_call("gather0", group0)
    WG0, WU0 = whole(WG0, 4, D, n_ff), whole(WU0, 4, D, n_ff)
    pv = jnp.transpose(PV.reshape(4, 2, pg), (1, 0, 2)).reshape(2, D)
    pn_full, ps_full = pv[0:1], pv[1:2]

    diff = _pool_fwd(x0, pn_full)
    pre, x1 = _pool_mix(diff, PW, ps_full, x0)
    (h1,) = _norm_fwd("norm_ffn0", x1, ffn_norm[0:1])
    (silu0, dsu0, a0), (WD0, WKVA, WKVB, WQA, WQB, WO) = _ffn_up("ffn_up0", h1, WG0, WU0, comm=group1)
    WD0 = whole(WD0, 4, n_ff, D)
    WKVA = whole(WKVA, D, KL + 128)
    WKVB = whole(WKVB, 4, KL, -1)
    WQA = whole(WQA, D, QL)
    WQB = whole(WQB, 4, QL, -1)
    WO = whole(WO, H * VDIM, D)
    x2 = _ffn_down("ffn_down0", a0, WD0, x1)
    hk, ha = _norm_fwd("norm_attn", x2, jnp.stack([kv_in_norm, attn_norm[0]]))
    kvp = _mm_plain("kv_a", hk, WKVA, F32)
    ckv, kpe = _kv_post(kvp, kv_latent_norm[None], cos, sin)
    kvup = _mm_cols("kv_b", ckv, WKVB, BF)
    qa = _mm_plain("q_a", ha, WQA, F32)
    (ql,) = _norm_fwd("norm_q", qa, q_latent_norm)
    qp = _q_up(ql, WQB, cos, sin)
    (o, lse), (WG1, WU1, WD1) = _flash_fwd(qp, kvup, kpe, comm=group2)
    WG1, WU1, WD1 = whole(WG1, 4, D, n_ff), whole(WU1, 4, D, n_ff), whole(WD1, 4, n_ff, D)
    x3 = _mm_plain("attn_out", o, WO, F32, res=x2)
    (h3,) = _norm_fwd("norm_ffn1", x3, ffn_norm[1:2])
    (silu1, dsu1, a1), _ = _ffn_up("ffn_up1", h3, WG1, WU1)
    x4 = _ffn_down("ffn_down1", a1, WD1, x3)
    loss_part, dx4, d_final, dx4b = _final_loss(x4, final_norm[None], tgt)

    def pair_views(grads):
        return [_canon(a.reshape((4, 2, a.shape[1] // 2) + a.shape[2:]), 2) for a in grads]

    def pair_sums(tag, full, theirs):
        return [_pair_add(f"pair_add{tag}_{i}", a, b, core_s) for i, (a, b) in enumerate(zip(full, theirs))]

    def pair_stage(tag, grads):
        full = pair_views(grads)
        return pair_sums(tag, full, _comm_call("reduce_pair" + tag, _pair_comm(full)))

    dWD1 = _mm_tn_rows("ffn_dwd1", a1, dx4b, 4, BF)
    dg1, du1 = _ffn_bwd_act("ffn_bwd_act1", dx4b, WD1, silu1, dsu1)
    dWG1 = _mm_tn_cols("ffn_dwg1", h3, dg1, 4, BF)
    dWU1 = _mm_tn_cols("ffn_dwu1", h3, du1, 4, BF)
    full1 = pair_views([dWG1, dWU1, dWD1])
    dh3, theirs1 = _ffn_dh("ffn_dh1", dg1, WG1, du1, WU1, comm=_pair_comm(full1))
    sums1 = pair_sums("1", full1, theirs1)
    dx3, d_ffn1, dx3b = _norm_bwd("norm_ffn1_bwd", x3, ffn_norm[1:2], [dh3], dx4)

    do, delta = _attn_out_dx(dx3b, WO, o)
    dWO = _mm_tn_plain("attn_out_dw", o, dx3b, BF)
    tb = _tile(T, 512)
    lse_rows = lse.reshape(H, T // tb, 1, tb)
    delta_rows = delta.reshape(H, T // tb, 1, tb)
    (dkvup, dkpe_h, dq_raw), landed1 = _flash_bwd(qp, kvup, kpe, do, lse_rows, delta_rows,
                                                  comm=_chips_comm(sums1))
    dqp = _dq_finish(dq_raw, cos, sin)
    dql = _mm_nt_cols("q_b_dx", dqp, WQB, F32)
    dWQB = _mm_tn_cols("q_b_dw", ql, dqp, 4, BF)
    _, d_qln, dqa = _norm_bwd("norm_q_bwd", qa, q_latent_norm, [dql], None)
    dha = _mm_nt_plain("q_a_dx", dqa, WQA, F32)
    dWQA = _mm_tn_plain("q_a_dw", ha, dqa, BF)
    dckv = _mm_nt_cols("kv_b_dx", dkvup, WKVB, F32)
    dWKVB = _mm_tn_cols("kv_b_dw", ckv, dkvup, 4, BF)
    dkvp, d_kvln = _kv_post_bwd(kvp, kv_latent_norm[None], dckv, dkpe_h, cos, sin)
    dhk = _mm_nt_plain("kv_a_dx", dkvp, WKVA, F32)
    dWKVA = _mm_tn_plain("kv_a_dw", hk, dkvp, BF)
    dx2, d_n2, dx2b = _norm_bwd("norm_attn_bwd", x2, jnp.stack([kv_in_norm, attn_norm[0]]), [dhk, dha], dx3)

    def chip_sums(tag, sums, landed):
        return [_chip_sum(f"chip_sum{tag}_{i}", a, b, chip_s) for i, (a, b) in enumerate(zip(sums, landed))]

    sums_a = pair_stage("a", [dWKVA.reshape(4, D // 4, KL + 128), dWKVB, dWQA.reshape(4, D // 4, QL), dWQB,
                              dWO.reshape(4, H * VDIM // 4, D)])
    dWD0, landed_a = _mm_tn_rows("ffn_dwd0", a0, dx2b, 4, BF, comm=_chips_comm(sums_a))
    halves_a1 = chip_sums("a", sums_a, landed_a) + chip_sums("1", sums1, landed1)
    dg0, du0 = _ffn_bwd_act("ffn_bwd_act0", dx2b, WD0, silu0, dsu0)
    dWG0 = _mm_tn_cols("ffn_dwg0", h1, dg0, 4, BF)
    dWU0, other_a1 = _mm_tn_cols("ffn_dwu0", h1, du0, 4, BF, comm=_share_comm(halves_a1))
    sums0 = pair_stage("0", [dWG0, dWU0, dWD0])
    dh1, landed0 = _ffn_dh("ffn_dh0", dg0, WG0, du0, WU0, comm=_chips_comm(sums0))
    dx1, d_ffn0, _ = _norm_bwd("norm_ffn0_bwd", x1, ffn_norm[0:1], [dh1], dx2)

    dpre, d_ps = _pool_dpre(dx1, pre, ps_full)
    ddiff = _pool_ddiff(dpre, PW)
    dPW = _pool_dw(diff, dpre)
    dx0, d_pn = _pool_bwd(x0, pn_full, ddiff, dx1)

    sums_p = pair_stage("p", [dPW])
    landed_p = _comm_call("reduce_chips", _chips_comm(sums_p))
    halves_p0 = chip_sums("p", sums_p, landed_p) + chip_sums("0", sums0, landed0)
    other_p0 = _comm_call("share_halves", _share_comm(halves_p0))
    halves = halves_p0[:1] + halves_a1[:5] + halves_p0[1:] + halves_a1[5:]
    other = list(other_p0[:1]) + list(other_a1[:5]) + list(other_p0[1:]) + list(other_a1[5:])

    def rows(a):
        return a.reshape((1, -1, a.shape[-1]))

    def unpad_kva(a):
        return rows(a)[:, :, :n_kva]

    def unpad_qb(a):
        return rows(a).reshape(1, -1, hs, HEAD_PAD)[:, :, :, :NOPE + ROPE].reshape(1, -1, hs * (NOPE + ROPE))

    def layers(a0_, a1_):
        return jnp.concatenate([rows(a0_), rows(a1_)], axis=0)

    def by_name(hv):
        return {"pool_w": rows(hv[0]), "w_kv_a": unpad_kva(hv[1]), "w_kv_b": rows(hv[2]), "w_q_a": rows(hv[3]),
                "w_q_b": unpad_qb(hv[4]), "w_o": rows(hv[5]), "w_gate": layers(hv[6], hv[9]),
                "w_up": layers(hv[7], hv[10]), "w_down": layers(hv[8], hv[11])}

    big = ["pool_w", "w_kv_a", "w_kv_b", "w_q_a", "w_q_b", "w_o", "w_gate", "w_up", "w_down"]
    g_mine, g_other = by_name(halves), by_name(other)

    lat = jnp.concatenate([d_kvln[0], d_qln[0], jnp.zeros((D - KL - QL,), F32)])
    lrow = jnp.pad(loss_part[0], (0, D - 1))
    small = jnp.stack([d_n2[0], d_n2[1], d_ffn0[0], d_ffn1[0], d_final[0], lat, d_pn[0], d_ps[0], lrow]
                      + [jnp.zeros((D,), F32)] * 7)
    red = _allreduce_small(small)
    loss = red[8, 0]
    g_kv_in, g_attn, g_final = red[0], red[1:2], red[4]
    g_ffn = red[2:4]
    g_kvln, g_qln = red[5, :KL], red[5:6, KL:KL + QL]
    g_pn = lax.dynamic_slice(red[6:7], (0, chip * pg), (1, pg))
    g_ps = lax.dynamic_slice(red[7:8], (0, chip * pg), (1, pg))

    grads = {"pool_norm": g_pn, "pool_scale": g_ps, "kv_in_norm": g_kv_in, "kv_latent_norm": g_kvln,
             "attn_norm": g_attn, "q_latent_norm": g_qln, "ffn_norm": g_ffn, "final_norm": g_final}
    weights = dict(pool_norm=pool_norm, pool_w=pool_w, pool_scale=pool_scale, kv_in_norm=kv_in_norm, w_kv_a=w_kv_a,
                   kv_latent_norm=kv_latent_norm, w_kv_b=w_kv_b, attn_norm=attn_norm, w_q_a=w_q_a,
                   q_latent_norm=q_latent_norm, w_q_b=w_q_b, w_o=w_o, ffn_norm=ffn_norm, w_gate=w_gate, w_up=w_up,
                   w_down=w_down, final_norm=final_norm)
    ms = dict(pool_norm=m_pool_norm, pool_w=m_pool_w, pool_scale=m_pool_scale, kv_in_norm=m_kv_in_norm,
              w_kv_a=m_w_kv_a, kv_latent_norm=m_kv_latent_norm, w_kv_b=m_w_kv_b, attn_norm=m_attn_norm,
              w_q_a=m_w_q_a, q_latent_norm=m_q_latent_norm, w_q_b=m_w_q_b, w_o=m_w_o, ffn_norm=m_ffn_norm,
              w_gate=m_w_gate, w_up=m_w_up, w_down=m_w_down, final_norm=m_final_norm)
    vs = dict(pool_norm=v_pool_norm, pool_w=v_pool_w, pool_scale=v_pool_scale, kv_in_norm=v_kv_in_norm,
              w_kv_a=v_w_kv_a, kv_latent_norm=v_kv_latent_norm, w_kv_b=v_w_kv_b, attn_norm=v_attn_norm,
              w_q_a=v_w_q_a, q_latent_norm=v_q_latent_norm, w_q_b=v_w_q_b, w_o=v_w_o, ffn_norm=v_ffn_norm,
              w_gate=v_w_gate, w_up=v_w_up, w_down=v_w_down, final_norm=v_final_norm)
    names = list(weights)

    def as2d(a):
        return a.reshape((-1, a.shape[-1]))

    delta_w, new_m, new_v = {}, {}, {}
    for nm in big:
        n_layers = g_mine[nm].shape[0]

        def two_halves(a):
            a = as2d(a)
            return a.reshape(n_layers, 2, a.shape[0] // (2 * n_layers), a.shape[1])

        g, d, m2, v2 = _adamw_halves("adamw_" + nm, two_halves(weights[nm]), two_halves(ms[nm]),
                                     two_halves(vs[nm]), g_mine[nm], g_other[nm], core_s)
        shp = weights[nm].shape
        grads[nm], delta_w[nm], new_m[nm], new_v[nm] = g.reshape(shp), d.reshape(shp), m2.reshape(shp), v2.reshape(shp)
    groups = [["kv_in_norm", "attn_norm", "ffn_norm", "final_norm"], ["kv_latent_norm", "q_latent_norm"],
              ["pool_norm", "pool_scale"]]
    for gi, grp in enumerate(groups):
        cat = lambda t: jnp.concatenate([as2d(t[nm]) for nm in grp], axis=0)
        d, m2, v2 = _adamw(f"adamw_vec{gi}", cat(weights), cat(grads), cat(ms), cat(vs))
        r0 = 0
        for nm in grp:
            shp = weights[nm].shape
            r = as2d(weights[nm]).shape[0]
            delta_w[nm], new_m[nm], new_v[nm] = (d[r0:r0 + r].reshape(shp), m2[r0:r0 + r].reshape(shp),
                                                 v2[r0:r0 + r].reshape(shp))
            r0 += r

    return (loss, dx0[None], *[grads[nm].reshape(weights[nm].shape) for nm in names],
            *[delta_w[nm] for nm in names], *[new_m[nm] for nm in names], *[new_v[nm] for nm in names])
```

```python
import functools
import math

import jax
import jax.numpy as jnp
from jax import lax
from jax.experimental import pallas as pl
from jax.experimental.pallas import tpu as pltpu

BF = jnp.bfloat16
F32 = jnp.float32
MESH = pl.DeviceIdType.MESH

N_HEADS = 16
NOPE = 128
ROPE = 64
VDIM = 128
HEAD_PAD = 256
ROPE_BASE = 10000.0
ATTN_SCALE = 1.0 / math.sqrt(NOPE + ROPE)
POOL_WINDOWS = (2, 4, 8, 16)
HALO = 16
NORM_EPS = 1e-6
ADAM_LR, ADAM_B1, ADAM_B2, ADAM_EPS, ADAM_WD, ADAM_STEP = 0.001, 0.9, 0.999, 1e-08, 0.01, 10
NEG = -1e30

V7X_VMEM_BYTES = 64 * 1024 * 1024
VMEM_CEILING = V7X_VMEM_BYTES - 8 * 1024 * 1024

NN = (((1,), (0,)), ((), ()))
NT = (((1,), (1,)), ((), ()))
TN = (((0,), (0,)), ((), ()))


def _cparams(sem, vmem_mb):
    return pltpu.CompilerParams(dimension_semantics=sem,
                                vmem_limit_bytes=min(vmem_mb * 1024 * 1024, VMEM_CEILING))


def _tile(n, pref):
    if n <= pref:
        return n
    t = (pref // 128) * 128
    while t > 128 and n % t:
        t -= 128
    assert n % t == 0, (n, pref)
    return t


def _sds(shape, dtype):
    return jax.ShapeDtypeStruct(shape, dtype)


ANY = pl.BlockSpec(memory_space=pl.ANY)


class _Comm:
    def __init__(self, ins, out_shapes, alias, n_sems, start, finish):
        self.ins, self.out_shapes, self.alias, self.n_sems = list(ins), list(out_shapes), dict(alias), n_sems
        self.start, self.finish = start, finish


def _call(name, body, grid, in_specs, out_specs, out_shape, scratch, vmem_mb, operands, comm=None):
    in_specs, out_specs, out_shape, scratch = list(in_specs), list(out_specs), list(out_shape), list(scratch)
    params = _cparams(("arbitrary",) * len(grid), vmem_mb)
    if comm is None:
        res = pl.pallas_call(body, name=name, grid=grid, in_specs=in_specs, out_specs=out_specs,
                             out_shape=out_shape, scratch_shapes=scratch, compiler_params=params)(*operands)
        return list(res), []
    n_in, n_out, n_scr = len(in_specs), len(out_specs), len(scratch)
    c_in, c_out = len(comm.ins), len(comm.out_shapes)

    def hosted(*refs):
        ins, cin = refs[:n_in], refs[n_in:n_in + c_in]
        o0 = n_in + c_in
        outs, cout = refs[o0:o0 + n_out], refs[o0 + n_out:o0 + n_out + c_out]
        s0 = o0 + n_out + c_out
        scr, (send, recv) = refs[s0:s0 + n_scr], refs[s0 + n_scr:]
        ids = [pl.program_id(d) for d in range(len(grid))]
        first = functools.reduce(jnp.logical_and, [i == 0 for i in ids])
        last = functools.reduce(jnp.logical_and, [i == g - 1 for i, g in zip(ids, grid)])

        @pl.when(first)
        def _():
            comm.start(cin, cout, send, recv)

        body(*ins, *outs, *scr)

        @pl.when(last)
        def _():
            comm.finish(cin, cout, send, recv)

    res = pl.pallas_call(
        hosted, name=name, grid=grid, in_specs=in_specs + [ANY] * c_in, out_specs=out_specs + [ANY] * c_out,
        out_shape=out_shape + comm.out_shapes,
        scratch_shapes=scratch + [pltpu.SemaphoreType.DMA((comm.n_sems,)), pltpu.SemaphoreType.DMA((comm.n_sems,))],
        input_output_aliases={n_in + k: n_out + v for k, v in comm.alias.items()}, compiler_params=params,
    )(*operands, *comm.ins)
    return list(res[:n_out]), list(res[n_out:])


def _comm_call(name, comm):
    c_in, c_out = len(comm.ins), len(comm.out_shapes)

    def body(*refs):
        cin, cout = refs[:c_in], refs[c_in:c_in + c_out]
        send, recv = refs[c_in + c_out:]
        comm.start(cin, cout, send, recv)
        comm.finish(cin, cout, send, recv)

    return pl.pallas_call(
        body, name=name, in_specs=[ANY] * c_in, out_specs=[ANY] * c_out, out_shape=comm.out_shapes,
        scratch_shapes=[pltpu.SemaphoreType.DMA((comm.n_sems,)), pltpu.SemaphoreType.DMA((comm.n_sems,))],
        input_output_aliases=dict(comm.alias),
    )(*comm.ins)


def _matmul(name, pairs, pair_specs, dims, grid, nk, extra, extra_specs, out_shapes, out_specs,
            acc_shape, epilogue, vmem_mb, comm=None):
    n_p, n_e, n_o = len(pairs) // 2, len(extra), len(out_shapes)

    def body(*refs):
        ab = refs[:2 * n_p]
        ex = refs[2 * n_p:2 * n_p + n_e]
        outs = refs[2 * n_p + n_e:2 * n_p + n_e + n_o]

        def partial_sum():
            tot = None
            for p in range(n_p):
                a = ab[2 * p][...]
                b = ab[2 * p + 1][...]
                if a.ndim > 2:
                    a = a.reshape(-1, a.shape[-1])
                if b.ndim > 2:
                    b = b.reshape(-1, b.shape[-1])
                d = lax.dot_general(a.astype(BF), b.astype(BF), dims, preferred_element_type=F32)
                tot = d if tot is None else tot + d
            return tot

        if nk == 1:
            epilogue(partial_sum(), ex, outs)
        else:
            acc = refs[-1]
            kk = pl.program_id(2)

            @pl.when(kk == 0)
            def _():
                acc[...] = partial_sum()

            @pl.when(kk > 0)
            def _():
                acc[...] += partial_sum()

            @pl.when(kk == nk - 1)
            def _():
                epilogue(acc[...], ex, outs)

    scratch = [] if nk == 1 else [pltpu.VMEM(acc_shape, F32)]
    res, comm_res = _call(name, body, grid, list(pair_specs) + list(extra_specs), out_specs, out_shapes,
                          scratch, vmem_mb, list(pairs) + list(extra), comm)
    return res if comm is None else (res, comm_res)


def _epi_store(acc, ex, outs):
    outs[0][...] = acc.reshape(outs[0].shape).astype(outs[0].dtype)


def _epi_residual(acc, ex, outs):
    outs[0][...] = (acc + ex[0][...]).astype(outs[0].dtype)


def _swap_halves(x):
    lane = lax.broadcasted_iota(jnp.int32, x.shape, 1)
    return jnp.where((lane % 64) < 32, pltpu.roll(x, 96, 1), pltpu.roll(x, 32, 1))


def _mm_plain(name, a, b, out_dtype, res=None, tm=1024, tn=1024):
    M, K = a.shape
    N = b.shape[1]
    tm, tn = _tile(M, tm), _tile(N, tn)
    extra, extra_specs, epi = [], [], _epi_store
    if res is not None:
        extra, extra_specs, epi = [res], [pl.BlockSpec((tm, tn), lambda i, j, k: (i, j))], _epi_residual
    return _matmul(
        name, [a, b],
        [pl.BlockSpec((tm, K), lambda i, j, k: (i, 0)), pl.BlockSpec((K, tn), lambda i, j, k: (0, j))],
        NN, (M // tm, N // tn, 1), 1, extra, extra_specs,
        [_sds((M, N), out_dtype)], [pl.BlockSpec((tm, tn), lambda i, j, k: (i, j))],
        None, epi, 48)[0]


def _mm_cols(name, a, b3, out_dtype, epilogue=_epi_store, extra=(), extra_specs=(), tm=1024):
    M, K = a.shape
    G, _, n = b3.shape
    tm = _tile(M, tm)
    return _matmul(
        name, [a, b3],
        [pl.BlockSpec((tm, K), lambda i, j, k: (i, 0)), pl.BlockSpec((None, K, n), lambda i, j, k: (j, 0, 0))],
        NN, (M // tm, G, 1), 1, list(extra), list(extra_specs),
        [_sds((M, G * n), out_dtype)], [pl.BlockSpec((tm, n), lambda i, j, k: (i, j))],
        None, epilogue, 40)[0]


def _mm_nt_plain(name, a, b, out_dtype, tm=1024, tn=1024):
    M, K = a.shape
    N = b.shape[0]
    tm, tn = _tile(M, tm), _tile(N, tn)
    return _matmul(
        name, [a, b],
        [pl.BlockSpec((tm, K), lambda i, j, k: (i, 0)), pl.BlockSpec((tn, K), lambda i, j, k: (j, 0))],
        NT, (M // tm, N // tn, 1), 1, [], [],
        [_sds((M, N), out_dtype)], [pl.BlockSpec((tm, tn), lambda i, j, k: (i, j))],
        None, _epi_store, 44)[0]


def _mm_nt_cols(name, a, b3, out_dtype, tm=2048):
    M = a.shape[0]
    G, K, n = b3.shape
    tm = _tile(M, tm)
    return _matmul(
        name, [a, b3],
        [pl.BlockSpec((tm, n), lambda i, j, k: (i, k)), pl.BlockSpec((None, K, n), lambda i, j, k: (k, 0, 0))],
        NT, (M // tm, 1, G), G, [], [],
        [_sds((M, K), out_dtype)], [pl.BlockSpec((tm, K), lambda i, j, k: (i, 0))],
        (tm, K), _epi_store, 40)[0]


def _mm_tn_plain(name, a, b, out_dtype, tt=1024, tn=1024):
    T, K = a.shape
    N = b.shape[1]
    tt, tn = _tile(T, tt), _tile(N, tn)
    return _matmul(
        name, [a, b],
        [pl.BlockSpec((tt, K), lambda i, j, k: (k, 0)), pl.BlockSpec((tt, tn), lambda i, j, k: (k, j))],
        TN, (1, N // tn, T // tt), T // tt, [], [],
        [_sds((K, N), out_dtype)], [pl.BlockSpec((K, tn), lambda i, j, k: (0, j))],
        (K, tn), _epi_store, 48)[0]


def _mm_tn_cols(name, a, b, G, out_dtype, tt=1024, comm=None):
    T, K = a.shape
    n = b.shape[1] // G
    tt = _tile(T, tt)
    r = _matmul(
        name, [a, b],
        [pl.BlockSpec((tt, K), lambda i, j, k: (k, 0)), pl.BlockSpec((tt, n), lambda i, j, k: (k, j))],
        TN, (1, G, T // tt), T // tt, [], [],
        [_sds((G, K, n), out_dtype)], [pl.BlockSpec((None, K, n), lambda i, j, k: (j, 0, 0))],
        (K, n), _epi_store, 48, comm)
    return r[0] if comm is None else (r[0][0], r[1])


def _norm_fwd(name, x, gains, tb=512):
    T, D = x.shape
    G = gains.shape[0]
    tb = _tile(T, tb)

    def body(x_ref, g_ref, *outs):
        xv = x_ref[...]
        xh = xv * lax.rsqrt(jnp.mean(xv * xv, axis=-1, keepdims=True) + NORM_EPS)
        for g in range(G):
            outs[g][...] = (xh * g_ref[g:g + 1, :]).astype(BF)

    return pl.pallas_call(
        body, name=name, grid=(T // tb,),
        in_specs=[pl.BlockSpec((tb, D), lambda i: (i, 0)), pl.BlockSpec((G, D), lambda i: (0, 0))],
        out_specs=[pl.BlockSpec((tb, D), lambda i: (i, 0))] * G,
        out_shape=[_sds((T, D), BF)] * G,
        compiler_params=_cparams(("arbitrary",), 40),
    )(x, gains)


def _norm_bwd(name, x, gains, dhs, dres, tb=256):
    T, D = x.shape
    G = gains.shape[0]
    tb = _tile(T, tb)
    has_res = dres is not None

    def body(*refs):
        x_ref, g_ref = refs[0], refs[1]
        dh_refs = refs[2:2 + G]
        res_ref = refs[2 + G] if has_res else None
        dx_ref, dg_ref, dxb_ref = refs[-3], refs[-2], refs[-1]
        i = pl.program_id(0)
        xv = x_ref[...]
        r = lax.rsqrt(jnp.mean(xv * xv, axis=-1, keepdims=True) + NORM_EPS)
        xh = xv * r
        dx = res_ref[...] if has_res else jnp.zeros_like(xv)
        rows = []
        for g in range(G):
            dh = dh_refs[g][...].astype(F32)
            dy = dh * g_ref[g:g + 1, :]
            dx = dx + r * (dy - xh * jnp.mean(dy * xh, axis=-1, keepdims=True))
            rows.append(jnp.sum(dh * xh, axis=0, keepdims=True))
        dx_ref[...] = dx
        dxb_ref[...] = dx.astype(BF)

        @pl.when(i == 0)
        def _():
            for g in range(G):
                dg_ref[g:g + 1, :] = rows[g]

        @pl.when(i > 0)
        def _():
            for g in range(G):
                dg_ref[g:g + 1, :] += rows[g]

    blk = pl.BlockSpec((tb, D), lambda i: (i, 0))
    ins = [x, gains] + list(dhs) + ([dres] if has_res else [])
    in_specs = [blk, pl.BlockSpec((G, D), lambda i: (0, 0))] + [blk] * (G + (1 if has_res else 0))
    return pl.pallas_call(
        body, name=name, grid=(T // tb,), in_specs=in_specs,
        out_specs=[blk, pl.BlockSpec((G, D), lambda i: (0, 0)), blk],
        out_shape=[_sds((T, D), F32), _sds((G, D), F32), _sds((T, D), BF)],
        compiler_params=_cparams(("arbitrary",), 48),
    )(*ins)


def _pool_fwd(x, gain, tb=256):
    T, D = x.shape
    tb = _tile(T, tb)
    pg = D // len(POOL_WINDOWS)
    per = tb // HALO

    def body(x_ref, xp_ref, g_ref, diff_ref):
        i = pl.program_id(0)
        xx = jnp.concatenate([xp_ref[...], x_ref[...]], axis=0)
        h = xx * lax.rsqrt(jnp.mean(xx * xx, axis=-1, keepdims=True) + NORM_EPS) * g_ref[...]
        row = lax.broadcasted_iota(jnp.int32, (HALO + tb, 1), 0)
        h = jnp.where((row >= HALO) | (i > 0), h, 0.0)
        t = i * tb + row[HALO:] - HALO
        for g, w in enumerate(POOL_WINDOWS):
            hg = h[:, g * pg:(g + 1) * pg]
            s, k = hg, 1
            while k < w:
                s = s + pltpu.roll(s, k, 0)
                k *= 2
            cnt = jnp.minimum(t + 1, w).astype(F32)
            diff_ref[:, g * pg:(g + 1) * pg] = (s[HALO:] / cnt - hg[HALO:]).astype(BF)

    return pl.pallas_call(
        body, name="pool_fwd", grid=(T // tb,),
        in_specs=[pl.BlockSpec((tb, D), lambda i: (i, 0)),
                  pl.BlockSpec((HALO, D), lambda i: (jnp.maximum(i * per - 1, 0), 0)),
                  pl.BlockSpec((1, D), lambda i: (0, 0))],
        out_specs=pl.BlockSpec((tb, D), lambda i: (i, 0)),
        out_shape=_sds((T, D), BF),
        compiler_params=_cparams(("arbitrary",), 40),
    )(x, x, gain)


def _pool_bwd(x, gain, ddiff, dres, tb=256):
    T, D = x.shape
    tb = _tile(T, tb)
    pg = D // len(POOL_WINDOWS)
    per = tb // HALO
    nblk = T // HALO

    def body(x_ref, g_ref, dd_ref, ddn_ref, res_ref, dx_ref, dg_ref):
        i = pl.program_id(0)
        last = i == T // tb - 1
        dd = jnp.concatenate([dd_ref[...], ddn_ref[...]], axis=0)
        row = lax.broadcasted_iota(jnp.int32, (tb + HALO, 1), 0)
        dd = jnp.where((row < tb) | jnp.logical_not(last), dd, 0.0)
        t = i * tb + row
        parts = []
        for g, w in enumerate(POOL_WINDOWS):
            dg_ = dd[:, g * pg:(g + 1) * pg]
            e = dg_ / jnp.minimum(t + 1, w).astype(F32)
            s, k = e, 1
            while k < w:
                s = s + pltpu.roll(s, tb + HALO - k, 0)
                k *= 2
            parts.append(s[:tb] - dg_[:tb])
        dh = jnp.concatenate(parts, axis=1)
        xv = x_ref[...]
        r = lax.rsqrt(jnp.mean(xv * xv, axis=-1, keepdims=True) + NORM_EPS)
        xh = xv * r
        dy = dh * g_ref[...]
        dx_ref[...] = res_ref[...] + r * (dy - xh * jnp.mean(dy * xh, axis=-1, keepdims=True))
        part = jnp.sum(dh * xh, axis=0, keepdims=True)

        @pl.when(i == 0)
        def _():
            dg_ref[...] = part

        @pl.when(i > 0)
        def _():
            dg_ref[...] += part

    blk = pl.BlockSpec((tb, D), lambda i: (i, 0))
    return pl.pallas_call(
        body, name="pool_bwd", grid=(T // tb,),
        in_specs=[blk, pl.BlockSpec((1, D), lambda i: (0, 0)), blk,
                  pl.BlockSpec((HALO, D), lambda i: (jnp.minimum((i + 1) * per, nblk - 1), 0)), blk],
        out_specs=[blk, pl.BlockSpec((1, D), lambda i: (0, 0))],
        out_shape=[_sds((T, D), F32), _sds((1, D), F32)],
        compiler_params=_cparams(("arbitrary",), 48),
    )(x, gain, ddiff, ddiff, dres)


def _pool_w_spec(pg):
    return pl.BlockSpec((4, None, None, pg // 4, pg), lambda i, j, k: (0, j // 2, j % 2, 0, 0))


def _pool_mix(diff, pw, scale, x, tm=1024):
    T, D = x.shape
    pg = D // 4
    tm = _tile(T, tm)

    def epi(acc, ex, outs):
        outs[0][...] = acc
        outs[1][...] = ex[1][...] + acc * ex[0][...]

    blk = pl.BlockSpec((tm, pg), lambda i, j, k: (i, j))
    return _matmul(
        "pool_mix", [diff, pw], [blk, _pool_w_spec(pg)], NN, (T // tm, 4, 1), 1,
        [scale, x], [pl.BlockSpec((1, pg), lambda i, j, k: (0, j)), blk],
        [_sds((T, D), F32), _sds((T, D), F32)], [blk, blk], None, epi, 32)


def _pool_dpre(dx, pre, scale, tb=512):
    T, D = dx.shape
    tb = _tile(T, tb)

    def body(dx_ref, pre_ref, s_ref, dpre_ref, ds_ref):
        i = pl.program_id(0)
        d = dx_ref[...]
        dpre_ref[...] = (d * s_ref[...]).astype(BF)
        part = jnp.sum(d * pre_ref[...], axis=0, keepdims=True)

        @pl.when(i == 0)
        def _():
            ds_ref[...] = part

        @pl.when(i > 0)
        def _():
            ds_ref[...] += part

    blk = pl.BlockSpec((tb, D), lambda i: (i, 0))
    vec = pl.BlockSpec((1, D), lambda i: (0, 0))
    return pl.pallas_call(
        body, name="pool_dpre", grid=(T // tb,), in_specs=[blk, blk, vec], out_specs=[blk, vec],
        out_shape=[_sds((T, D), BF), _sds((1, D), F32)],
        compiler_params=_cparams(("arbitrary",), 40),
    )(dx, pre, scale)


def _pool_ddiff(dpre, pw, tm=1024):
    T, D = dpre.shape
    pg = D // 4
    tm = _tile(T, tm)
    blk = pl.BlockSpec((tm, pg), lambda i, j, k: (i, j))
    return _matmul("pool_ddiff", [dpre, pw], [blk, _pool_w_spec(pg)], NT, (T // tm, 4, 1), 1, [], [],
                   [_sds((T, D), F32)], [blk], None, _epi_store, 32)[0]


def _pool_dw(diff, dpre, tt=1024):
    T, D = diff.shape
    pg = D // 4
    tt = _tile(T, tt)
    blk = pl.BlockSpec((tt, pg), lambda i, j, k: (k, j))
    return _matmul("pool_dw", [diff, dpre], [blk, blk], TN, (1, 4, T // tt), T // tt, [], [],
                   [_sds((4, 2, 2, pg // 4, pg), BF)], [_pool_w_spec(pg)], (pg, pg), _epi_store, 32)[0]


def _ffn_up(name, h, wg, wu, tm=512, comm=None):
    T, D = h.shape
    n = wg.shape[-1]
    F = 4 * n
    tm = _tile(T, tm)

    def body(h_ref, wg_ref, wu_ref, s_ref, ds_ref, a_ref):
        hv = h_ref[...]
        g = jnp.dot(hv, wg_ref[...], preferred_element_type=F32)
        u = jnp.dot(hv, wu_ref[...], preferred_element_type=F32)
        sig = jax.nn.sigmoid(g)
        silu = g * sig
        s_ref[...] = silu.astype(BF)
        ds_ref[...] = (u * (sig * (1.0 + g * (1.0 - sig)))).astype(BF)
        a_ref[...] = (silu * u).astype(BF)

    w_spec = pl.BlockSpec((None, D, n), lambda j, i: (j, 0, 0))
    o_spec = pl.BlockSpec((tm, n), lambda j, i: (i, j))
    return _call(name, body, (4, T // tm), [pl.BlockSpec((tm, D), lambda j, i: (i, 0)), w_spec, w_spec],
                 [o_spec] * 3, [_sds((T, F), BF)] * 3, [], 56, [h, wg, wu], comm)


def _ffn_down(name, a, wd, x, tm=1024, tn=1024):
    T, D = x.shape
    n = wd.shape[1]
    tm, tn = _tile(T, tm), _tile(D, tn)
    return _matmul(
        name, [a, wd],
        [pl.BlockSpec((tm, n), lambda i, j, k: (i, k)),
         pl.BlockSpec((None, n, tn), lambda i, j, k: (k, 0, j))],
        NN, (T // tm, D // tn, 4), 4, [x], [pl.BlockSpec((tm, tn), lambda i, j, k: (i, j))],
        [_sds((T, D), F32)], [pl.BlockSpec((tm, tn), lambda i, j, k: (i, j))],
        (tm, tn), _epi_residual, 40)[0]


def _ffn_bwd_act(name, dx, wd, silu, dsilu_up, tm=1024):
    T, D = dx.shape
    n = wd.shape[1]
    tm = _tile(T, tm)

    def body(dx_ref, w_ref, s_ref, ds_ref, dg_ref, du_ref):
        da = lax.dot_general(dx_ref[...], w_ref[...], NT, preferred_element_type=F32)
        dg_ref[...] = (da * ds_ref[...].astype(F32)).astype(BF)
        du_ref[...] = (da * s_ref[...].astype(F32)).astype(BF)

    blk = pl.BlockSpec((tm, n), lambda i, j: (i, j))
    return pl.pallas_call(
        body, name=name, grid=(T // tm, 4),
        in_specs=[pl.BlockSpec((tm, D), lambda i, j: (i, 0)), pl.BlockSpec((None, n, D), lambda i, j: (j, 0, 0)),
                  blk, blk],
        out_specs=[blk, blk], out_shape=[_sds((T, 4 * n), BF)] * 2,
        compiler_params=_cparams(("arbitrary", "arbitrary"), 56),
    )(dx, wd, silu, dsilu_up)


def _ffn_dh(name, dg, wg, du, wu, tm=512, comm=None):
    T = dg.shape[0]
    D, n = wg.shape[1], wg.shape[2]
    tm = _tile(T, tm)
    a_spec = pl.BlockSpec((tm, n), lambda i, j, k: (i, k))
    w_spec = pl.BlockSpec((None, D, n), lambda i, j, k: (k, 0, 0))
    r = _matmul(
        name, [dg, wg, du, wu], [a_spec, w_spec, a_spec, w_spec], NT, (T // tm, 1, 4), 4, [], [],
        [_sds((T, D), F32)], [pl.BlockSpec((tm, D), lambda i, j, k: (i, 0))], (tm, D), _epi_store, 56, comm)
    return (r[0], []) if comm is None else (r[0][0], r[1])


def _mm_tn_rows(name, a, b, G, out_dtype, tt=1024, tn=1024, comm=None):
    T, N = b.shape
    n = a.shape[1] // G
    tt, tn = _tile(T, tt), _tile(N, tn)
    nj = N // tn
    r = _matmul(
        name, [a, b],
        [pl.BlockSpec((tt, n), lambda i, j, k: (k, j // nj)), pl.BlockSpec((tt, tn), lambda i, j, k: (k, j % nj))],
        TN, (1, G * nj, T // tt), T // tt, [], [],
        [_sds((G, n, N), out_dtype)], [pl.BlockSpec((None, n, tn), lambda i, j, k: (j // nj, 0, j % nj))],
        (n, tn), _epi_store, 48, comm)
    return r[0] if comm is None else (r[0][0], r[1])


def _kv_post(kvp, gain, cos, sin, tb=512):
    T, W = kvp.shape
    KL = W - 128
    tb = _tile(T, tb)

    def body(kv_ref, g_ref, c_ref, s_ref, ckv_ref, kpe_ref):
        lat = kv_ref[:, :KL]
        ckv_ref[...] = (lat * lax.rsqrt(jnp.mean(lat * lat, axis=-1, keepdims=True) + NORM_EPS)
                        * g_ref[...]).astype(BF)
        pe = kv_ref[:, KL:]
        kpe_ref[...] = (pe * c_ref[...] + _swap_halves(pe) * s_ref[...]).astype(BF)

    tab = pl.BlockSpec((tb, 128), lambda i: (i, 0))
    return pl.pallas_call(
        body, name="kv_post", grid=(T // tb,),
        in_specs=[pl.BlockSpec((tb, W), lambda i: (i, 0)), pl.BlockSpec((1, KL), lambda i: (0, 0)), tab, tab],
        out_specs=[pl.BlockSpec((tb, KL), lambda i: (i, 0)), tab],
        out_shape=[_sds((T, KL), BF), _sds((T, 128), BF)],
        compiler_params=_cparams(("arbitrary",), 32),
    )(kvp, gain, cos, sin)


def _kv_post_bwd(kvp, gain, dckv, dkpe_heads, cos, sin, tb=256):
    T, W = kvp.shape
    KL = W - 128
    H = dkpe_heads.shape[0]
    tb = _tile(T, tb)

    def body(kv_ref, g_ref, dc_ref, dk_ref, c_ref, s_ref, out_ref, dg_ref):
        i = pl.program_id(0)
        lat = kv_ref[:, :KL]
        r = lax.rsqrt(jnp.mean(lat * lat, axis=-1, keepdims=True) + NORM_EPS)
        xh = lat * r
        dh = dc_ref[...]
        dy = dh * g_ref[...]
        out_ref[:, :KL] = (r * (dy - xh * jnp.mean(dy * xh, axis=-1, keepdims=True))).astype(BF)
        d = dk_ref[0]
        for h in range(1, H):
            d = d + dk_ref[h]
        out_ref[:, KL:] = (d * c_ref[...] - _swap_halves(d) * s_ref[...]).astype(BF)
        part = jnp.sum(dh * xh, axis=0, keepdims=True)

        @pl.when(i == 0)
        def _():
            dg_ref[...] = part

        @pl.when(i > 0)
        def _():
            dg_ref[...] += part

    tab = pl.BlockSpec((tb, 128), lambda i: (i, 0))
    vec = pl.BlockSpec((1, KL), lambda i: (0, 0))
    return pl.pallas_call(
        body, name="kv_post_bwd", grid=(T // tb,),
        in_specs=[pl.BlockSpec((tb, W), lambda i: (i, 0)), vec, pl.BlockSpec((tb, KL), lambda i: (i, 0)),
                  pl.BlockSpec((H, tb, 128), lambda i: (0, i, 0)), tab, tab],
        out_specs=[pl.BlockSpec((tb, W), lambda i: (i, 0)), vec],
        out_shape=[_sds((T, W), BF), _sds((1, KL), F32)],
        compiler_params=_cparams(("arbitrary",), 32),
    )(kvp, gain, dckv, dkpe_heads, cos, sin)


def _q_up(ql, wqb, cos, sin, tm=1024):
    n = wqb.shape[2]
    tm = _tile(ql.shape[0], tm)

    def epi(acc, ex, outs):
        c, s = ex[0][...] * LOG2_SCALE, ex[1][...] * LOG2_SCALE
        for j in range(n // HEAD_PAD):
            a0 = j * HEAD_PAD
            outs[0][:, a0:a0 + NOPE] = (acc[:, a0:a0 + NOPE] * LOG2_SCALE).astype(BF)
            pe = acc[:, a0 + NOPE:a0 + HEAD_PAD]
            outs[0][:, a0 + NOPE:a0 + HEAD_PAD] = (pe * c + _swap_halves(pe) * s).astype(BF)

    tab = pl.BlockSpec((tm, 128), lambda i, j, k: (i, 0))
    return _mm_cols("q_up", ql, wqb, BF, epilogue=epi, extra=[cos, sin], extra_specs=[tab, tab], tm=tm)


def _causal_mask(tb):
    r = lax.broadcasted_iota(jnp.int32, (tb, tb), 0)
    c = lax.broadcasted_iota(jnp.int32, (tb, tb), 1)
    return r, c


HP = 2
LOG2_SCALE = ATTN_SCALE * math.log2(math.e)


def _fill_keys(k_scr, kv_ref, kpe_ref):
    for hh in range(HP):
        k_scr[hh, :, :NOPE] = kv_ref[:, hh * HEAD_PAD:hh * HEAD_PAD + NOPE]
        k_scr[hh, :, NOPE:] = kpe_ref[...]


def _flash_fwd(qp, kvup, kpe, tq=256, tk=512, comm=None):
    T = qp.shape[0]
    H = qp.shape[1] // HEAD_PAD
    tk = _tile(T, tk)
    tq = min(tq, tk)

    def body(q_ref, kv_ref, kpe_ref, o_ref, lse_ref, k_scr, m_scr, l_scr, acc_scr, sa_scr, sb_scr):
        iq = pl.program_id(1)
        nf = (iq * tq) // tk
        moff = iq * tq - nf * tk

        @pl.when(iq == 0)
        def _():
            _fill_keys(k_scr, kv_ref, kpe_ref)

        qs = [q_ref[:, hh * HEAD_PAD:(hh + 1) * HEAD_PAD] for hh in range(HP)]

        for hh in range(HP):
            m_scr[hh] = jnp.full((tq, 128), NEG, F32)
            l_scr[hh] = jnp.zeros((tq, 128), F32)
            acc_scr[hh] = jnp.zeros((tq, VDIM), F32)

        def scores(ik, s_buf):
            off = pl.multiple_of(ik * tk, tk)
            for hh in range(HP):
                s_buf[hh] = lax.dot_general(qs[hh], k_scr[hh, pl.ds(off, tk), :], NT, preferred_element_type=F32)

        def update(ik, s_buf, masked):
            off = pl.multiple_of(ik * tk, tk)
            ps = []
            for hh in range(HP):
                s = s_buf[hh]
                if masked:
                    r = lax.broadcasted_iota(jnp.int32, (tq, tk), 0)
                    c = lax.broadcasted_iota(jnp.int32, (tq, tk), 1)
                    s = jnp.where(c <= r + moff, s, NEG)
                m = m_scr[hh]
                m2 = jnp.maximum(m, jnp.max(s, axis=-1, keepdims=True))
                p = jnp.exp2(s - jnp.concatenate([m2] * (tk // 128), axis=1))
                a = jnp.exp2(m - m2)
                l_scr[hh] = a * l_scr[hh] + jnp.sum(p, axis=-1, keepdims=True)
                m_scr[hh] = m2
                ps.append((a, p.astype(BF)))
            for hh in range(HP):
                a, p = ps[hh]
                v = kv_ref[pl.ds(off, tk), hh * HEAD_PAD + NOPE:(hh + 1) * HEAD_PAD]
                acc_scr[hh] = a * acc_scr[hh] + lax.dot_general(p, v, NN, preferred_element_type=F32)

        def trip(j, carry):
            scores(2 * j + 1, sb_scr)
            update(2 * j, sa_scr, False)
            scores(2 * j + 2, sa_scr)
            update(2 * j + 1, sb_scr, False)
            return carry

        scores(0, sa_scr)
        lax.fori_loop(0, nf // 2, trip, 0)

        @pl.when(nf % 2 == 0)
        def _():
            update(nf, sa_scr, True)

        @pl.when(nf % 2 == 1)
        def _():
            scores(nf, sb_scr)
            update(nf - 1, sa_scr, False)
            update(nf, sb_scr, True)
        for hh in range(HP):
            l = l_scr[hh]
            o_ref[:, hh * VDIM:(hh + 1) * VDIM] = (acc_scr[hh] / l).astype(BF)
            lse_ref[hh] = (m_scr[hh] + jnp.log2(l))[:, :1]

    return _call(
        "flash_fwd", body, (H // HP, T // tq),
        [pl.BlockSpec((tq, HP * HEAD_PAD), lambda h, i: (i, h)),
         pl.BlockSpec((T, HP * HEAD_PAD), lambda h, i: (0, h)),
         pl.BlockSpec((T, 128), lambda h, i: (0, 0))],
        [pl.BlockSpec((tq, HP * VDIM), lambda h, i: (i, h)),
         pl.BlockSpec((HP, tq, 1), lambda h, i: (h, i, 0))],
        [_sds((T, H * VDIM), BF), _sds((H, T, 1), F32)],
        [pltpu.VMEM((HP, T, HEAD_PAD), BF), pltpu.VMEM((HP, tq, 128), F32), pltpu.VMEM((HP, tq, 128), F32),
         pltpu.VMEM((HP, tq, VDIM), F32), pltpu.VMEM((HP, tq, tk), F32), pltpu.VMEM((HP, tq, tk), F32)],
        56, [qp, kvup, kpe], comm)


def _attn_out_dx(dx, wo, o, tm=1024, tn=1024):
    M, K = dx.shape
    N = wo.shape[0]
    tm, tn = _tile(M, tm), _tile(N, tn)
    hb = tn // VDIM

    def epi(acc, ex, outs):
        d = acc.astype(BF)
        outs[0][...] = d
        for h in range(hb):
            cols = slice(h * VDIM, (h + 1) * VDIM)
            outs[1][h] = jnp.sum(ex[0][:, cols].astype(F32) * d[:, cols].astype(F32), axis=-1, keepdims=True)

    blk = pl.BlockSpec((tm, tn), lambda i, j, k: (i, j))
    return _matmul(
        "attn_out_dx", [dx, wo],
        [pl.BlockSpec((tm, K), lambda i, j, k: (i, 0)), pl.BlockSpec((tn, K), lambda i, j, k: (j, 0))],
        NT, (M // tm, N // tn, 1), 1, [o], [blk],
        [_sds((M, N), BF), _sds((N // VDIM, M, 1), F32)],
        [blk, pl.BlockSpec((hb, tm, 1), lambda i, j, k: (j, i, 0))], None, epi, 48)


def _flash_bwd(qp, kvup, kpe, do, lse_rows, delta_rows, tb=512, comm=None):
    T = qp.shape[0]
    H = qp.shape[1] // HEAD_PAD
    tb = _tile(T, tb)
    nq = T // tb

    def body(kv_ref, kpe_ref, q_ref, do_ref, lse_ref, dl_ref, dkv_ref, dkpe_ref, dq_ref, dk_scr, dv_scr,
             sa_scr, pa_scr, sb_scr, pb_scr):
        ik = pl.program_id(1)

        @pl.when(ik == 0)
        def _():
            dq_ref[...] = jnp.zeros_like(dq_ref)

        k = jnp.concatenate([kv_ref[:, :NOPE], kpe_ref[...]], axis=1)
        v = kv_ref[:, NOPE:]
        dk_scr[...] = jnp.zeros_like(dk_scr)
        dv_scr[...] = jnp.zeros_like(dv_scr)

        def scores(iq, bufs):
            off = pl.multiple_of(iq * tb, tb)
            bufs[0][...] = lax.dot_general(k, q_ref[pl.ds(off, tb), :], NT, preferred_element_type=F32)
            bufs[1][...] = lax.dot_general(v, do_ref[pl.ds(off, tb), :], NT, preferred_element_type=F32)

        def update(iq, bufs, masked):
            off = pl.multiple_of(iq * tb, tb)
            q = q_ref[pl.ds(off, tb), :]
            d = do_ref[pl.ds(off, tb), :]
            st = bufs[0][...]
            if masked:
                r, c = _causal_mask(tb)
                st = jnp.where(r <= c, st, NEG)
            pt = jnp.exp2(st - lse_ref[iq])
            dst = (pt * (bufs[1][...] - dl_ref[iq])).astype(BF)
            dv_scr[...] += lax.dot_general(pt.astype(BF), d, NN, preferred_element_type=F32)
            dk_scr[...] += lax.dot_general(dst, q, NN, preferred_element_type=F32)
            dq_ref[pl.ds(off, tb), :] += lax.dot_general(dst, k, TN, preferred_element_type=F32)

        A, B = (sa_scr, pa_scr), (sb_scr, pb_scr)
        n = nq - 1 - ik
        scores(ik, A)

        @pl.when(n == 0)
        def _():
            update(ik, A, True)

        @pl.when(n > 0)
        def _():
            scores(ik + 1, B)
            update(ik, A, True)

            def trip(j, carry):
                b = ik + 1 + 2 * j
                scores(b + 1, A)
                update(b, B, False)
                scores(b + 2, B)
                update(b + 1, A, False)
                return carry

            pairs = (n - 1) // 2
            lax.fori_loop(0, pairs, trip, 0)
            last = ik + 1 + 2 * pairs

            @pl.when(last == nq - 1)
            def _():
                update(last, B, False)

            @pl.when(last < nq - 1)
            def _():
                scores(last + 1, A)
                update(last, B, False)
                update(last + 1, A, False)
        dk = dk_scr[...] * (ATTN_SCALE / LOG2_SCALE)
        dkv_ref[:, :NOPE] = dk[:, :NOPE].astype(BF)
        dkv_ref[:, NOPE:] = dv_scr[...].astype(BF)
        dkpe_ref[...] = dk[:, NOPE:]

    rows = pl.BlockSpec((None, nq, 1, tb), lambda h, i: (h, 0, 0, 0))
    return _call(
        "flash_bwd", body, (H, nq),
        [pl.BlockSpec((tb, HEAD_PAD), lambda h, i: (i, h)), pl.BlockSpec((tb, 128), lambda h, i: (i, 0)),
         pl.BlockSpec((T, HEAD_PAD), lambda h, i: (0, h)), pl.BlockSpec((T, VDIM), lambda h, i: (0, h)), rows, rows],
        [pl.BlockSpec((tb, HEAD_PAD), lambda h, i: (i, h)), pl.BlockSpec((None, tb, 128), lambda h, i: (h, i, 0)),
         pl.BlockSpec((T, HEAD_PAD), lambda h, i: (0, h))],
        [_sds((T, H * HEAD_PAD), BF), _sds((H, T, 128), F32), _sds((T, H * HEAD_PAD), F32)],
        [pltpu.VMEM((tb, HEAD_PAD), F32), pltpu.VMEM((tb, VDIM), F32)] + [pltpu.VMEM((tb, tb), F32)] * 4, 56,
        [kvup, kpe, qp, do, lse_rows, delta_rows], comm)


def _dq_finish(dq_raw, cos, sin, tb=256):
    T, W = dq_raw.shape
    tb = _tile(T, tb)

    def body(dq_ref, c_ref, s_ref, o_ref):
        c, s = c_ref[...] * ATTN_SCALE, s_ref[...] * ATTN_SCALE
        for h in range(W // HEAD_PAD):
            a0 = h * HEAD_PAD
            o_ref[:, a0:a0 + NOPE] = (dq_ref[:, a0:a0 + NOPE] * ATTN_SCALE).astype(BF)
            dpe = dq_ref[:, a0 + NOPE:a0 + HEAD_PAD]
            o_ref[:, a0 + NOPE:a0 + HEAD_PAD] = (dpe * c - _swap_halves(dpe) * s).astype(BF)

    blk = pl.BlockSpec((tb, W), lambda i: (i, 0))
    tab = pl.BlockSpec((tb, 128), lambda i: (i, 0))
    return pl.pallas_call(
        body, name="dq_finish", grid=(T // tb,), in_specs=[blk, tab, tab], out_specs=blk,
        out_shape=_sds((T, W), BF), compiler_params=_cparams(("arbitrary",), 40),
    )(dq_raw, cos, sin)


def _final_loss(x, gain, target, tb=512):
    T, D = x.shape
    tb = _tile(T, tb)

    def body(x_ref, g_ref, t_ref, loss_ref, dx_ref, dg_ref, dxb_ref):
        i = pl.program_id(0)
        xv = x_ref[...]
        gv = g_ref[...]
        r = lax.rsqrt(jnp.mean(xv * xv, axis=-1, keepdims=True) + NORM_EPS)
        xh = xv * r
        e = xh * gv - t_ref[...]
        lpart = 0.5 * jnp.sum(jnp.mean(e * e, axis=-1, keepdims=True), axis=0, keepdims=True)
        dy = e / D
        dyg = dy * gv
        dx = r * (dyg - xh * jnp.mean(dyg * xh, axis=-1, keepdims=True))
        dx_ref[...] = dx
        dxb_ref[...] = dx.astype(BF)
        gpart = jnp.sum(dy * xh, axis=0, keepdims=True)

        @pl.when(i == 0)
        def _():
            loss_ref[...] = lpart
            dg_ref[...] = gpart

        @pl.when(i > 0)
        def _():
            loss_ref[...] += lpart
            dg_ref[...] += gpart

    blk = pl.BlockSpec((tb, D), lambda i: (i, 0))
    vec = pl.BlockSpec((1, D), lambda i: (0, 0))
    return pl.pallas_call(
        body, name="final_loss", grid=(T // tb,), in_specs=[blk, vec, blk],
        out_specs=[pl.BlockSpec((1, 1), lambda i: (0, 0)), blk, vec, blk],
        out_shape=[_sds((1, 1), F32), _sds((T, D), F32), _sds((1, D), F32), _sds((T, D), BF)],
        compiler_params=_cparams(("arbitrary",), 56),
    )(x, gain, target)


def _row_tile(R, pref=256):
    t = (min(R, pref) // 16) * 16
    while t >= 16:
        if R % t == 0:
            return t
        t -= 16
    return R


def _prefetch_call(body, name, grid, in_specs, out_specs, out_shape, scalar, operands, vmem_mb=32):
    return pl.pallas_call(
        body, name=name, out_shape=out_shape,
        grid_spec=pltpu.PrefetchScalarGridSpec(num_scalar_prefetch=1, grid=grid, in_specs=in_specs,
                                               out_specs=out_specs),
        compiler_params=_cparams(("arbitrary",) * len(grid), vmem_mb),
    )(scalar, *operands)


def _place_own(name, s, chip):
    _, n, R, C = s.shape
    tr = _row_tile(R, 512)

    def body(chip_ref, s_ref, o_ref):
        o_ref[...] = s_ref[...]

    return _prefetch_call(
        body, name, (2, n, R // tr),
        [pl.BlockSpec((None, None, tr, C), lambda h, j, r, cr: (h, j, r, 0))],
        pl.BlockSpec((None, None, None, tr, C), lambda h, j, r, cr: (cr[0], h, j, r, 0)),
        _sds((4,) + s.shape, s.dtype), chip, [s])


def _pair_add(name, g, theirs, core):
    _, _, n, R, C = g.shape
    tr = _row_tile(R)

    def body(c_ref, a_ref, b_ref, o_ref):
        o_ref[...] = (a_ref[...].astype(F32) + b_ref[...].astype(F32)).astype(BF)

    blk = pl.BlockSpec((None, None, tr, C), lambda k, j, r, cr: (k, j, r, 0))
    return _prefetch_call(
        body, name, (4, n, R // tr),
        [pl.BlockSpec((None, None, None, tr, C), lambda k, j, r, cr: (k, cr[0], j, r, 0)), blk], blk,
        _sds(theirs.shape, BF), core, [g, theirs])


def _chip_sum(name, sums, landed, chip):
    _, n, R, C = sums.shape
    tr = _row_tile(R)

    def body(chip_ref, own_ref, l_ref, o_ref):
        s = own_ref[...].astype(F32)
        for j in range(3):
            s = s + l_ref[j].astype(F32)
        o_ref[...] = s

    return _prefetch_call(
        body, name, (n, R // tr),
        [pl.BlockSpec((None, None, tr, C), lambda j, r, cr: (cr[0], j, r, 0)),
         pl.BlockSpec((3, None, tr, C), lambda j, r, cr: (0, j, r, 0))],
        pl.BlockSpec((None, tr, C), lambda j, r, cr: (j, r, 0)),
        _sds((n, R, C), F32), chip, [sums, landed])


def _adamw_halves(name, w, m, v, g_mine, g_theirs, core, tr=128):
    L, _, R, C = w.shape
    tr = _row_tile(R, tr)
    c1 = 1.0 - ADAM_B1 ** ADAM_STEP
    c2 = 1.0 - ADAM_B2 ** ADAM_STEP

    def body(c_ref, w_ref, m_ref, v_ref, gm_ref, gt_ref, g_ref, d_ref, nm_ref, nv_ref):
        gv = jnp.where(pl.program_id(1) == c_ref[0], gm_ref[...], gt_ref[...])
        nm = ADAM_B1 * m_ref[...] + (1.0 - ADAM_B1) * gv
        nv = ADAM_B2 * v_ref[...] + (1.0 - ADAM_B2) * (gv * gv)
        g_ref[...] = gv
        nm_ref[...] = nm
        nv_ref[...] = nv
        d_ref[...] = -ADAM_LR * ((nm / c1) / (jnp.sqrt(nv / c2) + ADAM_EPS) + ADAM_WD * w_ref[...])

    full = pl.BlockSpec((None, None, tr, C), lambda l, h, r, cr: (l, h, r, 0))
    half = pl.BlockSpec((None, tr, C), lambda l, h, r, cr: (l, r, 0))
    return _prefetch_call(
        body, name, (L, 2, R // tr), [full, full, full, half, half], [full] * 4,
        [_sds(w.shape, F32)] * 4, core, [w, m, v, g_mine, g_theirs])


def _adamw(name, w, g, m, v, tr=128):
    R, C = w.shape
    tr = _tile(R, tr) if R % 8 == 0 else R
    c1 = 1.0 - ADAM_B1 ** ADAM_STEP
    c2 = 1.0 - ADAM_B2 ** ADAM_STEP

    def body(w_ref, g_ref, m_ref, v_ref, d_ref, nm_ref, nv_ref):
        gv = g_ref[...]
        nm = ADAM_B1 * m_ref[...] + (1.0 - ADAM_B1) * gv
        nv = ADAM_B2 * v_ref[...] + (1.0 - ADAM_B2) * (gv * gv)
        nm_ref[...] = nm
        nv_ref[...] = nv
        d_ref[...] = -ADAM_LR * ((nm / c1) / (jnp.sqrt(nv / c2) + ADAM_EPS) + ADAM_WD * w_ref[...])

    blk = pl.BlockSpec((tr, C), lambda i: (i, 0))
    return pl.pallas_call(
        body, name=name, grid=(R // tr,), in_specs=[blk] * 4, out_specs=[blk] * 3,
        out_shape=[_sds((R, C), F32)] * 3, compiler_params=_cparams(("arbitrary",), 32),
    )(w, g, m, v)


def _place():
    x, y, c = lax.axis_index("x"), lax.axis_index("y"), lax.axis_index("c")
    chips = [(1 - x, y), (x, 1 - y), (1 - x, 1 - y)]
    return x, y, c, chips


def _gather_comm(shards, placed):
    n = len(shards)

    def copies(cin, cout, send, recv):
        src, dst = cin[:n], cout
        x, y, c, chips = _place()
        me = 2 * x + y

        def remote(i, k, s, d, to):
            return pltpu.make_async_remote_copy(src_ref=s, dst_ref=d, send_sem=send.at[6 * i + k],
                                                recv_sem=recv.at[6 * i + k], device_id=to, device_id_type=MESH)

        first = [remote(i, j, src[i].at[c], dst[i].at[me, c], (*chip, c))
                 for i in range(n) for j, chip in enumerate(chips)]
        return remote, first, dst, (x, y, c, chips)

    def start(cin, cout, send, recv):
        for cp in copies(cin, cout, send, recv)[1]:
            cp.start()

    def finish(cin, cout, send, recv):
        remote, first, dst, (x, y, c, chips) = copies(cin, cout, send, recv)
        sibling = (x, y, 1 - c)
        passed = []
        for i in range(n):
            for j, (px, py) in enumerate(chips):
                slot = dst[i].at[2 * px + py, c]
                remote(i, j, slot, slot, (px, py, c)).wait_recv()
                cp = remote(i, 3 + j, slot, slot, sibling)
                cp.start()
                passed.append(cp)
        for i in range(n):
            for j, (px, py) in enumerate(chips):
                slot = dst[i].at[2 * px + py, 1 - c]
                remote(i, 3 + j, slot, slot, sibling).wait_recv()
        for cp in first + passed:
            cp.wait_send()

    return _Comm(list(shards) + list(placed), [_sds(p.shape, p.dtype) for p in placed],
                 {n + i: i for i in range(n)}, 6 * n, start, finish)


def _pair_comm(grads):
    n = len(grads)

    def copies(cin, cout, send, recv):
        x, y, c, _ = _place()
        return [pltpu.make_async_remote_copy(
            src_ref=cin[i].at[k, 1 - c], dst_ref=cout[i].at[k], send_sem=send.at[4 * i + k],
            recv_sem=recv.at[4 * i + k], device_id=(x, y, 1 - c), device_id_type=MESH)
            for i in range(n) for k in range(4)]

    def start(cin, cout, send, recv):
        for cp in copies(cin, cout, send, recv):
            cp.start()

    def finish(cin, cout, send, recv):
        for cp in copies(cin, cout, send, recv):
            cp.wait()

    return _Comm(grads, [_sds((4,) + a.shape[2:], a.dtype) for a in grads], {}, 4 * n, start, finish)


def _chips_comm(sums):
    n = len(sums)

    def copies(cin, cout, send, recv):
        x, y, c, chips = _place()
        return [pltpu.make_async_remote_copy(
            src_ref=cin[i].at[2 * px + py], dst_ref=cout[i].at[j], send_sem=send.at[3 * i + j],
            recv_sem=recv.at[3 * i + j], device_id=(px, py, c), device_id_type=MESH)
            for i in range(n) for j, (px, py) in enumerate(chips)]

    def start(cin, cout, send, recv):
        for cp in copies(cin, cout, send, recv):
            cp.start()

    def finish(cin, cout, send, recv):
        for cp in copies(cin, cout, send, recv):
            cp.wait()

    return _Comm(sums, [_sds((3,) + a.shape[1:], a.dtype) for a in sums], {}, 3 * n, start, finish)


def _share_comm(halves):
    n = len(halves)

    def copies(cin, cout, send, recv):
        x, y, c, _ = _place()
        return [pltpu.make_async_remote_copy(
            src_ref=cin[i], dst_ref=cout[i], send_sem=send.at[i], recv_sem=recv.at[i],
            device_id=(x, y, 1 - c), device_id_type=MESH) for i in range(n)]

    def start(cin, cout, send, recv):
        for cp in copies(cin, cout, send, recv):
            cp.start()

    def finish(cin, cout, send, recv):
        for cp in copies(cin, cout, send, recv):
            cp.wait()

    return _Comm(halves, [_sds(a.shape, a.dtype) for a in halves], {}, n, start, finish)


def _allreduce_small(v):
    R, C = v.shape

    def body(v_ref, out_ref, land, send, recv):
        x, y, c, _ = _place()
        me = 4 * x + 2 * y + c
        cps = []
        for m in range(1, 8):
            fx, fy, fc = (m >> 2) & 1, (m >> 1) & 1, m & 1
            peer = (x ^ fx, y ^ fy, c ^ fc)
            cp = pltpu.make_async_remote_copy(
                src_ref=v_ref, dst_ref=land.at[me], send_sem=send.at[m - 1], recv_sem=recv.at[m - 1],
                device_id=peer, device_id_type=MESH)
            cp.start()
            cps.append(cp)
        land[me] = v_ref[...]
        for cp in cps:
            cp.wait()
        s = land[0]
        for d in range(1, 8):
            s = s + land[d]
        out_ref[...] = s

    return pl.pallas_call(
        body, name="allreduce_small",
        in_specs=[pl.BlockSpec(memory_space=pltpu.VMEM)], out_specs=pl.BlockSpec(memory_space=pltpu.VMEM),
        out_shape=_sds((R, C), F32),
        scratch_shapes=[pltpu.VMEM((8, R, C), F32), pltpu.SemaphoreType.DMA((7,)), pltpu.SemaphoreType.DMA((7,))],
    )(v)


def _halves(a):
    return a.reshape((2, a.shape[0] // 2) + a.shape[1:])


def _canon(a, lead):
    piece = a.shape[lead:]
    return a.reshape(a.shape[:lead] + (int(math.prod(piece[:-2])),) + piece[-2:])


def _rope_tables(pos):
    half = ROPE // 2
    inv_freq = ROPE_BASE ** (-jnp.arange(half, dtype=F32) / half)
    ang = pos.astype(F32)[:, None] * inv_freq
    cos, sin = jnp.cos(ang), jnp.sin(ang)
    return jnp.tile(cos, (1, 4)), jnp.concatenate([-sin, sin, -sin, sin], axis=1)


def kernel(x, positions, pool_norm, pool_w, pool_scale, kv_in_norm, w_kv_a, kv_latent_norm, w_kv_b, attn_norm, w_q_a, q_latent_norm, w_q_b, w_o, ffn_norm, w_gate, w_up, w_down, final_norm, loss_target, m_pool_norm, m_pool_w, m_pool_scale, m_kv_in_norm, m_w_kv_a, m_kv_latent_norm, m_w_kv_b, m_attn_norm, m_w_q_a, m_q_latent_norm, m_w_q_b, m_w_o, m_ffn_norm, m_w_gate, m_w_up, m_w_down, m_final_norm, v_pool_norm, v_pool_w, v_pool_scale, v_kv_in_norm, v_w_kv_a, v_kv_latent_norm, v_w_kv_b, v_attn_norm, v_w_q_a, v_q_latent_norm, v_w_q_b, v_w_o, v_ffn_norm, v_w_gate, v_w_up, v_w_down, v_final_norm):
    T, D = x.shape[1], x.shape[2]
    H = N_HEADS
    KL = kv_latent_norm.shape[0]
    QL = q_latent_norm.shape[1]
    pg = D // 4
    x0, tgt = x[0], loss_target[0]
    cos, sin = _rope_tables(positions[0])
    chip = 2 * lax.axis_index("x") + lax.axis_index("y")

    n_kva = w_kv_a.shape[1]
    wkva_s = jnp.pad(w_kv_a, ((0, 0), (0, KL + 128 - n_kva)))
    hs = w_q_b.shape[2] // (NOPE + ROPE)
    wqb_s = jnp.pad(w_q_b[0].reshape(QL, hs, NOPE + ROPE), ((0, 0), (0, 0), (0, HEAD_PAD - NOPE - ROPE)))
    wqb_s = wqb_s.reshape(QL, hs * HEAD_PAD)
    chip_s = chip.astype(jnp.int32).reshape(1)
    core_s = lax.axis_index("c").astype(jnp.int32).reshape(1)

    def halved(a):
        return _canon(_halves(a.astype(BF)), 1)

    def gather_group(tag, shards):
        placed = [_place_own(f"place_own{tag}_{i}", s, chip_s) for i, s in enumerate(shards)]
        return _gather_comm(shards, placed)

    def whole(a, *shape):
        return a.reshape(shape)

    n_ff = w_gate.shape[2]
    group0 = gather_group("0", [halved(pool_w[0]), _canon(jnp.stack([pool_norm, pool_scale]), 1),
                                halved(w_gate[0]), halved(w_up[0])])
    group1 = gather_group("1", [halved(w_down[0]), halved(wkva_s), halved(w_kv_b), halved(w_q_a[0]),
                                halved(wqb_s), halved(w_o[0])])
    group2 = gather_group("2", [halved(w_gate[1]), halved(w_up[1]), halved(w_down[1])])
    PW, PV, WG0, WU0 = _comm_call("gather0", group0)
    WG0, WU0 = whole(WG0, 4, D, n_ff), whole(WU0, 4, D, n_ff)
    pv = jnp.transpose(PV.reshape(4, 2, pg), (1, 0, 2)).reshape(2, D)
    pn_full, ps_full = pv[0:1], pv[1:2]

    diff = _pool_fwd(x0, pn_full)
    pre, x1 = _pool_mix(diff, PW, ps_full, x0)
    (h1,) = _norm_fwd("norm_ffn0", x1, ffn_norm[0:1])
    (silu0, dsu0, a0), (WD0, WKVA, WKVB, WQA, WQB, WO) = _ffn_up("ffn_up0", h1, WG0, WU0, comm=group1)
    WD0 = whole(WD0, 4, n_ff, D)
    WKVA = whole(WKVA, D, KL + 128)
    WKVB = whole(WKVB, 4, KL, -1)
    WQA = whole(WQA, D, QL)
    WQB = whole(WQB, 4, QL, -1)
    WO = whole(WO, H * VDIM, D)
    x2 = _ffn_down("ffn_down0", a0, WD0, x1)
    hk, ha = _norm_fwd("norm_attn", x2, jnp.stack([kv_in_norm, attn_norm[0]]))
    kvp = _mm_plain("kv_a", hk, WKVA, F32)
    ckv, kpe = _kv_post(kvp, kv_latent_norm[None], cos, sin)
    kvup = _mm_cols("kv_b", ckv, WKVB, BF)
    qa = _mm_plain("q_a", ha, WQA, F32)
    (ql,) = _norm_fwd("norm_q", qa, q_latent_norm)
    qp = _q_up(ql, WQB, cos, sin)
    (o, lse), (WG1, WU1, WD1) = _flash_fwd(qp, kvup, kpe, comm=group2)
    WG1, WU1, WD1 = whole(WG1, 4, D, n_ff), whole(WU1, 4, D, n_ff), whole(WD1, 4, n_ff, D)
    x3 = _mm_plain("attn_out", o, WO, F32, res=x2)
    (h3,) = _norm_fwd("norm_ffn1", x3, ffn_norm[1:2])
    (silu1, dsu1, a1), _ = _ffn_up("ffn_up1", h3, WG1, WU1)
    x4 = _ffn_down("ffn_down1", a1, WD1, x3)
    loss_part, dx4, d_final, dx4b = _final_loss(x4, final_norm[None], tgt)

    def pair_views(grads):
        return [_canon(a.reshape((4, 2, a.shape[1] // 2) + a.shape[2:]), 2) for a in grads]

    def pair_sums(tag, full, theirs):
        return [_pair_add(f"pair_add{tag}_{i}", a, b, core_s) for i, (a, b) in enumerate(zip(full, theirs))]

    def pair_stage(tag, grads):
        full = pair_views(grads)
        return pair_sums(tag, full, _comm_call("reduce_pair" + tag, _pair_comm(full)))

    dWD1 = _mm_tn_rows("ffn_dwd1", a1, dx4b, 4, BF)
    dg1, du1 = _ffn_bwd_act("ffn_bwd_act1", dx4b, WD1, silu1, dsu1)
    dWG1 = _mm_tn_cols("ffn_dwg1", h3, dg1, 4, BF)
    dWU1 = _mm_tn_cols("ffn_dwu1", h3, du1, 4, BF)
    full1 = pair_views([dWG1, dWU1, dWD1])
    dh3, theirs1 = _ffn_dh("ffn_dh1", dg1, WG1, du1, WU1, comm=_pair_comm(full1))
    sums1 = pair_sums("1", full1, theirs1)
    dx3, d_ffn1, dx3b = _norm_bwd("norm_ffn1_bwd", x3, ffn_norm[1:2], [dh3], dx4)

    do, delta = _attn_out_dx(dx3b, WO, o)
    dWO = _mm_tn_plain("attn_out_dw", o, dx3b, BF)
    tb = _tile(T, 512)
    lse_rows = lse.reshape(H, T // tb, 1, tb)
    delta_rows = delta.reshape(H, T // tb, 1, tb)
    (dkvup, dkpe_h, dq_raw), landed1 = _flash_bwd(qp, kvup, kpe, do, lse_rows, delta_rows,
                                                  comm=_chips_comm(sums1))
    dqp = _dq_finish(dq_raw, cos, sin)
    dql = _mm_nt_cols("q_b_dx", dqp, WQB, F32)
    dWQB = _mm_tn_cols("q_b_dw", ql, dqp, 4, BF)
    _, d_qln, dqa = _norm_bwd("norm_q_bwd", qa, q_latent_norm, [dql], None)
    dha = _mm_nt_plain("q_a_dx", dqa, WQA, F32)
    dWQA = _mm_tn_plain("q_a_dw", ha, dqa, BF)
    dckv = _mm_nt_cols("kv_b_dx", dkvup, WKVB, F32)
    dWKVB = _mm_tn_cols("kv_b_dw", ckv, dkvup, 4, BF)
    dkvp, d_kvln = _kv_post_bwd(kvp, kv_latent_norm[None], dckv, dkpe_h, cos, sin)
    dhk = _mm_nt_plain("kv_a_dx", dkvp, WKVA, F32)
    dWKVA = _mm_tn_plain("kv_a_dw", hk, dkvp, BF)
    dx2, d_n2, dx2b = _norm_bwd("norm_attn_bwd", x2, jnp.stack([kv_in_norm, attn_norm[0]]), [dhk, dha], dx3)

    def chip_sums(tag, sums, landed):
        return [_chip_sum(f"chip_sum{tag}_{i}", a, b, chip_s) for i, (a, b) in enumerate(zip(sums, landed))]

    sums_a = pair_stage("a", [dWKVA.reshape(4, D // 4, KL + 128), dWKVB, dWQA.reshape(4, D // 4, QL), dWQB,
                              dWO.reshape(4, H * VDIM // 4, D)])
    dWD0, landed_a = _mm_tn_rows("ffn_dwd0", a0, dx2b, 4, BF, comm=_chips_comm(sums_a))
    halves_a1 = chip_sums("a", sums_a, landed_a) + chip_sums("1", sums1, landed1)
    dg0, du0 = _ffn_bwd_act("ffn_bwd_act0", dx2b, WD0, silu0, dsu0)
    dWG0 = _mm_tn_cols("ffn_dwg0", h1, dg0, 4, BF)
    dWU0, other_a1 = _mm_tn_cols("ffn_dwu0", h1, du0, 4, BF, comm=_share_comm(halves_a1))
    sums0 = pair_stage("0", [dWG0, dWU0, dWD0])
    dh1, landed0 = _ffn_dh("ffn_dh0", dg0, WG0, du0, WU0, comm=_chips_comm(sums0))
    dx1, d_ffn0, _ = _norm_bwd("norm_ffn0_bwd", x1, ffn_norm[0:1], [dh1], dx2)

    dpre, d_ps = _pool_dpre(dx1, pre, ps_full)
    ddiff = _pool_ddiff(dpre, PW)
    dPW = _pool_dw(diff, dpre)
    dx0, d_pn = _pool_bwd(x0, pn_full, ddiff, dx1)

    sums_p = pair_stage("p", [dPW])
    landed_p = _comm_call("reduce_chips", _chips_comm(sums_p))
    halves_p0 = chip_sums("p", sums_p, landed_p) + chip_sums("0", sums0, landed0)
    other_p0 = _comm_call("share_halves", _share_comm(halves_p0))
    halves = halves_p0[:1] + halves_a1[:5] + halves_p0[1:] + halves_a1[5:]
    other = list(other_p0[:1]) + list(other_a1[:5]) + list(other_p0[1:]) + list(other_a1[5:])

    def rows(a):
        return a.reshape((1, -1, a.shape[-1]))

    def unpad_kva(a):
        return rows(a)[:, :, :n_kva]

    def unpad_qb(a):
        return rows(a).reshape(1, -1, hs, HEAD_PAD)[:, :, :, :NOPE + ROPE].reshape(1, -1, hs * (NOPE + ROPE))

    def layers(a0_, a1_):
        return jnp.concatenate([rows(a0_), rows(a1_)], axis=0)

    def by_name(hv):
        return {"pool_w": rows(hv[0]), "w_kv_a": unpad_kva(hv[1]), "w_kv_b": rows(hv[2]), "w_q_a": rows(hv[3]),
                "w_q_b": unpad_qb(hv[4]), "w_o": rows(hv[5]), "w_gate": layers(hv[6], hv[9]),
                "w_up": layers(hv[7], hv[10]), "w_down": layers(hv[8], hv[11])}

    big = ["pool_w", "w_kv_a", "w_kv_b", "w_q_a", "w_q_b", "w_o", "w_gate", "w_up", "w_down"]
    g_mine, g_other = by_name(halves), by_name(other)

    lat = jnp.concatenate([d_kvln[0], d_qln[0], jnp.zeros((D - KL - QL,), F32)])
    lrow = jnp.pad(loss_part[0], (0, D - 1))
    small = jnp.stack([d_n2[0], d_n2[1], d_ffn0[0], d_ffn1[0], d_final[0], lat, d_pn[0], d_ps[0], lrow]
                      + [jnp.zeros((D,), F32)] * 7)
    red = _allreduce_small(small)
    loss = red[8, 0]
    g_kv_in, g_attn, g_final = red[0], red[1:2], red[4]
    g_ffn = red[2:4]
    g_kvln, g_qln = red[5, :KL], red[5:6, KL:KL + QL]
    g_pn = lax.dynamic_slice(red[6:7], (0, chip * pg), (1, pg))
    g_ps = lax.dynamic_slice(red[7:8], (0, chip * pg), (1, pg))

    grads = {"pool_norm": g_pn, "pool_scale": g_ps, "kv_in_norm": g_kv_in, "kv_latent_norm": g_kvln,
             "attn_norm": g_attn, "q_latent_norm": g_qln, "ffn_norm": g_ffn, "final_norm": g_final}
    weights = dict(pool_norm=pool_norm, pool_w=pool_w, pool_scale=pool_scale, kv_in_norm=kv_in_norm, w_kv_a=w_kv_a,
                   kv_latent_norm=kv_latent_norm, w_kv_b=w_kv_b, attn_norm=attn_norm, w_q_a=w_q_a,
                   q_latent_norm=q_latent_norm, w_q_b=w_q_b, w_o=w_o, ffn_norm=ffn_norm, w_gate=w_gate, w_up=w_up,
                   w_down=w_down, final_norm=final_norm)
    ms = dict(pool_norm=m_pool_norm, pool_w=m_pool_w, pool_scale=m_pool_scale, kv_in_norm=m_kv_in_norm,
              w_kv_a=m_w_kv_a, kv_latent_norm=m_kv_latent_norm, w_kv_b=m_w_kv_b, attn_norm=m_attn_norm,
              w_q_a=m_w_q_a, q_latent_norm=m_q_latent_norm, w_q_b=m_w_q_b, w_o=m_w_o, ffn_norm=m_ffn_norm,
              w_gate=m_w_gate, w_up=m_w_up, w_down=m_w_down, final_norm=m_final_norm)
    vs = dict(pool_norm=v_pool_norm, pool_w=v_pool_w, pool_scale=v_pool_scale, kv_in_norm=v_kv_in_norm,
              w_kv_a=v_w_kv_a, kv_latent_norm=v_kv_latent_norm, w_kv_b=v_w_kv_b, attn_norm=v_attn_norm,
              w_q_a=v_w_q_a, q_latent_norm=v_q_latent_norm, w_q_b=v_w_q_b, w_o=v_w_o, ffn_norm=v_ffn_norm,
              w_gate=v_w_gate, w_up=v_w_up, w_down=v_w_down, final_norm=v_final_norm)
    names = list(weights)

    def as2d(a):
        return a.reshape((-1, a.shape[-1]))

    delta_w, new_m, new_v = {}, {}, {}
    for nm in big:
        n_layers = g_mine[nm].shape[0]

        def two_halves(a):
            a = as2d(a)
            return a.reshape(n_layers, 2, a.shape[0] // (2 * n_layers), a.shape[1])

        g, d, m2, v2 = _adamw_halves("adamw_" + nm, two_halves(weights[nm]), two_halves(ms[nm]),
                                     two_halves(vs[nm]), g_mine[nm], g_other[nm], core_s)
        shp = weights[nm].shape
        grads[nm], delta_w[nm], new_m[nm], new_v[nm] = g.reshape(shp), d.reshape(shp), m2.reshape(shp), v2.reshape(shp)
    groups = [["kv_in_norm", "attn_norm", "ffn_norm", "final_norm"], ["kv_latent_norm", "q_latent_norm"],
              ["pool_norm", "pool_scale"]]
    for gi, grp in enumerate(groups):
        cat = lambda t: jnp.concatenate([as2d(t[nm]) for nm in grp], axis=0)
        d, m2, v2 = _adamw(f"adamw_vec{gi}", cat(weights), cat(grads), cat(ms), cat(vs))
        r0 = 0
        for nm in grp:
            shp = weights[nm].shape
            r = as2d(weights[nm]).shape[0]
            delta_w[nm], new_m[nm], new_v[nm] = (d[r0:r0 + r].reshape(shp), m2[r0:r0 + r].reshape(shp),
                                                 v2[r0:r0 + r].reshape(shp))
            r0 += r

    return (loss, dx0[None], *[grads[nm].reshape(weights[nm].shape) for nm in names],
            *[delta_w[nm] for nm in names], *[new_m[nm] for nm in names], *[new_v[nm] for nm in names])
```
